```python
import math
import jax, jax.numpy as jnp
from jax import lax
import numpy as np

D_MODEL = 1024
BATCH = 4
SEQ = 4096
DEPTH = 1

HEAD_DIM = 64
A_HEADS = 8
A_KV_HEADS = 2
A_WINDOW = 128
B_HEADS = 8
B_KV_GROUPS = 2
CMP_BLOCK = 32
CMP_STRIDE = 16
CMP_HIDDEN = 256
SLC_BLOCK = 64
SLC_TOPK = 16
NSA_WINDOW = 512
NUM_BUCKETS = 32
MAX_DISTANCE = 128
D_FF = 2816
CONV_WIDTH = 3
Q_BLOCK = 128
EPS = 1e-6
NEG = -1e30
BIG = 1e30

A_Q_DIM = A_HEADS * HEAD_DIM
A_KV_DIM = A_KV_HEADS * HEAD_DIM
B_Q_DIM = B_HEADS * HEAD_DIM
B_KV_DIM = B_KV_GROUPS * HEAD_DIM
B_GATE_DIM = 3 * B_HEADS
IN_SPLITS = (A_Q_DIM, A_KV_DIM, A_KV_DIM, B_Q_DIM, B_KV_DIM, B_KV_DIM, B_KV_DIM, B_KV_DIM, B_KV_DIM, B_KV_DIM, B_GATE_DIM, D_MODEL, D_MODEL)
IN_DIM = sum(IN_SPLITS)
IN_OFFSETS = tuple(int(o) for o in np.cumsum(IN_SPLITS)[:-1])

kernel_name = 'hybrid_swa_nsa_convffn'


def rmsnorm(x, g):
    xf = x.astype(jnp.float32)
    y = xf * lax.rsqrt(jnp.mean(xf * xf, axis=-1, keepdims=True) + EPS)
    return (y * g.astype(jnp.float32)).astype(x.dtype)


def t5_bucket(dist):
    dist = jnp.maximum(dist, 0)
    max_exact = NUM_BUCKETS // 2
    d = jnp.maximum(dist, 1).astype(jnp.float32)
    large = max_exact + (jnp.log(d / max_exact) / math.log(MAX_DISTANCE / max_exact)
                         * (NUM_BUCKETS - max_exact)).astype(jnp.int32)
    large = jnp.minimum(large, NUM_BUCKETS - 1)
    return jnp.where(dist < max_exact, dist, large)


def head_bias(table_gr, dist):
    return jnp.transpose(table_gr[t5_bucket(dist)], (2, 3, 0, 1))


def selection_overlap(n_cmp, n_slc):
    r = SLC_BLOCK // CMP_STRIDE
    c = CMP_BLOCK // CMP_STRIDE
    j, m, n = np.meshgrid(np.arange(n_slc), np.arange(r), np.arange(c), indexing='ij')
    i = r * j + m - n
    ok = (i >= 0) & (i < n_cmp)
    mat = np.zeros((n_cmp, n_slc), np.float32)
    np.add.at(mat, (i[ok], j[ok]), 1.0)
    return mat


def causal_dwconv(u, w, b):
    c = u.shape[-1]
    out = lax.conv_general_dilated(u, w[:, None, :].astype(u.dtype), window_strides=(1,),
                                   padding=[(CONV_WIDTH - 1, 0)],
                                   dimension_numbers=('NWC', 'WIO', 'NWC'),
                                   feature_group_count=c)
    return out + b


def swa_sink_attention(q, k, v, sinks, table):
    bsz, seq, n_heads, dh = q.shape
    n_groups = k.shape[2]
    rep = n_heads // n_groups
    span = A_WINDOW + Q_BLOCK
    qg = q.reshape(bsz, seq, n_groups, rep, dh)
    kp = jnp.pad(k, ((0, 0), (A_WINDOW, 0), (0, 0), (0, 0)))
    vp = jnp.pad(v, ((0, 0), (A_WINDOW, 0), (0, 0), (0, 0)))
    qi = jnp.arange(Q_BLOCK)[:, None]
    ki = jnp.arange(span)[None, :]
    dist = qi + A_WINDOW - ki
    in_window = (dist >= 0) & (dist < A_WINDOW)
    bias = head_bias(table.reshape(NUM_BUCKETS, n_groups, rep), dist)
    sink = sinks.astype(jnp.float32).reshape(n_groups, rep)
    scale = dh ** -0.5

    def block(b):
        start = b * Q_BLOCK
        qb = lax.dynamic_slice_in_dim(qg, start, Q_BLOCK, axis=1)
        kb = lax.dynamic_slice_in_dim(kp, start, span, axis=1)
        vb = lax.dynamic_slice_in_dim(vp, start, span, axis=1)
        valid = in_window & (start - A_WINDOW + ki >= 0)
        logits = jnp.einsum('bqgrd,bkgd->bgrqk', qb, kb, preferred_element_type=jnp.float32) * scale + bias
        logits = jnp.where(valid, logits, NEG)
        sink_col = jnp.broadcast_to(sink[None, :, :, None, None], logits.shape[:-1] + (1,))
        probs = jax.nn.softmax(jnp.concatenate([logits, sink_col], axis=-1), axis=-1)[..., :span]
        out = jnp.einsum('bgrqk,bkgd->bqgrd', probs.astype(v.dtype), vb)
        return out.reshape(bsz, Q_BLOCK, n_heads * dh)

    out = lax.map(block, jnp.arange(seq // Q_BLOCK))
    return jnp.transpose(out, (1, 0, 2, 3)).reshape(bsz, seq, n_heads * dh)


def compress(kv, pos, w1, w2):
    bsz, seq, n_groups, dh = kv.shape
    n_cmp = (seq - CMP_BLOCK) // CMP_STRIDE + 1
    idx = jnp.arange(n_cmp)[:, None] * CMP_STRIDE + jnp.arange(CMP_BLOCK)[None, :]
    blocks = kv[:, idx] + pos[None, None, :, None, :]
    blocks = jnp.transpose(blocks, (0, 1, 3, 2, 4)).reshape(bsz, n_cmp, n_groups, CMP_BLOCK * dh)
    return jax.nn.gelu(blocks @ w1) @ w2


def nsa_attention(q, k_cmp, v_cmp, k_slc, v_slc, k_win, v_win, gates, table):
    bsz, seq, n_heads, dh = q.shape
    n_groups = k_slc.shape[2]
    rep = n_heads // n_groups
    n_cmp = k_cmp.shape[1]
    n_slc = seq // SLC_BLOCK
    topk = min(SLC_TOPK, n_slc)
    n_tok = topk * SLC_BLOCK
    scale = dh ** -0.5
    qg = q.reshape(bsz, seq, n_groups, rep, dh)
    table_gr = table.reshape(NUM_BUCKETS, n_groups, rep)
    table_g = jnp.transpose(table_gr, (1, 0, 2))
    cmp_end = jnp.arange(n_cmp) * CMP_STRIDE + CMP_BLOCK - 1
    overlap = jnp.asarray(selection_overlap(n_cmp, n_slc))
    k_slc_g = jnp.transpose(k_slc, (0, 2, 1, 3))
    v_slc_g = jnp.transpose(v_slc, (0, 2, 1, 3))
    span = NSA_WINDOW + Q_BLOCK
    kwp = jnp.pad(k_win, ((0, 0), (NSA_WINDOW, 0), (0, 0), (0, 0)))
    vwp = jnp.pad(v_win, ((0, 0), (NSA_WINDOW, 0), (0, 0), (0, 0)))
    qi = jnp.arange(Q_BLOCK)[:, None]
    ki = jnp.arange(span)[None, :]
    dist_w = qi + NSA_WINDOW - ki
    in_window = (dist_w >= 0) & (dist_w < NSA_WINDOW)
    bias_w = head_bias(table_gr, dist_w)
    bidx = jnp.arange(bsz)[:, None, None, None]
    gidx = jnp.arange(n_groups)[None, :, None, None]
    slc_ids = jnp.arange(n_slc)[None, :]

    def block(b):
        start = b * Q_BLOCK
        qpos = start + jnp.arange(Q_BLOCK)
        qb = lax.dynamic_slice_in_dim(qg, start, Q_BLOCK, axis=1)
        dist_c = qpos[:, None] - cmp_end[None, :]
        valid_c = dist_c >= 0
        lc = jnp.einsum('bqgrd,bcgd->bgrqc', qb, k_cmp, preferred_element_type=jnp.float32) * scale
        lc = jnp.where(valid_c, lc + head_bias(table_gr, dist_c), NEG)
        p_c = jax.nn.softmax(lc, axis=-1) * valid_c
        o_c = jnp.einsum('bgrqc,bcgd->bqgrd', p_c.astype(v_cmp.dtype), v_cmp)
        imp = jnp.einsum('bgrqc,cj->bgqj', p_c, overlap)
        qblk = (qpos // SLC_BLOCK)[:, None]
        forced = (slc_ids == 0) | (slc_ids == qblk) | (slc_ids == qblk - 1)
        score = jnp.where(forced, BIG, jnp.where(slc_ids > qblk, NEG, imp))
        _, sel = lax.top_k(score, topk)
        tok = (sel[..., None] * SLC_BLOCK + jnp.arange(SLC_BLOCK)).reshape(bsz, n_groups, Q_BLOCK, n_tok)
        ks = k_slc_g[bidx, gidx, tok]
        vs = v_slc_g[bidx, gidx, tok]
        dist_s = qpos[None, None, :, None] - tok
        bias_s = jnp.transpose(table_g[gidx, t5_bucket(dist_s)], (0, 1, 4, 2, 3))
        ls = jnp.einsum('bqgrd,bgqtd->bgrqt', qb, ks, preferred_element_type=jnp.float32) * scale + bias_s
        ls = jnp.where((dist_s >= 0)[:, :, None], ls, NEG)
        p_s = jax.nn.softmax(ls, axis=-1)
        o_s = jnp.einsum('bgrqt,bgqtd->bqgrd', p_s.astype(v_slc.dtype), vs)
        kb = lax.dynamic_slice_in_dim(kwp, start, span, axis=1)
        vb = lax.dynamic_slice_in_dim(vwp, start, span, axis=1)
        valid_w = in_window & (start - NSA_WINDOW + ki >= 0)
        lw = jnp.einsum('bqgrd,bkgd->bgrqk', qb, kb, preferred_element_type=jnp.float32) * scale + bias_w
        p_w = jax.nn.softmax(jnp.where(valid_w, lw, NEG), axis=-1)
        o_w = jnp.einsum('bgrqk,bkgd->bqgrd', p_w.astype(v_win.dtype), vb)
        gb = lax.dynamic_slice_in_dim(gates, start, Q_BLOCK, axis=1)
        out = gb[..., 0:1] * o_c + gb[..., 1:2] * o_s + gb[..., 2:3] * o_w
        return out.reshape(bsz, Q_BLOCK, n_heads * dh)

    out = lax.map(block, jnp.arange(seq // Q_BLOCK))
    return jnp.transpose(out, (1, 0, 2, 3)).reshape(bsz, seq, n_heads * dh)


def setup_inputs(seed: int = 0) -> dict:
    key = jax.random.key(seed)
    ks = jax.random.split(key, 20)
    f32 = jnp.float32

    def nrm(k, shape, scale):
        return jax.random.normal(k, shape, f32) * scale

    L, D = DEPTH, D_MODEL
    cmp_in = CMP_BLOCK * HEAD_DIM
    return {
        'x': nrm(ks[0], (BATCH, SEQ, D), 1.0),
        'norm_mix': 1.0 + nrm(ks[1], (L, D), 0.02),
        'w_in': nrm(ks[2], (L, D, IN_DIM), D ** -0.5),
        'attn_sinks': nrm(ks[3], (L, A_HEADS), 0.5),
        'cmp_pos_k': nrm(ks[4], (L, CMP_BLOCK, HEAD_DIM), 0.1),
        'cmp_w1_k': nrm(ks[5], (L, cmp_in, CMP_HIDDEN), cmp_in ** -0.5),
        'cmp_w2_k': nrm(ks[6], (L, CMP_HIDDEN, HEAD_DIM), CMP_HIDDEN ** -0.5),
        'cmp_pos_v': nrm(ks[7], (L, CMP_BLOCK, HEAD_DIM), 0.1),
        'cmp_w1_v': nrm(ks[8], (L, cmp_in, CMP_HIDDEN), cmp_in ** -0.5),
        'cmp_w2_v': nrm(ks[9], (L, CMP_HIDDEN, HEAD_DIM), CMP_HIDDEN ** -0.5),
        'w_up_a': nrm(ks[10], (L, A_Q_DIM, D), A_Q_DIM ** -0.5),
        'w_up_b': nrm(ks[11], (L, B_Q_DIM, D), B_Q_DIM ** -0.5),
        'w_out': nrm(ks[12], (L, D, D), D ** -0.5),
        'norm_ffn': 1.0 + nrm(ks[13], (L, D), 0.02),
        'w_ffn_in': nrm(ks[14], (L, D, 2 * D_FF), D ** -0.5),
        'conv_w': nrm(ks[15], (L, CONV_WIDTH, 2 * D_FF), CONV_WIDTH ** -0.5),
        'conv_b': nrm(ks[16], (L, 2 * D_FF), 0.01),
        'w_ffn_out': nrm(ks[17], (L, D_FF, D), D_FF ** -0.5),
        'rel_bias_table': nrm(ks[18], (NUM_BUCKETS, A_HEADS + B_HEADS), 0.5),
        'norm_final': 1.0 + nrm(ks[19], (D,), 0.02),
    }


def reference(x, norm_mix, w_in, attn_sinks, cmp_pos_k, cmp_w1_k, cmp_w2_k, cmp_pos_v, cmp_w1_v, cmp_w2_v,
              w_up_a, w_up_b, w_out, norm_ffn, w_ffn_in, conv_w, conv_b, w_ffn_out, rel_bias_table, norm_final):
    bsz, seq, _ = x.shape
    table_a = rel_bias_table[:, :A_HEADS]
    table_b = rel_bias_table[:, A_HEADS:]

    def heads(t, n):
        return t.reshape(bsz, seq, n, HEAD_DIM)

    for l in range(DEPTH):
        h = rmsnorm(x, norm_mix[l])
        proj = h @ w_in[l]
        (qa, ka, va, qb, kc, vc, ksl, vsl, kw, vw, g_nsa, g_a, g_b) = jnp.split(proj, IN_OFFSETS, axis=-1)
        y_a = swa_sink_attention(heads(qa, A_HEADS), heads(ka, A_KV_HEADS), heads(va, A_KV_HEADS),
                                 attn_sinks[l], table_a)
        k_cmp = compress(heads(kc, B_KV_GROUPS), cmp_pos_k[l], cmp_w1_k[l], cmp_w2_k[l])
        v_cmp = compress(heads(vc, B_KV_GROUPS), cmp_pos_v[l], cmp_w1_v[l], cmp_w2_v[l])
        nsa_gates = jax.nn.sigmoid(g_nsa).reshape(bsz, seq, B_KV_GROUPS, B_HEADS // B_KV_GROUPS, 3)
        y_b = nsa_attention(heads(qb, B_HEADS), k_cmp, v_cmp, heads(ksl, B_KV_GROUPS), heads(vsl, B_KV_GROUPS),
                            heads(kw, B_KV_GROUPS), heads(vw, B_KV_GROUPS), nsa_gates, table_b)
        merged = jax.nn.sigmoid(g_a) * (y_a @ w_up_a[l]) + jax.nn.sigmoid(g_b) * (y_b @ w_up_b[l])
        x = x + merged @ w_out[l]
        h = rmsnorm(x, norm_ffn[l])
        ug = causal_dwconv(h @ w_ffn_in[l], conv_w[l], conv_b[l])
        u, g = jnp.split(ug, 2, axis=-1)
        x = x + (jax.nn.silu(g) * u) @ w_ffn_out[l]
    return rmsnorm(x, norm_final)
```

```python
import functools
import math

import numpy as np
import jax
import jax.numpy as jnp
from jax import lax
from jax.experimental import pallas as pl
from jax.experimental.pallas import tpu as pltpu

F32 = jnp.float32
BF16 = jnp.bfloat16

HEAD_DIM = 64
A_HEADS = 8
A_KV_HEADS = 2
A_WINDOW = 128
B_HEADS = 8
B_KV_GROUPS = 2
REP = 4
CMP_BLOCK = 32
CMP_STRIDE = 16
SLC_BLOCK = 64
SLC_TOPK = 16
NSA_WINDOW = 512
NUM_BUCKETS = 32
MAX_DISTANCE = 128
CONV_WIDTH = 3
Q_BLOCK = 128
EPS = 1e-6
NEG = -1e30
BIG = 1e30
SCALE = HEAD_DIM ** -0.5

V7X_LANES = 128
V7X_SUBLANES = 8
V7X_VMEM_BYTES = 64 * 1024 * 1024
VMEM_LIMIT = 56 * 1024 * 1024

NT_DIMS = (((1,), (1,)), ((), ()))


def _cparams(semantics):
    return pltpu.CompilerParams(dimension_semantics=semantics, vmem_limit_bytes=VMEM_LIMIT)


def _bucket_np(dist):
    dist = np.maximum(dist, 0)
    max_exact = NUM_BUCKETS // 2
    d = np.maximum(dist, 1).astype(np.float64)
    large = max_exact + (np.log(d / max_exact) / math.log(MAX_DISTANCE / max_exact)
                         * (NUM_BUCKETS - max_exact)).astype(np.int32)
    large = np.minimum(large, NUM_BUCKETS - 1)
    return np.where(dist < max_exact, dist, large).astype(np.int32)


def _band_bias(table, window, shift_far):
    span = window + Q_BLOCK
    qi = np.arange(Q_BLOCK)[:, None]
    ki = np.arange(span)[None, :]
    dist = qi + window - ki
    ok = dist >= 0 if shift_far else (dist >= 0) & (dist < window)
    vals = table[_bucket_np(dist)]
    if shift_far:
        vals = vals - table[NUM_BUCKETS - 1][None, None, :]
    vals = jnp.where(ok[:, :, None], vals, NEG)
    n_heads = table.shape[1]
    vals = jnp.transpose(vals, (2, 0, 1))
    return vals.reshape(n_heads // REP, REP * Q_BLOCK, span)


def _cmp_bias_cols(table):
    q = np.arange(Q_BLOCK)
    rho = (q - (CMP_BLOCK - 1)) % CMP_STRIDE
    dist = rho[:, None] + CMP_STRIDE * np.arange(9)[None, :]
    vals = table[_bucket_np(dist)]
    vals = jnp.transpose(vals, (2, 0, 1))
    n_heads = table.shape[1]
    vals = vals.reshape(n_heads // REP, REP * Q_BLOCK, 9)
    return jnp.pad(vals, ((0, 0), (0, 0), (0, V7X_LANES - 9)))


def _overlap_t(n_cmp_pad, n_slc, n_cmp):
    r = SLC_BLOCK // CMP_STRIDE
    c = CMP_BLOCK // CMP_STRIDE
    j, m, n = np.meshgrid(np.arange(n_slc), np.arange(r), np.arange(c), indexing='ij')
    i = r * j + m - n
    ok = (i >= 0) & (i < n_cmp)
    mat = np.zeros((n_slc, n_cmp_pad), np.float32)
    np.add.at(mat, (j[ok], i[ok]), 1.0)
    return mat


def _rms(x, g):
    return x * lax.rsqrt(jnp.mean(x * x, axis=-1, keepdims=True) + EPS) * g


def _proj_kernel(x_ref, g_ref, w_ref, o16_ref, o32_ref, *, n16, nq, chunk):
    hb = _rms(x_ref[...], g_ref[...]).astype(BF16)
    for c0 in range(0, n16, chunk):
        r = jnp.dot(hb, w_ref[:, c0:c0 + chunk], preferred_element_type=F32)
        if c0 < nq:
            r = r * SCALE
        o16_ref[:, c0:c0 + chunk] = r.astype(BF16)
    n32 = o32_ref.shape[1]
    r = jnp.dot(hb, w_ref[:, n16:n16 + n32], preferred_element_type=F32)
    ncv = n32 - V7X_LANES
    o32_ref[:, :ncv] = r[:, :ncv]
    o32_ref[:, ncv:] = jax.nn.sigmoid(r[:, ncv:])


def _proj(x2d, g, w1, n16, n32, nq, tm):
    n, d = x2d.shape
    kern = functools.partial(_proj_kernel, n16=n16, nq=nq, chunk=256)
    return pl.pallas_call(
        kern,
        grid=(n // tm,),
        in_specs=[pl.BlockSpec((tm, d), lambda i: (i, 0)),
                  pl.BlockSpec((1, d), lambda i: (0, 0)),
                  pl.BlockSpec((d, n16 + n32), lambda i: (0, 0))],
        out_specs=[pl.BlockSpec((tm, n16), lambda i: (i, 0)),
                   pl.BlockSpec((tm, n32), lambda i: (i, 0))],
        out_shape=[jax.ShapeDtypeStruct((n, n16), BF16),
                   jax.ShapeDtypeStruct((n, n32), F32)],
        compiler_params=_cparams(("parallel",)),
        name="proj",
    )(x2d, g, w1)


def _compress_kernel(r_ref, pos_ref, w1_ref, w2_ref, o_ref, *, n_cmp):
    r = r_ref[...]
    a = jnp.dot((r + pos_ref[0:1, :]).astype(BF16), w1_ref[0], preferred_element_type=F32)
    b = jnp.dot((r + pos_ref[1:2, :]).astype(BF16), w1_ref[1], preferred_element_type=F32)
    h1 = a + jnp.concatenate([b[1:], b[:1]], axis=0)
    o = jnp.dot(jax.nn.gelu(h1).astype(BF16), w2_ref[...], preferred_element_type=F32)
    row = lax.broadcasted_iota(jnp.int32, o.shape, 0)
    o_ref[...] = jnp.where(row < n_cmp, o, 0.0).astype(o_ref.dtype)


def _compress(r, pos, w1, w2, n_cmp):
    bsz, ng, ncp, kdim = r.shape
    hid = w1.shape[-1]
    kern = functools.partial(_compress_kernel, n_cmp=n_cmp)
    return pl.pallas_call(
        kern,
        grid=(bsz, ng),
        in_specs=[pl.BlockSpec((None, None, ncp, kdim), lambda b, g: (b, g, 0, 0)),
                  pl.BlockSpec((2, kdim), lambda b, g: (0, 0)),
                  pl.BlockSpec((2, kdim, hid), lambda b, g: (0, 0, 0)),
                  pl.BlockSpec((hid, HEAD_DIM), lambda b, g: (0, 0))],
        out_specs=pl.BlockSpec((None, None, ncp, HEAD_DIM), lambda b, g: (b, g, 0, 0)),
        out_shape=jax.ShapeDtypeStruct((bsz, ng, ncp, HEAD_DIM), BF16),
        compiler_params=_cparams(("parallel", "parallel")),
        name="compress",
    )(r, pos, w1, w2)


def _stack_heads(q, g):
    return jnp.concatenate(
        [q[:, (REP * g + r) * HEAD_DIM:(REP * g + r + 1) * HEAD_DIM] for r in range(REP)], axis=0)


def _unstack_heads(per_group):
    cols = []
    for o in per_group:
        cols += [o[r * Q_BLOCK:(r + 1) * Q_BLOCK] for r in range(REP)]
    return jnp.concatenate(cols, axis=1)


def _swa_kernel(sink_ref, q_ref, kp_ref, kc_ref, vp_ref, vc_ref, bias_ref, o_ref):
    i = pl.program_id(1)
    q = q_ref[...]
    k = jnp.concatenate([kp_ref[...], kc_ref[...]], axis=0)
    v = jnp.concatenate([vp_ref[...], vc_ref[...]], axis=0)
    col = lax.broadcasted_iota(jnp.int32, (1, 2 * Q_BLOCK), 1)
    pad_keys = (col < A_WINDOW) & (i == 0)
    outs = []
    for g in range(A_KV_HEADS):
        kg = k[:, g * HEAD_DIM:(g + 1) * HEAD_DIM]
        vg = v[:, g * HEAD_DIM:(g + 1) * HEAD_DIM]
        s = lax.dot_general(_stack_heads(q, g), kg, NT_DIMS, preferred_element_type=F32)
        s = jnp.where(pad_keys, NEG, s + bias_ref[g])
        sink = jnp.concatenate(
            [jnp.full((Q_BLOCK, 1), sink_ref[REP * g + r], F32) for r in range(REP)], axis=0)
        m = jnp.maximum(jnp.max(s, axis=-1, keepdims=True), sink)
        p = jnp.exp(s - m)
        l = jnp.sum(p, axis=-1, keepdims=True) + jnp.exp(sink - m)
        o = jnp.dot(p.astype(BF16), vg, preferred_element_type=F32)
        outs.append(o / l)
    o_ref[...] = _unstack_heads(outs).astype(o_ref.dtype)


def _swa(p16, sinks, bias, bsz, seq, col_q, col_k, col_v):
    nb = seq // Q_BLOCK
    qw = A_HEADS * HEAD_DIM
    kw = A_KV_HEADS * HEAD_DIM
    prev = lambda b, i: jnp.maximum(i - 1, 0)
    return pl.pallas_call(
        _swa_kernel,
        grid=(bsz, nb),
        in_specs=[pl.BlockSpec(memory_space=pltpu.SMEM),
                  pl.BlockSpec((None, Q_BLOCK, qw), lambda b, i: (b, i, col_q // qw)),
                  pl.BlockSpec((None, Q_BLOCK, kw), lambda b, i: (b, prev(b, i), col_k // kw)),
                  pl.BlockSpec((None, Q_BLOCK, kw), lambda b, i: (b, i, col_k // kw)),
                  pl.BlockSpec((None, Q_BLOCK, kw), lambda b, i: (b, prev(b, i), col_v // kw)),
                  pl.BlockSpec((None, Q_BLOCK, kw), lambda b, i: (b, i, col_v // kw)),
                  pl.BlockSpec(bias.shape, lambda b, i: (0, 0, 0))],
        out_specs=pl.BlockSpec((None, Q_BLOCK, qw), lambda b, i: (b, i, 0)),
        out_shape=jax.ShapeDtypeStruct((bsz, seq, qw), BF16),
        compiler_params=_cparams(("parallel", "arbitrary")),
        name="swa",
    )(sinks, p16, p16, p16, p16, p16, bias)


def _nsa_kernel(q_ref, kcmp_ref, vcmp_ref, ksl_ref, vsl_ref, kw_ref, vw_ref, gate_ref,
                fc_ref, bw_ref, bs_ref, ovt_ref, oh_ref, o_ref, *, topk):
    i = pl.program_id(1)
    q = q_ref[...]
    gate = gate_ref[...]
    n_slc, ncp = ovt_ref.shape
    rows = REP * Q_BLOCK
    n_win = NSA_WINDOW // Q_BLOCK + 1

    qrow = lax.broadcasted_iota(jnp.int32, (rows, 1), 0) & (Q_BLOCK - 1)
    cmax = jnp.right_shift(qrow + i * Q_BLOCK - (CMP_BLOCK - 1), 4)
    kk = cmax - lax.broadcasted_iota(jnp.int32, (rows, ncp), 1)
    valid_c = kk >= 0

    jt = lax.broadcasted_iota(jnp.int32, (n_slc, Q_BLOCK), 0)
    qblk = 2 * i + (lax.broadcasted_iota(jnp.int32, (1, Q_BLOCK), 1) >= SLC_BLOCK).astype(jnp.int32)
    forced = (jt == 0) | (jt == qblk) | (jt == qblk - 1)

    outs = []
    for g in range(B_KV_GROUPS):
        lanes = slice(g * HEAD_DIM, (g + 1) * HEAD_DIM)
        qg = _stack_heads(q, g)

        fc = fc_ref[g]
        bias = jnp.broadcast_to(fc[:, 8:9], (rows, ncp))
        for k in range(8):
            bias = jnp.where(kk == k, fc[:, k:k + 1], bias)
        s = lax.dot_general(qg, kcmp_ref[g], NT_DIMS, preferred_element_type=F32)
        s = jnp.where(valid_c, s + bias, NEG)
        m = jnp.max(s, axis=-1, keepdims=True)
        p = jnp.where(valid_c, jnp.exp(s - m), 0.0)
        l = jnp.sum(p, axis=-1, keepdims=True)
        p = p / jnp.where(l > 0.0, l, 1.0)
        o_c = jnp.dot(p.astype(BF16), vcmp_ref[g], preferred_element_type=F32)

        psum = p[0:Q_BLOCK]
        for r in range(1, REP):
            psum = psum + p[r * Q_BLOCK:(r + 1) * Q_BLOCK]
        imp_t = lax.dot_general(ovt_ref[...], psum, NT_DIMS, preferred_element_type=F32,
                                precision=lax.Precision.HIGHEST)
        score = jnp.where(forced, BIG, jnp.where(jt > qblk, NEG, imp_t))
        rank = jnp.zeros((n_slc, Q_BLOCK), jnp.int32)
        for c in range(n_slc):
            row = score[c:c + 1, :]
            ahead = (row > score) | ((row == score) & (jt > c))
            rank = rank + ahead.astype(jnp.int32)
        sel = jnp.transpose((rank < topk).astype(F32))
        selneg = jnp.where(sel > 0.5, 0.0, NEG).astype(BF16)
        qaug = jnp.concatenate(
            [qg, jnp.concatenate([selneg] * REP, axis=0)], axis=1)

        def sel_tile(t, carry, extra):
            m_i, l_i, acc = carry
            r0 = pl.multiple_of(t * Q_BLOCK, Q_BLOCK)
            kaug = jnp.concatenate([ksl_ref[pl.ds(r0, Q_BLOCK), lanes],
                                    oh_ref[pl.ds(r0, Q_BLOCK), :]], axis=1)
            st = lax.dot_general(qaug, kaug, NT_DIMS, preferred_element_type=F32)
            if extra is not None:
                st = st + extra
            m_n = jnp.maximum(m_i, jnp.max(st, axis=-1, keepdims=True))
            alpha = jnp.exp(m_i - m_n)
            pt = jnp.exp(st - m_n)
            l_n = alpha * l_i + jnp.sum(pt, axis=-1, keepdims=True)
            acc_n = alpha * acc + jnp.dot(pt.astype(BF16), vsl_ref[pl.ds(r0, Q_BLOCK), lanes],
                                          preferred_element_type=F32)
            return m_n, l_n, acc_n

        carry = (jnp.full((rows, 1), NEG, F32), jnp.zeros((rows, 1), F32),
                 jnp.zeros((rows, HEAD_DIM), F32))
        carry = lax.fori_loop(0, jnp.maximum(i - 1, 0),
                              lambda t, c: sel_tile(t, c, None), carry)
        bs = bs_ref[g]
        prev_bias = bs[:, :Q_BLOCK] + jnp.where(i > 0, 0.0, NEG)
        carry = sel_tile(jnp.maximum(i - 1, 0), carry, prev_bias)
        m_s, l_s, acc_s = sel_tile(i, carry, bs[:, Q_BLOCK:])
        o_s = acc_s / l_s

        bw = bw_ref[g]
        parts, vparts = [], []
        for j in range(n_win):
            kb = i - (n_win - 1) + j
            r0 = pl.multiple_of(jnp.maximum(kb, 0) * Q_BLOCK, Q_BLOCK)
            sj = lax.dot_general(qg, kw_ref[pl.ds(r0, Q_BLOCK), lanes], NT_DIMS,
                                 preferred_element_type=F32)
            parts.append(sj + bw[:, j * Q_BLOCK:(j + 1) * Q_BLOCK] + jnp.where(kb >= 0, 0.0, NEG))
            vparts.append(vw_ref[pl.ds(r0, Q_BLOCK), lanes])
        sw = jnp.concatenate(parts, axis=1)
        mw = jnp.max(sw, axis=-1, keepdims=True)
        pw = jnp.exp(sw - mw)
        lw = jnp.sum(pw, axis=-1, keepdims=True)
        o_w = jnp.dot(pw.astype(BF16), jnp.concatenate(vparts, axis=0),
                      preferred_element_type=F32) / lw

        comb = []
        for r in range(REP):
            h = REP * g + r
            rs = slice(r * Q_BLOCK, (r + 1) * Q_BLOCK)
            comb.append(gate[:, 3 * h:3 * h + 1] * o_c[rs]
                        + gate[:, 3 * h + 1:3 * h + 2] * o_s[rs]
                        + gate[:, 3 * h + 2:3 * h + 3] * o_w[rs])
        outs.append(jnp.concatenate(comb, axis=0))
    o_ref[...] = _unstack_heads(outs).astype(o_ref.dtype)


def _nsa(p16, p32, kcmp, vcmp, fc, bw, bs, ovt, oh, bsz, seq, cols16, col_gate, topk):
    nb = seq // Q_BLOCK
    qw = B_HEADS * HEAD_DIM
    kw = B_KV_GROUPS * HEAD_DIM
    ncp = kcmp.shape[2]
    full = lambda c: pl.BlockSpec((None, seq, kw), lambda b, i: (b, 0, c // kw))
    const3 = lambda a: pl.BlockSpec(a.shape, lambda b, i: (0, 0, 0))
    const2 = lambda a: pl.BlockSpec(a.shape, lambda b, i: (0, 0))
    cmp_spec = pl.BlockSpec((None, B_KV_GROUPS, ncp, HEAD_DIM), lambda b, i: (b, 0, 0, 0))
    kern = functools.partial(_nsa_kernel, topk=topk)
    return pl.pallas_call(
        kern,
        grid=(bsz, nb),
        in_specs=[pl.BlockSpec((None, Q_BLOCK, qw), lambda b, i: (b, i, cols16["q"] // qw)),
                  cmp_spec, cmp_spec,
                  full(cols16["ksl"]), full(cols16["vsl"]), full(cols16["kw"]), full(cols16["vw"]),
                  pl.BlockSpec((None, Q_BLOCK, V7X_LANES), lambda b, i: (b, i, col_gate // V7X_LANES)),
                  const3(fc), const3(bw), const3(bs), const2(ovt), const2(oh)],
        out_specs=pl.BlockSpec((None, Q_BLOCK, qw), lambda b, i: (b, i, 0)),
        out_shape=jax.ShapeDtypeStruct((bsz, seq, qw), BF16),
        compiler_params=_cparams(("parallel", "arbitrary")),
        name="nsa",
    )(p16, kcmp, vcmp, p16, p16, p16, p16, p32, fc, bw, bs, ovt, oh)


def _merge_kernel(x_ref, g_ref, ya_ref, yb_ref, wg_ref, wua_ref, wub_ref, wo_ref, o_ref):
    x = x_ref[...]
    d = x.shape[1]
    hb = _rms(x, g_ref[...]).astype(BF16)
    ga = jax.nn.sigmoid(jnp.dot(hb, wg_ref[:, :d], preferred_element_type=F32))
    gb = jax.nn.sigmoid(jnp.dot(hb, wg_ref[:, d:], preferred_element_type=F32))
    ua = jnp.dot(ya_ref[...], wua_ref[...], preferred_element_type=F32)
    ub = jnp.dot(yb_ref[...], wub_ref[...], preferred_element_type=F32)
    merged = ga * ua + gb * ub
    o_ref[...] = x + jnp.dot(merged.astype(BF16), wo_ref[...], preferred_element_type=F32)


def _merge(x2d, g, ya, yb, wg, wua, wub, wo, tm):
    n, d = x2d.shape
    const = lambda a: pl.BlockSpec(a.shape, lambda i: (0, 0))
    row = lambda a: pl.BlockSpec((tm, a.shape[1]), lambda i: (i, 0))
    return pl.pallas_call(
        _merge_kernel,
        grid=(n // tm,),
        in_specs=[row(x2d), const(g), row(ya), row(yb), const(wg), const(wua), const(wub), const(wo)],
        out_specs=row(x2d),
        out_shape=jax.ShapeDtypeStruct((n, d), F32),
        compiler_params=_cparams(("parallel",)),
        name="merge",
    )(x2d, g, ya, yb, wg, wua, wub, wo)


def _ffn_kernel(xc_ref, xp_ref, gn_ref, wu_ref, wg_ref, cwu_ref, cwg_ref, cbu_ref, cbg_ref,
                wo_ref, gf_ref, o_ref, *, chunk):
    i = pl.program_id(1)
    xc = xc_ref[...]
    tm = xc.shape[0]
    halo = xp_ref.shape[0]
    gn = gn_ref[...]
    hc = _rms(xc, gn).astype(BF16)
    hp = _rms(xp_ref[...], gn).astype(BF16)
    live = jnp.where(i > 0, 1.0, 0.0)
    d_ff = wu_ref.shape[1]

    def conv(w_ref, cw_ref, cb_ref, c0):
        cur = jnp.dot(hc, w_ref[:, c0:c0 + chunk], preferred_element_type=F32)
        prv = jnp.dot(hp, w_ref[:, c0:c0 + chunk], preferred_element_type=F32) * live
        ext = jnp.concatenate([prv, cur], axis=0)
        cw = cw_ref[:, c0:c0 + chunk]
        out = cb_ref[:, c0:c0 + chunk]
        for k in range(CONV_WIDTH):
            off = halo - (CONV_WIDTH - 1) + k
            out = out + cw[k:k + 1, :] * ext[off:off + tm]
        return out

    acc = jnp.zeros(xc.shape, F32)
    for c0 in range(0, d_ff, chunk):
        u = conv(wu_ref, cwu_ref, cbu_ref, c0)
        gt = conv(wg_ref, cwg_ref, cbg_ref, c0)
        act = (jax.nn.silu(gt) * u).astype(BF16)
        acc = acc + jnp.dot(act, wo_ref[c0:c0 + chunk, :], preferred_element_type=F32)
    o_ref[...] = _rms(xc + acc, gf_ref[...])


def _ffn(x1, gn, wu, wg, cwu, cwg, cbu, cbg, wo, gf, tm, chunk):
    bsz, seq, d = x1.shape
    halo = V7X_SUBLANES
    const = lambda a: pl.BlockSpec(a.shape, lambda b, i: (0, 0))
    kern = functools.partial(_ffn_kernel, chunk=chunk)
    return pl.pallas_call(
        kern,
        grid=(bsz, seq // tm),
        in_specs=[pl.BlockSpec((None, tm, d), lambda b, i: (b, i, 0)),
                  pl.BlockSpec((None, halo, d),
                               lambda b, i: (b, jnp.maximum(i * (tm // halo) - 1, 0), 0)),
                  const(gn), const(wu), const(wg), const(cwu), const(cwg), const(cbu), const(cbg),
                  const(wo), const(gf)],
        out_specs=pl.BlockSpec((None, tm, d), lambda b, i: (b, i, 0)),
        out_shape=jax.ShapeDtypeStruct((bsz, seq, d), F32),
        compiler_params=_cparams(("parallel", "arbitrary")),
        name="ffn",
    )(x1, x1, gn, wu, wg, cwu, cwg, cbu, cbg, wo, gf)


def _mixers(x, norm_mix, w_in, attn_sinks, cmp_pos_k, cmp_w1_k, cmp_w2_k, cmp_pos_v, cmp_w1_v,
            cmp_w2_v, table):
    bsz, seq, d = x.shape
    n = bsz * seq
    aq, akv = A_HEADS * HEAD_DIM, A_KV_HEADS * HEAD_DIM
    bq, bkv = B_HEADS * HEAD_DIM, B_KV_GROUPS * HEAD_DIM
    n_gate = 3 * B_HEADS
    splits = (aq, akv, akv, bq, bkv, bkv, bkv, bkv, bkv, bkv, n_gate, d, d)
    off = np.concatenate([[0], np.cumsum(splits)]).astype(int)
    seg = lambda k: w_in[:, off[k]:off[k + 1]]
    w_gate_nsa = jnp.pad(seg(10), ((0, 0), (0, V7X_LANES - n_gate)))
    w1 = jnp.concatenate([seg(0), seg(3), seg(1), seg(2), seg(6), seg(7), seg(8), seg(9),
                          seg(4), seg(5), w_gate_nsa], axis=1).astype(BF16)
    n16 = aq + bq + 6 * bkv
    n32 = 2 * bkv + V7X_LANES
    cols16 = {"q": aq, "ksl": aq + bq + 2 * akv, "vsl": aq + bq + 2 * akv + bkv,
              "kw": aq + bq + 2 * akv + 2 * bkv, "vw": aq + bq + 2 * akv + 3 * bkv}
    tm = min(512, n)
    x2d = x.reshape(n, d)
    g_mix = norm_mix.reshape(1, d)

    p16, p32 = _proj(x2d, g_mix, w1, n16, n32, aq + bq, tm)
    p16 = p16.reshape(bsz, seq, n16)
    p32 = p32.reshape(bsz, seq, n32)

    table_a, table_b = table[:, :A_HEADS], table[:, A_HEADS:]
    bias_a = _band_bias(table_a, A_WINDOW, shift_far=False)
    y_a = _swa(p16, attn_sinks, bias_a, bsz, seq, 0, aq + bq, aq + bq + akv)

    ncp = seq // CMP_STRIDE
    n_cmp = (seq - CMP_BLOCK) // CMP_STRIDE + 1
    n_slc = seq // SLC_BLOCK
    topk = min(SLC_TOPK, n_slc)

    def rows16(t):
        t = t.reshape(bsz, ncp, CMP_STRIDE, B_KV_GROUPS, HEAD_DIM)
        return jnp.transpose(t, (0, 3, 1, 2, 4)).reshape(bsz, B_KV_GROUPS, ncp, CMP_STRIDE * HEAD_DIM)

    half = CMP_STRIDE * HEAD_DIM
    k_cmp = _compress(rows16(p32[:, :, :bkv]), cmp_pos_k.reshape(2, half),
                      cmp_w1_k.reshape(2, half, -1).astype(BF16), cmp_w2_k.astype(BF16), n_cmp)
    v_cmp = _compress(rows16(p32[:, :, bkv:2 * bkv]), cmp_pos_v.reshape(2, half),
                      cmp_w1_v.reshape(2, half, -1).astype(BF16), cmp_w2_v.astype(BF16), n_cmp)

    fc = _cmp_bias_cols(table_b)
    bw = _band_bias(table_b, NSA_WINDOW, shift_far=False)
    bs = _band_bias(table_b, Q_BLOCK, shift_far=True)
    ovt = jnp.asarray(_overlap_t(ncp, n_slc, n_cmp))
    oh = jnp.asarray(np.arange(seq)[:, None] // SLC_BLOCK == np.arange(n_slc)[None, :], BF16)
    y_b = _nsa(p16, p32, k_cmp, v_cmp, fc, bw, bs, ovt, oh, bsz, seq, cols16, 2 * bkv, topk)
    return y_a, y_b, w_in[:, off[11]:off[13]]


def _layer(x, norm_mix, w_in, attn_sinks, cmp_pos_k, cmp_w1_k, cmp_w2_k, cmp_pos_v, cmp_w1_v,
           cmp_w2_v, w_up_a, w_up_b, w_out, norm_ffn, w_ffn_in, conv_w, conv_b, w_ffn_out,
           table, norm_final):
    bsz, seq, d = x.shape
    n = bsz * seq
    y_a, y_b, w_merge_gates = _mixers(x, norm_mix, w_in, attn_sinks, cmp_pos_k, cmp_w1_k, cmp_w2_k,
                                      cmp_pos_v, cmp_w1_v, cmp_w2_v, table)

    x1 = _merge(x.reshape(n, d), norm_mix.reshape(1, d), y_a.reshape(n, -1), y_b.reshape(n, -1),
                w_merge_gates.astype(BF16), w_up_a.astype(BF16), w_up_b.astype(BF16),
                w_out.astype(BF16), min(512, n))

    d_ff = w_ffn_out.shape[0]
    out = _ffn(x1.reshape(bsz, seq, d), norm_ffn.reshape(1, d),
               w_ffn_in[:, :d_ff].astype(BF16), w_ffn_in[:, d_ff:].astype(BF16),
               conv_w[:, :d_ff], conv_w[:, d_ff:], conv_b[:d_ff].reshape(1, d_ff),
               conv_b[d_ff:].reshape(1, d_ff), w_ffn_out.astype(BF16), norm_final.reshape(1, d),
               min(512, seq), 256)
    return out


def kernel(x, norm_mix, w_in, attn_sinks, cmp_pos_k, cmp_w1_k, cmp_w2_k, cmp_pos_v, cmp_w1_v, cmp_w2_v, w_up_a, w_up_b, w_out, norm_ffn, w_ffn_in, conv_w, conv_b, w_ffn_out, rel_bias_table, norm_final):
    assert norm_mix.shape[0] == 1, "single-layer block"
    return _layer(x, norm_mix[0], w_in[0], attn_sinks[0], cmp_pos_k[0], cmp_w1_k[0], cmp_w2_k[0],
                  cmp_pos_v[0], cmp_w1_v[0], cmp_w2_v[0], w_up_a[0], w_up_b[0], w_out[0],
                  norm_ffn[0], w_ffn_in[0], conv_w[0], conv_b[0], w_ffn_out[0], rel_bias_table,
                  norm_final)
```

```python
import functools
import math

import numpy as np
import jax
import jax.numpy as jnp
from jax import lax
from jax.experimental import pallas as pl
from jax.experimental.pallas import tpu as pltpu

F32 = jnp.float32
BF16 = jnp.bfloat16

HEAD_DIM = 64
A_HEADS = 8
A_KV_HEADS = 2
A_WINDOW = 128
B_HEADS = 8
B_KV_GROUPS = 2
REP = 4
CMP_BLOCK = 32
CMP_STRIDE = 16
SLC_BLOCK = 64
SLC_TOPK = 16
NSA_WINDOW = 512
NUM_BUCKETS = 32
MAX_DISTANCE = 128
CONV_WIDTH = 3
Q_BLOCK = 128
EPS = 1e-6
NEG = -1e30
BIG = 1e30
SCALE = HEAD_DIM ** -0.5

V7X_LANES = 128
V7X_SUBLANES = 8
V7X_VMEM_BYTES = 64 * 1024 * 1024
VMEM_LIMIT = 56 * 1024 * 1024

NT_DIMS = (((1,), (1,)), ((), ()))


def _cparams(semantics):
    return pltpu.CompilerParams(dimension_semantics=semantics, vmem_limit_bytes=VMEM_LIMIT)


def _bucket_np(dist):
    dist = np.maximum(dist, 0)
    max_exact = NUM_BUCKETS // 2
    d = np.maximum(dist, 1).astype(np.float64)
    large = max_exact + (np.log(d / max_exact) / math.log(MAX_DISTANCE / max_exact)
                         * (NUM_BUCKETS - max_exact)).astype(np.int32)
    large = np.minimum(large, NUM_BUCKETS - 1)
    return np.where(dist < max_exact, dist, large).astype(np.int32)


def _table_lookup(table, dist):
    idx = _bucket_np(dist).reshape(-1)
    onehot = np.zeros((NUM_BUCKETS, idx.size), np.float32)
    onehot[idx, np.arange(idx.size)] = 1.0
    vals = jnp.dot(table.T, jnp.asarray(onehot), precision=lax.Precision.HIGHEST)
    return vals.reshape((table.shape[1],) + dist.shape)


def _band_bias_t(table, n_keys, offset, window, shift_far):
    length = n_keys + Q_BLOCK
    m = np.arange(length)
    m = np.where(m < Q_BLOCK, m, m - length)
    dist = m + offset
    ok = dist >= 0 if window is None else (dist >= 0) & (dist < window)
    u = _table_lookup(table, dist)
    if shift_far:
        u = u - table[NUM_BUCKETS - 1][:, None]
    u = jnp.where(ok[None, :], u, NEG)
    n_heads = table.shape[1]
    t = jnp.tile(u, (1, n_keys))[:, :n_keys * (length - 1)]
    return t.reshape(n_heads, n_keys, length - 1)[:, :, :Q_BLOCK]


def _keys_by_lanes(t):
    n_heads, n_keys, _ = t.shape
    return jnp.transpose(t, (1, 0, 2)).reshape(n_keys, n_heads * Q_BLOCK)


def _cmp_bias_rows(table):
    q = np.arange(Q_BLOCK)
    rho = (q - (CMP_BLOCK - 1)) % CMP_STRIDE
    dist = CMP_STRIDE * np.arange(9)[:, None] + rho[None, :]
    vals = _keys_by_lanes(_table_lookup(table, dist))
    return jnp.pad(vals, ((0, 16 - 9), (0, 0)))


def _overlap_t(n_cmp_pad, n_slc, n_cmp):
    r = SLC_BLOCK // CMP_STRIDE
    c = CMP_BLOCK // CMP_STRIDE
    j, m, n = np.meshgrid(np.arange(n_slc), np.arange(r), np.arange(c), indexing='ij')
    i = r * j + m - n
    ok = (i >= 0) & (i < n_cmp)
    mat = np.zeros((n_slc, n_cmp_pad), np.float32)
    np.add.at(mat, (j[ok], i[ok]), 1.0)
    return mat


def _rms(x, g):
    return x * lax.rsqrt(jnp.mean(x * x, axis=-1, keepdims=True) + EPS) * g


def _proj_kernel(x_ref, g_ref, w_ref, o16_ref, o32_ref, *, n16, nq, chunk):
    hb = _rms(x_ref[...], g_ref[...]).astype(BF16)
    for c0 in range(0, n16, chunk):
        r = jnp.dot(hb, w_ref[:, c0:c0 + chunk], preferred_element_type=F32)
        if c0 < nq:
            r = r * SCALE
        o16_ref[:, c0:c0 + chunk] = r.astype(BF16)
    n32 = o32_ref.shape[1]
    r = jnp.dot(hb, w_ref[:, n16:n16 + n32], preferred_element_type=F32)
    ncv = n32 - V7X_LANES
    o32_ref[:, :ncv] = r[:, :ncv]
    o32_ref[:, ncv:] = jax.nn.sigmoid(r[:, ncv:])


def _proj(x2d, g, w1, n16, n32, nq, tm):
    n, d = x2d.shape
    kern = functools.partial(_proj_kernel, n16=n16, nq=nq, chunk=256)
    return pl.pallas_call(
        kern,
        grid=(n // tm,),
        in_specs=[pl.BlockSpec((tm, d), lambda i: (i, 0)),
                  pl.BlockSpec((1, d), lambda i: (0, 0)),
                  pl.BlockSpec((d, n16 + n32), lambda i: (0, 0))],
        out_specs=[pl.BlockSpec((tm, n16), lambda i: (i, 0)),
                   pl.BlockSpec((tm, n32), lambda i: (i, 0))],
        out_shape=[jax.ShapeDtypeStruct((n, n16), BF16),
                   jax.ShapeDtypeStruct((n, n32), F32)],
        compiler_params=_cparams(("parallel",)),
        name="proj",
    )(x2d, g, w1)


def _compress_kernel(r_ref, pos_ref, w1_ref, w2_ref, o_ref, *, n_cmp, transpose_out):
    outs = []
    for g in range(r_ref.shape[0]):
        r = r_ref[g]
        a = jnp.dot((r + pos_ref[0:1, :]).astype(BF16), w1_ref[0], preferred_element_type=F32)
        b = jnp.dot((r + pos_ref[1:2, :]).astype(BF16), w1_ref[1], preferred_element_type=F32)
        h1 = a + jnp.concatenate([b[1:], b[:1]], axis=0)
        o = jnp.dot(jax.nn.gelu(h1).astype(BF16), w2_ref[...], preferred_element_type=F32)
        row = lax.broadcasted_iota(jnp.int32, o.shape, 0)
        outs.append(jnp.where(row < n_cmp, o, 0.0))
    o = jnp.concatenate(outs, axis=1)
    if transpose_out:
        o = jnp.transpose(o)
    o_ref[...] = o.astype(o_ref.dtype)


def _compress(r, pos, w1, w2, n_cmp, transpose_out):
    bsz, ng, ncp, kdim = r.shape
    hid = w1.shape[-1]
    oshape = (ng * HEAD_DIM, ncp) if transpose_out else (ncp, ng * HEAD_DIM)
    kern = functools.partial(_compress_kernel, n_cmp=n_cmp, transpose_out=transpose_out)
    return pl.pallas_call(
        kern,
        grid=(bsz,),
        in_specs=[pl.BlockSpec((None, ng, ncp, kdim), lambda b: (b, 0, 0, 0)),
                  pl.BlockSpec((2, kdim), lambda b: (0, 0)),
                  pl.BlockSpec((2, kdim, hid), lambda b: (0, 0, 0)),
                  pl.BlockSpec((hid, HEAD_DIM), lambda b: (0, 0))],
        out_specs=pl.BlockSpec((None,) + oshape, lambda b: (b, 0, 0)),
        out_shape=jax.ShapeDtypeStruct((bsz,) + oshape, BF16),
        compiler_params=_cparams(("parallel",)),
        name="compress",
    )(r, pos, w1, w2)


def _stack_heads(q, g):
    return jnp.concatenate(
        [q[:, (REP * g + r) * HEAD_DIM:(REP * g + r + 1) * HEAD_DIM] for r in range(REP)], axis=0)


def _unstack_heads(per_group):
    cols = []
    for o in per_group:
        cols += [o[r * Q_BLOCK:(r + 1) * Q_BLOCK] for r in range(REP)]
    return jnp.concatenate(cols, axis=1)


def _swa_kernel(sink_ref, q_ref, kp_ref, kc_ref, vp_ref, vc_ref, bias_ref, o_ref):
    i = pl.program_id(1)
    q = q_ref[...]
    k = jnp.concatenate([kp_ref[...], kc_ref[...]], axis=0)
    v = jnp.concatenate([vp_ref[...], vc_ref[...]], axis=0)
    col = lax.broadcasted_iota(jnp.int32, (1, 2 * Q_BLOCK), 1)
    pad_keys = (col < A_WINDOW) & (i == 0)
    outs = []
    for g in range(A_KV_HEADS):
        kg = k[:, g * HEAD_DIM:(g + 1) * HEAD_DIM]
        vg = v[:, g * HEAD_DIM:(g + 1) * HEAD_DIM]
        s = lax.dot_general(_stack_heads(q, g), kg, NT_DIMS, preferred_element_type=F32)
        s = jnp.where(pad_keys, NEG, s + bias_ref[g])
        sink = jnp.concatenate(
            [jnp.full((Q_BLOCK, 1), sink_ref[REP * g + r], F32) for r in range(REP)], axis=0)
        m = jnp.maximum(jnp.max(s, axis=-1, keepdims=True), sink)
        p = jnp.exp(s - m)
        l = jnp.sum(p, axis=-1, keepdims=True) + jnp.exp(sink - m)
        o = jnp.dot(p.astype(BF16), vg, preferred_element_type=F32)
        outs.append(o / l)
    o_ref[...] = _unstack_heads(outs).astype(o_ref.dtype)


def _swa(p16, sinks, bias, bsz, seq, col_q, col_k, col_v):
    nb = seq // Q_BLOCK
    qw = A_HEADS * HEAD_DIM
    kw = A_KV_HEADS * HEAD_DIM
    prev = lambda b, i: jnp.maximum(i - 1, 0)
    return pl.pallas_call(
        _swa_kernel,
        grid=(bsz, nb),
        in_specs=[pl.BlockSpec(memory_space=pltpu.SMEM),
                  pl.BlockSpec((None, Q_BLOCK, qw), lambda b, i: (b, i, col_q // qw)),
                  pl.BlockSpec((None, Q_BLOCK, kw), lambda b, i: (b, prev(b, i), col_k // kw)),
                  pl.BlockSpec((None, Q_BLOCK, kw), lambda b, i: (b, i, col_k // kw)),
                  pl.BlockSpec((None, Q_BLOCK, kw), lambda b, i: (b, prev(b, i), col_v // kw)),
                  pl.BlockSpec((None, Q_BLOCK, kw), lambda b, i: (b, i, col_v // kw)),
                  pl.BlockSpec(bias.shape, lambda b, i: (0, 0, 0))],
        out_specs=pl.BlockSpec((None, Q_BLOCK, qw), lambda b, i: (b, i, 0)),
        out_shape=jax.ShapeDtypeStruct((bsz, seq, qw), BF16),
        compiler_params=_cparams(("parallel", "arbitrary")),
        name="swa",
    )(sinks, p16, p16, p16, p16, p16, bias)


SEL_TILE = 2 * Q_BLOCK
AUG_LANES = 2 * V7X_LANES


def _nsa_kernel(q_ref, kcmp_ref, vcmpt_ref, kaug_ref, vslt_ref, kw_ref, vwt_ref, gate_ref,
                fc_ref, bw_ref, bn_ref, ovt_ref, o_ref, *, topk):
    i = pl.program_id(1)
    n_slc, ncp = ovt_ref.shape
    gl = REP * Q_BLOCK
    nl = B_KV_GROUPS * gl
    gd = B_KV_GROUPS * HEAD_DIM
    n_win = NSA_WINDOW // Q_BLOCK + 1

    qt = jnp.transpose(q_ref[...].astype(F32)).astype(BF16)
    zero = jnp.zeros((HEAD_DIM, Q_BLOCK), BF16)
    wq = jnp.concatenate([
        jnp.concatenate([qt[(REP * g + r) * HEAD_DIM:(REP * g + r + 1) * HEAD_DIM] if gg == g else zero
                         for gg in range(B_KV_GROUPS) for r in range(REP)], axis=1)
        for g in range(B_KV_GROUPS)], axis=0)

    qlane = lax.broadcasted_iota(jnp.int32, (1, nl), 1) & (Q_BLOCK - 1)
    cmax = jnp.right_shift(qlane + i * Q_BLOCK - (CMP_BLOCK - 1), 4)
    kk = cmax - lax.broadcasted_iota(jnp.int32, (ncp, nl), 0)
    valid_c = kk >= 0
    bias = jnp.broadcast_to(fc_ref[8:9, :], (ncp, nl))
    for k in range(8):
        bias = jnp.where(kk == k, fc_ref[k:k + 1, :], bias)
    s = jnp.dot(kcmp_ref[...], wq, preferred_element_type=F32)
    s = jnp.where(valid_c, s + bias, NEG)
    m = jnp.max(s, axis=0, keepdims=True)
    p = jnp.where(valid_c, jnp.exp(s - m), 0.0)
    l = jnp.sum(p, axis=0, keepdims=True)
    p = p * (1.0 / jnp.where(l > 0.0, l, 1.0))
    o_c = jnp.dot(vcmpt_ref[...], p.astype(BF16), preferred_element_type=F32)

    psum = jnp.concatenate(
        [sum(p[:, g * gl + r * Q_BLOCK:g * gl + (r + 1) * Q_BLOCK] for r in range(REP))
         for g in range(B_KV_GROUPS)], axis=1)
    imp = jnp.dot(ovt_ref[...], psum, preferred_element_type=F32,
                  precision=lax.Precision.HIGHEST)
    sl = B_KV_GROUPS * Q_BLOCK
    jt = lax.broadcasted_iota(jnp.int32, (n_slc, sl), 0)
    second_half = (lax.broadcasted_iota(jnp.int32, (1, sl), 1) & (Q_BLOCK - 1)) >= SLC_BLOCK
    qblk = 2 * i + second_half.astype(jnp.int32)
    forced = (jt == 0) | (jt == qblk) | (jt == qblk - 1)
    score = jnp.where(forced, BIG, jnp.where(jt > qblk, NEG, imp))
    rank = jnp.zeros((n_slc, sl), jnp.int32)
    for c in range(n_slc):
        row = score[c:c + 1, :]
        ahead = (row > score) | ((row == score) & (jt > c))
        rank = rank + ahead.astype(jnp.int32)
    selneg = jnp.where(rank < topk, 0.0, NEG).astype(BF16)
    wsel = jnp.concatenate([selneg[:, g * Q_BLOCK:(g + 1) * Q_BLOCK]
                            for g in range(B_KV_GROUPS) for _ in range(REP)], axis=1)
    w = jnp.concatenate([wq, wsel, jnp.zeros((AUG_LANES - gd - n_slc, nl), BF16)], axis=0)

    def sel_tile(t, carry, extra):
        m_i, l_i, a0, a1 = carry
        r0 = pl.multiple_of(t * SEL_TILE, SEL_TILE)
        st = jnp.dot(kaug_ref[pl.ds(r0, SEL_TILE), :], w, preferred_element_type=F32)
        if extra is not None:
            st = st + extra
        m_n = jnp.maximum(m_i, jnp.max(st, axis=0, keepdims=True))
        alpha = jnp.exp(m_i - m_n)
        pt = jnp.exp(st - m_n)
        l_n = alpha * l_i + jnp.sum(pt, axis=0, keepdims=True)
        vt = jnp.concatenate([vslt_ref[2 * t], vslt_ref[2 * t + 1]], axis=1)
        pv = jnp.dot(vt, pt.astype(BF16), preferred_element_type=F32)
        a0 = alpha[:, :gl] * a0 + pv[:HEAD_DIM, :gl]
        a1 = alpha[:, gl:] * a1 + pv[HEAD_DIM:, gl:]
        return m_n, l_n, a0, a1

    last = i // 2
    par = i % 2
    carry = (jnp.full((1, nl), NEG, F32), jnp.zeros((1, nl), F32),
             jnp.zeros((HEAD_DIM, gl), F32), jnp.zeros((HEAD_DIM, gl), F32))
    carry = lax.fori_loop(0, jnp.maximum(last - 1, 0), lambda t, c: sel_tile(t, c, None), carry)
    before = bn_ref[par, 0:SEL_TILE, :] + jnp.where(last > 0, 0.0, NEG)
    carry = sel_tile(jnp.maximum(last - 1, 0), carry, before)
    m_s, l_s, a0, a1 = sel_tile(last, carry, bn_ref[par, SEL_TILE:2 * SEL_TILE, :])
    inv_s = 1.0 / l_s
    o_s = (a0 * inv_s[:, :gl], a1 * inv_s[:, gl:])

    parts, vparts = [], []
    for j in range(n_win):
        kb = i - (n_win - 1) + j
        kbc = jnp.maximum(kb, 0)
        r0 = pl.multiple_of(kbc * Q_BLOCK, Q_BLOCK)
        sj = jnp.dot(kw_ref[pl.ds(r0, Q_BLOCK), :], wq, preferred_element_type=F32)
        parts.append(sj + bw_ref[j * Q_BLOCK:(j + 1) * Q_BLOCK, :] + jnp.where(kb >= 0, 0.0, NEG))
        vparts.append(vwt_ref[kbc])
    sw = jnp.concatenate(parts, axis=0)
    mw = jnp.max(sw, axis=0, keepdims=True)
    pw = jnp.exp(sw - mw)
    inv_w = 1.0 / jnp.sum(pw, axis=0, keepdims=True)
    o_w = jnp.dot(jnp.concatenate(vparts, axis=1), pw.astype(BF16),
                  preferred_element_type=F32) * inv_w

    gt = jnp.transpose(gate_ref[...])
    heads = []
    for g in range(B_KV_GROUPS):
        rows = slice(g * HEAD_DIM, (g + 1) * HEAD_DIM)
        for r in range(REP):
            h = REP * g + r
            lanes = slice(g * gl + r * Q_BLOCK, g * gl + (r + 1) * Q_BLOCK)
            heads.append(gt[3 * h:3 * h + 1, :] * o_c[rows, lanes]
                         + gt[3 * h + 1:3 * h + 2, :] * o_s[g][:, r * Q_BLOCK:(r + 1) * Q_BLOCK]
                         + gt[3 * h + 2:3 * h + 3, :] * o_w[rows, lanes])
    o_ref[...] = jnp.transpose(jnp.concatenate(heads, axis=0)).astype(o_ref.dtype)


def _nsa(p16, p32, kcmp, vcmpt, kaug, vslt, vwt, fc, bw, bn, ovt, bsz, seq, col_q, col_kw, col_gate,
         topk):
    nb = seq // Q_BLOCK
    qw = B_HEADS * HEAD_DIM
    kw = B_KV_GROUPS * HEAD_DIM
    per_batch = lambda a: pl.BlockSpec((None,) + a.shape[1:], lambda b, i: (b,) + (0,) * (a.ndim - 1))
    const = lambda a: pl.BlockSpec(a.shape, lambda b, i: (0,) * a.ndim)
    kern = functools.partial(_nsa_kernel, topk=topk)
    return pl.pallas_call(
        kern,
        grid=(bsz, nb),
        in_specs=[pl.BlockSpec((None, Q_BLOCK, qw), lambda b, i: (b, i, col_q // qw)),
                  per_batch(kcmp), per_batch(vcmpt), per_batch(kaug), per_batch(vslt),
                  pl.BlockSpec((None, seq, kw), lambda b, i: (b, 0, col_kw // kw)),
                  per_batch(vwt),
                  pl.BlockSpec((None, Q_BLOCK, V7X_LANES), lambda b, i: (b, i, col_gate // V7X_LANES)),
                  const(fc), const(bw), const(bn), const(ovt)],
        out_specs=pl.BlockSpec((None, Q_BLOCK, qw), lambda b, i: (b, i, 0)),
        out_shape=jax.ShapeDtypeStruct((bsz, seq, qw), BF16),
        compiler_params=_cparams(("parallel", "arbitrary")),
        name="nsa",
    )(p16, kcmp, vcmpt, kaug, vslt, p16, vwt, p32, fc, bw, bn, ovt)


def _merge_kernel(x_ref, g_ref, ya_ref, yb_ref, wg_ref, wua_ref, wub_ref, wo_ref, o_ref):
    x = x_ref[...]
    d = x.shape[1]
    hb = _rms(x, g_ref[...]).astype(BF16)
    ga = jax.nn.sigmoid(jnp.dot(hb, wg_ref[:, :d], preferred_element_type=F32))
    gb = jax.nn.sigmoid(jnp.dot(hb, wg_ref[:, d:], preferred_element_type=F32))
    ua = jnp.dot(ya_ref[...], wua_ref[...], preferred_element_type=F32)
    ub = jnp.dot(yb_ref[...], wub_ref[...], preferred_element_type=F32)
    merged = ga * ua + gb * ub
    o_ref[...] = x + jnp.dot(merged.astype(BF16), wo_ref[...], preferred_element_type=F32)


def _merge(x2d, g, ya, yb, wg, wua, wub, wo, tm):
    n, d = x2d.shape
    const = lambda a: pl.BlockSpec(a.shape, lambda i: (0, 0))
    row = lambda a: pl.BlockSpec((tm, a.shape[1]), lambda i: (i, 0))
    return pl.pallas_call(
        _merge_kernel,
        grid=(n // tm,),
        in_specs=[row(x2d), const(g), row(ya), row(yb), const(wg), const(wua), const(wub), const(wo)],
        out_specs=row(x2d),
        out_shape=jax.ShapeDtypeStruct((n, d), F32),
        compiler_params=_cparams(("parallel",)),
        name="merge",
    )(x2d, g, ya, yb, wg, wua, wub, wo)


def _ffn_kernel(xc_ref, xp_ref, gn_ref, wu_ref, wg_ref, cwu_ref, cwg_ref, cbu_ref, cbg_ref,
                wo_ref, gf_ref, o_ref, *, chunk):
    i = pl.program_id(1)
    xc = xc_ref[...]
    tm = xc.shape[0]
    halo = xp_ref.shape[0]
    gn = gn_ref[...]
    hc = _rms(xc, gn).astype(BF16)
    hp = _rms(xp_ref[...], gn).astype(BF16)
    live = jnp.where(i > 0, 1.0, 0.0)
    d_ff = wu_ref.shape[1]

    def conv(w_ref, cw_ref, cb_ref, c0):
        cur = jnp.dot(hc, w_ref[:, c0:c0 + chunk], preferred_element_type=F32)
        prv = jnp.dot(hp, w_ref[:, c0:c0 + chunk], preferred_element_type=F32) * live
        ext = jnp.concatenate([prv, cur], axis=0)
        cw = cw_ref[:, c0:c0 + chunk]
        out = cb_ref[:, c0:c0 + chunk]
        for k in range(CONV_WIDTH):
            off = halo - (CONV_WIDTH - 1) + k
            out = out + cw[k:k + 1, :] * ext[off:off + tm]
        return out

    acc = jnp.zeros(xc.shape, F32)
    for c0 in range(0, d_ff, chunk):
        u = conv(wu_ref, cwu_ref, cbu_ref, c0)
        gt = conv(wg_ref, cwg_ref, cbg_ref, c0)
        act = (jax.nn.silu(gt) * u).astype(BF16)
        acc = acc + jnp.dot(act, wo_ref[c0:c0 + chunk, :], preferred_element_type=F32)
    o_ref[...] = _rms(xc + acc, gf_ref[...])


def _ffn(x1, gn, wu, wg, cwu, cwg, cbu, cbg, wo, gf, tm, chunk):
    bsz, seq, d = x1.shape
    halo = V7X_SUBLANES
    const = lambda a: pl.BlockSpec(a.shape, lambda b, i: (0, 0))
    kern = functools.partial(_ffn_kernel, chunk=chunk)
    return pl.pallas_call(
        kern,
        grid=(bsz, seq // tm),
        in_specs=[pl.BlockSpec((None, tm, d), lambda b, i: (b, i, 0)),
                  pl.BlockSpec((None, halo, d),
                               lambda b, i: (b, jnp.maximum(i * (tm // halo) - 1, 0), 0)),
                  const(gn), const(wu), const(wg), const(cwu), const(cwg), const(cbu), const(cbg),
                  const(wo), const(gf)],
        out_specs=pl.BlockSpec((None, tm, d), lambda b, i: (b, i, 0)),
        out_shape=jax.ShapeDtypeStruct((bsz, seq, d), F32),
        compiler_params=_cparams(("parallel", "arbitrary")),
        name="ffn",
    )(x1, x1, gn, wu, wg, cwu, cwg, cbu, cbg, wo, gf)


def _mixers(x, norm_mix, w_in, attn_sinks, cmp_pos_k, cmp_w1_k, cmp_w2_k, cmp_pos_v, cmp_w1_v,
            cmp_w2_v, table):
    bsz, seq, d = x.shape
    n = bsz * seq
    aq, akv = A_HEADS * HEAD_DIM, A_KV_HEADS * HEAD_DIM
    bq, bkv = B_HEADS * HEAD_DIM, B_KV_GROUPS * HEAD_DIM
    n_gate = 3 * B_HEADS
    assert seq % SEL_TILE == 0 and seq // SLC_BLOCK <= AUG_LANES - bkv
    splits = (aq, akv, akv, bq, bkv, bkv, bkv, bkv, bkv, bkv, n_gate, d, d)
    off = np.concatenate([[0], np.cumsum(splits)]).astype(int)
    seg = lambda k: w_in[:, off[k]:off[k + 1]]
    w_gate_nsa = jnp.pad(seg(10), ((0, 0), (0, V7X_LANES - n_gate)))
    w1 = jnp.concatenate([seg(0), seg(3), seg(1), seg(2), seg(6), seg(7), seg(8), seg(9),
                          seg(4), seg(5), w_gate_nsa], axis=1).astype(BF16)
    n16 = aq + bq + 6 * bkv
    n32 = 2 * bkv + V7X_LANES
    col_ka, col_va = aq + bq, aq + bq + akv
    col_ksl, col_vsl, col_kw, col_vw = (aq + bq + 2 * akv + k * bkv for k in range(4))
    tm = min(512, n)
    x2d = x.reshape(n, d)
    g_mix = norm_mix.reshape(1, d)

    p16, p32 = _proj(x2d, g_mix, w1, n16, n32, aq + bq, tm)
    p16 = p16.reshape(bsz, seq, n16)
    p32 = p32.reshape(bsz, seq, n32)

    table_a, table_b = table[:, :A_HEADS], table[:, A_HEADS:]
    bias_a = _band_bias_t(table_a, A_WINDOW + Q_BLOCK, A_WINDOW, A_WINDOW, shift_far=False)
    bias_a = jnp.transpose(bias_a, (0, 2, 1)).reshape(A_KV_HEADS, REP * Q_BLOCK, A_WINDOW + Q_BLOCK)
    y_a = _swa(p16, attn_sinks, bias_a, bsz, seq, 0, col_ka, col_va)

    ncp = seq // CMP_STRIDE
    n_cmp = (seq - CMP_BLOCK) // CMP_STRIDE + 1
    n_slc = seq // SLC_BLOCK
    topk = min(SLC_TOPK, n_slc)

    def rows16(t):
        t = t.reshape(bsz, ncp, CMP_STRIDE, B_KV_GROUPS, HEAD_DIM)
        return jnp.transpose(t, (0, 3, 1, 2, 4)).reshape(bsz, B_KV_GROUPS, ncp, CMP_STRIDE * HEAD_DIM)

    half = CMP_STRIDE * HEAD_DIM
    k_cmp = _compress(rows16(p32[:, :, :bkv]), cmp_pos_k.reshape(2, half),
                      cmp_w1_k.reshape(2, half, -1).astype(BF16), cmp_w2_k.astype(BF16), n_cmp, False)
    v_cmp_t = _compress(rows16(p32[:, :, bkv:2 * bkv]), cmp_pos_v.reshape(2, half),
                        cmp_w1_v.reshape(2, half, -1).astype(BF16), cmp_w2_v.astype(BF16), n_cmp, True)

    onehot = np.zeros((seq, AUG_LANES - bkv), np.float32)
    onehot[np.arange(seq), np.arange(seq) // SLC_BLOCK] = 1.0
    kaug = jnp.concatenate([p16[:, :, col_ksl:col_ksl + bkv],
                            jnp.broadcast_to(jnp.asarray(onehot, BF16), (bsz,) + onehot.shape)], axis=2)

    def key_blocks_t(col):
        t = p16[:, :, col:col + bkv].reshape(bsz, seq // Q_BLOCK, Q_BLOCK, bkv)
        return jnp.transpose(t, (0, 1, 3, 2))

    fc = _cmp_bias_rows(table_b)
    bw = _keys_by_lanes(_band_bias_t(table_b, NSA_WINDOW + Q_BLOCK, NSA_WINDOW, NSA_WINDOW, False))
    bn = jnp.stack([_keys_by_lanes(_band_bias_t(table_b, 2 * SEL_TILE, SEL_TILE + par * Q_BLOCK,
                                                None, True)) for par in range(2)])
    ovt = jnp.asarray(_overlap_t(ncp, n_slc, n_cmp))
    y_b = _nsa(p16, p32, k_cmp, v_cmp_t, kaug, key_blocks_t(col_vsl), key_blocks_t(col_vw),
               fc, bw, bn, ovt, bsz, seq, aq, col_kw, 2 * bkv, topk)
    return y_a, y_b, w_in[:, off[11]:off[13]]


def _layer(x, norm_mix, w_in, attn_sinks, cmp_pos_k, cmp_w1_k, cmp_w2_k, cmp_pos_v, cmp_w1_v,
           cmp_w2_v, w_up_a, w_up_b, w_out, norm_ffn, w_ffn_in, conv_w, conv_b, w_ffn_out,
           table, norm_final):
    bsz, seq, d = x.shape
    n = bsz * seq
    y_a, y_b, w_merge_gates = _mixers(x, norm_mix, w_in, attn_sinks, cmp_pos_k, cmp_w1_k, cmp_w2_k,
                                      cmp_pos_v, cmp_w1_v, cmp_w2_v, table)

    x1 = _merge(x.reshape(n, d), norm_mix.reshape(1, d), y_a.reshape(n, -1), y_b.reshape(n, -1),
                w_merge_gates.astype(BF16), w_up_a.astype(BF16), w_up_b.astype(BF16),
                w_out.astype(BF16), min(512, n))

    d_ff = w_ffn_out.shape[0]
    out = _ffn(x1.reshape(bsz, seq, d), norm_ffn.reshape(1, d),
               w_ffn_in[:, :d_ff].astype(BF16), w_ffn_in[:, d_ff:].astype(BF16),
               conv_w[:, :d_ff], conv_w[:, d_ff:], conv_b[:d_ff].reshape(1, d_ff),
               conv_b[d_ff:].reshape(1, d_ff), w_ffn_out.astype(BF16), norm_final.reshape(1, d),
               min(512, seq), 256)
    return out


def kernel(x, norm_mix, w_in, attn_sinks, cmp_pos_k, cmp_w1_k, cmp_w2_k, cmp_pos_v, cmp_w1_v, cmp_w2_v, w_up_a, w_up_b, w_out, norm_ffn, w_ffn_in, conv_w, conv_b, w_ffn_out, rel_bias_table, norm_final):
    assert norm_mix.shape[0] == 1, "single-layer block"
    return _layer(x, norm_mix[0], w_in[0], attn_sinks[0], cmp_pos_k[0], cmp_w1_k[0], cmp_w2_k[0],
                  cmp_pos_v[0], cmp_w1_v[0], cmp_w2_v[0], w_up_a[0], w_up_b[0], w_out[0],
                  norm_ffn[0], w_ffn_in[0], conv_w[0], conv_b[0], w_ffn_out[0], rel_bias_table,
                  norm_final)
```

```python
import functools
import math

import numpy as np
import jax
import jax.numpy as jnp
from jax import lax
from jax.experimental import pallas as pl
from jax.experimental.pallas import tpu as pltpu

F32 = jnp.float32
BF16 = jnp.bfloat16

HEAD_DIM = 64
A_HEADS = 8
A_KV_HEADS = 2
A_WINDOW = 128
B_HEADS = 8
B_KV_GROUPS = 2
REP = 4
CMP_BLOCK = 32
CMP_STRIDE = 16
SLC_BLOCK = 64
SLC_TOPK = 16
NSA_WINDOW = 512
NUM_BUCKETS = 32
MAX_DISTANCE = 128
CONV_WIDTH = 3
Q_BLOCK = 128
EPS = 1e-6
NEG = -1e30
BIG = 1e30
SCALE = HEAD_DIM ** -0.5

V7X_LANES = 128
V7X_SUBLANES = 8
V7X_VMEM_BYTES = 64 * 1024 * 1024
VMEM_LIMIT = 56 * 1024 * 1024

NT_DIMS = (((1,), (1,)), ((), ()))


def _cparams(semantics):
    return pltpu.CompilerParams(dimension_semantics=semantics, vmem_limit_bytes=VMEM_LIMIT)


def _bucket_np(dist):
    dist = np.maximum(dist, 0)
    max_exact = NUM_BUCKETS // 2
    d = np.maximum(dist, 1).astype(np.float64)
    large = max_exact + (np.log(d / max_exact) / math.log(MAX_DISTANCE / max_exact)
                         * (NUM_BUCKETS - max_exact)).astype(np.int32)
    large = np.minimum(large, NUM_BUCKETS - 1)
    return np.where(dist < max_exact, dist, large).astype(np.int32)


def _table_lookup(table, dist):
    idx = _bucket_np(dist).reshape(-1)
    onehot = np.zeros((NUM_BUCKETS, idx.size), np.float32)
    onehot[idx, np.arange(idx.size)] = 1.0
    vals = jnp.dot(table.T, jnp.asarray(onehot), precision=lax.Precision.HIGHEST)
    return vals.reshape((table.shape[1],) + dist.shape)


def _band_bias_t(table, n_keys, offset, window, shift_far):
    length = n_keys + Q_BLOCK
    m = np.arange(length)
    m = np.where(m < Q_BLOCK, m, m - length)
    dist = m + offset
    ok = dist >= 0 if window is None else (dist >= 0) & (dist < window)
    u = _table_lookup(table, dist)
    if shift_far:
        u = u - table[NUM_BUCKETS - 1][:, None]
    u = jnp.where(ok[None, :], u, NEG)
    n_heads = table.shape[1]
    t = jnp.tile(u, (1, n_keys))[:, :n_keys * (length - 1)]
    return t.reshape(n_heads, n_keys, length - 1)[:, :, :Q_BLOCK]


def _keys_by_lanes(t):
    n_heads, n_keys, _ = t.shape
    return jnp.transpose(t, (1, 0, 2)).reshape(n_keys, n_heads * Q_BLOCK)


def _cmp_bias_rows(table):
    q = np.arange(Q_BLOCK)
    rho = (q - (CMP_BLOCK - 1)) % CMP_STRIDE
    dist = CMP_STRIDE * np.arange(9)[:, None] + rho[None, :]
    vals = _keys_by_lanes(_table_lookup(table, dist))
    vals = jnp.concatenate([vals[:8] - vals[8:9], vals[8:9]], axis=0)
    return jnp.pad(vals, ((0, 16 - 9), (0, 0)))


def _overlap_t(n_cmp_pad, n_slc, n_cmp):
    r = SLC_BLOCK // CMP_STRIDE
    c = CMP_BLOCK // CMP_STRIDE
    j, m, n = np.meshgrid(np.arange(n_slc), np.arange(r), np.arange(c), indexing='ij')
    i = r * j + m - n
    ok = (i >= 0) & (i < n_cmp)
    mat = np.zeros((n_slc, n_cmp_pad), np.float32)
    np.add.at(mat, (j[ok], i[ok]), 1.0)
    return mat


def _rms(x, g):
    return x * lax.rsqrt(jnp.mean(x * x, axis=-1, keepdims=True) + EPS) * g


def _proj_kernel(x_ref, g_ref, w_ref, o16_ref, o32_ref, *, n16, nq, chunk):
    hb = _rms(x_ref[...], g_ref[...]).astype(BF16)
    for c0 in range(0, n16, chunk):
        r = jnp.dot(hb, w_ref[:, c0:c0 + chunk], preferred_element_type=F32)
        if c0 < nq:
            r = r * SCALE
        o16_ref[:, c0:c0 + chunk] = r.astype(BF16)
    n32 = o32_ref.shape[1]
    r = jnp.dot(hb, w_ref[:, n16:n16 + n32], preferred_element_type=F32)
    ncv = n32 - V7X_LANES
    o32_ref[:, :ncv] = r[:, :ncv]
    o32_ref[:, ncv:] = jax.nn.sigmoid(r[:, ncv:])


def _proj(x2d, g, w1, n16, n32, nq, tm):
    n, d = x2d.shape
    kern = functools.partial(_proj_kernel, n16=n16, nq=nq, chunk=256)
    return pl.pallas_call(
        kern,
        grid=(n // tm,),
        in_specs=[pl.BlockSpec((tm, d), lambda i: (i, 0)),
                  pl.BlockSpec((1, d), lambda i: (0, 0)),
                  pl.BlockSpec((d, n16 + n32), lambda i: (0, 0))],
        out_specs=[pl.BlockSpec((tm, n16), lambda i: (i, 0)),
                   pl.BlockSpec((tm, n32), lambda i: (i, 0))],
        out_shape=[jax.ShapeDtypeStruct((n, n16), BF16),
                   jax.ShapeDtypeStruct((n, n32), F32)],
        compiler_params=_cparams(("parallel",)),
        name="proj",
    )(x2d, g, w1)


def _compress_kernel(r_ref, pos_ref, w1_ref, w2_ref, o_ref, *, n_cmp, transpose_out):
    outs = []
    for g in range(r_ref.shape[0]):
        r = r_ref[g]
        a = jnp.dot((r + pos_ref[0:1, :]).astype(BF16), w1_ref[0], preferred_element_type=F32)
        b = jnp.dot((r + pos_ref[1:2, :]).astype(BF16), w1_ref[1], preferred_element_type=F32)
        h1 = a + jnp.concatenate([b[1:], b[:1]], axis=0)
        o = jnp.dot(jax.nn.gelu(h1).astype(BF16), w2_ref[...], preferred_element_type=F32)
        row = lax.broadcasted_iota(jnp.int32, o.shape, 0)
        outs.append(jnp.where(row < n_cmp, o, 0.0))
    o = jnp.concatenate(outs, axis=1)
    if transpose_out:
        o = jnp.transpose(o)
    o_ref[...] = o.astype(o_ref.dtype)


def _compress(r, pos, w1, w2, n_cmp, transpose_out):
    bsz, ng, ncp, kdim = r.shape
    hid = w1.shape[-1]
    oshape = (ng * HEAD_DIM, ncp) if transpose_out else (ncp, ng * HEAD_DIM)
    kern = functools.partial(_compress_kernel, n_cmp=n_cmp, transpose_out=transpose_out)
    return pl.pallas_call(
        kern,
        grid=(bsz,),
        in_specs=[pl.BlockSpec((None, ng, ncp, kdim), lambda b: (b, 0, 0, 0)),
                  pl.BlockSpec((2, kdim), lambda b: (0, 0)),
                  pl.BlockSpec((2, kdim, hid), lambda b: (0, 0, 0)),
                  pl.BlockSpec((hid, HEAD_DIM), lambda b: (0, 0))],
        out_specs=pl.BlockSpec((None,) + oshape, lambda b: (b, 0, 0)),
        out_shape=jax.ShapeDtypeStruct((bsz,) + oshape, BF16),
        compiler_params=_cparams(("parallel",)),
        name="compress",
    )(r, pos, w1, w2)


def _stack_heads(q, g):
    return jnp.concatenate(
        [q[:, (REP * g + r) * HEAD_DIM:(REP * g + r + 1) * HEAD_DIM] for r in range(REP)], axis=0)


def _unstack_heads(per_group):
    cols = []
    for o in per_group:
        cols += [o[r * Q_BLOCK:(r + 1) * Q_BLOCK] for r in range(REP)]
    return jnp.concatenate(cols, axis=1)


def _swa_kernel(sink_ref, q_ref, kp_ref, kc_ref, vp_ref, vc_ref, bias_ref, o_ref):
    i = pl.program_id(1)
    q = q_ref[...]
    k = jnp.concatenate([kp_ref[...], kc_ref[...]], axis=0)
    v = jnp.concatenate([vp_ref[...], vc_ref[...]], axis=0)
    col = lax.broadcasted_iota(jnp.int32, (1, 2 * Q_BLOCK), 1)
    pad_keys = (col < A_WINDOW) & (i == 0)
    outs = []
    for g in range(A_KV_HEADS):
        kg = k[:, g * HEAD_DIM:(g + 1) * HEAD_DIM]
        vg = v[:, g * HEAD_DIM:(g + 1) * HEAD_DIM]
        s = lax.dot_general(_stack_heads(q, g), kg, NT_DIMS, preferred_element_type=F32)
        s = jnp.where(pad_keys, NEG, s + bias_ref[g])
        sink = jnp.concatenate(
            [jnp.full((Q_BLOCK, 1), sink_ref[REP * g + r], F32) for r in range(REP)], axis=0)
        m = jnp.maximum(jnp.max(s, axis=-1, keepdims=True), sink)
        p = jnp.exp(s - m)
        l = jnp.sum(p, axis=-1, keepdims=True) + jnp.exp(sink - m)
        o = jnp.dot(p.astype(BF16), vg, preferred_element_type=F32)
        outs.append(o / l)
    o_ref[...] = _unstack_heads(outs).astype(o_ref.dtype)


def _swa(p16, sinks, bias, bsz, seq, col_q, col_k, col_v):
    nb = seq // Q_BLOCK
    qw = A_HEADS * HEAD_DIM
    kw = A_KV_HEADS * HEAD_DIM
    prev = lambda b, i: jnp.maximum(i - 1, 0)
    return pl.pallas_call(
        _swa_kernel,
        grid=(bsz, nb),
        in_specs=[pl.BlockSpec(memory_space=pltpu.SMEM),
                  pl.BlockSpec((None, Q_BLOCK, qw), lambda b, i: (b, i, col_q // qw)),
                  pl.BlockSpec((None, Q_BLOCK, kw), lambda b, i: (b, prev(b, i), col_k // kw)),
                  pl.BlockSpec((None, Q_BLOCK, kw), lambda b, i: (b, i, col_k // kw)),
                  pl.BlockSpec((None, Q_BLOCK, kw), lambda b, i: (b, prev(b, i), col_v // kw)),
                  pl.BlockSpec((None, Q_BLOCK, kw), lambda b, i: (b, i, col_v // kw)),
                  pl.BlockSpec(bias.shape, lambda b, i: (0, 0, 0))],
        out_specs=pl.BlockSpec((None, Q_BLOCK, qw), lambda b, i: (b, i, 0)),
        out_shape=jax.ShapeDtypeStruct((bsz, seq, qw), BF16),
        compiler_params=_cparams(("parallel", "arbitrary")),
        name="swa",
    )(sinks, p16, p16, p16, p16, p16, bias)


SEL_TILE = 2 * Q_BLOCK
AUG_LANES = 2 * V7X_LANES
CMP_BAND = 24


def _nsa_kernel(q_ref, kcmp_ref, vcmpt_ref, kaug_ref, vslt_ref, kw_ref, vwt_ref, gate_ref,
                fc_ref, bw_ref, bn_ref, ovt_ref, o_ref, sc_ref, *, topk):
    i = pl.program_id(1)
    n_slc, ncp = ovt_ref.shape
    gl = REP * Q_BLOCK
    nl = B_KV_GROUPS * gl
    gd = B_KV_GROUPS * HEAD_DIM
    n_win = NSA_WINDOW // Q_BLOCK + 1

    qt = jnp.transpose(q_ref[...].astype(F32)).astype(BF16)
    zero = jnp.zeros((HEAD_DIM, Q_BLOCK), BF16)
    wq = jnp.concatenate([
        jnp.concatenate([qt[(REP * g + r) * HEAD_DIM:(REP * g + r + 1) * HEAD_DIM] if gg == g else zero
                         for gg in range(B_KV_GROUPS) for r in range(REP)], axis=1)
        for g in range(B_KV_GROUPS)], axis=0)

    qlane = lax.broadcasted_iota(jnp.int32, (1, nl), 1) & (Q_BLOCK - 1)
    cmax = jnp.right_shift(qlane + i * Q_BLOCK - (CMP_BLOCK - 1), 4)
    valid_c = lax.broadcasted_iota(jnp.int32, (ncp, nl), 0) <= cmax
    s = jnp.dot(kcmp_ref[...], wq, preferred_element_type=F32) + fc_ref[8:9, :]
    sc_ref[...] = jnp.where(valid_c, s, NEG)
    b0 = pl.multiple_of(jnp.clip(8 * i - 16, 0, ncp - CMP_BAND), V7X_SUBLANES)
    kkb = cmax - (b0 + lax.broadcasted_iota(jnp.int32, (CMP_BAND, nl), 0))
    delta = jnp.zeros((CMP_BAND, nl), F32)
    for k in range(8):
        delta = jnp.where(kkb == k, fc_ref[k:k + 1, :], delta)
    sc_ref[pl.ds(b0, CMP_BAND), :] = sc_ref[pl.ds(b0, CMP_BAND), :] + delta
    s = sc_ref[...]
    m = jnp.max(s, axis=0, keepdims=True)
    p = jnp.exp(s - m)
    l = jnp.sum(p, axis=0, keepdims=True)
    p = p * jnp.where(cmax >= 0, 1.0 / l, 0.0)
    o_c = jnp.dot(vcmpt_ref[...], p.astype(BF16), preferred_element_type=F32)

    psum = jnp.concatenate(
        [sum(p[:, g * gl + r * Q_BLOCK:g * gl + (r + 1) * Q_BLOCK] for r in range(REP))
         for g in range(B_KV_GROUPS)], axis=1)
    imp = jnp.dot(ovt_ref[...], psum, preferred_element_type=F32,
                  precision=lax.Precision.HIGHEST)
    sl = B_KV_GROUPS * Q_BLOCK
    jt = lax.broadcasted_iota(jnp.int32, (n_slc, sl), 0)
    second_half = (lax.broadcasted_iota(jnp.int32, (1, sl), 1) & (Q_BLOCK - 1)) >= SLC_BLOCK
    qblk = 2 * i + second_half.astype(jnp.int32)
    forced = (jt == 0) | (jt == qblk) | (jt == qblk - 1)
    score = jnp.where(forced, BIG, jnp.where(jt > qblk, NEG, imp))
    rank = jnp.zeros((n_slc, sl), jnp.int32)
    for c in range(n_slc):
        row = score[c:c + 1, :]
        ahead = (row > score) | ((row == score) & (jt > c))
        rank = rank + ahead.astype(jnp.int32)
    selneg = jnp.where(rank < topk, 0.0, NEG).astype(BF16)
    wsel = jnp.concatenate([selneg[:, g * Q_BLOCK:(g + 1) * Q_BLOCK]
                            for g in range(B_KV_GROUPS) for _ in range(REP)], axis=1)
    w = jnp.concatenate([wq, wsel, jnp.zeros((AUG_LANES - gd - n_slc, nl), BF16)], axis=0)

    def sel_tile(t, carry, extra, n_tiles=1):
        m_i, l_i, a0, a1 = carry
        width = n_tiles * SEL_TILE
        r0 = pl.multiple_of(t * SEL_TILE, SEL_TILE)
        st = jnp.dot(kaug_ref[pl.ds(r0, width), :], w, preferred_element_type=F32)
        if extra is not None:
            st = st + extra
        m_n = jnp.maximum(m_i, jnp.max(st, axis=0, keepdims=True))
        alpha = jnp.exp(m_i - m_n)
        pt = jnp.exp(st - m_n)
        l_n = alpha * l_i + jnp.sum(pt, axis=0, keepdims=True)
        blk0 = t * (SEL_TILE // Q_BLOCK)
        vt = jnp.concatenate([vslt_ref[blk0 + j] for j in range(width // Q_BLOCK)], axis=1)
        pv = jnp.dot(vt, pt.astype(BF16), preferred_element_type=F32)
        a0 = alpha[:, :gl] * a0 + pv[:HEAD_DIM, :gl]
        a1 = alpha[:, gl:] * a1 + pv[HEAD_DIM:, gl:]
        return m_n, l_n, a0, a1

    last = i // 2
    par = i % 2
    n_far = jnp.maximum(last - 1, 0)
    carry = (jnp.full((1, nl), NEG, F32), jnp.zeros((1, nl), F32),
             jnp.zeros((HEAD_DIM, gl), F32), jnp.zeros((HEAD_DIM, gl), F32))
    carry = lax.fori_loop(0, n_far // 2, lambda t, c: sel_tile(2 * t, c, None, 2), carry)
    carry = lax.fori_loop(n_far // 2 * 2, n_far, lambda t, c: sel_tile(t, c, None), carry)
    before = bn_ref[par, 0:SEL_TILE, :] + jnp.where(last > 0, 0.0, NEG)
    carry = sel_tile(jnp.maximum(last - 1, 0), carry, before)
    m_s, l_s, a0, a1 = sel_tile(last, carry, bn_ref[par, SEL_TILE:2 * SEL_TILE, :])
    inv_s = 1.0 / l_s
    o_s = (a0 * inv_s[:, :gl], a1 * inv_s[:, gl:])

    span = n_win * Q_BLOCK
    kb0 = jnp.maximum(i - (n_win - 1), 0)
    shift = jnp.maximum(n_win - 1 - i, 0)
    sw = jnp.dot(kw_ref[pl.ds(pl.multiple_of(kb0 * Q_BLOCK, Q_BLOCK), span), :], wq,
                 preferred_element_type=F32)
    sw = sw + bw_ref[pl.ds(pl.multiple_of(shift * Q_BLOCK, Q_BLOCK), span), :]
    mw = jnp.max(sw, axis=0, keepdims=True)
    pw = jnp.exp(sw - mw)
    inv_w = 1.0 / jnp.sum(pw, axis=0, keepdims=True)
    vwt = jnp.concatenate([vwt_ref[kb0 + j] for j in range(n_win)], axis=1)
    o_w = jnp.dot(vwt, pw.astype(BF16), preferred_element_type=F32) * inv_w

    gt = jnp.transpose(gate_ref[...])
    heads = []
    for g in range(B_KV_GROUPS):
        rows = slice(g * HEAD_DIM, (g + 1) * HEAD_DIM)
        for r in range(REP):
            h = REP * g + r
            lanes = slice(g * gl + r * Q_BLOCK, g * gl + (r + 1) * Q_BLOCK)
            heads.append(gt[3 * h:3 * h + 1, :] * o_c[rows, lanes]
                         + gt[3 * h + 1:3 * h + 2, :] * o_s[g][:, r * Q_BLOCK:(r + 1) * Q_BLOCK]
                         + gt[3 * h + 2:3 * h + 3, :] * o_w[rows, lanes])
    o_ref[...] = jnp.transpose(jnp.concatenate(heads, axis=0)).astype(o_ref.dtype)


def _nsa(p16, p32, kcmp, vcmpt, kaug, vslt, vwt, fc, bw, bn, ovt, bsz, seq, col_q, col_kw, col_gate,
         topk):
    nb = seq // Q_BLOCK
    qw = B_HEADS * HEAD_DIM
    kw = B_KV_GROUPS * HEAD_DIM
    per_batch = lambda a: pl.BlockSpec((None,) + a.shape[1:], lambda b, i: (b,) + (0,) * (a.ndim - 1))
    const = lambda a: pl.BlockSpec(a.shape, lambda b, i: (0,) * a.ndim)
    kern = functools.partial(_nsa_kernel, topk=topk)
    return pl.pallas_call(
        kern,
        grid=(bsz, nb),
        in_specs=[pl.BlockSpec((None, Q_BLOCK, qw), lambda b, i: (b, i, col_q // qw)),
                  per_batch(kcmp), per_batch(vcmpt), per_batch(kaug), per_batch(vslt),
                  pl.BlockSpec((None, seq, kw), lambda b, i: (b, 0, col_kw // kw)),
                  per_batch(vwt),
                  pl.BlockSpec((None, Q_BLOCK, V7X_LANES), lambda b, i: (b, i, col_gate // V7X_LANES)),
                  const(fc), const(bw), const(bn), const(ovt)],
        out_specs=pl.BlockSpec((None, Q_BLOCK, qw), lambda b, i: (b, i, 0)),
        out_shape=jax.ShapeDtypeStruct((bsz, seq, qw), BF16),
        scratch_shapes=[pltpu.VMEM((kcmp.shape[1], B_KV_GROUPS * REP * Q_BLOCK), F32)],
        compiler_params=_cparams(("parallel", "arbitrary")),
        name="nsa",
    )(p16, kcmp, vcmpt, kaug, vslt, p16, vwt, p32, fc, bw, bn, ovt)


def _merge_kernel(x_ref, g_ref, ya_ref, yb_ref, wg_ref, wua_ref, wub_ref, wo_ref, o_ref):
    x = x_ref[...]
    d = x.shape[1]
    hb = _rms(x, g_ref[...]).astype(BF16)
    ga = jax.nn.sigmoid(jnp.dot(hb, wg_ref[:, :d], preferred_element_type=F32))
    gb = jax.nn.sigmoid(jnp.dot(hb, wg_ref[:, d:], preferred_element_type=F32))
    ua = jnp.dot(ya_ref[...], wua_ref[...], preferred_element_type=F32)
    ub = jnp.dot(yb_ref[...], wub_ref[...], preferred_element_type=F32)
    merged = ga * ua + gb * ub
    o_ref[...] = x + jnp.dot(merged.astype(BF16), wo_ref[...], preferred_element_type=F32)


def _merge(x2d, g, ya, yb, wg, wua, wub, wo, tm):
    n, d = x2d.shape
    const = lambda a: pl.BlockSpec(a.shape, lambda i: (0, 0))
    row = lambda a: pl.BlockSpec((tm, a.shape[1]), lambda i: (i, 0))
    return pl.pallas_call(
        _merge_kernel,
        grid=(n // tm,),
        in_specs=[row(x2d), const(g), row(ya), row(yb), const(wg), const(wua), const(wub), const(wo)],
        out_specs=row(x2d),
        out_shape=jax.ShapeDtypeStruct((n, d), F32),
        compiler_params=_cparams(("parallel",)),
        name="merge",
    )(x2d, g, ya, yb, wg, wua, wub, wo)


def _ffn_kernel(xc_ref, xp_ref, gn_ref, wu_ref, wg_ref, cwu_ref, cwg_ref, cbu_ref, cbg_ref,
                wo_ref, gf_ref, o_ref, *, chunk):
    i = pl.program_id(1)
    xc = xc_ref[...]
    tm = xc.shape[0]
    halo = xp_ref.shape[0]
    gn = gn_ref[...]
    hc = _rms(xc, gn).astype(BF16)
    hp = _rms(xp_ref[...], gn).astype(BF16)
    live = jnp.where(i > 0, 1.0, 0.0)
    d_ff = wu_ref.shape[1]

    def conv(w_ref, cw_ref, cb_ref, c0):
        cur = jnp.dot(hc, w_ref[:, c0:c0 + chunk], preferred_element_type=F32)
        prv = jnp.dot(hp, w_ref[:, c0:c0 + chunk], preferred_element_type=F32) * live
        ext = jnp.concatenate([prv, cur], axis=0)
        cw = cw_ref[:, c0:c0 + chunk]
        out = cb_ref[:, c0:c0 + chunk]
        for k in range(CONV_WIDTH):
            off = halo - (CONV_WIDTH - 1) + k
            out = out + cw[k:k + 1, :] * ext[off:off + tm]
        return out

    acc = jnp.zeros(xc.shape, F32)
    for c0 in range(0, d_ff, chunk):
        u = conv(wu_ref, cwu_ref, cbu_ref, c0)
        gt = conv(wg_ref, cwg_ref, cbg_ref, c0)
        act = (jax.nn.silu(gt) * u).astype(BF16)
        acc = acc + jnp.dot(act, wo_ref[c0:c0 + chunk, :], preferred_element_type=F32)
    o_ref[...] = _rms(xc + acc, gf_ref[...])


def _ffn(x1, gn, wu, wg, cwu, cwg, cbu, cbg, wo, gf, tm, chunk):
    bsz, seq, d = x1.shape
    halo = V7X_SUBLANES
    const = lambda a: pl.BlockSpec(a.shape, lambda b, i: (0, 0))
    kern = functools.partial(_ffn_kernel, chunk=chunk)
    return pl.pallas_call(
        kern,
        grid=(bsz, seq // tm),
        in_specs=[pl.BlockSpec((None, tm, d), lambda b, i: (b, i, 0)),
                  pl.BlockSpec((None, halo, d),
                               lambda b, i: (b, jnp.maximum(i * (tm // halo) - 1, 0), 0)),
                  const(gn), const(wu), const(wg), const(cwu), const(cwg), const(cbu), const(cbg),
                  const(wo), const(gf)],
        out_specs=pl.BlockSpec((None, tm, d), lambda b, i: (b, i, 0)),
        out_shape=jax.ShapeDtypeStruct((bsz, seq, d), F32),
        compiler_params=_cparams(("parallel", "arbitrary")),
        name="ffn",
    )(x1, x1, gn, wu, wg, cwu, cwg, cbu, cbg, wo, gf)


def _mixers(x, norm_mix, w_in, attn_sinks, cmp_pos_k, cmp_w1_k, cmp_w2_k, cmp_pos_v, cmp_w1_v,
            cmp_w2_v, table):
    bsz, seq, d = x.shape
    n = bsz * seq
    aq, akv = A_HEADS * HEAD_DIM, A_KV_HEADS * HEAD_DIM
    bq, bkv = B_HEADS * HEAD_DIM, B_KV_GROUPS * HEAD_DIM
    n_gate = 3 * B_HEADS
    assert seq % SEL_TILE == 0 and seq // SLC_BLOCK <= AUG_LANES - bkv
    splits = (aq, akv, akv, bq, bkv, bkv, bkv, bkv, bkv, bkv, n_gate, d, d)
    off = np.concatenate([[0], np.cumsum(splits)]).astype(int)
    seg = lambda k: w_in[:, off[k]:off[k + 1]]
    w_gate_nsa = jnp.pad(seg(10), ((0, 0), (0, V7X_LANES - n_gate)))
    w1 = jnp.concatenate([seg(0), seg(3), seg(1), seg(2), seg(6), seg(7), seg(8), seg(9),
                          seg(4), seg(5), w_gate_nsa], axis=1).astype(BF16)
    n16 = aq + bq + 6 * bkv
    n32 = 2 * bkv + V7X_LANES
    col_ka, col_va = aq + bq, aq + bq + akv
    col_ksl, col_vsl, col_kw, col_vw = (aq + bq + 2 * akv + k * bkv for k in range(4))
    tm = min(512, n)
    x2d = x.reshape(n, d)
    g_mix = norm_mix.reshape(1, d)

    p16, p32 = _proj(x2d, g_mix, w1, n16, n32, aq + bq, tm)
    p16 = p16.reshape(bsz, seq, n16)
    p32 = p32.reshape(bsz, seq, n32)

    table_a, table_b = table[:, :A_HEADS], table[:, A_HEADS:]
    bias_a = _band_bias_t(table_a, A_WINDOW + Q_BLOCK, A_WINDOW, A_WINDOW, shift_far=False)
    bias_a = jnp.transpose(bias_a, (0, 2, 1)).reshape(A_KV_HEADS, REP * Q_BLOCK, A_WINDOW + Q_BLOCK)
    y_a = _swa(p16, attn_sinks, bias_a, bsz, seq, 0, col_ka, col_va)

    ncp = seq // CMP_STRIDE
    n_cmp = (seq - CMP_BLOCK) // CMP_STRIDE + 1
    n_slc = seq // SLC_BLOCK
    topk = min(SLC_TOPK, n_slc)

    def rows16(t):
        t = t.reshape(bsz, ncp, CMP_STRIDE, B_KV_GROUPS, HEAD_DIM)
        return jnp.transpose(t, (0, 3, 1, 2, 4)).reshape(bsz, B_KV_GROUPS, ncp, CMP_STRIDE * HEAD_DIM)

    half = CMP_STRIDE * HEAD_DIM
    k_cmp = _compress(rows16(p32[:, :, :bkv]), cmp_pos_k.reshape(2, half),
                      cmp_w1_k.reshape(2, half, -1).astype(BF16), cmp_w2_k.astype(BF16), n_cmp, False)
    v_cmp_t = _compress(rows16(p32[:, :, bkv:2 * bkv]), cmp_pos_v.reshape(2, half),
                        cmp_w1_v.reshape(2, half, -1).astype(BF16), cmp_w2_v.astype(BF16), n_cmp, True)

    onehot = np.zeros((seq, AUG_LANES - bkv), np.float32)
    onehot[np.arange(seq), np.arange(seq) // SLC_BLOCK] = 1.0
    kaug = jnp.concatenate([p16[:, :, col_ksl:col_ksl + bkv],
                            jnp.broadcast_to(jnp.asarray(onehot, BF16), (bsz,) + onehot.shape)], axis=2)

    def key_blocks_t(col):
        t = p16[:, :, col:col + bkv].reshape(bsz, seq // Q_BLOCK, Q_BLOCK, bkv)
        return jnp.transpose(t, (0, 1, 3, 2))

    fc = _cmp_bias_rows(table_b)
    bw = _keys_by_lanes(_band_bias_t(table_b, NSA_WINDOW + Q_BLOCK, NSA_WINDOW, NSA_WINDOW, False))
    bw = jnp.pad(bw, ((0, NSA_WINDOW), (0, 0)), constant_values=NEG)
    bn = jnp.stack([_keys_by_lanes(_band_bias_t(table_b, 2 * SEL_TILE, SEL_TILE + par * Q_BLOCK,
                                                None, True)) for par in range(2)])
    ovt = jnp.asarray(_overlap_t(ncp, n_slc, n_cmp))
    y_b = _nsa(p16, p32, k_cmp, v_cmp_t, kaug, key_blocks_t(col_vsl), key_blocks_t(col_vw),
               fc, bw, bn, ovt, bsz, seq, aq, col_kw, 2 * bkv, topk)
    return y_a, y_b, w_in[:, off[11]:off[13]]


def _layer(x, norm_mix, w_in, attn_sinks, cmp_pos_k, cmp_w1_k, cmp_w2_k, cmp_pos_v, cmp_w1_v,
           cmp_w2_v, w_up_a, w_up_b, w_out, norm_ffn, w_ffn_in, conv_w, conv_b, w_ffn_out,
           table, norm_final):
    bsz, seq, d = x.shape
    n = bsz * seq
    y_a, y_b, w_merge_gates = _mixers(x, norm_mix, w_in, attn_sinks, cmp_pos_k, cmp_w1_k, cmp_w2_k,
                                      cmp_pos_v, cmp_w1_v, cmp_w2_v, table)

    x1 = _merge(x.reshape(n, d), norm_mix.reshape(1, d), y_a.reshape(n, -1), y_b.reshape(n, -1),
                w_merge_gates.astype(BF16), w_up_a.astype(BF16), w_up_b.astype(BF16),
                w_out.astype(BF16), min(512, n))

    d_ff = w_ffn_out.shape[0]
    out = _ffn(x1.reshape(bsz, seq, d), norm_ffn.reshape(1, d),
               w_ffn_in[:, :d_ff].astype(BF16), w_ffn_in[:, d_ff:].astype(BF16),
               conv_w[:, :d_ff], conv_w[:, d_ff:], conv_b[:d_ff].reshape(1, d_ff),
               conv_b[d_ff:].reshape(1, d_ff), w_ffn_out.astype(BF16), norm_final.reshape(1, d),
               min(512, seq), 256)
    return out


def kernel(x, norm_mix, w_in, attn_sinks, cmp_pos_k, cmp_w1_k, cmp_w2_k, cmp_pos_v, cmp_w1_v, cmp_w2_v, w_up_a, w_up_b, w_out, norm_ffn, w_ffn_in, conv_w, conv_b, w_ffn_out, rel_bias_table, norm_final):
    assert norm_mix.shape[0] == 1, "single-layer block"
    return _layer(x, norm_mix[0], w_in[0], attn_sinks[0], cmp_pos_k[0], cmp_w1_k[0], cmp_w2_k[0],
                  cmp_pos_v[0], cmp_w1_v[0], cmp_w2_v[0], w_up_a[0], w_up_b[0], w_out[0],
                  norm_ffn[0], w_ffn_in[0], conv_w[0], conv_b[0], w_ffn_out[0], rel_bias_table,
                  norm_final)
```

```python
import functools
import math

import numpy as np
import jax
import jax.numpy as jnp
from jax import lax
from jax.experimental import pallas as pl
from jax.experimental.pallas import tpu as pltpu

F32 = jnp.float32
BF16 = jnp.bfloat16

HEAD_DIM = 64
A_HEADS = 8
A_KV_HEADS = 2
A_WINDOW = 128
B_HEADS = 8
B_KV_GROUPS = 2
REP = 4
CMP_BLOCK = 32
CMP_STRIDE = 16
SLC_BLOCK = 64
SLC_SHIFT = 6
SLC_TOPK = 16
NSA_WINDOW = 512
NUM_BUCKETS = 32
MAX_DISTANCE = 128
CONV_WIDTH = 3
Q_BLOCK = 128
EPS = 1e-6
NEG = -1e30
BIG = 1e30
SCALE = HEAD_DIM ** -0.5

V7X_LANES = 128
V7X_SUBLANES = 8
V7X_VMEM_BYTES = 64 * 1024 * 1024
VMEM_LIMIT = 56 * 1024 * 1024


def _cparams(semantics):
    return pltpu.CompilerParams(dimension_semantics=semantics, vmem_limit_bytes=VMEM_LIMIT)


def _bucket_np(dist):
    dist = np.maximum(dist, 0)
    max_exact = NUM_BUCKETS // 2
    d = np.maximum(dist, 1).astype(np.float64)
    large = max_exact + (np.log(d / max_exact) / math.log(MAX_DISTANCE / max_exact)
                         * (NUM_BUCKETS - max_exact)).astype(np.int32)
    large = np.minimum(large, NUM_BUCKETS - 1)
    return np.where(dist < max_exact, dist, large).astype(np.int32)


def _table_lookup(table, dist):
    idx = _bucket_np(dist).reshape(-1)
    onehot = np.zeros((NUM_BUCKETS, idx.size), np.float32)
    onehot[idx, np.arange(idx.size)] = 1.0
    vals = jnp.dot(table.T, jnp.asarray(onehot), precision=lax.Precision.HIGHEST)
    return vals.reshape((table.shape[1],) + dist.shape)


def _band_bias_t(table, n_keys, offset, window, shift_far):
    length = n_keys + Q_BLOCK
    m = np.arange(length)
    m = np.where(m < Q_BLOCK, m, m - length)
    dist = m + offset
    ok = dist >= 0 if window is None else (dist >= 0) & (dist < window)
    u = _table_lookup(table, dist)
    if shift_far:
        u = u - table[NUM_BUCKETS - 1][:, None]
    u = jnp.where(ok[None, :], u, NEG)
    n_heads = table.shape[1]
    t = jnp.tile(u, (1, n_keys))[:, :n_keys * (length - 1)]
    return t.reshape(n_heads, n_keys, length - 1)[:, :, :Q_BLOCK]


def _keys_by_lanes(t):
    n_heads, n_keys, _ = t.shape
    return jnp.transpose(t, (1, 0, 2)).reshape(n_keys, n_heads * Q_BLOCK)


def _cmp_bias_rows(table):
    q = np.arange(Q_BLOCK)
    rho = (q - (CMP_BLOCK - 1)) % CMP_STRIDE
    dist = CMP_STRIDE * np.arange(9)[:, None] + rho[None, :]
    vals = _keys_by_lanes(_table_lookup(table, dist))
    vals = jnp.concatenate([vals[:8] - vals[8:9], vals[8:9]], axis=0)
    return jnp.pad(vals, ((0, 16 - 9), (0, 0)))


def _overlap_t(n_cmp_pad, n_slc, n_cmp):
    r = SLC_BLOCK // CMP_STRIDE
    c = CMP_BLOCK // CMP_STRIDE
    j, m, n = np.meshgrid(np.arange(n_slc), np.arange(r), np.arange(c), indexing='ij')
    i = r * j + m - n
    ok = (i >= 0) & (i < n_cmp)
    mat = np.zeros((n_slc, n_cmp_pad), np.float32)
    np.add.at(mat, (j[ok], i[ok]), 1.0)
    return mat


def _rms(x, g):
    return x * lax.rsqrt(jnp.mean(x * x, axis=-1, keepdims=True) + EPS) * g


PROJ_CHUNK = 256
VT_A, VT_SL, VT_W = 0, 1, 2


def _proj_kernel(x_ref, g_ref, w_ref, o16_ref, kaug_ref, vt_ref, o32_ref, *, nq, seq):
    tm = x_ref.shape[0]
    kv = B_KV_GROUPS * HEAD_DIM
    n16 = o16_ref.shape[1]
    hb = _rms(x_ref[...], g_ref[...]).astype(BF16)

    def cols(c0, width=PROJ_CHUNK):
        return jnp.dot(hb, w_ref[:, c0:c0 + width], preferred_element_type=F32)

    def put_transposed(a, r):
        for j in range(tm // Q_BLOCK):
            vt_ref[a, j] = jnp.transpose(r[j * Q_BLOCK:(j + 1) * Q_BLOCK, :]).astype(BF16)

    for c0 in range(0, n16, PROJ_CHUNK):
        r = cols(c0)
        o16_ref[:, c0:c0 + PROJ_CHUNK] = (r * SCALE if c0 < nq else r).astype(BF16)
    r = cols(n16)
    pos = (pl.program_id(0) * tm) % seq + lax.broadcasted_iota(jnp.int32, (tm, kv), 0)
    onehot = lax.broadcasted_iota(jnp.int32, (tm, kv), 1) == jnp.right_shift(pos, SLC_SHIFT)
    kaug_ref[:, :kv] = r[:, :kv].astype(BF16)
    kaug_ref[:, kv:] = jnp.where(onehot, 1.0, 0.0).astype(BF16)
    put_transposed(0, r[:, kv:])
    r = cols(n16 + 2 * kv)
    put_transposed(1, r[:, :kv])
    put_transposed(2, r[:, kv:])
    o32_ref[:, :2 * kv] = cols(n16 + 4 * kv)
    o32_ref[:, 2 * kv:] = jax.nn.sigmoid(cols(n16 + 6 * kv, V7X_LANES))


def _proj(x2d, g, w1, n16, nq, seq, tm):
    n, d = x2d.shape
    kv = B_KV_GROUPS * HEAD_DIM
    n32 = 2 * kv + V7X_LANES
    assert AUG_LANES == 2 * kv and seq % tm == 0 and tm % Q_BLOCK == 0
    kern = functools.partial(_proj_kernel, nq=nq, seq=seq)
    return pl.pallas_call(
        kern,
        grid=(n // tm,),
        in_specs=[pl.BlockSpec((tm, d), lambda i: (i, 0)),
                  pl.BlockSpec((1, d), lambda i: (0, 0)),
                  pl.BlockSpec(w1.shape, lambda i: (0, 0))],
        out_specs=[pl.BlockSpec((tm, n16), lambda i: (i, 0)),
                   pl.BlockSpec((tm, AUG_LANES), lambda i: (i, 0)),
                   pl.BlockSpec((3, tm // Q_BLOCK, kv, Q_BLOCK), lambda i: (0, i, 0, 0)),
                   pl.BlockSpec((tm, n32), lambda i: (i, 0))],
        out_shape=[jax.ShapeDtypeStruct((n, n16), BF16),
                   jax.ShapeDtypeStruct((n, AUG_LANES), BF16),
                   jax.ShapeDtypeStruct((3, n // Q_BLOCK, kv, Q_BLOCK), BF16),
                   jax.ShapeDtypeStruct((n, n32), F32)],
        compiler_params=_cparams(("parallel",)),
        name="proj",
    )(x2d, g, w1)


CMP_PAIR = 2


def _compress_kernel(x_ref, pos_ref, w1_ref, w2_ref, o_ref, *, n_cmp, transpose_out):
    ncp = x_ref.shape[0] // CMP_STRIDE
    halves = []
    for half in range(CMP_BLOCK // CMP_STRIDE):
        acc = None
        for l0 in range(0, CMP_STRIDE, CMP_PAIR):
            lhs = jnp.concatenate(
                [(x_ref[pl.ds(l0 + k, ncp, stride=CMP_STRIDE), :]
                  + pos_ref[half * CMP_STRIDE + l0 + k:half * CMP_STRIDE + l0 + k + 1, :]).astype(BF16)
                 for k in range(CMP_PAIR)], axis=1)
            part = jnp.dot(lhs, w1_ref[(half * CMP_STRIDE + l0) // CMP_PAIR],
                           preferred_element_type=F32)
            acc = part if acc is None else acc + part
        halves.append(acc)
    top, bottom = halves
    h1 = top + jnp.concatenate([bottom[1:], bottom[:1]], axis=0)
    o = jnp.dot(jax.nn.gelu(h1).astype(BF16), w2_ref[...], preferred_element_type=F32)
    row = lax.broadcasted_iota(jnp.int32, o.shape, 0)
    o = jnp.where(row < n_cmp, o, 0.0)
    if transpose_out:
        o = jnp.transpose(o)
    o_ref[...] = o.astype(o_ref.dtype)


def _compress(p32, col, pos, w1, w2, n_cmp, transpose_out):
    bsz, seq, _ = p32.shape
    ncp = seq // CMP_STRIDE
    gd = B_KV_GROUPS * HEAD_DIM
    hid = w1.shape[-1]
    eye = jnp.eye(B_KV_GROUPS, dtype=F32)
    w1bd = jnp.einsum('gh,lde->lgdhe', eye, w1.reshape(CMP_BLOCK, HEAD_DIM, hid))
    w1bd = w1bd.reshape(CMP_BLOCK // CMP_PAIR, CMP_PAIR * gd, B_KV_GROUPS * hid).astype(BF16)
    w2bd = jnp.einsum('gh,ed->gehd', eye, w2).reshape(B_KV_GROUPS * hid, gd).astype(BF16)
    pos2 = jnp.tile(pos, (1, B_KV_GROUPS))
    oshape = (gd, ncp) if transpose_out else (ncp, gd)
    const = lambda a: pl.BlockSpec(a.shape, lambda b: (0,) * a.ndim)
    kern = functools.partial(_compress_kernel, n_cmp=n_cmp, transpose_out=transpose_out)
    return pl.pallas_call(
        kern,
        grid=(bsz,),
        in_specs=[pl.BlockSpec((None, seq, gd), lambda b: (b, 0, col // gd)),
                  const(pos2), const(w1bd), const(w2bd)],
        out_specs=pl.BlockSpec((None,) + oshape, lambda b: (b, 0, 0)),
        out_shape=jax.ShapeDtypeStruct((bsz,) + oshape, BF16),
        compiler_params=_cparams(("parallel",)),
        name="compress",
    )(p32, pos2, w1bd, w2bd)


def _block_diag_qt(q):
    qt = jnp.transpose(q.astype(F32)).astype(BF16)
    zero = jnp.zeros((HEAD_DIM, Q_BLOCK), BF16)
    n_groups = q.shape[1] // (REP * HEAD_DIM)
    return jnp.concatenate([
        jnp.concatenate([qt[(REP * g + r) * HEAD_DIM:(REP * g + r + 1) * HEAD_DIM] if gg == g else zero
                         for gg in range(n_groups) for r in range(REP)], axis=1)
        for g in range(n_groups)], axis=0)


def _heads_to_rows(o_t):
    n_heads = o_t.shape[1] // Q_BLOCK
    heads = [o_t[(h // REP) * HEAD_DIM:(h // REP + 1) * HEAD_DIM, h * Q_BLOCK:(h + 1) * Q_BLOCK]
             for h in range(n_heads)]
    return jnp.transpose(jnp.concatenate(heads, axis=0))


def _swa_kernel(q_ref, k_ref, vt_ref, bias_ref, sink_ref, o_ref):
    i = pl.program_id(1)
    n_blk = A_WINDOW // Q_BLOCK + 1
    span = n_blk * Q_BLOCK
    wq = _block_diag_qt(q_ref[...])
    kb0 = jnp.maximum(i - (n_blk - 1), 0)
    shift = jnp.maximum(n_blk - 1 - i, 0)
    s = jnp.dot(k_ref[pl.ds(pl.multiple_of(kb0 * Q_BLOCK, Q_BLOCK), span), :], wq,
                preferred_element_type=F32)
    s = s + bias_ref[pl.ds(pl.multiple_of(shift * Q_BLOCK, Q_BLOCK), span), :]
    sink = sink_ref[...]
    m = jnp.maximum(jnp.max(s, axis=0, keepdims=True), sink)
    p = jnp.exp(s - m)
    inv = 1.0 / (jnp.sum(p, axis=0, keepdims=True) + jnp.exp(sink - m))
    vt = jnp.concatenate([vt_ref[kb0 + j] for j in range(n_blk)], axis=1)
    o_t = jnp.dot(vt, p.astype(BF16), preferred_element_type=F32) * inv
    o_ref[...] = _heads_to_rows(o_t).astype(o_ref.dtype)


def _swa(p16, vt, bias, sink_row, bsz, seq, col_q, col_k):
    nb = seq // Q_BLOCK
    qw = A_HEADS * HEAD_DIM
    kw = A_KV_HEADS * HEAD_DIM
    const = lambda a: pl.BlockSpec(a.shape, lambda b, i: (0,) * a.ndim)
    return pl.pallas_call(
        _swa_kernel,
        grid=(bsz, nb),
        in_specs=[pl.BlockSpec((None, Q_BLOCK, qw), lambda b, i: (b, i, col_q // qw)),
                  pl.BlockSpec((None, seq, kw), lambda b, i: (b, 0, col_k // kw)),
                  pl.BlockSpec((None, None) + vt.shape[2:], lambda b, i: (VT_A, b, 0, 0, 0)),
                  const(bias), const(sink_row)],
        out_specs=pl.BlockSpec((None, Q_BLOCK, qw), lambda b, i: (b, i, 0)),
        out_shape=jax.ShapeDtypeStruct((bsz, seq, qw), BF16),
        compiler_params=_cparams(("parallel", "arbitrary")),
        name="swa",
    )(p16, p16, vt, bias, sink_row)


SEL_TILE = 2 * Q_BLOCK
AUG_LANES = 2 * V7X_LANES
CMP_BAND = 24


def _nsa_kernel(q_ref, kcmp_ref, vcmpt_ref, kaug_ref, vslt_ref, kw_ref, vwt_ref, gate_ref,
                fc_ref, bw_ref, bn_ref, ovt_ref, o_ref, sc_ref, *, topk):
    i = pl.program_id(1)
    n_slc, ncp = ovt_ref.shape
    gl = REP * Q_BLOCK
    nl = B_KV_GROUPS * gl
    gd = B_KV_GROUPS * HEAD_DIM
    n_win = NSA_WINDOW // Q_BLOCK + 1

    wq = _block_diag_qt(q_ref[...])

    qlane = lax.broadcasted_iota(jnp.int32, (1, nl), 1) & (Q_BLOCK - 1)
    cmax = jnp.right_shift(qlane + i * Q_BLOCK - (CMP_BLOCK - 1), 4)
    valid_c = lax.broadcasted_iota(jnp.int32, (ncp, nl), 0) <= cmax
    s = jnp.dot(kcmp_ref[...], wq, preferred_element_type=F32) + fc_ref[8:9, :]
    sc_ref[...] = jnp.where(valid_c, s, NEG)
    b0 = pl.multiple_of(jnp.clip(8 * i - 16, 0, ncp - CMP_BAND), V7X_SUBLANES)
    kkb = cmax - (b0 + lax.broadcasted_iota(jnp.int32, (CMP_BAND, nl), 0))
    delta = jnp.zeros((CMP_BAND, nl), F32)
    for k in range(8):
        delta = jnp.where(kkb == k, fc_ref[k:k + 1, :], delta)
    sc_ref[pl.ds(b0, CMP_BAND), :] = sc_ref[pl.ds(b0, CMP_BAND), :] + delta
    s = sc_ref[...]
    m = jnp.max(s, axis=0, keepdims=True)
    p = jnp.exp(s - m)
    l = jnp.sum(p, axis=0, keepdims=True)
    p = p * jnp.where(cmax >= 0, 1.0 / l, 0.0)
    o_c = jnp.dot(vcmpt_ref[...], p.astype(BF16), preferred_element_type=F32)

    psum = jnp.concatenate(
        [sum(p[:, g * gl + r * Q_BLOCK:g * gl + (r + 1) * Q_BLOCK] for r in range(REP))
         for g in range(B_KV_GROUPS)], axis=1)
    imp = jnp.dot(ovt_ref[...], psum, preferred_element_type=F32,
                  precision=lax.Precision.HIGHEST)
    sl = B_KV_GROUPS * Q_BLOCK
    jt = lax.broadcasted_iota(jnp.int32, (n_slc, sl), 0)
    second_half = (lax.broadcasted_iota(jnp.int32, (1, sl), 1) & (Q_BLOCK - 1)) >= SLC_BLOCK
    qblk = 2 * i + second_half.astype(jnp.int32)
    forced = (jt == 0) | (jt == qblk) | (jt == qblk - 1)
    score = jnp.where(forced, BIG, jnp.where(jt > qblk, NEG, imp))
    rank = jnp.zeros((n_slc, sl), jnp.int32)
    for c in range(n_slc):
        row = score[c:c + 1, :]
        ahead = (row > score) | ((row == score) & (jt > c))
        rank = rank + ahead.astype(jnp.int32)
    selneg = jnp.where(rank < topk, 0.0, NEG).astype(BF16)
    wsel = jnp.concatenate([selneg[:, g * Q_BLOCK:(g + 1) * Q_BLOCK]
                            for g in range(B_KV_GROUPS) for _ in range(REP)], axis=1)
    w = jnp.concatenate([wq, wsel, jnp.zeros((AUG_LANES - gd - n_slc, nl), BF16)], axis=0)

    def sel_tile(t, carry, extra, n_tiles=1):
        m_i, l_i, a0, a1 = carry
        width = n_tiles * SEL_TILE
        r0 = pl.multiple_of(t * SEL_TILE, SEL_TILE)
        st = jnp.dot(kaug_ref[pl.ds(r0, width), :], w, preferred_element_type=F32)
        if extra is not None:
            st = st + extra
        m_n = jnp.maximum(m_i, jnp.max(st, axis=0, keepdims=True))
        alpha = jnp.exp(m_i - m_n)
        pt = jnp.exp(st - m_n)
        l_n = alpha * l_i + jnp.sum(pt, axis=0, keepdims=True)
        blk0 = t * (SEL_TILE // Q_BLOCK)
        vt = jnp.concatenate([vslt_ref[blk0 + j] for j in range(width // Q_BLOCK)], axis=1)
        pv = jnp.dot(vt, pt.astype(BF16), preferred_element_type=F32)
        a0 = alpha[:, :gl] * a0 + pv[:HEAD_DIM, :gl]
        a1 = alpha[:, gl:] * a1 + pv[HEAD_DIM:, gl:]
        return m_n, l_n, a0, a1

    last = i // 2
    par = i % 2
    n_far = jnp.maximum(last - 1, 0)
    carry = (jnp.full((1, nl), NEG, F32), jnp.zeros((1, nl), F32),
             jnp.zeros((HEAD_DIM, gl), F32), jnp.zeros((HEAD_DIM, gl), F32))
    carry = lax.fori_loop(0, n_far // 2, lambda t, c: sel_tile(2 * t, c, None, 2), carry)
    carry = lax.fori_loop(n_far // 2 * 2, n_far, lambda t, c: sel_tile(t, c, None), carry)
    before = bn_ref[par, 0:SEL_TILE, :] + jnp.where(last > 0, 0.0, NEG)
    carry = sel_tile(jnp.maximum(last - 1, 0), carry, before)
    m_s, l_s, a0, a1 = sel_tile(last, carry, bn_ref[par, SEL_TILE:2 * SEL_TILE, :])
    inv_s = 1.0 / l_s
    o_s = (a0 * inv_s[:, :gl], a1 * inv_s[:, gl:])

    span = n_win * Q_BLOCK
    kb0 = jnp.maximum(i - (n_win - 1), 0)
    shift = jnp.maximum(n_win - 1 - i, 0)
    sw = jnp.dot(kw_ref[pl.ds(pl.multiple_of(kb0 * Q_BLOCK, Q_BLOCK), span), :], wq,
                 preferred_element_type=F32)
    sw = sw + bw_ref[pl.ds(pl.multiple_of(shift * Q_BLOCK, Q_BLOCK), span), :]
    mw = jnp.max(sw, axis=0, keepdims=True)
    pw = jnp.exp(sw - mw)
    inv_w = 1.0 / jnp.sum(pw, axis=0, keepdims=True)
    vwt = jnp.concatenate([vwt_ref[kb0 + j] for j in range(n_win)], axis=1)
    o_w = jnp.dot(vwt, pw.astype(BF16), preferred_element_type=F32) * inv_w

    gt = jnp.transpose(gate_ref[...])
    heads = []
    for g in range(B_KV_GROUPS):
        rows = slice(g * HEAD_DIM, (g + 1) * HEAD_DIM)
        for r in range(REP):
            h = REP * g + r
            lanes = slice(g * gl + r * Q_BLOCK, g * gl + (r + 1) * Q_BLOCK)
            heads.append(gt[3 * h:3 * h + 1, :] * o_c[rows, lanes]
                         + gt[3 * h + 1:3 * h + 2, :] * o_s[g][:, r * Q_BLOCK:(r + 1) * Q_BLOCK]
                         + gt[3 * h + 2:3 * h + 3, :] * o_w[rows, lanes])
    o_ref[...] = jnp.transpose(jnp.concatenate(heads, axis=0)).astype(o_ref.dtype)


def _nsa(p16, p32, kcmp, vcmpt, kaug, vt, fc, bw, bn, ovt, bsz, seq, col_q, col_kw, col_gate, topk):
    nb = seq // Q_BLOCK
    qw = B_HEADS * HEAD_DIM
    kw = B_KV_GROUPS * HEAD_DIM
    per_batch = lambda a: pl.BlockSpec((None,) + a.shape[1:], lambda b, i: (b,) + (0,) * (a.ndim - 1))
    const = lambda a: pl.BlockSpec(a.shape, lambda b, i: (0,) * a.ndim)
    vt_seg = lambda a: pl.BlockSpec((None, None) + vt.shape[2:], lambda b, i: (a, b, 0, 0, 0))
    kern = functools.partial(_nsa_kernel, topk=topk)
    return pl.pallas_call(
        kern,
        grid=(bsz, nb),
        in_specs=[pl.BlockSpec((None, Q_BLOCK, qw), lambda b, i: (b, i, col_q // qw)),
                  per_batch(kcmp), per_batch(vcmpt), per_batch(kaug), vt_seg(VT_SL),
                  pl.BlockSpec((None, seq, kw), lambda b, i: (b, 0, col_kw // kw)),
                  vt_seg(VT_W),
                  pl.BlockSpec((None, Q_BLOCK, V7X_LANES), lambda b, i: (b, i, col_gate // V7X_LANES)),
                  const(fc), const(bw), const(bn), const(ovt)],
        out_specs=pl.BlockSpec((None, Q_BLOCK, qw), lambda b, i: (b, i, 0)),
        out_shape=jax.ShapeDtypeStruct((bsz, seq, qw), BF16),
        scratch_shapes=[pltpu.VMEM((kcmp.shape[1], B_KV_GROUPS * REP * Q_BLOCK), F32)],
        compiler_params=_cparams(("parallel", "arbitrary")),
        name="nsa",
    )(p16, kcmp, vcmpt, kaug, vt, p16, vt, p32, fc, bw, bn, ovt)


def _merge_kernel(x_ref, g_ref, ya_ref, yb_ref, wg_ref, wua_ref, wub_ref, wo_ref, o_ref):
    x = x_ref[...]
    d = x.shape[1]
    hb = _rms(x, g_ref[...]).astype(BF16)
    ga = jax.nn.sigmoid(jnp.dot(hb, wg_ref[:, :d], preferred_element_type=F32))
    gb = jax.nn.sigmoid(jnp.dot(hb, wg_ref[:, d:], preferred_element_type=F32))
    ua = jnp.dot(ya_ref[...], wua_ref[...], preferred_element_type=F32)
    ub = jnp.dot(yb_ref[...], wub_ref[...], preferred_element_type=F32)
    merged = ga * ua + gb * ub
    o_ref[...] = x + jnp.dot(merged.astype(BF16), wo_ref[...], preferred_element_type=F32)


def _merge(x2d, g, ya, yb, wg, wua, wub, wo, tm):
    n, d = x2d.shape
    const = lambda a: pl.BlockSpec(a.shape, lambda i: (0, 0))
    row = lambda a: pl.BlockSpec((tm, a.shape[1]), lambda i: (i, 0))
    return pl.pallas_call(
        _merge_kernel,
        grid=(n // tm,),
        in_specs=[row(x2d), const(g), row(ya), row(yb), const(wg), const(wua), const(wub), const(wo)],
        out_specs=row(x2d),
        out_shape=jax.ShapeDtypeStruct((n, d), F32),
        compiler_params=_cparams(("parallel",)),
        name="merge",
    )(x2d, g, ya, yb, wg, wua, wub, wo)


FFN_HALO = 16


def _ffn_kernel(xc_ref, xp_ref, gn_ref, wi_ref, cw_ref, cb_ref, wo_ref, gf_ref, o_ref, *, chunk):
    i = pl.program_id(1)
    xc = xc_ref[...]
    tm = xc.shape[0]
    gn = gn_ref[...]
    hp = _rms(xp_ref[...], gn) * jnp.where(i > 0, 1.0, 0.0)
    h = jnp.concatenate([hp, _rms(xc, gn)], axis=0).astype(BF16)
    d_ff = wo_ref.shape[0]

    def up(c0):
        return jnp.dot(h, wi_ref[:, c0:c0 + chunk], preferred_element_type=F32)

    def conv(ext, c0):
        cw = cw_ref[:, c0:c0 + chunk]
        out = cb_ref[:, c0:c0 + chunk]
        for k in range(CONV_WIDTH):
            off = FFN_HALO - (CONV_WIDTH - 1) + k
            out = out + cw[k:k + 1, :] * ext[off:off + tm]
        return out

    acc = jnp.zeros(xc.shape, F32)
    nxt = (up(0), up(d_ff))
    for c0 in range(0, d_ff, chunk):
        ext_u, ext_g = nxt
        if c0 + chunk < d_ff:
            nxt = (up(c0 + chunk), up(d_ff + c0 + chunk))
        act = (jax.nn.silu(conv(ext_g, d_ff + c0)) * conv(ext_u, c0)).astype(BF16)
        acc = acc + jnp.dot(act, wo_ref[c0:c0 + chunk, :], preferred_element_type=F32)
    o_ref[...] = _rms(xc + acc, gf_ref[...])


def _ffn(x1, gn, wi, cw, cb, wo, gf, tm, chunk):
    bsz, seq, d = x1.shape
    const = lambda a: pl.BlockSpec(a.shape, lambda b, i: (0, 0))
    kern = functools.partial(_ffn_kernel, chunk=chunk)
    return pl.pallas_call(
        kern,
        grid=(bsz, seq // tm),
        in_specs=[pl.BlockSpec((None, tm, d), lambda b, i: (b, i, 0)),
                  pl.BlockSpec((None, FFN_HALO, d),
                               lambda b, i: (b, jnp.maximum(i * (tm // FFN_HALO) - 1, 0), 0)),
                  const(gn), const(wi), const(cw), const(cb), const(wo), const(gf)],
        out_specs=pl.BlockSpec((None, tm, d), lambda b, i: (b, i, 0)),
        out_shape=jax.ShapeDtypeStruct((bsz, seq, d), F32),
        compiler_params=_cparams(("parallel", "arbitrary")),
        name="ffn",
    )(x1, x1, gn, wi, cw, cb, wo, gf)


def _mixers(x, norm_mix, w_in, attn_sinks, cmp_pos_k, cmp_w1_k, cmp_w2_k, cmp_pos_v, cmp_w1_v,
            cmp_w2_v, table):
    bsz, seq, d = x.shape
    n = bsz * seq
    aq, akv = A_HEADS * HEAD_DIM, A_KV_HEADS * HEAD_DIM
    bq, bkv = B_HEADS * HEAD_DIM, B_KV_GROUPS * HEAD_DIM
    n_gate = 3 * B_HEADS
    assert seq % SEL_TILE == 0 and seq // SLC_BLOCK <= AUG_LANES - bkv
    splits = (aq, akv, akv, bq, bkv, bkv, bkv, bkv, bkv, bkv, n_gate, d, d)
    off = np.concatenate([[0], np.cumsum(splits)]).astype(int)
    seg = lambda k: w_in[:, off[k]:off[k + 1]]
    assert akv == bkv
    w_gate_nsa = jnp.pad(seg(10), ((0, 0), (0, V7X_LANES - n_gate)))
    w1 = jnp.concatenate([seg(0), seg(3), seg(1), seg(8), seg(6), seg(2), seg(7), seg(9),
                          seg(4), seg(5), w_gate_nsa], axis=1).astype(BF16)
    n16 = aq + bq + 2 * bkv
    col_ka, col_kw = aq + bq, aq + bq + akv
    tm = min(512, seq)
    x2d = x.reshape(n, d)
    g_mix = norm_mix.reshape(1, d)

    p16, kaug, vt, p32 = _proj(x2d, g_mix, w1, n16, aq + bq, seq, tm)
    p16 = p16.reshape(bsz, seq, n16)
    kaug = kaug.reshape(bsz, seq, AUG_LANES)
    vt = vt.reshape(3, bsz, seq // Q_BLOCK, bkv, Q_BLOCK)
    p32 = p32.reshape(bsz, seq, 2 * bkv + V7X_LANES)

    table_a, table_b = table[:, :A_HEADS], table[:, A_HEADS:]
    bias_a = _keys_by_lanes(_band_bias_t(table_a, A_WINDOW + Q_BLOCK, A_WINDOW, A_WINDOW, False))
    bias_a = jnp.pad(bias_a, ((0, A_WINDOW), (0, 0)), constant_values=NEG)
    sink_row = jnp.repeat(attn_sinks, Q_BLOCK).reshape(1, A_HEADS * Q_BLOCK)
    y_a = _swa(p16, vt, bias_a, sink_row, bsz, seq, 0, col_ka)

    ncp = seq // CMP_STRIDE
    n_cmp = (seq - CMP_BLOCK) // CMP_STRIDE + 1
    n_slc = seq // SLC_BLOCK
    topk = min(SLC_TOPK, n_slc)

    k_cmp = _compress(p32, 0, cmp_pos_k, cmp_w1_k, cmp_w2_k, n_cmp, False)
    v_cmp_t = _compress(p32, bkv, cmp_pos_v, cmp_w1_v, cmp_w2_v, n_cmp, True)

    fc = _cmp_bias_rows(table_b)
    bw = _keys_by_lanes(_band_bias_t(table_b, NSA_WINDOW + Q_BLOCK, NSA_WINDOW, NSA_WINDOW, False))
    bw = jnp.pad(bw, ((0, NSA_WINDOW), (0, 0)), constant_values=NEG)
    bn = jnp.stack([_keys_by_lanes(_band_bias_t(table_b, 2 * SEL_TILE, SEL_TILE + par * Q_BLOCK,
                                                None, True)) for par in range(2)])
    ovt = jnp.asarray(_overlap_t(ncp, n_slc, n_cmp))
    y_b = _nsa(p16, p32, k_cmp, v_cmp_t, kaug, vt, fc, bw, bn, ovt, bsz, seq, aq, col_kw, 2 * bkv,
               topk)
    return y_a, y_b, w_in[:, off[11]:off[13]]


def _layer(x, norm_mix, w_in, attn_sinks, cmp_pos_k, cmp_w1_k, cmp_w2_k, cmp_pos_v, cmp_w1_v,
           cmp_w2_v, w_up_a, w_up_b, w_out, norm_ffn, w_ffn_in, conv_w, conv_b, w_ffn_out,
           table, norm_final):
    bsz, seq, d = x.shape
    n = bsz * seq
    y_a, y_b, w_merge_gates = _mixers(x, norm_mix, w_in, attn_sinks, cmp_pos_k, cmp_w1_k, cmp_w2_k,
                                      cmp_pos_v, cmp_w1_v, cmp_w2_v, table)

    x1 = _merge(x.reshape(n, d), norm_mix.reshape(1, d), y_a.reshape(n, -1), y_b.reshape(n, -1),
                w_merge_gates.astype(BF16), w_up_a.astype(BF16), w_up_b.astype(BF16),
                w_out.astype(BF16), min(512, n))

    return _ffn(x1.reshape(bsz, seq, d), norm_ffn.reshape(1, d), w_ffn_in.astype(BF16), conv_w,
                conv_b.reshape(1, -1), w_ffn_out.astype(BF16), norm_final.reshape(1, d),
                min(512, seq), 256)


def kernel(x, norm_mix, w_in, attn_sinks, cmp_pos_k, cmp_w1_k, cmp_w2_k, cmp_pos_v, cmp_w1_v, cmp_w2_v, w_up_a, w_up_b, w_out, norm_ffn, w_ffn_in, conv_w, conv_b, w_ffn_out, rel_bias_table, norm_final):
    assert norm_mix.shape[0] == 1, "single-layer block"
    return _layer(x, norm_mix[0], w_in[0], attn_sinks[0], cmp_pos_k[0], cmp_w1_k[0], cmp_w2_k[0],
                  cmp_pos_v[0], cmp_w1_v[0], cmp_w2_v[0], w_up_a[0], w_up_b[0], w_out[0],
                  norm_ffn[0], w_ffn_in[0], conv_w[0], conv_b[0], w_ffn_out[0], rel_bias_table,
                  norm_final)
```

```python
import functools
import math

import numpy as np
import jax
import jax.numpy as jnp
from jax import lax
from jax.experimental import pallas as pl
from jax.experimental.pallas import tpu as pltpu

F32 = jnp.float32
BF16 = jnp.bfloat16

HEAD_DIM = 64
A_HEADS = 8
A_KV_HEADS = 2
A_WINDOW = 128
B_HEADS = 8
B_KV_GROUPS = 2
REP = 4
CMP_BLOCK = 32
CMP_STRIDE = 16
SLC_BLOCK = 64
SLC_SHIFT = 6
SLC_TOPK = 16
NSA_WINDOW = 512
NUM_BUCKETS = 32
MAX_DISTANCE = 128
CONV_WIDTH = 3
Q_BLOCK = 128
EPS = 1e-6
NEG = -1e30
BIG = 1e30
SCALE = HEAD_DIM ** -0.5
LOG2E = math.log2(math.e)
Q_SCALE = SCALE * LOG2E
SUM_ROWS = 16

V7X_LANES = 128
V7X_SUBLANES = 8
V7X_VMEM_BYTES = 64 * 1024 * 1024
VMEM_LIMIT = 56 * 1024 * 1024


def _cparams(semantics):
    return pltpu.CompilerParams(dimension_semantics=semantics, vmem_limit_bytes=VMEM_LIMIT)


def _bucket_np(dist):
    dist = np.maximum(dist, 0)
    max_exact = NUM_BUCKETS // 2
    d = np.maximum(dist, 1).astype(np.float64)
    large = max_exact + (np.log(d / max_exact) / math.log(MAX_DISTANCE / max_exact)
                         * (NUM_BUCKETS - max_exact)).astype(np.int32)
    large = np.minimum(large, NUM_BUCKETS - 1)
    return np.where(dist < max_exact, dist, large).astype(np.int32)


def _table_lookup(table, dist):
    idx = _bucket_np(dist).reshape(-1)
    onehot = np.zeros((NUM_BUCKETS, idx.size), np.float32)
    onehot[idx, np.arange(idx.size)] = 1.0
    vals = jnp.dot(table.T, jnp.asarray(onehot), precision=lax.Precision.HIGHEST)
    return vals.reshape((table.shape[1],) + dist.shape)


def _band_bias_t(table, n_keys, offset, window, shift_far):
    length = n_keys + Q_BLOCK
    m = np.arange(length)
    m = np.where(m < Q_BLOCK, m, m - length)
    dist = m + offset
    ok = dist >= 0 if window is None else (dist >= 0) & (dist < window)
    u = _table_lookup(table, dist)
    if shift_far:
        u = u - table[NUM_BUCKETS - 1][:, None]
    u = jnp.where(ok[None, :], u, NEG)
    n_heads = table.shape[1]
    t = jnp.tile(u, (1, n_keys))[:, :n_keys * (length - 1)]
    return t.reshape(n_heads, n_keys, length - 1)[:, :, :Q_BLOCK]


def _keys_by_lanes(t):
    n_heads, n_keys, _ = t.shape
    return jnp.transpose(t, (1, 0, 2)).reshape(n_keys, n_heads * Q_BLOCK)


def _cmp_bias_rows(table):
    q = np.arange(Q_BLOCK)
    rho = (q - (CMP_BLOCK - 1)) % CMP_STRIDE
    dist = CMP_STRIDE * np.arange(9)[:, None] + rho[None, :]
    vals = _keys_by_lanes(_table_lookup(table, dist))
    vals = jnp.concatenate([vals[:8] - vals[8:9], vals[8:9]], axis=0)
    return jnp.pad(vals, ((0, 16 - 9), (0, 0)))


def _overlap_t(n_cmp_pad, n_slc, n_cmp):
    r = SLC_BLOCK // CMP_STRIDE
    c = CMP_BLOCK // CMP_STRIDE
    j, m, n = np.meshgrid(np.arange(n_slc), np.arange(r), np.arange(c), indexing='ij')
    i = r * j + m - n
    ok = (i >= 0) & (i < n_cmp)
    mat = np.zeros((n_slc, n_cmp_pad), np.float32)
    np.add.at(mat, (j[ok], i[ok]), 1.0)
    return mat


def _rms(x, g):
    return x * lax.rsqrt(jnp.mean(x * x, axis=-1, keepdims=True) + EPS) * g


PROJ_CHUNK = 256
VT_A, VT_SL, VT_W = 0, 1, 2


def _proj_kernel(x_ref, g_ref, w_ref, o16_ref, kaug_ref, vt_ref, o32_ref, *, nq, seq):
    tm = x_ref.shape[0]
    kv = B_KV_GROUPS * HEAD_DIM
    n16 = o16_ref.shape[1]
    hb = _rms(x_ref[...], g_ref[...]).astype(BF16)

    def cols(c0, width=PROJ_CHUNK):
        return jnp.dot(hb, w_ref[:, c0:c0 + width], preferred_element_type=F32)

    def put_transposed(a, r):
        for j in range(tm // Q_BLOCK):
            vt_ref[a, j] = jnp.transpose(r[j * Q_BLOCK:(j + 1) * Q_BLOCK, :]).astype(BF16)

    for c0 in range(0, n16, PROJ_CHUNK):
        r = cols(c0)
        o16_ref[:, c0:c0 + PROJ_CHUNK] = (r * Q_SCALE if c0 < nq else r).astype(BF16)
    r = cols(n16)
    pos = (pl.program_id(0) * tm) % seq + lax.broadcasted_iota(jnp.int32, (tm, kv), 0)
    onehot = lax.broadcasted_iota(jnp.int32, (tm, kv), 1) == jnp.right_shift(pos, SLC_SHIFT)
    kaug_ref[:, :kv] = r[:, :kv].astype(BF16)
    kaug_ref[:, kv:] = jnp.where(onehot, 1.0, 0.0).astype(BF16)
    put_transposed(0, r[:, kv:])
    r = cols(n16 + 2 * kv)
    put_transposed(1, r[:, :kv])
    put_transposed(2, r[:, kv:])
    o32_ref[:, :2 * kv] = cols(n16 + 4 * kv)
    o32_ref[:, 2 * kv:] = jax.nn.sigmoid(cols(n16 + 6 * kv, V7X_LANES))


def _proj(x2d, g, w1, n16, nq, seq, tm):
    n, d = x2d.shape
    kv = B_KV_GROUPS * HEAD_DIM
    n32 = 2 * kv + V7X_LANES
    assert AUG_LANES == 2 * kv and seq % tm == 0 and tm % Q_BLOCK == 0
    kern = functools.partial(_proj_kernel, nq=nq, seq=seq)
    return pl.pallas_call(
        kern,
        grid=(n // tm,),
        in_specs=[pl.BlockSpec((tm, d), lambda i: (i, 0)),
                  pl.BlockSpec((1, d), lambda i: (0, 0)),
                  pl.BlockSpec(w1.shape, lambda i: (0, 0))],
        out_specs=[pl.BlockSpec((tm, n16), lambda i: (i, 0)),
                   pl.BlockSpec((tm, AUG_LANES), lambda i: (i, 0)),
                   pl.BlockSpec((3, tm // Q_BLOCK, kv, Q_BLOCK), lambda i: (0, i, 0, 0)),
                   pl.BlockSpec((tm, n32), lambda i: (i, 0))],
        out_shape=[jax.ShapeDtypeStruct((n, n16), BF16),
                   jax.ShapeDtypeStruct((n, AUG_LANES), BF16),
                   jax.ShapeDtypeStruct((3, n // Q_BLOCK, kv, Q_BLOCK), BF16),
                   jax.ShapeDtypeStruct((n, n32), F32)],
        compiler_params=_cparams(("parallel",)),
        name="proj",
    )(x2d, g, w1)


CMP_PAIR = 2


def _compress_kernel(x_ref, pos_ref, w1_ref, w2_ref, o_ref, *, n_cmp, transpose_out):
    ncp = x_ref.shape[0] // CMP_STRIDE
    halves = []
    for half in range(CMP_BLOCK // CMP_STRIDE):
        acc = None
        for l0 in range(0, CMP_STRIDE, CMP_PAIR):
            lhs = jnp.concatenate(
                [(x_ref[pl.ds(l0 + k, ncp, stride=CMP_STRIDE), :]
                  + pos_ref[half * CMP_STRIDE + l0 + k:half * CMP_STRIDE + l0 + k + 1, :]).astype(BF16)
                 for k in range(CMP_PAIR)], axis=1)
            part = jnp.dot(lhs, w1_ref[(half * CMP_STRIDE + l0) // CMP_PAIR],
                           preferred_element_type=F32)
            acc = part if acc is None else acc + part
        halves.append(acc)
    top, bottom = halves
    h1 = top + jnp.concatenate([bottom[1:], bottom[:1]], axis=0)
    o = jnp.dot(jax.nn.gelu(h1).astype(BF16), w2_ref[...], preferred_element_type=F32)
    row = lax.broadcasted_iota(jnp.int32, o.shape, 0)
    o = jnp.where(row < n_cmp, o, 0.0)
    if transpose_out:
        o = jnp.transpose(o)
    o_ref[...] = o.astype(o_ref.dtype)


def _compress(p32, col, pos, w1, w2, n_cmp, transpose_out):
    bsz, seq, _ = p32.shape
    ncp = seq // CMP_STRIDE
    gd = B_KV_GROUPS * HEAD_DIM
    hid = w1.shape[-1]
    eye = jnp.eye(B_KV_GROUPS, dtype=F32)
    w1bd = jnp.einsum('gh,lde->lgdhe', eye, w1.reshape(CMP_BLOCK, HEAD_DIM, hid))
    w1bd = w1bd.reshape(CMP_BLOCK // CMP_PAIR, CMP_PAIR * gd, B_KV_GROUPS * hid).astype(BF16)
    w2bd = jnp.einsum('gh,ed->gehd', eye, w2).reshape(B_KV_GROUPS * hid, gd).astype(BF16)
    pos2 = jnp.tile(pos, (1, B_KV_GROUPS))
    oshape = (gd, ncp) if transpose_out else (ncp, gd)
    const = lambda a: pl.BlockSpec(a.shape, lambda b: (0,) * a.ndim)
    kern = functools.partial(_compress_kernel, n_cmp=n_cmp, transpose_out=transpose_out)
    return pl.pallas_call(
        kern,
        grid=(bsz,),
        in_specs=[pl.BlockSpec((None, seq, gd), lambda b: (b, 0, col // gd)),
                  const(pos2), const(w1bd), const(w2bd)],
        out_specs=pl.BlockSpec((None,) + oshape, lambda b: (b, 0, 0)),
        out_shape=jax.ShapeDtypeStruct((bsz,) + oshape, BF16),
        compiler_params=_cparams(("parallel",)),
        name="compress",
    )(p32, pos2, w1bd, w2bd)


def _block_diag_qt(q):
    qt = jnp.transpose(q.astype(F32)).astype(BF16)
    zero = jnp.zeros((HEAD_DIM, Q_BLOCK), BF16)
    n_groups = q.shape[1] // (REP * HEAD_DIM)
    return jnp.concatenate([
        jnp.concatenate([qt[(REP * g + r) * HEAD_DIM:(REP * g + r + 1) * HEAD_DIM] if gg == g else zero
                         for gg in range(n_groups) for r in range(REP)], axis=1)
        for g in range(n_groups)], axis=0)


def _pv_and_sum(vt, p):
    rows = vt.shape[0]
    ext = jnp.concatenate([vt, jnp.ones((SUM_ROWS, vt.shape[1]), BF16)], axis=0)
    pv = jnp.dot(ext, p.astype(BF16), preferred_element_type=F32)
    return pv[:rows], pv[rows:rows + 1]


def _heads_to_rows(o_t):
    n_heads = o_t.shape[1] // Q_BLOCK
    heads = [o_t[(h // REP) * HEAD_DIM:(h // REP + 1) * HEAD_DIM, h * Q_BLOCK:(h + 1) * Q_BLOCK]
             for h in range(n_heads)]
    return jnp.transpose(jnp.concatenate(heads, axis=0))


def _swa_kernel(q_ref, k_ref, vt_ref, bias_ref, sink_ref, o_ref):
    i = pl.program_id(1)
    n_blk = A_WINDOW // Q_BLOCK + 1
    span = n_blk * Q_BLOCK
    wq = _block_diag_qt(q_ref[...])
    kb0 = jnp.maximum(i - (n_blk - 1), 0)
    shift = jnp.maximum(n_blk - 1 - i, 0)
    s = jnp.dot(k_ref[pl.ds(pl.multiple_of(kb0 * Q_BLOCK, Q_BLOCK), span), :], wq,
                preferred_element_type=F32)
    s = s + bias_ref[pl.ds(pl.multiple_of(shift * Q_BLOCK, Q_BLOCK), span), :]
    sink = sink_ref[...]
    m = jnp.maximum(jnp.max(s, axis=0, keepdims=True), sink)
    p = jnp.exp2(s - m)
    vt = jnp.concatenate([vt_ref[kb0 + j] for j in range(n_blk)], axis=1)
    o_t, l = _pv_and_sum(vt, p)
    o_t = o_t * (1.0 / (l + jnp.exp2(sink - m)))
    o_ref[...] = _heads_to_rows(o_t).astype(o_ref.dtype)


def _swa(p16, vt, bias, sink_row, bsz, seq, col_q, col_k):
    nb = seq // Q_BLOCK
    qw = A_HEADS * HEAD_DIM
    kw = A_KV_HEADS * HEAD_DIM
    const = lambda a: pl.BlockSpec(a.shape, lambda b, i: (0,) * a.ndim)
    return pl.pallas_call(
        _swa_kernel,
        grid=(bsz, nb),
        in_specs=[pl.BlockSpec((None, Q_BLOCK, qw), lambda b, i: (b, i, col_q // qw)),
                  pl.BlockSpec((None, seq, kw), lambda b, i: (b, 0, col_k // kw)),
                  pl.BlockSpec((None, None) + vt.shape[2:], lambda b, i: (VT_A, b, 0, 0, 0)),
                  const(bias), const(sink_row)],
        out_specs=pl.BlockSpec((None, Q_BLOCK, qw), lambda b, i: (b, i, 0)),
        out_shape=jax.ShapeDtypeStruct((bsz, seq, qw), BF16),
        compiler_params=_cparams(("parallel", "arbitrary")),
        name="swa",
    )(p16, p16, vt, bias, sink_row)


SEL_TILE = 2 * Q_BLOCK
AUG_LANES = 2 * V7X_LANES
CMP_BAND = 24


def _nsa_kernel(q_ref, kcmp_ref, vcmpt_ref, kaug_ref, vslt_ref, kw_ref, vwt_ref, gate_ref,
                fc_ref, bw_ref, bn_ref, ovt_ref, o_ref, sc_ref, s0_ref, s1_ref, *, topk):
    i = pl.program_id(1)
    n_slc, ncp = ovt_ref.shape
    gl = REP * Q_BLOCK
    nl = B_KV_GROUPS * gl
    gd = B_KV_GROUPS * HEAD_DIM
    n_win = NSA_WINDOW // Q_BLOCK + 1

    wq = _block_diag_qt(q_ref[...])

    qlane = lax.broadcasted_iota(jnp.int32, (1, nl), 1) & (Q_BLOCK - 1)
    cmax = jnp.right_shift(qlane + i * Q_BLOCK - (CMP_BLOCK - 1), 4)
    valid_c = lax.broadcasted_iota(jnp.int32, (ncp, nl), 0) <= cmax
    s = jnp.dot(kcmp_ref[...], wq, preferred_element_type=F32) + fc_ref[8:9, :]
    sc_ref[...] = jnp.where(valid_c, s, NEG)
    b0 = pl.multiple_of(jnp.clip(8 * i - 16, 0, ncp - CMP_BAND), V7X_SUBLANES)
    kkb = cmax - (b0 + lax.broadcasted_iota(jnp.int32, (CMP_BAND, nl), 0))
    delta = jnp.zeros((CMP_BAND, nl), F32)
    for k in range(8):
        delta = jnp.where(kkb == k, fc_ref[k:k + 1, :], delta)
    sc_ref[pl.ds(b0, CMP_BAND), :] = sc_ref[pl.ds(b0, CMP_BAND), :] + delta
    s = sc_ref[...]
    m = jnp.max(s, axis=0, keepdims=True)
    p = jnp.exp2(s - m)
    l = jnp.sum(p, axis=0, keepdims=True)
    p = p * jnp.where(cmax >= 0, 1.0 / l, 0.0)
    o_c = jnp.dot(vcmpt_ref[...], p.astype(BF16), preferred_element_type=F32)
    psum = jnp.concatenate(
        [sum(p[:, g * gl + r * Q_BLOCK:g * gl + (r + 1) * Q_BLOCK] for r in range(REP))
         for g in range(B_KV_GROUPS)], axis=1)
    imp = jnp.dot(ovt_ref[...], psum, preferred_element_type=F32,
                  precision=lax.Precision.HIGHEST)

    span = n_win * Q_BLOCK
    kb0 = jnp.maximum(i - (n_win - 1), 0)
    shift = jnp.maximum(n_win - 1 - i, 0)
    sw = jnp.dot(kw_ref[pl.ds(pl.multiple_of(kb0 * Q_BLOCK, Q_BLOCK), span), :], wq,
                 preferred_element_type=F32)
    sw = sw + bw_ref[pl.ds(pl.multiple_of(shift * Q_BLOCK, Q_BLOCK), span), :]
    mw = jnp.max(sw, axis=0, keepdims=True)
    pw = jnp.exp2(sw - mw)
    vwt = jnp.concatenate([vwt_ref[kb0 + j] for j in range(n_win)], axis=1)
    o_w, l_w = _pv_and_sum(vwt, pw)
    o_w = o_w * (1.0 / l_w)

    sl = B_KV_GROUPS * Q_BLOCK
    jt = lax.broadcasted_iota(jnp.int32, (n_slc, sl), 0)
    second_half = (lax.broadcasted_iota(jnp.int32, (1, sl), 1) & (Q_BLOCK - 1)) >= SLC_BLOCK
    qblk = 2 * i + second_half.astype(jnp.int32)
    forced = (jt == 0) | (jt == qblk) | (jt == qblk - 1)
    score = jnp.where(forced, BIG, jnp.where(jt > qblk, NEG, imp))
    rank = jnp.zeros((n_slc, sl), jnp.int32)
    for c in range(n_slc):
        row = score[c:c + 1, :]
        ahead = (row > score) | ((row == score) & (jt > c))
        rank = rank + ahead.astype(jnp.int32)
    selneg = jnp.where(rank < topk, 0.0, NEG).astype(BF16)
    wsel = jnp.concatenate([selneg[:, g * Q_BLOCK:(g + 1) * Q_BLOCK]
                            for g in range(B_KV_GROUPS) for _ in range(REP)], axis=1)
    w = jnp.concatenate([wq, wsel, jnp.zeros((AUG_LANES - gd - n_slc, nl), BF16)], axis=0)

    def scores(t):
        r0 = pl.multiple_of(t * SEL_TILE, SEL_TILE)
        return jnp.dot(kaug_ref[pl.ds(r0, SEL_TILE), :], w, preferred_element_type=F32)

    def update(carry, st, t):
        m_i, l_i, a0, a1 = carry
        m_n = jnp.maximum(m_i, jnp.max(st, axis=0, keepdims=True))
        alpha = jnp.exp2(m_i - m_n)
        pt = jnp.exp2(st - m_n)
        blk0 = t * (SEL_TILE // Q_BLOCK)
        vt = jnp.concatenate([vslt_ref[blk0 + j] for j in range(SEL_TILE // Q_BLOCK)], axis=1)
        pv, l_t = _pv_and_sum(vt, pt)
        a0 = alpha[:, :gl] * a0 + pv[:HEAD_DIM, :gl]
        a1 = alpha[:, gl:] * a1 + pv[HEAD_DIM:, gl:]
        return m_n, alpha * l_i + l_t, a0, a1

    last = i // 2
    par = i % 2
    n_far = jnp.maximum(last - 1, 0)
    carry = (jnp.full((1, nl), NEG, F32), jnp.zeros((1, nl), F32),
             jnp.zeros((HEAD_DIM, gl), F32), jnp.zeros((HEAD_DIM, gl), F32))
    s0_ref[...] = scores(0)

    def two_tiles(u, carry):
        t = 2 * u
        s1_ref[...] = scores(t + 1)
        carry = update(carry, s0_ref[...], t)
        s0_ref[...] = scores(t + 2)
        return update(carry, s1_ref[...], t + 1)

    def one_tile(t, carry):
        s1_ref[...] = scores(t + 1)
        carry = update(carry, s0_ref[...], t)
        s0_ref[...] = s1_ref[...]
        return carry

    carry = lax.fori_loop(0, n_far // 2, two_tiles, carry)
    carry = lax.fori_loop(n_far // 2 * 2, n_far, one_tile, carry)
    s1_ref[...] = scores(last)
    before = bn_ref[par, 0:SEL_TILE, :] + jnp.where(last > 0, 0.0, NEG)
    carry = update(carry, s0_ref[...] + before, jnp.maximum(last - 1, 0))
    m_s, l_s, a0, a1 = update(carry, s1_ref[...] + bn_ref[par, SEL_TILE:2 * SEL_TILE, :], last)
    inv_s = 1.0 / l_s
    o_s = (a0 * inv_s[:, :gl], a1 * inv_s[:, gl:])

    gt = jnp.transpose(gate_ref[...])
    heads = []
    for g in range(B_KV_GROUPS):
        rows = slice(g * HEAD_DIM, (g + 1) * HEAD_DIM)
        for r in range(REP):
            h = REP * g + r
            lanes = slice(g * gl + r * Q_BLOCK, g * gl + (r + 1) * Q_BLOCK)
            heads.append(gt[3 * h:3 * h + 1, :] * o_c[rows, lanes]
                         + gt[3 * h + 1:3 * h + 2, :] * o_s[g][:, r * Q_BLOCK:(r + 1) * Q_BLOCK]
                         + gt[3 * h + 2:3 * h + 3, :] * o_w[rows, lanes])
    o_ref[...] = jnp.transpose(jnp.concatenate(heads, axis=0)).astype(o_ref.dtype)


def _nsa(p16, p32, kcmp, vcmpt, kaug, vt, fc, bw, bn, ovt, bsz, seq, col_q, col_kw, col_gate, topk):
    nb = seq // Q_BLOCK
    qw = B_HEADS * HEAD_DIM
    kw = B_KV_GROUPS * HEAD_DIM
    per_batch = lambda a: pl.BlockSpec((None,) + a.shape[1:], lambda b, i: (b,) + (0,) * (a.ndim - 1))
    const = lambda a: pl.BlockSpec(a.shape, lambda b, i: (0,) * a.ndim)
    vt_seg = lambda a: pl.BlockSpec((None, None) + vt.shape[2:], lambda b, i: (a, b, 0, 0, 0))
    kern = functools.partial(_nsa_kernel, topk=topk)
    return pl.pallas_call(
        kern,
        grid=(bsz, nb),
        in_specs=[pl.BlockSpec((None, Q_BLOCK, qw), lambda b, i: (b, i, col_q // qw)),
                  per_batch(kcmp), per_batch(vcmpt), per_batch(kaug), vt_seg(VT_SL),
                  pl.BlockSpec((None, seq, kw), lambda b, i: (b, 0, col_kw // kw)),
                  vt_seg(VT_W),
                  pl.BlockSpec((None, Q_BLOCK, V7X_LANES), lambda b, i: (b, i, col_gate // V7X_LANES)),
                  const(fc), const(bw), const(bn), const(ovt)],
        out_specs=pl.BlockSpec((None, Q_BLOCK, qw), lambda b, i: (b, i, 0)),
        out_shape=jax.ShapeDtypeStruct((bsz, seq, qw), BF16),
        scratch_shapes=[pltpu.VMEM((kcmp.shape[1], B_HEADS * Q_BLOCK), F32),
                        pltpu.VMEM((SEL_TILE, B_HEADS * Q_BLOCK), F32),
                        pltpu.VMEM((SEL_TILE, B_HEADS * Q_BLOCK), F32)],
        compiler_params=_cparams(("parallel", "arbitrary")),
        name="nsa",
    )(p16, kcmp, vcmpt, kaug, vt, p16, vt, p32, fc, bw, bn, ovt)


def _merge_kernel(x_ref, g_ref, ya_ref, yb_ref, wg_ref, wua_ref, wub_ref, wo_ref, o_ref):
    x = x_ref[...]
    d = x.shape[1]
    hb = _rms(x, g_ref[...]).astype(BF16)
    ga = jax.nn.sigmoid(jnp.dot(hb, wg_ref[:, :d], preferred_element_type=F32))
    gb = jax.nn.sigmoid(jnp.dot(hb, wg_ref[:, d:], preferred_element_type=F32))
    ua = jnp.dot(ya_ref[...], wua_ref[...], preferred_element_type=F32)
    ub = jnp.dot(yb_ref[...], wub_ref[...], preferred_element_type=F32)
    merged = ga * ua + gb * ub
    o_ref[...] = x + jnp.dot(merged.astype(BF16), wo_ref[...], preferred_element_type=F32)


def _merge(x2d, g, ya, yb, wg, wua, wub, wo, tm):
    n, d = x2d.shape
    const = lambda a: pl.BlockSpec(a.shape, lambda i: (0, 0))
    row = lambda a: pl.BlockSpec((tm, a.shape[1]), lambda i: (i, 0))
    return pl.pallas_call(
        _merge_kernel,
        grid=(n // tm,),
        in_specs=[row(x2d), const(g), row(ya), row(yb), const(wg), const(wua), const(wub), const(wo)],
        out_specs=row(x2d),
        out_shape=jax.ShapeDtypeStruct((n, d), F32),
        compiler_params=_cparams(("parallel",)),
        name="merge",
    )(x2d, g, ya, yb, wg, wua, wub, wo)


FFN_HALO = 16


def _ffn_kernel(xc_ref, xp_ref, gn_ref, wi_ref, cw_ref, cb_ref, wo_ref, gf_ref, o_ref, *, chunk):
    i = pl.program_id(1)
    xc = xc_ref[...]
    tm = xc.shape[0]
    gn = gn_ref[...]
    hp = _rms(xp_ref[...], gn) * jnp.where(i > 0, 1.0, 0.0)
    h = jnp.concatenate([hp, _rms(xc, gn)], axis=0).astype(BF16)
    d_ff = wo_ref.shape[0]

    def up(c0):
        return jnp.dot(h, wi_ref[:, c0:c0 + chunk], preferred_element_type=F32)

    def conv(ext, c0):
        cw = cw_ref[:, c0:c0 + chunk]
        out = cb_ref[:, c0:c0 + chunk]
        for k in range(CONV_WIDTH):
            off = FFN_HALO - (CONV_WIDTH - 1) + k
            out = out + cw[k:k + 1, :] * ext[off:off + tm]
        return out

    acc = jnp.zeros(xc.shape, F32)
    nxt = (up(0), up(d_ff))
    for c0 in range(0, d_ff, chunk):
        ext_u, ext_g = nxt
        if c0 + chunk < d_ff:
            nxt = (up(c0 + chunk), up(d_ff + c0 + chunk))
        act = (jax.nn.silu(conv(ext_g, d_ff + c0)) * conv(ext_u, c0)).astype(BF16)
        acc = acc + jnp.dot(act, wo_ref[c0:c0 + chunk, :], preferred_element_type=F32)
    o_ref[...] = _rms(xc + acc, gf_ref[...])


def _ffn(x1, gn, wi, cw, cb, wo, gf, tm, chunk):
    bsz, seq, d = x1.shape
    const = lambda a: pl.BlockSpec(a.shape, lambda b, i: (0, 0))
    kern = functools.partial(_ffn_kernel, chunk=chunk)
    return pl.pallas_call(
        kern,
        grid=(bsz, seq // tm),
        in_specs=[pl.BlockSpec((None, tm, d), lambda b, i: (b, i, 0)),
                  pl.BlockSpec((None, FFN_HALO, d),
                               lambda b, i: (b, jnp.maximum(i * (tm // FFN_HALO) - 1, 0), 0)),
                  const(gn), const(wi), const(cw), const(cb), const(wo), const(gf)],
        out_specs=pl.BlockSpec((None, tm, d), lambda b, i: (b, i, 0)),
        out_shape=jax.ShapeDtypeStruct((bsz, seq, d), F32),
        compiler_params=_cparams(("parallel", "arbitrary")),
        name="ffn",
    )(x1, x1, gn, wi, cw, cb, wo, gf)


def _mixers(x, norm_mix, w_in, attn_sinks, cmp_pos_k, cmp_w1_k, cmp_w2_k, cmp_pos_v, cmp_w1_v,
            cmp_w2_v, table):
    bsz, seq, d = x.shape
    n = bsz * seq
    aq, akv = A_HEADS * HEAD_DIM, A_KV_HEADS * HEAD_DIM
    bq, bkv = B_HEADS * HEAD_DIM, B_KV_GROUPS * HEAD_DIM
    n_gate = 3 * B_HEADS
    assert seq % SEL_TILE == 0 and seq // SLC_BLOCK <= AUG_LANES - bkv
    splits = (aq, akv, akv, bq, bkv, bkv, bkv, bkv, bkv, bkv, n_gate, d, d)
    off = np.concatenate([[0], np.cumsum(splits)]).astype(int)
    seg = lambda k: w_in[:, off[k]:off[k + 1]]
    assert akv == bkv
    w_gate_nsa = jnp.pad(seg(10), ((0, 0), (0, V7X_LANES - n_gate)))
    w1 = jnp.concatenate([seg(0), seg(3), seg(1), seg(8), seg(6), seg(2), seg(7), seg(9),
                          seg(4), seg(5), w_gate_nsa], axis=1).astype(BF16)
    n16 = aq + bq + 2 * bkv
    col_ka, col_kw = aq + bq, aq + bq + akv
    tm = min(512, seq)
    x2d = x.reshape(n, d)
    g_mix = norm_mix.reshape(1, d)

    p16, kaug, vt, p32 = _proj(x2d, g_mix, w1, n16, aq + bq, seq, tm)
    p16 = p16.reshape(bsz, seq, n16)
    kaug = kaug.reshape(bsz, seq, AUG_LANES)
    vt = vt.reshape(3, bsz, seq // Q_BLOCK, bkv, Q_BLOCK)
    p32 = p32.reshape(bsz, seq, 2 * bkv + V7X_LANES)

    table = table * LOG2E
    table_a, table_b = table[:, :A_HEADS], table[:, A_HEADS:]
    bias_a = _keys_by_lanes(_band_bias_t(table_a, A_WINDOW + Q_BLOCK, A_WINDOW, A_WINDOW, False))
    bias_a = jnp.pad(bias_a, ((0, A_WINDOW), (0, 0)), constant_values=NEG)
    sink_row = jnp.repeat(attn_sinks * LOG2E, Q_BLOCK).reshape(1, A_HEADS * Q_BLOCK)
    y_a = _swa(p16, vt, bias_a, sink_row, bsz, seq, 0, col_ka)

    ncp = seq // CMP_STRIDE
    n_cmp = (seq - CMP_BLOCK) // CMP_STRIDE + 1
    n_slc = seq // SLC_BLOCK
    topk = min(SLC_TOPK, n_slc)

    k_cmp = _compress(p32, 0, cmp_pos_k, cmp_w1_k, cmp_w2_k, n_cmp, False)
    v_cmp_t = _compress(p32, bkv, cmp_pos_v, cmp_w1_v, cmp_w2_v, n_cmp, True)

    fc = _cmp_bias_rows(table_b)
    bw = _keys_by_lanes(_band_bias_t(table_b, NSA_WINDOW + Q_BLOCK, NSA_WINDOW, NSA_WINDOW, False))
    bw = jnp.pad(bw, ((0, NSA_WINDOW), (0, 0)), constant_values=NEG)
    bn = jnp.stack([_keys_by_lanes(_band_bias_t(table_b, 2 * SEL_TILE, SEL_TILE + par * Q_BLOCK,
                                                None, True)) for par in range(2)])
    ovt = jnp.asarray(_overlap_t(ncp, n_slc, n_cmp))
    y_b = _nsa(p16, p32, k_cmp, v_cmp_t, kaug, vt, fc, bw, bn, ovt, bsz, seq, aq, col_kw, 2 * bkv,
               topk)
    return y_a, y_b, w_in[:, off[11]:off[13]]


def _layer(x, norm_mix, w_in, attn_sinks, cmp_pos_k, cmp_w1_k, cmp_w2_k, cmp_pos_v, cmp_w1_v,
           cmp_w2_v, w_up_a, w_up_b, w_out, norm_ffn, w_ffn_in, conv_w, conv_b, w_ffn_out,
           table, norm_final):
    bsz, seq, d = x.shape
    n = bsz * seq
    y_a, y_b, w_merge_gates = _mixers(x, norm_mix, w_in, attn_sinks, cmp_pos_k, cmp_w1_k, cmp_w2_k,
                                      cmp_pos_v, cmp_w1_v, cmp_w2_v, table)

    x1 = _merge(x.reshape(n, d), norm_mix.reshape(1, d), y_a.reshape(n, -1), y_b.reshape(n, -1),
                w_merge_gates.astype(BF16), w_up_a.astype(BF16), w_up_b.astype(BF16),
                w_out.astype(BF16), min(512, n))

    return _ffn(x1.reshape(bsz, seq, d), norm_ffn.reshape(1, d), w_ffn_in.astype(BF16), conv_w,
                conv_b.reshape(1, -1), w_ffn_out.astype(BF16), norm_final.reshape(1, d),
                min(512, seq), 256)


def kernel(x, norm_mix, w_in, attn_sinks, cmp_pos_k, cmp_w1_k, cmp_w2_k, cmp_pos_v, cmp_w1_v, cmp_w2_v, w_up_a, w_up_b, w_out, norm_ffn, w_ffn_in, conv_w, conv_b, w_ffn_out, rel_bias_table, norm_final):
    assert norm_mix.shape[0] == 1, "single-layer block"
    return _layer(x, norm_mix[0], w_in[0], attn_sinks[0], cmp_pos_k[0], cmp_w1_k[0], cmp_w2_k[0],
                  cmp_pos_v[0], cmp_w1_v[0], cmp_w2_v[0], w_up_a[0], w_up_b[0], w_out[0],
                  norm_ffn[0], w_ffn_in[0], conv_w[0], conv_b[0], w_ffn_out[0], rel_bias_table,
                  norm_final)
```

```python
import functools
import math

import numpy as np
import jax
import jax.numpy as jnp
from jax import lax
from jax.experimental import pallas as pl
from jax.experimental.pallas import tpu as pltpu

F32 = jnp.float32
BF16 = jnp.bfloat16

HEAD_DIM = 64
A_HEADS = 8
A_KV_HEADS = 2
A_WINDOW = 128
B_HEADS = 8
B_KV_GROUPS = 2
REP = 4
CMP_BLOCK = 32
CMP_STRIDE = 16
SLC_BLOCK = 64
SLC_SHIFT = 6
SLC_TOPK = 16
NSA_WINDOW = 512
NUM_BUCKETS = 32
MAX_DISTANCE = 128
CONV_WIDTH = 3
Q_BLOCK = 128
EPS = 1e-6
NEG = -1e30
BIG = 1e30
SCALE = HEAD_DIM ** -0.5
LOG2E = math.log2(math.e)
Q_SCALE = SCALE * LOG2E
SUM_ROWS = 16

V7X_LANES = 128
V7X_SUBLANES = 8
V7X_VMEM_BYTES = 64 * 1024 * 1024
VMEM_LIMIT = 56 * 1024 * 1024


def _cparams(semantics):
    return pltpu.CompilerParams(dimension_semantics=semantics, vmem_limit_bytes=VMEM_LIMIT)


def _bucket_np(dist):
    dist = np.maximum(dist, 0)
    max_exact = NUM_BUCKETS // 2
    d = np.maximum(dist, 1).astype(np.float64)
    large = max_exact + (np.log(d / max_exact) / math.log(MAX_DISTANCE / max_exact)
                         * (NUM_BUCKETS - max_exact)).astype(np.int32)
    large = np.minimum(large, NUM_BUCKETS - 1)
    return np.where(dist < max_exact, dist, large).astype(np.int32)


def _table_lookup(table, dist):
    idx = _bucket_np(dist).reshape(-1)
    onehot = np.zeros((NUM_BUCKETS, idx.size), np.float32)
    onehot[idx, np.arange(idx.size)] = 1.0
    vals = jnp.dot(table.T, jnp.asarray(onehot), precision=lax.Precision.HIGHEST)
    return vals.reshape((table.shape[1],) + dist.shape)


def _band_bias_t(table, n_keys, offset, window, shift_far):
    length = n_keys + Q_BLOCK
    m = np.arange(length)
    m = np.where(m < Q_BLOCK, m, m - length)
    dist = m + offset
    ok = dist >= 0 if window is None else (dist >= 0) & (dist < window)
    u = _table_lookup(table, dist)
    if shift_far:
        u = u - table[NUM_BUCKETS - 1][:, None]
    u = jnp.where(ok[None, :], u, NEG)
    n_heads = table.shape[1]
    t = jnp.tile(u, (1, n_keys))[:, :n_keys * (length - 1)]
    return t.reshape(n_heads, n_keys, length - 1)[:, :, :Q_BLOCK]


def _keys_by_lanes(t):
    n_heads, n_keys, _ = t.shape
    return jnp.transpose(t, (1, 0, 2)).reshape(n_keys, n_heads * Q_BLOCK)


def _cmp_bias_rows(table):
    q = np.arange(Q_BLOCK)
    rho = (q - (CMP_BLOCK - 1)) % CMP_STRIDE
    dist = CMP_STRIDE * np.arange(9)[:, None] + rho[None, :]
    vals = _keys_by_lanes(_table_lookup(table, dist))
    vals = jnp.concatenate([vals[:8] - vals[8:9], vals[8:9]], axis=0)
    return jnp.pad(vals, ((0, 16 - 9), (0, 0)))


def _overlap_t(n_cmp_pad, n_slc, n_cmp):
    r = SLC_BLOCK // CMP_STRIDE
    c = CMP_BLOCK // CMP_STRIDE
    j, m, n = np.meshgrid(np.arange(n_slc), np.arange(r), np.arange(c), indexing='ij')
    i = r * j + m - n
    ok = (i >= 0) & (i < n_cmp)
    mat = np.zeros((n_slc, n_cmp_pad), np.float32)
    np.add.at(mat, (j[ok], i[ok]), 1.0)
    return mat


def _rms(x, g):
    return x * lax.rsqrt(jnp.mean(x * x, axis=-1, keepdims=True) + EPS) * g


PROJ_CHUNK = 256
VT_A, VT_SL, VT_W = 0, 1, 2
QT_A, QT_B = 0, 1
KK_A, KK_W = 0, 1


def _proj_kernel(x_ref, g_ref, w_ref, qt_ref, kk_ref, kaug_ref, vt_ref, o32_ref, *, seq):
    tm = x_ref.shape[0]
    kv = B_KV_GROUPS * HEAD_DIM
    qw = qt_ref.shape[2]
    hb = _rms(x_ref[...], g_ref[...]).astype(BF16)

    def cols(c0, width=PROJ_CHUNK):
        return jnp.dot(hb, w_ref[:, c0:c0 + width], preferred_element_type=F32)

    def put_transposed(ref, a, r, row0=0):
        for j in range(tm // Q_BLOCK):
            ref[a, j, row0:row0 + r.shape[1], :] = jnp.transpose(
                r[j * Q_BLOCK:(j + 1) * Q_BLOCK, :]).astype(BF16)

    for c0 in range(0, 2 * qw, PROJ_CHUNK):
        put_transposed(qt_ref, c0 // qw, cols(c0) * Q_SCALE, c0 % qw)
    kk_ref[...] = cols(2 * qw).astype(BF16)
    r = cols(2 * qw + 2 * kv)
    pos = (pl.program_id(0) * tm) % seq + lax.broadcasted_iota(jnp.int32, (tm, kv), 0)
    onehot = lax.broadcasted_iota(jnp.int32, (tm, kv), 1) == jnp.right_shift(pos, SLC_SHIFT)
    kaug_ref[:, :kv] = r[:, :kv].astype(BF16)
    kaug_ref[:, kv:] = jnp.where(onehot, 1.0, 0.0).astype(BF16)
    put_transposed(vt_ref, VT_A, r[:, kv:])
    r = cols(2 * qw + 4 * kv)
    put_transposed(vt_ref, VT_SL, r[:, :kv])
    put_transposed(vt_ref, VT_W, r[:, kv:])
    o32_ref[:, :2 * kv] = cols(2 * qw + 6 * kv)
    o32_ref[:, 2 * kv:] = jax.nn.sigmoid(cols(2 * qw + 8 * kv, V7X_LANES))


def _proj(x2d, g, w1, qw, seq, tm):
    n, d = x2d.shape
    kv = B_KV_GROUPS * HEAD_DIM
    n32 = 2 * kv + V7X_LANES
    nb = tm // Q_BLOCK
    assert AUG_LANES == 2 * kv == PROJ_CHUNK and seq % tm == 0 and tm % Q_BLOCK == 0
    rows = lambda width: pl.BlockSpec((tm, width), lambda i: (i, 0))
    blocks_t = lambda a, width: pl.BlockSpec((a, nb, width, Q_BLOCK), lambda i: (0, i, 0, 0))
    kern = functools.partial(_proj_kernel, seq=seq)
    return pl.pallas_call(
        kern,
        grid=(n // tm,),
        in_specs=[rows(d), pl.BlockSpec((1, d), lambda i: (0, 0)),
                  pl.BlockSpec(w1.shape, lambda i: (0, 0))],
        out_specs=[blocks_t(2, qw), rows(2 * kv), rows(AUG_LANES), blocks_t(3, kv), rows(n32)],
        out_shape=[jax.ShapeDtypeStruct((2, n // Q_BLOCK, qw, Q_BLOCK), BF16),
                   jax.ShapeDtypeStruct((n, 2 * kv), BF16),
                   jax.ShapeDtypeStruct((n, AUG_LANES), BF16),
                   jax.ShapeDtypeStruct((3, n // Q_BLOCK, kv, Q_BLOCK), BF16),
                   jax.ShapeDtypeStruct((n, n32), F32)],
        compiler_params=_cparams(("parallel",)),
        name="proj",
    )(x2d, g, w1)


CMP_PAIR = 2


def _compress_kernel(x_ref, pos_ref, w1_ref, w2_ref, o_ref, *, n_cmp, transpose_out):
    ncp = x_ref.shape[0] // CMP_STRIDE
    halves = []
    for half in range(CMP_BLOCK // CMP_STRIDE):
        acc = None
        for l0 in range(0, CMP_STRIDE, CMP_PAIR):
            lhs = jnp.concatenate(
                [(x_ref[pl.ds(l0 + k, ncp, stride=CMP_STRIDE), :]
                  + pos_ref[half * CMP_STRIDE + l0 + k:half * CMP_STRIDE + l0 + k + 1, :]).astype(BF16)
                 for k in range(CMP_PAIR)], axis=1)
            part = jnp.dot(lhs, w1_ref[(half * CMP_STRIDE + l0) // CMP_PAIR],
                           preferred_element_type=F32)
            acc = part if acc is None else acc + part
        halves.append(acc)
    top, bottom = halves
    h1 = top + jnp.concatenate([bottom[1:], bottom[:1]], axis=0)
    o = jnp.dot(jax.nn.gelu(h1).astype(BF16), w2_ref[...], preferred_element_type=F32)
    row = lax.broadcasted_iota(jnp.int32, o.shape, 0)
    o = jnp.where(row < n_cmp, o, 0.0)
    if transpose_out:
        o = jnp.transpose(o)
    o_ref[...] = o.astype(o_ref.dtype)


def _compress(p32, col, pos, w1, w2, n_cmp, transpose_out):
    bsz, seq, _ = p32.shape
    ncp = seq // CMP_STRIDE
    gd = B_KV_GROUPS * HEAD_DIM
    hid = w1.shape[-1]
    eye = jnp.eye(B_KV_GROUPS, dtype=F32)
    w1bd = jnp.einsum('gh,lde->lgdhe', eye, w1.reshape(CMP_BLOCK, HEAD_DIM, hid))
    w1bd = w1bd.reshape(CMP_BLOCK // CMP_PAIR, CMP_PAIR * gd, B_KV_GROUPS * hid).astype(BF16)
    w2bd = jnp.einsum('gh,ed->gehd', eye, w2).reshape(B_KV_GROUPS * hid, gd).astype(BF16)
    pos2 = jnp.tile(pos, (1, B_KV_GROUPS))
    oshape = (gd, ncp) if transpose_out else (ncp, gd)
    const = lambda a: pl.BlockSpec(a.shape, lambda b: (0,) * a.ndim)
    kern = functools.partial(_compress_kernel, n_cmp=n_cmp, transpose_out=transpose_out)
    return pl.pallas_call(
        kern,
        grid=(bsz,),
        in_specs=[pl.BlockSpec((None, seq, gd), lambda b: (b, 0, col // gd)),
                  const(pos2), const(w1bd), const(w2bd)],
        out_specs=pl.BlockSpec((None,) + oshape, lambda b: (b, 0, 0)),
        out_shape=jax.ShapeDtypeStruct((bsz,) + oshape, BF16),
        compiler_params=_cparams(("parallel",)),
        name="compress",
    )(p32, pos2, w1bd, w2bd)


def _block_diag_qt(qt):
    zero = jnp.zeros((HEAD_DIM, Q_BLOCK), BF16)
    n_groups = qt.shape[0] // (REP * HEAD_DIM)
    return jnp.concatenate([
        jnp.concatenate([qt[(REP * g + r) * HEAD_DIM:(REP * g + r + 1) * HEAD_DIM] if gg == g else zero
                         for gg in range(n_groups) for r in range(REP)], axis=1)
        for g in range(n_groups)], axis=0)


def _pv_and_sum(vt, p):
    rows = vt.shape[0]
    ext = jnp.concatenate([vt, jnp.ones((SUM_ROWS, vt.shape[1]), BF16)], axis=0)
    pv = jnp.dot(ext, p.astype(BF16), preferred_element_type=F32)
    return pv[:rows], pv[rows:rows + 1]


def _heads_to_rows(o_t):
    n_heads = o_t.shape[1] // Q_BLOCK
    heads = [o_t[(h // REP) * HEAD_DIM:(h // REP + 1) * HEAD_DIM, h * Q_BLOCK:(h + 1) * Q_BLOCK]
             for h in range(n_heads)]
    return jnp.transpose(jnp.concatenate(heads, axis=0))


def _banded_scores(i, wq, k_ref, bias_ref, window):
    n_blk = window // Q_BLOCK + 1
    span = n_blk * Q_BLOCK
    kb0 = jnp.maximum(i - (n_blk - 1), 0)
    shift = jnp.maximum(n_blk - 1 - i, 0)
    s = jnp.dot(k_ref[pl.ds(pl.multiple_of(kb0 * Q_BLOCK, Q_BLOCK), span), :], wq,
                preferred_element_type=F32)
    return s + bias_ref[pl.ds(pl.multiple_of(shift * Q_BLOCK, Q_BLOCK), span), :], kb0


def _swa_block(i, s, kb0, vt_ref, sink_ref):
    sink = sink_ref[...]
    m = jnp.maximum(jnp.max(s, axis=0, keepdims=True), sink)
    p = jnp.exp2(s - m)
    vt = jnp.concatenate([vt_ref[kb0 + j] for j in range(A_WINDOW // Q_BLOCK + 1)], axis=1)
    o_t, l = _pv_and_sum(vt, p)
    return _heads_to_rows(o_t * (1.0 / (l + jnp.exp2(sink - m))))


SEL_TILE = 2 * Q_BLOCK
AUG_LANES = 2 * V7X_LANES
CMP_BAND = 24


def _mixers_kernel(qa_ref, ka_ref, vat_ref, ba_ref, sink_ref,
                   q_ref, kcmp_ref, vcmpt_ref, kaug_ref, vslt_ref, kw_ref, vwt_ref, gate_ref,
                   fc_ref, bw_ref, bn_ref, ovt_ref, oa_ref, o_ref, sc_ref, s0_ref, s1_ref, *, topk):
    i = pl.program_id(1)
    n_slc, ncp = ovt_ref.shape
    gl = REP * Q_BLOCK
    nl = B_KV_GROUPS * gl
    gd = B_KV_GROUPS * HEAD_DIM
    n_win = NSA_WINDOW // Q_BLOCK + 1

    sa, kb0_a = _banded_scores(i, _block_diag_qt(qa_ref[...]), ka_ref, ba_ref, A_WINDOW)
    oa_ref[...] = _swa_block(i, sa, kb0_a, vat_ref, sink_ref).astype(oa_ref.dtype)

    wq = _block_diag_qt(q_ref[...])

    qlane = lax.broadcasted_iota(jnp.int32, (1, nl), 1) & (Q_BLOCK - 1)
    cmax = jnp.right_shift(qlane + i * Q_BLOCK - (CMP_BLOCK - 1), 4)
    valid_c = lax.broadcasted_iota(jnp.int32, (ncp, nl), 0) <= cmax
    s = jnp.dot(kcmp_ref[...], wq, preferred_element_type=F32) + fc_ref[8:9, :]
    sc_ref[...] = jnp.where(valid_c, s, NEG)
    b0 = pl.multiple_of(jnp.clip(8 * i - 16, 0, ncp - CMP_BAND), V7X_SUBLANES)
    kkb = cmax - (b0 + lax.broadcasted_iota(jnp.int32, (CMP_BAND, nl), 0))
    delta = jnp.zeros((CMP_BAND, nl), F32)
    for k in range(8):
        delta = jnp.where(kkb == k, fc_ref[k:k + 1, :], delta)
    sc_ref[pl.ds(b0, CMP_BAND), :] = sc_ref[pl.ds(b0, CMP_BAND), :] + delta
    s = sc_ref[...]
    m = jnp.max(s, axis=0, keepdims=True)
    p = jnp.exp2(s - m)
    l = jnp.sum(p, axis=0, keepdims=True)
    p = p * jnp.where(cmax >= 0, 1.0 / l, 0.0)
    o_c = jnp.dot(vcmpt_ref[...], p.astype(BF16), preferred_element_type=F32)
    psum = jnp.concatenate(
        [sum(p[:, g * gl + r * Q_BLOCK:g * gl + (r + 1) * Q_BLOCK] for r in range(REP))
         for g in range(B_KV_GROUPS)], axis=1)
    imp = jnp.dot(ovt_ref[...], psum, preferred_element_type=F32,
                  precision=lax.Precision.HIGHEST)

    sw, kb0 = _banded_scores(i, wq, kw_ref, bw_ref, NSA_WINDOW)
    mw = jnp.max(sw, axis=0, keepdims=True)
    pw = jnp.exp2(sw - mw)
    vwt = jnp.concatenate([vwt_ref[kb0 + j] for j in range(n_win)], axis=1)
    o_w, l_w = _pv_and_sum(vwt, pw)
    o_w = o_w * (1.0 / l_w)

    sl = B_KV_GROUPS * Q_BLOCK
    jt = lax.broadcasted_iota(jnp.int32, (n_slc, sl), 0)
    second_half = (lax.broadcasted_iota(jnp.int32, (1, sl), 1) & (Q_BLOCK - 1)) >= SLC_BLOCK
    qblk = 2 * i + second_half.astype(jnp.int32)
    forced = (jt == 0) | (jt == qblk) | (jt == qblk - 1)
    score = jnp.where(forced, BIG, jnp.where(jt > qblk, NEG, imp))
    rank = jnp.zeros((n_slc, sl), jnp.int32)
    for c in range(n_slc):
        row = score[c:c + 1, :]
        ahead = (row > score) | ((row == score) & (jt > c))
        rank = rank + ahead.astype(jnp.int32)
    selneg = jnp.where(rank < topk, 0.0, NEG).astype(BF16)
    wsel = jnp.concatenate([selneg[:, g * Q_BLOCK:(g + 1) * Q_BLOCK]
                            for g in range(B_KV_GROUPS) for _ in range(REP)], axis=1)
    w = jnp.concatenate([wq, wsel, jnp.zeros((AUG_LANES - gd - n_slc, nl), BF16)], axis=0)

    def scores(t):
        r0 = pl.multiple_of(t * SEL_TILE, SEL_TILE)
        return jnp.dot(kaug_ref[pl.ds(r0, SEL_TILE), :], w, preferred_element_type=F32)

    def update(carry, st, t):
        m_i, l_i, a0, a1 = carry
        m_n = jnp.maximum(m_i, jnp.max(st, axis=0, keepdims=True))
        alpha = jnp.exp2(m_i - m_n)
        pt = jnp.exp2(st - m_n)
        blk0 = t * (SEL_TILE // Q_BLOCK)
        vt = jnp.concatenate([vslt_ref[blk0 + j] for j in range(SEL_TILE // Q_BLOCK)], axis=1)
        pv, l_t = _pv_and_sum(vt, pt)
        a0 = alpha[:, :gl] * a0 + pv[:HEAD_DIM, :gl]
        a1 = alpha[:, gl:] * a1 + pv[HEAD_DIM:, gl:]
        return m_n, alpha * l_i + l_t, a0, a1

    last = i // 2
    par = i % 2
    n_far = jnp.maximum(last - 1, 0)
    carry = (jnp.full((1, nl), NEG, F32), jnp.zeros((1, nl), F32),
             jnp.zeros((HEAD_DIM, gl), F32), jnp.zeros((HEAD_DIM, gl), F32))
    s0_ref[...] = scores(0)

    def two_tiles(u, carry):
        t = 2 * u
        s1_ref[...] = scores(t + 1)
        carry = update(carry, s0_ref[...], t)
        s0_ref[...] = scores(t + 2)
        return update(carry, s1_ref[...], t + 1)

    def one_tile(t, carry):
        s1_ref[...] = scores(t + 1)
        carry = update(carry, s0_ref[...], t)
        s0_ref[...] = s1_ref[...]
        return carry

    carry = lax.fori_loop(0, n_far // 2, two_tiles, carry)
    carry = lax.fori_loop(n_far // 2 * 2, n_far, one_tile, carry)
    s1_ref[...] = scores(last)
    before = bn_ref[par, 0:SEL_TILE, :] + jnp.where(last > 0, 0.0, NEG)
    carry = update(carry, s0_ref[...] + before, jnp.maximum(last - 1, 0))
    m_s, l_s, a0, a1 = update(carry, s1_ref[...] + bn_ref[par, SEL_TILE:2 * SEL_TILE, :], last)
    inv_s = 1.0 / l_s
    o_s = (a0 * inv_s[:, :gl], a1 * inv_s[:, gl:])

    gt = jnp.transpose(gate_ref[...])
    heads = []
    for g in range(B_KV_GROUPS):
        rows = slice(g * HEAD_DIM, (g + 1) * HEAD_DIM)
        for r in range(REP):
            h = REP * g + r
            lanes = slice(g * gl + r * Q_BLOCK, g * gl + (r + 1) * Q_BLOCK)
            heads.append(gt[3 * h:3 * h + 1, :] * o_c[rows, lanes]
                         + gt[3 * h + 1:3 * h + 2, :] * o_s[g][:, r * Q_BLOCK:(r + 1) * Q_BLOCK]
                         + gt[3 * h + 2:3 * h + 3, :] * o_w[rows, lanes])
    o_ref[...] = jnp.transpose(jnp.concatenate(heads, axis=0)).astype(o_ref.dtype)


def _mixers_call(qt, kk, vt, p32, kcmp, vcmpt, kaug, ba, sink_row, fc, bw, bn, ovt, bsz, seq,
                 col_gate, topk):
    nb = seq // Q_BLOCK
    assert A_HEADS == B_HEADS and A_KV_HEADS == B_KV_GROUPS
    qw = B_HEADS * HEAD_DIM
    kw = B_KV_GROUPS * HEAD_DIM
    per_batch = lambda a: pl.BlockSpec((None,) + a.shape[1:], lambda b, i: (b,) + (0,) * (a.ndim - 1))
    const = lambda a: pl.BlockSpec(a.shape, lambda b, i: (0,) * a.ndim)
    qt_seg = lambda a: pl.BlockSpec((None, None, None, qw, Q_BLOCK), lambda b, i: (a, b, i, 0, 0))
    kk_seg = lambda a: pl.BlockSpec((None, seq, kw), lambda b, i: (b, 0, a))
    vt_seg = lambda a: pl.BlockSpec((None, None) + vt.shape[2:], lambda b, i: (a, b, 0, 0, 0))
    y_spec = pl.BlockSpec((None, Q_BLOCK, qw), lambda b, i: (b, i, 0))
    kern = functools.partial(_mixers_kernel, topk=topk)
    return pl.pallas_call(
        kern,
        grid=(bsz, nb),
        in_specs=[qt_seg(QT_A), kk_seg(KK_A), vt_seg(VT_A), const(ba), const(sink_row),
                  qt_seg(QT_B), per_batch(kcmp), per_batch(vcmpt), per_batch(kaug), vt_seg(VT_SL),
                  kk_seg(KK_W), vt_seg(VT_W),
                  pl.BlockSpec((None, Q_BLOCK, V7X_LANES), lambda b, i: (b, i, col_gate // V7X_LANES)),
                  const(fc), const(bw), const(bn), const(ovt)],
        out_specs=[y_spec, y_spec],
        out_shape=[jax.ShapeDtypeStruct((bsz, seq, qw), BF16)] * 2,
        scratch_shapes=[pltpu.VMEM((kcmp.shape[1], B_HEADS * Q_BLOCK), F32),
                        pltpu.VMEM((SEL_TILE, B_HEADS * Q_BLOCK), F32),
                        pltpu.VMEM((SEL_TILE, B_HEADS * Q_BLOCK), F32)],
        compiler_params=_cparams(("parallel", "arbitrary")),
        name="mixers",
    )(qt, kk, vt, ba, sink_row, qt, kcmp, vcmpt, kaug, vt, kk, vt, p32, fc, bw, bn, ovt)


def _merge_kernel(x_ref, g_ref, ya_ref, yb_ref, wg_ref, wua_ref, wub_ref, wo_ref, o_ref):
    x = x_ref[...]
    d = x.shape[1]
    hb = _rms(x, g_ref[...]).astype(BF16)
    ga = jax.nn.sigmoid(jnp.dot(hb, wg_ref[:, :d], preferred_element_type=F32))
    gb = jax.nn.sigmoid(jnp.dot(hb, wg_ref[:, d:], preferred_element_type=F32))
    ua = jnp.dot(ya_ref[...], wua_ref[...], preferred_element_type=F32)
    ub = jnp.dot(yb_ref[...], wub_ref[...], preferred_element_type=F32)
    merged = ga * ua + gb * ub
    o_ref[...] = x + jnp.dot(merged.astype(BF16), wo_ref[...], preferred_element_type=F32)


def _merge(x2d, g, ya, yb, wg, wua, wub, wo, tm):
    n, d = x2d.shape
    const = lambda a: pl.BlockSpec(a.shape, lambda i: (0, 0))
    row = lambda a: pl.BlockSpec((tm, a.shape[1]), lambda i: (i, 0))
    return pl.pallas_call(
        _merge_kernel,
        grid=(n // tm,),
        in_specs=[row(x2d), const(g), row(ya), row(yb), const(wg), const(wua), const(wub), const(wo)],
        out_specs=row(x2d),
        out_shape=jax.ShapeDtypeStruct((n, d), F32),
        compiler_params=_cparams(("parallel",)),
        name="merge",
    )(x2d, g, ya, yb, wg, wua, wub, wo)


FFN_HALO = 16


def _ffn_kernel(xc_ref, xp_ref, gn_ref, wi_ref, cw_ref, cb_ref, wo_ref, gf_ref, o_ref, *, chunk):
    i = pl.program_id(1)
    xc = xc_ref[...]
    tm = xc.shape[0]
    gn = gn_ref[...]
    hp = _rms(xp_ref[...], gn) * jnp.where(i > 0, 1.0, 0.0)
    h = jnp.concatenate([hp, _rms(xc, gn)], axis=0).astype(BF16)
    d_ff = wo_ref.shape[0]

    def up(c0):
        return jnp.dot(h, wi_ref[:, c0:c0 + chunk], preferred_element_type=F32)

    def conv(ext, c0):
        cw = cw_ref[:, c0:c0 + chunk]
        out = cb_ref[:, c0:c0 + chunk]
        for k in range(CONV_WIDTH):
            off = FFN_HALO - (CONV_WIDTH - 1) + k
            out = out + cw[k:k + 1, :] * ext[off:off + tm]
        return out

    acc = jnp.zeros(xc.shape, F32)
    nxt = (up(0), up(d_ff))
    for c0 in range(0, d_ff, chunk):
        ext_u, ext_g = nxt
        if c0 + chunk < d_ff:
            nxt = (up(c0 + chunk), up(d_ff + c0 + chunk))
        act = (jax.nn.silu(conv(ext_g, d_ff + c0)) * conv(ext_u, c0)).astype(BF16)
        acc = acc + jnp.dot(act, wo_ref[c0:c0 + chunk, :], preferred_element_type=F32)
    o_ref[...] = _rms(xc + acc, gf_ref[...])


def _ffn(x1, gn, wi, cw, cb, wo, gf, tm, chunk):
    bsz, seq, d = x1.shape
    const = lambda a: pl.BlockSpec(a.shape, lambda b, i: (0, 0))
    kern = functools.partial(_ffn_kernel, chunk=chunk)
    return pl.pallas_call(
        kern,
        grid=(bsz, seq // tm),
        in_specs=[pl.BlockSpec((None, tm, d), lambda b, i: (b, i, 0)),
                  pl.BlockSpec((None, FFN_HALO, d),
                               lambda b, i: (b, jnp.maximum(i * (tm // FFN_HALO) - 1, 0), 0)),
                  const(gn), const(wi), const(cw), const(cb), const(wo), const(gf)],
        out_specs=pl.BlockSpec((None, tm, d), lambda b, i: (b, i, 0)),
        out_shape=jax.ShapeDtypeStruct((bsz, seq, d), F32),
        compiler_params=_cparams(("parallel", "arbitrary")),
        name="ffn",
    )(x1, x1, gn, wi, cw, cb, wo, gf)


def _mixers(x, norm_mix, w_in, attn_sinks, cmp_pos_k, cmp_w1_k, cmp_w2_k, cmp_pos_v, cmp_w1_v,
            cmp_w2_v, table):
    bsz, seq, d = x.shape
    n = bsz * seq
    aq, akv = A_HEADS * HEAD_DIM, A_KV_HEADS * HEAD_DIM
    bq, bkv = B_HEADS * HEAD_DIM, B_KV_GROUPS * HEAD_DIM
    n_gate = 3 * B_HEADS
    assert seq % SEL_TILE == 0 and seq // SLC_BLOCK <= AUG_LANES - bkv
    splits = (aq, akv, akv, bq, bkv, bkv, bkv, bkv, bkv, bkv, n_gate, d, d)
    off = np.concatenate([[0], np.cumsum(splits)]).astype(int)
    seg = lambda k: w_in[:, off[k]:off[k + 1]]
    assert akv == bkv and aq == bq
    w_gate_nsa = jnp.pad(seg(10), ((0, 0), (0, V7X_LANES - n_gate)))
    w1 = jnp.concatenate([seg(0), seg(3), seg(1), seg(8), seg(6), seg(2), seg(7), seg(9),
                          seg(4), seg(5), w_gate_nsa], axis=1).astype(BF16)
    tm = min(512, seq)
    g_mix = norm_mix.reshape(1, d)

    qt, kk, kaug, vt, p32 = _proj(x.reshape(n, d), g_mix, w1, aq, seq, tm)
    nb = seq // Q_BLOCK
    qt = qt.reshape(2, bsz, nb, aq, Q_BLOCK)
    kk = kk.reshape(bsz, seq, 2 * bkv)
    kaug = kaug.reshape(bsz, seq, AUG_LANES)
    vt = vt.reshape(3, bsz, nb, bkv, Q_BLOCK)
    p32 = p32.reshape(bsz, seq, 2 * bkv + V7X_LANES)

    table = table * LOG2E
    table_a, table_b = table[:, :A_HEADS], table[:, A_HEADS:]
    bias_a = _keys_by_lanes(_band_bias_t(table_a, A_WINDOW + Q_BLOCK, A_WINDOW, A_WINDOW, False))
    bias_a = jnp.pad(bias_a, ((0, A_WINDOW), (0, 0)), constant_values=NEG)
    sink_row = jnp.repeat(attn_sinks * LOG2E, Q_BLOCK).reshape(1, A_HEADS * Q_BLOCK)

    ncp = seq // CMP_STRIDE
    n_cmp = (seq - CMP_BLOCK) // CMP_STRIDE + 1
    n_slc = seq // SLC_BLOCK
    topk = min(SLC_TOPK, n_slc)

    k_cmp = _compress(p32, 0, cmp_pos_k, cmp_w1_k, cmp_w2_k, n_cmp, False)
    v_cmp_t = _compress(p32, bkv, cmp_pos_v, cmp_w1_v, cmp_w2_v, n_cmp, True)

    fc = _cmp_bias_rows(table_b)
    bw = _keys_by_lanes(_band_bias_t(table_b, NSA_WINDOW + Q_BLOCK, NSA_WINDOW, NSA_WINDOW, False))
    bw = jnp.pad(bw, ((0, NSA_WINDOW), (0, 0)), constant_values=NEG)
    bn = jnp.stack([_keys_by_lanes(_band_bias_t(table_b, 2 * SEL_TILE, SEL_TILE + par * Q_BLOCK,
                                                None, True)) for par in range(2)])
    ovt = jnp.asarray(_overlap_t(ncp, n_slc, n_cmp))
    y_a, y_b = _mixers_call(qt, kk, vt, p32, k_cmp, v_cmp_t, kaug, bias_a, sink_row, fc, bw, bn, ovt,
                            bsz, seq, 2 * bkv, topk)
    return y_a, y_b, w_in[:, off[11]:off[13]]


def _layer(x, norm_mix, w_in, attn_sinks, cmp_pos_k, cmp_w1_k, cmp_w2_k, cmp_pos_v, cmp_w1_v,
           cmp_w2_v, w_up_a, w_up_b, w_out, norm_ffn, w_ffn_in, conv_w, conv_b, w_ffn_out,
           table, norm_final):
    bsz, seq, d = x.shape
    n = bsz * seq
    y_a, y_b, w_merge_gates = _mixers(x, norm_mix, w_in, attn_sinks, cmp_pos_k, cmp_w1_k, cmp_w2_k,
                                      cmp_pos_v, cmp_w1_v, cmp_w2_v, table)

    x1 = _merge(x.reshape(n, d), norm_mix.reshape(1, d), y_a.reshape(n, -1), y_b.reshape(n, -1),
                w_merge_gates.astype(BF16), w_up_a.astype(BF16), w_up_b.astype(BF16),
                w_out.astype(BF16), min(512, n))

    return _ffn(x1.reshape(bsz, seq, d), norm_ffn.reshape(1, d), w_ffn_in.astype(BF16), conv_w,
                conv_b.reshape(1, -1), w_ffn_out.astype(BF16), norm_final.reshape(1, d),
                min(512, seq), 256)


def kernel(x, norm_mix, w_in, attn_sinks, cmp_pos_k, cmp_w1_k, cmp_w2_k, cmp_pos_v, cmp_w1_v, cmp_w2_v, w_up_a, w_up_b, w_out, norm_ffn, w_ffn_in, conv_w, conv_b, w_ffn_out, rel_bias_table, norm_final):
    assert norm_mix.shape[0] == 1, "single-layer block"
    return _layer(x, norm_mix[0], w_in[0], attn_sinks[0], cmp_pos_k[0], cmp_w1_k[0], cmp_w2_k[0],
                  cmp_pos_v[0], cmp_w1_v[0], cmp_w2_v[0], w_up_a[0], w_up_b[0], w_out[0],
                  norm_ffn[0], w_ffn_in[0], conv_w[0], conv_b[0], w_ffn_out[0], rel_bias_table,
                  norm_final)
```

```python
import functools
import math

import numpy as np
import jax
import jax.numpy as jnp
from jax import lax
from jax.experimental import pallas as pl
from jax.experimental.pallas import tpu as pltpu

F32 = jnp.float32
BF16 = jnp.bfloat16

HEAD_DIM = 64
A_HEADS = 8
A_KV_HEADS = 2
A_WINDOW = 128
B_HEADS = 8
B_KV_GROUPS = 2
REP = 4
CMP_BLOCK = 32
CMP_STRIDE = 16
SLC_BLOCK = 64
SLC_SHIFT = 6
SLC_TOPK = 16
NSA_WINDOW = 512
NUM_BUCKETS = 32
MAX_DISTANCE = 128
CONV_WIDTH = 3
Q_BLOCK = 128
EPS = 1e-6
NEG = -1e30
BIG = 1e30
SCALE = HEAD_DIM ** -0.5
LOG2E = math.log2(math.e)
Q_SCALE = SCALE * LOG2E
SUM_ROWS = 16

V7X_LANES = 128
V7X_SUBLANES = 8
V7X_VMEM_BYTES = 64 * 1024 * 1024
VMEM_LIMIT = 56 * 1024 * 1024


def _cparams(semantics):
    return pltpu.CompilerParams(dimension_semantics=semantics, vmem_limit_bytes=VMEM_LIMIT)


def _bucket_np(dist):
    dist = np.maximum(dist, 0)
    max_exact = NUM_BUCKETS // 2
    d = np.maximum(dist, 1).astype(np.float64)
    large = max_exact + (np.log(d / max_exact) / math.log(MAX_DISTANCE / max_exact)
                         * (NUM_BUCKETS - max_exact)).astype(np.int32)
    large = np.minimum(large, NUM_BUCKETS - 1)
    return np.where(dist < max_exact, dist, large).astype(np.int32)


def _table_lookup(table, dist):
    idx = _bucket_np(dist).reshape(-1)
    onehot = np.zeros((NUM_BUCKETS, idx.size), np.float32)
    onehot[idx, np.arange(idx.size)] = 1.0
    vals = jnp.dot(table.T, jnp.asarray(onehot), precision=lax.Precision.HIGHEST)
    return vals.reshape((table.shape[1],) + dist.shape)


def _band_vector(table, n_keys, offset, window, shift_far):
    length = n_keys + Q_BLOCK
    m = np.arange(length)
    m = np.where(m < Q_BLOCK, m, m - length)
    dist = m + offset
    ok = dist >= 0 if window is None else (dist >= 0) & (dist < window)
    u = _table_lookup(table, dist)
    if shift_far:
        u = u - table[NUM_BUCKETS - 1][:, None]
    return jnp.where(ok[None, :], u, NEG)


def _toeplitz_kernel(*refs, dest):
    n_tab = len(dest)
    covered = [0] * (len(refs) - n_tab)
    for u_ref, (j, row0) in zip(refs[:n_tab], dest):
        o_ref = refs[n_tab + j]
        n_heads, length = u_ref.shape
        n_keys = length - Q_BLOCK
        covered[j] = max(covered[j], row0 + n_keys)
        u2 = jnp.concatenate([u_ref[...], u_ref[...]], axis=1)
        for r0 in range(0, n_keys, Q_BLOCK):
            w0 = (-r0 - (Q_BLOCK - 1)) % length
            win = u2[:, w0:w0 + 2 * Q_BLOCK]
            for h in range(n_heads):
                x = jnp.broadcast_to(win[h:h + 1, :], (Q_BLOCK, 2 * Q_BLOCK))
                x = pltpu.roll(x, Q_BLOCK + 1, 1, stride=1, stride_axis=0)
                o_ref[row0 + r0:row0 + r0 + Q_BLOCK, h * Q_BLOCK:(h + 1) * Q_BLOCK] = x[:, :Q_BLOCK]
    for o_ref, done in zip(refs[n_tab:], covered):
        if o_ref.shape[0] > done:
            o_ref[done:, :] = jnp.full((o_ref.shape[0] - done, o_ref.shape[1]), NEG, F32)


def _toeplitz_tables(us, dest, n_rows):
    n_heads = us[0].shape[0]
    kern = functools.partial(_toeplitz_kernel, dest=tuple(dest))
    return pl.pallas_call(
        kern,
        out_shape=[jax.ShapeDtypeStruct((r, n_heads * Q_BLOCK), F32) for r in n_rows],
        compiler_params=pltpu.CompilerParams(vmem_limit_bytes=VMEM_LIMIT),
        name="bias_tables",
    )(*us)


def _cmp_bias_rows(table):
    q = np.arange(Q_BLOCK)
    rho = (q - (CMP_BLOCK - 1)) % CMP_STRIDE
    dist = CMP_STRIDE * np.arange(9)[:, None] + rho[None, :]
    vals = _table_lookup(table, dist)
    vals = jnp.transpose(vals, (1, 0, 2)).reshape(9, table.shape[1] * Q_BLOCK)
    vals = jnp.concatenate([vals[:8] - vals[8:9], vals[8:9]], axis=0)
    return jnp.pad(vals, ((0, 16 - 9), (0, 0)))


def _overlap_t(n_cmp_pad, n_slc, n_cmp):
    r = SLC_BLOCK // CMP_STRIDE
    c = CMP_BLOCK // CMP_STRIDE
    j, m, n = np.meshgrid(np.arange(n_slc), np.arange(r), np.arange(c), indexing='ij')
    i = r * j + m - n
    ok = (i >= 0) & (i < n_cmp)
    mat = np.zeros((n_slc, n_cmp_pad), np.float32)
    np.add.at(mat, (j[ok], i[ok]), 1.0)
    return mat


def _rms(x, g):
    return x * lax.rsqrt(jnp.mean(x * x, axis=-1, keepdims=True) + EPS) * g


PROJ_CHUNK = 256
VT_A, VT_SL, VT_W = 0, 1, 2
QT_A, QT_B = 0, 1
KK_A, KK_W = 0, 1


def _proj_kernel(x_ref, g_ref, w_ref, qt_ref, kk_ref, kaug_ref, vt_ref, o32_ref, *, seq):
    tm = x_ref.shape[0]
    kv = B_KV_GROUPS * HEAD_DIM
    qw = qt_ref.shape[2]
    hb = _rms(x_ref[...], g_ref[...]).astype(BF16)

    def cols(c0, width=PROJ_CHUNK):
        return jnp.dot(hb, w_ref[:, c0:c0 + width], preferred_element_type=F32)

    def put_transposed(ref, a, r, row0=0):
        for j in range(tm // Q_BLOCK):
            ref[a, j, row0:row0 + r.shape[1], :] = jnp.transpose(
                r[j * Q_BLOCK:(j + 1) * Q_BLOCK, :]).astype(BF16)

    for c0 in range(0, 2 * qw, PROJ_CHUNK):
        put_transposed(qt_ref, c0 // qw, cols(c0) * Q_SCALE, c0 % qw)
    kk_ref[...] = cols(2 * qw).astype(BF16)
    r = cols(2 * qw + 2 * kv)
    pos = (pl.program_id(0) * tm) % seq + lax.broadcasted_iota(jnp.int32, (tm, kv), 0)
    onehot = lax.broadcasted_iota(jnp.int32, (tm, kv), 1) == jnp.right_shift(pos, SLC_SHIFT)
    kaug_ref[:, :kv] = r[:, :kv].astype(BF16)
    kaug_ref[:, kv:] = jnp.where(onehot, 1.0, 0.0).astype(BF16)
    put_transposed(vt_ref, VT_A, r[:, kv:])
    r = cols(2 * qw + 4 * kv)
    put_transposed(vt_ref, VT_SL, r[:, :kv])
    put_transposed(vt_ref, VT_W, r[:, kv:])
    o32_ref[:, :2 * kv] = cols(2 * qw + 6 * kv)
    o32_ref[:, 2 * kv:] = jax.nn.sigmoid(cols(2 * qw + 8 * kv, V7X_LANES))


def _proj(x2d, g, w1, qw, seq, tm):
    n, d = x2d.shape
    kv = B_KV_GROUPS * HEAD_DIM
    n32 = 2 * kv + V7X_LANES
    nb = tm // Q_BLOCK
    assert AUG_LANES == 2 * kv == PROJ_CHUNK and seq % tm == 0 and tm % Q_BLOCK == 0
    rows = lambda width: pl.BlockSpec((tm, width), lambda i: (i, 0))
    blocks_t = lambda a, width: pl.BlockSpec((a, nb, width, Q_BLOCK), lambda i: (0, i, 0, 0))
    kern = functools.partial(_proj_kernel, seq=seq)
    return pl.pallas_call(
        kern,
        grid=(n // tm,),
        in_specs=[rows(d), pl.BlockSpec((1, d), lambda i: (0, 0)),
                  pl.BlockSpec(w1.shape, lambda i: (0, 0))],
        out_specs=[blocks_t(2, qw), rows(2 * kv), rows(AUG_LANES), blocks_t(3, kv), rows(n32)],
        out_shape=[jax.ShapeDtypeStruct((2, n // Q_BLOCK, qw, Q_BLOCK), BF16),
                   jax.ShapeDtypeStruct((n, 2 * kv), BF16),
                   jax.ShapeDtypeStruct((n, AUG_LANES), BF16),
                   jax.ShapeDtypeStruct((3, n // Q_BLOCK, kv, Q_BLOCK), BF16),
                   jax.ShapeDtypeStruct((n, n32), F32)],
        compiler_params=_cparams(("parallel",)),
        name="proj",
    )(x2d, g, w1)


CMP_PAIR = 2


def _compress_kernel(x_ref, pos_ref, w1_ref, w2_ref, o_ref, *, n_cmp, transpose_out):
    ncp = x_ref.shape[0] // CMP_STRIDE

    def group_diag(w):
        zero = jnp.zeros(w.shape, w.dtype)
        return jnp.concatenate(
            [jnp.concatenate([w if g == gg else zero for gg in range(B_KV_GROUPS)], axis=1)
             for g in range(B_KV_GROUPS)], axis=0)

    halves = []
    for half in range(CMP_BLOCK // CMP_STRIDE):
        acc = None
        for l0 in range(half * CMP_STRIDE, (half + 1) * CMP_STRIDE, CMP_PAIR):
            lhs = jnp.concatenate(
                [(x_ref[pl.ds(l % CMP_STRIDE, ncp, stride=CMP_STRIDE), :]
                  + jnp.concatenate([pos_ref[l:l + 1, :]] * B_KV_GROUPS, axis=1)).astype(BF16)
                 for l in range(l0, l0 + CMP_PAIR)], axis=1)
            rhs = jnp.concatenate([group_diag(w1_ref[l]) for l in range(l0, l0 + CMP_PAIR)], axis=0)
            part = jnp.dot(lhs, rhs, preferred_element_type=F32)
            acc = part if acc is None else acc + part
        halves.append(acc)
    top, bottom = halves
    h1 = top + jnp.concatenate([bottom[1:], bottom[:1]], axis=0)
    o = jnp.dot(jax.nn.gelu(h1).astype(BF16), group_diag(w2_ref[...]), preferred_element_type=F32)
    row = lax.broadcasted_iota(jnp.int32, o.shape, 0)
    o = jnp.where(row < n_cmp, o, 0.0)
    if transpose_out:
        o = jnp.transpose(o)
    o_ref[...] = o.astype(o_ref.dtype)


def _compress(p32, col, pos, w1, w2, n_cmp, transpose_out):
    bsz, seq, _ = p32.shape
    ncp = seq // CMP_STRIDE
    gd = B_KV_GROUPS * HEAD_DIM
    w1 = w1.reshape(CMP_BLOCK, HEAD_DIM, -1).astype(BF16)
    w2 = w2.astype(BF16)
    oshape = (gd, ncp) if transpose_out else (ncp, gd)
    const = lambda a: pl.BlockSpec(a.shape, lambda b: (0,) * a.ndim)
    kern = functools.partial(_compress_kernel, n_cmp=n_cmp, transpose_out=transpose_out)
    return pl.pallas_call(
        kern,
        grid=(bsz,),
        in_specs=[pl.BlockSpec((None, seq, gd), lambda b: (b, 0, col // gd)),
                  const(pos), const(w1), const(w2)],
        out_specs=pl.BlockSpec((None,) + oshape, lambda b: (b, 0, 0)),
        out_shape=jax.ShapeDtypeStruct((bsz,) + oshape, BF16),
        compiler_params=_cparams(("parallel",)),
        name="compress",
    )(p32, pos, w1, w2)


def _block_diag_qt(qt):
    zero = jnp.zeros((HEAD_DIM, Q_BLOCK), BF16)
    n_groups = qt.shape[0] // (REP * HEAD_DIM)
    return jnp.concatenate([
        jnp.concatenate([qt[(REP * g + r) * HEAD_DIM:(REP * g + r + 1) * HEAD_DIM] if gg == g else zero
                         for gg in range(n_groups) for r in range(REP)], axis=1)
        for g in range(n_groups)], axis=0)


def _pv_and_sum(vt, p):
    rows = vt.shape[0]
    ext = jnp.concatenate([vt, jnp.ones((SUM_ROWS, vt.shape[1]), BF16)], axis=0)
    pv = jnp.dot(ext, p.astype(BF16), preferred_element_type=F32)
    return pv[:rows], pv[rows:rows + 1]


def _heads_to_rows(o_t):
    n_heads = o_t.shape[1] // Q_BLOCK
    heads = [o_t[(h // REP) * HEAD_DIM:(h // REP + 1) * HEAD_DIM, h * Q_BLOCK:(h + 1) * Q_BLOCK]
             for h in range(n_heads)]
    return jnp.transpose(jnp.concatenate(heads, axis=0))


def _banded_scores(i, wq, k_ref, bias_ref, window):
    n_blk = window // Q_BLOCK + 1
    span = n_blk * Q_BLOCK
    kb0 = jnp.maximum(i - (n_blk - 1), 0)
    shift = jnp.maximum(n_blk - 1 - i, 0)
    s = jnp.dot(k_ref[pl.ds(pl.multiple_of(kb0 * Q_BLOCK, Q_BLOCK), span), :], wq,
                preferred_element_type=F32)
    return s + bias_ref[pl.ds(pl.multiple_of(shift * Q_BLOCK, Q_BLOCK), span), :], kb0


def _swa_block(i, s, kb0, vt_ref, sink_ref):
    sink = sink_ref[...]
    m = jnp.maximum(jnp.max(s, axis=0, keepdims=True), sink)
    p = jnp.exp2(s - m)
    vt = jnp.concatenate([vt_ref[kb0 + j] for j in range(A_WINDOW // Q_BLOCK + 1)], axis=1)
    o_t, l = _pv_and_sum(vt, p)
    return _heads_to_rows(o_t * (1.0 / (l + jnp.exp2(sink - m))))


SEL_TILE = 2 * Q_BLOCK
AUG_LANES = 2 * V7X_LANES
CMP_BAND = 24


def _mixers_kernel(qa_ref, ka_ref, vat_ref, ba_ref, sink_ref,
                   q_ref, kcmp_ref, vcmpt_ref, kaug_ref, vslt_ref, kw_ref, vwt_ref, gate_ref,
                   fc_ref, bw_ref, bn_ref, ovt_ref, oa_ref, o_ref, sc_ref, s0_ref, s1_ref, *, topk):
    i = pl.program_id(1)
    n_slc, ncp = ovt_ref.shape
    gl = REP * Q_BLOCK
    nl = B_KV_GROUPS * gl
    gd = B_KV_GROUPS * HEAD_DIM
    n_win = NSA_WINDOW // Q_BLOCK + 1

    sa, kb0_a = _banded_scores(i, _block_diag_qt(qa_ref[...]), ka_ref, ba_ref, A_WINDOW)
    oa_ref[...] = _swa_block(i, sa, kb0_a, vat_ref, sink_ref).astype(oa_ref.dtype)

    wq = _block_diag_qt(q_ref[...])

    qlane = lax.broadcasted_iota(jnp.int32, (1, nl), 1) & (Q_BLOCK - 1)
    cmax = jnp.right_shift(qlane + i * Q_BLOCK - (CMP_BLOCK - 1), 4)
    valid_c = lax.broadcasted_iota(jnp.int32, (ncp, nl), 0) <= cmax
    s = jnp.dot(kcmp_ref[...], wq, preferred_element_type=F32) + fc_ref[8:9, :]
    sc_ref[...] = jnp.where(valid_c, s, NEG)
    b0 = pl.multiple_of(jnp.clip(8 * i - 16, 0, ncp - CMP_BAND), V7X_SUBLANES)
    kkb = cmax - (b0 + lax.broadcasted_iota(jnp.int32, (CMP_BAND, nl), 0))
    delta = jnp.zeros((CMP_BAND, nl), F32)
    for k in range(8):
        delta = jnp.where(kkb == k, fc_ref[k:k + 1, :], delta)
    sc_ref[pl.ds(b0, CMP_BAND), :] = sc_ref[pl.ds(b0, CMP_BAND), :] + delta
    s = sc_ref[...]
    m = jnp.max(s, axis=0, keepdims=True)
    p = jnp.exp2(s - m)
    l = jnp.sum(p, axis=0, keepdims=True)
    p = p * jnp.where(cmax >= 0, 1.0 / l, 0.0)
    o_c = jnp.dot(vcmpt_ref[...], p.astype(BF16), preferred_element_type=F32)
    psum = jnp.concatenate(
        [sum(p[:, g * gl + r * Q_BLOCK:g * gl + (r + 1) * Q_BLOCK] for r in range(REP))
         for g in range(B_KV_GROUPS)], axis=1)
    imp = jnp.dot(ovt_ref[...], psum, preferred_element_type=F32,
                  precision=lax.Precision.HIGHEST)

    sw, kb0 = _banded_scores(i, wq, kw_ref, bw_ref, NSA_WINDOW)
    mw = jnp.max(sw, axis=0, keepdims=True)
    pw = jnp.exp2(sw - mw)
    vwt = jnp.concatenate([vwt_ref[kb0 + j] for j in range(n_win)], axis=1)
    o_w, l_w = _pv_and_sum(vwt, pw)
    o_w = o_w * (1.0 / l_w)

    sl = B_KV_GROUPS * Q_BLOCK
    jt = lax.broadcasted_iota(jnp.int32, (n_slc, sl), 0)
    second_half = (lax.broadcasted_iota(jnp.int32, (1, sl), 1) & (Q_BLOCK - 1)) >= SLC_BLOCK
    qblk = 2 * i + second_half.astype(jnp.int32)
    forced = (jt == 0) | (jt == qblk) | (jt == qblk - 1)
    score = jnp.where(forced, BIG, jnp.where(jt > qblk, NEG, imp))
    rank = jnp.zeros((n_slc, sl), jnp.int32)
    for c in range(n_slc):
        row = score[c:c + 1, :]
        ahead = (row > score) | ((row == score) & (jt > c))
        rank = rank + ahead.astype(jnp.int32)
    selneg = jnp.where(rank < topk, 0.0, NEG).astype(BF16)
    wsel = jnp.concatenate([selneg[:, g * Q_BLOCK:(g + 1) * Q_BLOCK]
                            for g in range(B_KV_GROUPS) for _ in range(REP)], axis=1)
    w = jnp.concatenate([wq, wsel, jnp.zeros((AUG_LANES - gd - n_slc, nl), BF16)], axis=0)

    def scores(t):
        r0 = pl.multiple_of(t * SEL_TILE, SEL_TILE)
        return jnp.dot(kaug_ref[pl.ds(r0, SEL_TILE), :], w, preferred_element_type=F32)

    def update(carry, st, t):
        m_i, l_i, a0, a1 = carry
        m_n = jnp.maximum(m_i, jnp.max(st, axis=0, keepdims=True))
        alpha = jnp.exp2(m_i - m_n)
        pt = jnp.exp2(st - m_n)
        blk0 = t * (SEL_TILE // Q_BLOCK)
        vt = jnp.concatenate([vslt_ref[blk0 + j] for j in range(SEL_TILE // Q_BLOCK)], axis=1)
        pv, l_t = _pv_and_sum(vt, pt)
        a0 = alpha[:, :gl] * a0 + pv[:HEAD_DIM, :gl]
        a1 = alpha[:, gl:] * a1 + pv[HEAD_DIM:, gl:]
        return m_n, alpha * l_i + l_t, a0, a1

    last = i // 2
    par = i % 2
    n_far = jnp.maximum(last - 1, 0)
    carry = (jnp.full((1, nl), NEG, F32), jnp.zeros((1, nl), F32),
             jnp.zeros((HEAD_DIM, gl), F32), jnp.zeros((HEAD_DIM, gl), F32))
    s0_ref[...] = scores(0)

    def two_tiles(u, carry):
        t = 2 * u
        s1_ref[...] = scores(t + 1)
        carry = update(carry, s0_ref[...], t)
        s0_ref[...] = scores(t + 2)
        return update(carry, s1_ref[...], t + 1)

    def one_tile(t, carry):
        s1_ref[...] = scores(t + 1)
        carry = update(carry, s0_ref[...], t)
        s0_ref[...] = s1_ref[...]
        return carry

    carry = lax.fori_loop(0, n_far // 2, two_tiles, carry)
    carry = lax.fori_loop(n_far // 2 * 2, n_far, one_tile, carry)
    s1_ref[...] = scores(last)
    near0 = pl.multiple_of(par * (2 * SEL_TILE), SEL_TILE)
    before = bn_ref[pl.ds(near0, SEL_TILE), :] + jnp.where(last > 0, 0.0, NEG)
    carry = update(carry, s0_ref[...] + before, jnp.maximum(last - 1, 0))
    m_s, l_s, a0, a1 = update(carry, s1_ref[...] + bn_ref[pl.ds(near0 + SEL_TILE, SEL_TILE), :], last)
    inv_s = 1.0 / l_s
    o_s = (a0 * inv_s[:, :gl], a1 * inv_s[:, gl:])

    gt = jnp.transpose(gate_ref[...])
    heads = []
    for g in range(B_KV_GROUPS):
        rows = slice(g * HEAD_DIM, (g + 1) * HEAD_DIM)
        for r in range(REP):
            h = REP * g + r
            lanes = slice(g * gl + r * Q_BLOCK, g * gl + (r + 1) * Q_BLOCK)
            heads.append(gt[3 * h:3 * h + 1, :] * o_c[rows, lanes]
                         + gt[3 * h + 1:3 * h + 2, :] * o_s[g][:, r * Q_BLOCK:(r + 1) * Q_BLOCK]
                         + gt[3 * h + 2:3 * h + 3, :] * o_w[rows, lanes])
    o_ref[...] = jnp.transpose(jnp.concatenate(heads, axis=0)).astype(o_ref.dtype)


def _mixers_call(qt, kk, vt, p32, kcmp, vcmpt, kaug, ba, sink_row, fc, bw, bn, ovt, bsz, seq,
                 col_gate, topk):
    nb = seq // Q_BLOCK
    assert A_HEADS == B_HEADS and A_KV_HEADS == B_KV_GROUPS
    qw = B_HEADS * HEAD_DIM
    kw = B_KV_GROUPS * HEAD_DIM
    per_batch = lambda a: pl.BlockSpec((None,) + a.shape[1:], lambda b, i: (b,) + (0,) * (a.ndim - 1))
    const = lambda a: pl.BlockSpec(a.shape, lambda b, i: (0,) * a.ndim)
    qt_seg = lambda a: pl.BlockSpec((None, None, None, qw, Q_BLOCK), lambda b, i: (a, b, i, 0, 0))
    kk_seg = lambda a: pl.BlockSpec((None, seq, kw), lambda b, i: (b, 0, a))
    vt_seg = lambda a: pl.BlockSpec((None, None) + vt.shape[2:], lambda b, i: (a, b, 0, 0, 0))
    y_spec = pl.BlockSpec((None, Q_BLOCK, qw), lambda b, i: (b, i, 0))
    kern = functools.partial(_mixers_kernel, topk=topk)
    return pl.pallas_call(
        kern,
        grid=(bsz, nb),
        in_specs=[qt_seg(QT_A), kk_seg(KK_A), vt_seg(VT_A), const(ba), const(sink_row),
                  qt_seg(QT_B), per_batch(kcmp), per_batch(vcmpt), per_batch(kaug), vt_seg(VT_SL),
                  kk_seg(KK_W), vt_seg(VT_W),
                  pl.BlockSpec((None, Q_BLOCK, V7X_LANES), lambda b, i: (b, i, col_gate // V7X_LANES)),
                  const(fc), const(bw), const(bn), const(ovt)],
        out_specs=[y_spec, y_spec],
        out_shape=[jax.ShapeDtypeStruct((bsz, seq, qw), BF16)] * 2,
        scratch_shapes=[pltpu.VMEM((kcmp.shape[1], B_HEADS * Q_BLOCK), F32),
                        pltpu.VMEM((SEL_TILE, B_HEADS * Q_BLOCK), F32),
                        pltpu.VMEM((SEL_TILE, B_HEADS * Q_BLOCK), F32)],
        compiler_params=_cparams(("parallel", "arbitrary")),
        name="mixers",
    )(qt, kk, vt, ba, sink_row, qt, kcmp, vcmpt, kaug, vt, kk, vt, p32, fc, bw, bn, ovt)


def _merge_kernel(x_ref, g_ref, ya_ref, yb_ref, wg_ref, wua_ref, wub_ref, wo_ref, o_ref):
    x = x_ref[...]
    d = x.shape[1]
    hb = _rms(x, g_ref[...]).astype(BF16)
    ga = jax.nn.sigmoid(jnp.dot(hb, wg_ref[:, :d], preferred_element_type=F32))
    gb = jax.nn.sigmoid(jnp.dot(hb, wg_ref[:, d:], preferred_element_type=F32))
    ua = jnp.dot(ya_ref[...], wua_ref[...], preferred_element_type=F32)
    ub = jnp.dot(yb_ref[...], wub_ref[...], preferred_element_type=F32)
    merged = ga * ua + gb * ub
    o_ref[...] = x + jnp.dot(merged.astype(BF16), wo_ref[...], preferred_element_type=F32)


def _merge(x2d, g, ya, yb, wg, wua, wub, wo, tm):
    n, d = x2d.shape
    const = lambda a: pl.BlockSpec(a.shape, lambda i: (0, 0))
    row = lambda a: pl.BlockSpec((tm, a.shape[1]), lambda i: (i, 0))
    return pl.pallas_call(
        _merge_kernel,
        grid=(n // tm,),
        in_specs=[row(x2d), const(g), row(ya), row(yb), const(wg), const(wua), const(wub), const(wo)],
        out_specs=row(x2d),
        out_shape=jax.ShapeDtypeStruct((n, d), F32),
        compiler_params=_cparams(("parallel",)),
        name="merge",
    )(x2d, g, ya, yb, wg, wua, wub, wo)


FFN_HALO = 16


def _ffn_kernel(xc_ref, xp_ref, gn_ref, wi_ref, cw_ref, cb_ref, wo_ref, gf_ref, o_ref, *, chunk):
    i = pl.program_id(1)
    xc = xc_ref[...]
    tm = xc.shape[0]
    gn = gn_ref[...]
    hp = _rms(xp_ref[...], gn) * jnp.where(i > 0, 1.0, 0.0)
    h = jnp.concatenate([hp, _rms(xc, gn)], axis=0).astype(BF16)
    d_ff = wo_ref.shape[0]

    def up(c0):
        return jnp.dot(h, wi_ref[:, c0:c0 + chunk], preferred_element_type=F32)

    def conv(ext, c0):
        cw = cw_ref[:, c0:c0 + chunk]
        out = cb_ref[:, c0:c0 + chunk]
        for k in range(CONV_WIDTH):
            off = FFN_HALO - (CONV_WIDTH - 1) + k
            out = out + cw[k:k + 1, :] * ext[off:off + tm]
        return out

    acc = jnp.zeros(xc.shape, F32)
    nxt = (up(0), up(d_ff))
    for c0 in range(0, d_ff, chunk):
        ext_u, ext_g = nxt
        if c0 + chunk < d_ff:
            nxt = (up(c0 + chunk), up(d_ff + c0 + chunk))
        act = (jax.nn.silu(conv(ext_g, d_ff + c0)) * conv(ext_u, c0)).astype(BF16)
        acc = acc + jnp.dot(act, wo_ref[c0:c0 + chunk, :], preferred_element_type=F32)
    o_ref[...] = _rms(xc + acc, gf_ref[...])


def _ffn(x1, gn, wi, cw, cb, wo, gf, tm, chunk):
    bsz, seq, d = x1.shape
    const = lambda a: pl.BlockSpec(a.shape, lambda b, i: (0, 0))
    kern = functools.partial(_ffn_kernel, chunk=chunk)
    return pl.pallas_call(
        kern,
        grid=(bsz, seq // tm),
        in_specs=[pl.BlockSpec((None, tm, d), lambda b, i: (b, i, 0)),
                  pl.BlockSpec((None, FFN_HALO, d),
                               lambda b, i: (b, jnp.maximum(i * (tm // FFN_HALO) - 1, 0), 0)),
                  const(gn), const(wi), const(cw), const(cb), const(wo), const(gf)],
        out_specs=pl.BlockSpec((None, tm, d), lambda b, i: (b, i, 0)),
        out_shape=jax.ShapeDtypeStruct((bsz, seq, d), F32),
        compiler_params=_cparams(("parallel", "arbitrary")),
        name="ffn",
    )(x1, x1, gn, wi, cw, cb, wo, gf)


def _mixers(x, norm_mix, w_in, attn_sinks, cmp_pos_k, cmp_w1_k, cmp_w2_k, cmp_pos_v, cmp_w1_v,
            cmp_w2_v, table):
    bsz, seq, d = x.shape
    n = bsz * seq
    aq, akv = A_HEADS * HEAD_DIM, A_KV_HEADS * HEAD_DIM
    bq, bkv = B_HEADS * HEAD_DIM, B_KV_GROUPS * HEAD_DIM
    n_gate = 3 * B_HEADS
    assert seq % SEL_TILE == 0 and seq // SLC_BLOCK <= AUG_LANES - bkv
    splits = (aq, akv, akv, bq, bkv, bkv, bkv, bkv, bkv, bkv, n_gate, d, d)
    off = np.concatenate([[0], np.cumsum(splits)]).astype(int)
    seg = lambda k: w_in[:, off[k]:off[k + 1]]
    assert akv == bkv and aq == bq
    w_gate_nsa = jnp.pad(seg(10), ((0, 0), (0, V7X_LANES - n_gate)))
    w1 = jnp.concatenate([seg(0), seg(3), seg(1), seg(8), seg(6), seg(2), seg(7), seg(9),
                          seg(4), seg(5), w_gate_nsa], axis=1).astype(BF16)
    tm = min(512, seq)
    g_mix = norm_mix.reshape(1, d)

    qt, kk, kaug, vt, p32 = _proj(x.reshape(n, d), g_mix, w1, aq, seq, tm)
    nb = seq // Q_BLOCK
    qt = qt.reshape(2, bsz, nb, aq, Q_BLOCK)
    kk = kk.reshape(bsz, seq, 2 * bkv)
    kaug = kaug.reshape(bsz, seq, AUG_LANES)
    vt = vt.reshape(3, bsz, nb, bkv, Q_BLOCK)
    p32 = p32.reshape(bsz, seq, 2 * bkv + V7X_LANES)

    table = table * LOG2E
    table_a, table_b = table[:, :A_HEADS], table[:, A_HEADS:]
    sink_row = jnp.repeat(attn_sinks * LOG2E, Q_BLOCK).reshape(1, A_HEADS * Q_BLOCK)

    ncp = seq // CMP_STRIDE
    n_cmp = (seq - CMP_BLOCK) // CMP_STRIDE + 1
    n_slc = seq // SLC_BLOCK
    topk = min(SLC_TOPK, n_slc)

    k_cmp = _compress(p32, 0, cmp_pos_k, cmp_w1_k, cmp_w2_k, n_cmp, False)
    v_cmp_t = _compress(p32, bkv, cmp_pos_v, cmp_w1_v, cmp_w2_v, n_cmp, True)

    fc = _cmp_bias_rows(table_b)
    win_a, win_b = A_WINDOW + Q_BLOCK, NSA_WINDOW + Q_BLOCK
    bias_a, bw, bn = _toeplitz_tables(
        [_band_vector(table_a, win_a, A_WINDOW, A_WINDOW, False),
         _band_vector(table_b, win_b, NSA_WINDOW, NSA_WINDOW, False),
         _band_vector(table_b, 2 * SEL_TILE, SEL_TILE, None, True),
         _band_vector(table_b, 2 * SEL_TILE, SEL_TILE + Q_BLOCK, None, True)],
        [(0, 0), (1, 0), (2, 0), (2, 2 * SEL_TILE)],
        [win_a + A_WINDOW, win_b + NSA_WINDOW, 4 * SEL_TILE])
    ovt = jnp.asarray(_overlap_t(ncp, n_slc, n_cmp))
    y_a, y_b = _mixers_call(qt, kk, vt, p32, k_cmp, v_cmp_t, kaug, bias_a, sink_row, fc, bw, bn, ovt,
                            bsz, seq, 2 * bkv, topk)
    return y_a, y_b, w_in[:, off[11]:off[13]]


def _layer(x, norm_mix, w_in, attn_sinks, cmp_pos_k, cmp_w1_k, cmp_w2_k, cmp_pos_v, cmp_w1_v,
           cmp_w2_v, w_up_a, w_up_b, w_out, norm_ffn, w_ffn_in, conv_w, conv_b, w_ffn_out,
           table, norm_final):
    bsz, seq, d = x.shape
    n = bsz * seq
    y_a, y_b, w_merge_gates = _mixers(x, norm_mix, w_in, attn_sinks, cmp_pos_k, cmp_w1_k, cmp_w2_k,
                                      cmp_pos_v, cmp_w1_v, cmp_w2_v, table)

    x1 = _merge(x.reshape(n, d), norm_mix.reshape(1, d), y_a.reshape(n, -1), y_b.reshape(n, -1),
                w_merge_gates.astype(BF16), w_up_a.astype(BF16), w_up_b.astype(BF16),
                w_out.astype(BF16), min(512, n))

    return _ffn(x1.reshape(bsz, seq, d), norm_ffn.reshape(1, d), w_ffn_in.astype(BF16), conv_w,
                conv_b.reshape(1, -1), w_ffn_out.astype(BF16), norm_final.reshape(1, d),
                min(512, seq), 256)


def kernel(x, norm_mix, w_in, attn_sinks, cmp_pos_k, cmp_w1_k, cmp_w2_k, cmp_pos_v, cmp_w1_v, cmp_w2_v, w_up_a, w_up_b, w_out, norm_ffn, w_ffn_in, conv_w, conv_b, w_ffn_out, rel_bias_table, norm_final):
    assert norm_mix.shape[0] == 1, "single-layer block"
    return _layer(x, norm_mix[0], w_in[0], attn_sinks[0], cmp_pos_k[0], cmp_w1_k[0], cmp_w2_k[0],
                  cmp_pos_v[0], cmp_w1_v[0], cmp_w2_v[0], w_up_a[0], w_up_b[0], w_out[0],
                  norm_ffn[0], w_ffn_in[0], conv_w[0], conv_b[0], w_ffn_out[0], rel_bias_table,
                  norm_final)
```

```python
import functools
import math

import numpy as np
import jax
import jax.numpy as jnp
from jax import lax
from jax.experimental import pallas as pl
from jax.experimental.pallas import tpu as pltpu

F32 = jnp.float32
BF16 = jnp.bfloat16

HEAD_DIM = 64
A_HEADS = 8
A_KV_HEADS = 2
A_WINDOW = 128
B_HEADS = 8
B_KV_GROUPS = 2
REP = 4
CMP_BLOCK = 32
CMP_STRIDE = 16
SLC_BLOCK = 64
SLC_SHIFT = 6
SLC_TOPK = 16
NSA_WINDOW = 512
NUM_BUCKETS = 32
MAX_DISTANCE = 128
CONV_WIDTH = 3
Q_BLOCK = 128
EPS = 1e-6
NEG = -1e30
BIG = 1e30
SCALE = HEAD_DIM ** -0.5
LOG2E = math.log2(math.e)
Q_SCALE = SCALE * LOG2E
SUM_ROWS = 16

V7X_LANES = 128
V7X_SUBLANES = 8
V7X_VMEM_BYTES = 64 * 1024 * 1024
VMEM_LIMIT = 56 * 1024 * 1024


def _cparams(semantics):
    return pltpu.CompilerParams(dimension_semantics=semantics, vmem_limit_bytes=VMEM_LIMIT)


def _bucket_np(dist):
    dist = np.maximum(dist, 0)
    max_exact = NUM_BUCKETS // 2
    d = np.maximum(dist, 1).astype(np.float64)
    large = max_exact + (np.log(d / max_exact) / math.log(MAX_DISTANCE / max_exact)
                         * (NUM_BUCKETS - max_exact)).astype(np.int32)
    large = np.minimum(large, NUM_BUCKETS - 1)
    return np.where(dist < max_exact, dist, large).astype(np.int32)


def _table_lookup(table, dist):
    idx = _bucket_np(dist).reshape(-1)
    onehot = np.zeros((NUM_BUCKETS, idx.size), np.float32)
    onehot[idx, np.arange(idx.size)] = 1.0
    vals = jnp.dot(table.T, jnp.asarray(onehot), precision=lax.Precision.HIGHEST)
    return vals.reshape((table.shape[1],) + dist.shape)


def _band_vector(table, n_keys, offset, window, shift_far):
    length = n_keys + Q_BLOCK
    m = np.arange(length)
    m = np.where(m < Q_BLOCK, m, m - length)
    dist = m + offset
    ok = dist >= 0 if window is None else (dist >= 0) & (dist < window)
    u = _table_lookup(table, dist)
    if shift_far:
        u = u - table[NUM_BUCKETS - 1][:, None]
    return jnp.where(ok[None, :], u, NEG)


def _toeplitz_kernel(*refs, dest):
    n_tab = len(dest)
    covered = [0] * (len(refs) - n_tab)
    for u_ref, (j, row0) in zip(refs[:n_tab], dest):
        o_ref = refs[n_tab + j]
        n_heads, length = u_ref.shape
        n_keys = length - Q_BLOCK
        covered[j] = max(covered[j], row0 + n_keys)
        u2 = jnp.concatenate([u_ref[...], u_ref[...]], axis=1)
        for r0 in range(0, n_keys, Q_BLOCK):
            w0 = (-r0 - (Q_BLOCK - 1)) % length
            win = u2[:, w0:w0 + 2 * Q_BLOCK]
            for h in range(n_heads):
                x = jnp.broadcast_to(win[h:h + 1, :], (Q_BLOCK, 2 * Q_BLOCK))
                x = pltpu.roll(x, Q_BLOCK + 1, 1, stride=1, stride_axis=0)
                o_ref[row0 + r0:row0 + r0 + Q_BLOCK, h * Q_BLOCK:(h + 1) * Q_BLOCK] = x[:, :Q_BLOCK]
    for o_ref, done in zip(refs[n_tab:], covered):
        if o_ref.shape[0] > done:
            o_ref[done:, :] = jnp.full((o_ref.shape[0] - done, o_ref.shape[1]), NEG, F32)


def _toeplitz_tables(us, dest, n_rows):
    n_heads = us[0].shape[0]
    kern = functools.partial(_toeplitz_kernel, dest=tuple(dest))
    return pl.pallas_call(
        kern,
        out_shape=[jax.ShapeDtypeStruct((r, n_heads * Q_BLOCK), F32) for r in n_rows],
        compiler_params=pltpu.CompilerParams(vmem_limit_bytes=VMEM_LIMIT),
        name="bias_tables",
    )(*us)


def _cmp_bias_rows(table):
    q = np.arange(Q_BLOCK)
    rho = (q - (CMP_BLOCK - 1)) % CMP_STRIDE
    dist = CMP_STRIDE * np.arange(9)[:, None] + rho[None, :]
    vals = _table_lookup(table, dist)
    vals = jnp.transpose(vals, (1, 0, 2)).reshape(9, table.shape[1] * Q_BLOCK)
    vals = jnp.concatenate([vals[:8] - vals[8:9], vals[8:9]], axis=0)
    return jnp.pad(vals, ((0, 16 - 9), (0, 0)))


def _overlap_t(n_cmp_pad, n_slc, n_cmp):
    r = SLC_BLOCK // CMP_STRIDE
    c = CMP_BLOCK // CMP_STRIDE
    j, m, n = np.meshgrid(np.arange(n_slc), np.arange(r), np.arange(c), indexing='ij')
    i = r * j + m - n
    ok = (i >= 0) & (i < n_cmp)
    mat = np.zeros((n_slc, n_cmp_pad), np.float32)
    np.add.at(mat, (j[ok], i[ok]), 1.0)
    return mat


def _rms(x, g):
    return x * lax.rsqrt(jnp.mean(x * x, axis=-1, keepdims=True) + EPS) * g


PROJ_CHUNK = 256
VT_A, VT_SL, VT_W = 0, 1, 2
QT_A, QT_B = 0, 1
KK_A, KK_W = 0, 1


def _proj_kernel(x_ref, g_ref, w_ref, qt_ref, kk_ref, kaug_ref, vt_ref, o32_ref, *, seq):
    tm = x_ref.shape[0]
    kv = B_KV_GROUPS * HEAD_DIM
    qw = qt_ref.shape[2]
    hb = _rms(x_ref[...], g_ref[...]).astype(BF16)

    def cols(c0, width=PROJ_CHUNK):
        return jnp.dot(hb, w_ref[:, c0:c0 + width], preferred_element_type=F32)

    def put_transposed(ref, a, r, row0=0):
        for j in range(tm // Q_BLOCK):
            ref[a, j, row0:row0 + r.shape[1], :] = jnp.transpose(
                r[j * Q_BLOCK:(j + 1) * Q_BLOCK, :]).astype(BF16)

    for c0 in range(0, 2 * qw, PROJ_CHUNK):
        put_transposed(qt_ref, c0 // qw, cols(c0) * Q_SCALE, c0 % qw)
    kk_ref[...] = cols(2 * qw).astype(BF16)
    r = cols(2 * qw + 2 * kv)
    pos = (pl.program_id(0) * tm) % seq + lax.broadcasted_iota(jnp.int32, (tm, kv), 0)
    onehot = lax.broadcasted_iota(jnp.int32, (tm, kv), 1) == jnp.right_shift(pos, SLC_SHIFT)
    kaug_ref[:, :kv] = r[:, :kv].astype(BF16)
    kaug_ref[:, kv:] = jnp.where(onehot, 1.0, 0.0).astype(BF16)
    put_transposed(vt_ref, VT_A, r[:, kv:])
    r = cols(2 * qw + 4 * kv)
    put_transposed(vt_ref, VT_SL, r[:, :kv])
    put_transposed(vt_ref, VT_W, r[:, kv:])
    o32_ref[:, :2 * kv] = cols(2 * qw + 6 * kv)
    o32_ref[:, 2 * kv:] = jax.nn.sigmoid(cols(2 * qw + 8 * kv, V7X_LANES))


def _proj(x2d, g, w1, qw, seq, tm):
    n, d = x2d.shape
    kv = B_KV_GROUPS * HEAD_DIM
    n32 = 2 * kv + V7X_LANES
    nb = tm // Q_BLOCK
    assert AUG_LANES == 2 * kv == PROJ_CHUNK and seq % tm == 0 and tm % Q_BLOCK == 0
    rows = lambda width: pl.BlockSpec((tm, width), lambda i: (i, 0))
    blocks_t = lambda a, width: pl.BlockSpec((a, nb, width, Q_BLOCK), lambda i: (0, i, 0, 0))
    kern = functools.partial(_proj_kernel, seq=seq)
    return pl.pallas_call(
        kern,
        grid=(n // tm,),
        in_specs=[rows(d), pl.BlockSpec((1, d), lambda i: (0, 0)),
                  pl.BlockSpec(w1.shape, lambda i: (0, 0))],
        out_specs=[blocks_t(2, qw), rows(2 * kv), rows(AUG_LANES), blocks_t(3, kv), rows(n32)],
        out_shape=[jax.ShapeDtypeStruct((2, n // Q_BLOCK, qw, Q_BLOCK), BF16),
                   jax.ShapeDtypeStruct((n, 2 * kv), BF16),
                   jax.ShapeDtypeStruct((n, AUG_LANES), BF16),
                   jax.ShapeDtypeStruct((3, n // Q_BLOCK, kv, Q_BLOCK), BF16),
                   jax.ShapeDtypeStruct((n, n32), F32)],
        compiler_params=_cparams(("parallel",)),
        name="proj",
    )(x2d, g, w1)


CMP_PAIR = 2


def _compress_kernel(x_ref, pos_ref, w1_ref, w2_ref, o_ref, *, n_cmp, transpose_out):
    ncp = x_ref.shape[0] // CMP_STRIDE

    def group_diag(w):
        zero = jnp.zeros(w.shape, w.dtype)
        return jnp.concatenate(
            [jnp.concatenate([w if g == gg else zero for gg in range(B_KV_GROUPS)], axis=1)
             for g in range(B_KV_GROUPS)], axis=0)

    halves = []
    for half in range(CMP_BLOCK // CMP_STRIDE):
        acc = None
        for l0 in range(half * CMP_STRIDE, (half + 1) * CMP_STRIDE, CMP_PAIR):
            lhs = jnp.concatenate(
                [(x_ref[pl.ds(l % CMP_STRIDE, ncp, stride=CMP_STRIDE), :]
                  + jnp.concatenate([pos_ref[l:l + 1, :]] * B_KV_GROUPS, axis=1)).astype(BF16)
                 for l in range(l0, l0 + CMP_PAIR)], axis=1)
            rhs = jnp.concatenate([group_diag(w1_ref[l]) for l in range(l0, l0 + CMP_PAIR)], axis=0)
            part = jnp.dot(lhs, rhs, preferred_element_type=F32)
            acc = part if acc is None else acc + part
        halves.append(acc)
    top, bottom = halves
    h1 = top + jnp.concatenate([bottom[1:], bottom[:1]], axis=0)
    o = jnp.dot(jax.nn.gelu(h1).astype(BF16), group_diag(w2_ref[...]), preferred_element_type=F32)
    row = lax.broadcasted_iota(jnp.int32, o.shape, 0)
    o = jnp.where(row < n_cmp, o, 0.0)
    if transpose_out:
        o = jnp.transpose(o)
    o_ref[...] = o.astype(o_ref.dtype)


def _compress(p32, col, pos, w1, w2, n_cmp, transpose_out):
    bsz, seq, _ = p32.shape
    ncp = seq // CMP_STRIDE
    gd = B_KV_GROUPS * HEAD_DIM
    w1 = w1.reshape(CMP_BLOCK, HEAD_DIM, -1).astype(BF16)
    w2 = w2.astype(BF16)
    oshape = (gd, ncp) if transpose_out else (ncp, gd)
    const = lambda a: pl.BlockSpec(a.shape, lambda b: (0,) * a.ndim)
    kern = functools.partial(_compress_kernel, n_cmp=n_cmp, transpose_out=transpose_out)
    return pl.pallas_call(
        kern,
        grid=(bsz,),
        in_specs=[pl.BlockSpec((None, seq, gd), lambda b: (b, 0, col // gd)),
                  const(pos), const(w1), const(w2)],
        out_specs=pl.BlockSpec((None,) + oshape, lambda b: (b, 0, 0)),
        out_shape=jax.ShapeDtypeStruct((bsz,) + oshape, BF16),
        compiler_params=_cparams(("parallel",)),
        name="compress",
    )(p32, pos, w1, w2)


def _block_diag_qt(qt):
    zero = jnp.zeros((HEAD_DIM, Q_BLOCK), BF16)
    n_groups = qt.shape[0] // (REP * HEAD_DIM)
    return jnp.concatenate([
        jnp.concatenate([qt[(REP * g + r) * HEAD_DIM:(REP * g + r + 1) * HEAD_DIM] if gg == g else zero
                         for gg in range(n_groups) for r in range(REP)], axis=1)
        for g in range(n_groups)], axis=0)


def _pv_by_group(vt, p, with_sum):
    n_groups = vt.shape[0] // HEAD_DIM
    gl = p.shape[1] // n_groups
    pb = p.astype(BF16)
    outs, sums = [], []
    for g in range(n_groups):
        lhs = vt[g * HEAD_DIM:(g + 1) * HEAD_DIM]
        if with_sum:
            lhs = jnp.concatenate([lhs, jnp.ones((SUM_ROWS, vt.shape[1]), BF16)], axis=0)
        r = jnp.dot(lhs, pb[:, g * gl:(g + 1) * gl], preferred_element_type=F32)
        outs.append(r[:HEAD_DIM])
        sums.append(r[HEAD_DIM:HEAD_DIM + 1])
    return outs, (jnp.concatenate(sums, axis=1) if with_sum else None)


def _scale_groups(outs, row):
    gl = outs[0].shape[1]
    return [o * row[:, g * gl:(g + 1) * gl] for g, o in enumerate(outs)]


def _head_block(outs, h):
    r = h % REP
    return outs[h // REP][:, r * Q_BLOCK:(r + 1) * Q_BLOCK]


def _banded_scores(i, wq, k_ref, bias_ref, window):
    n_blk = window // Q_BLOCK + 1
    span = n_blk * Q_BLOCK
    kb0 = jnp.maximum(i - (n_blk - 1), 0)
    shift = jnp.maximum(n_blk - 1 - i, 0)
    s = jnp.dot(k_ref[pl.ds(pl.multiple_of(kb0 * Q_BLOCK, Q_BLOCK), span), :], wq,
                preferred_element_type=F32)
    return s + bias_ref[pl.ds(pl.multiple_of(shift * Q_BLOCK, Q_BLOCK), span), :], kb0


def _swa_block(i, s, kb0, vt_ref, sink_ref):
    sink = sink_ref[...]
    m = jnp.maximum(jnp.max(s, axis=0, keepdims=True), sink)
    p = jnp.exp2(s - m)
    vt = jnp.concatenate([vt_ref[kb0 + j] for j in range(A_WINDOW // Q_BLOCK + 1)], axis=1)
    outs, l = _pv_by_group(vt, p, True)
    outs = _scale_groups(outs, 1.0 / (l + jnp.exp2(sink - m)))
    heads = [_head_block(outs, h) for h in range(A_HEADS)]
    return jnp.transpose(jnp.concatenate(heads, axis=0))


SEL_TILE = 2 * Q_BLOCK
AUG_LANES = 2 * V7X_LANES
CMP_BAND = 24


def _mixers_kernel(qa_ref, ka_ref, vat_ref, ba_ref, sink_ref,
                   q_ref, kcmp_ref, vcmpt_ref, kaug_ref, vslt_ref, kw_ref, vwt_ref, gate_ref,
                   fc_ref, bw_ref, bn_ref, ovt_ref, oa_ref, o_ref, sc_ref, s0_ref, s1_ref, *, topk):
    i = pl.program_id(1)
    n_slc, ncp = ovt_ref.shape
    gl = REP * Q_BLOCK
    nl = B_KV_GROUPS * gl
    gd = B_KV_GROUPS * HEAD_DIM
    n_win = NSA_WINDOW // Q_BLOCK + 1

    sa, kb0_a = _banded_scores(i, _block_diag_qt(qa_ref[...]), ka_ref, ba_ref, A_WINDOW)
    oa_ref[...] = _swa_block(i, sa, kb0_a, vat_ref, sink_ref).astype(oa_ref.dtype)

    wq = _block_diag_qt(q_ref[...])

    qlane = lax.broadcasted_iota(jnp.int32, (1, nl), 1) & (Q_BLOCK - 1)
    cmax = jnp.right_shift(qlane + i * Q_BLOCK - (CMP_BLOCK - 1), 4)
    valid_c = lax.broadcasted_iota(jnp.int32, (ncp, nl), 0) <= cmax
    s = jnp.dot(kcmp_ref[...], wq, preferred_element_type=F32) + fc_ref[8:9, :]
    sc_ref[...] = jnp.where(valid_c, s, NEG)
    b0 = pl.multiple_of(jnp.clip(8 * i - 16, 0, ncp - CMP_BAND), V7X_SUBLANES)
    kkb = cmax - (b0 + lax.broadcasted_iota(jnp.int32, (CMP_BAND, nl), 0))
    delta = jnp.zeros((CMP_BAND, nl), F32)
    for k in range(8):
        delta = jnp.where(kkb == k, fc_ref[k:k + 1, :], delta)
    sc_ref[pl.ds(b0, CMP_BAND), :] = sc_ref[pl.ds(b0, CMP_BAND), :] + delta
    s = sc_ref[...]
    m = jnp.max(s, axis=0, keepdims=True)
    p = jnp.exp2(s - m)
    l = jnp.sum(p, axis=0, keepdims=True)
    p = p * jnp.where(cmax >= 0, 1.0 / l, 0.0)
    o_c, _ = _pv_by_group(vcmpt_ref[...], p, False)
    psum = jnp.concatenate(
        [sum(p[:, g * gl + r * Q_BLOCK:g * gl + (r + 1) * Q_BLOCK] for r in range(REP))
         for g in range(B_KV_GROUPS)], axis=1)
    imp = jnp.dot(ovt_ref[...], psum, preferred_element_type=F32,
                  precision=lax.Precision.HIGHEST)

    sw, kb0 = _banded_scores(i, wq, kw_ref, bw_ref, NSA_WINDOW)
    mw = jnp.max(sw, axis=0, keepdims=True)
    pw = jnp.exp2(sw - mw)
    vwt = jnp.concatenate([vwt_ref[kb0 + j] for j in range(n_win)], axis=1)
    o_w, l_w = _pv_by_group(vwt, pw, True)
    o_w = _scale_groups(o_w, 1.0 / l_w)

    sl = B_KV_GROUPS * Q_BLOCK
    jt = lax.broadcasted_iota(jnp.int32, (n_slc, sl), 0)
    second_half = (lax.broadcasted_iota(jnp.int32, (1, sl), 1) & (Q_BLOCK - 1)) >= SLC_BLOCK
    qblk = 2 * i + second_half.astype(jnp.int32)
    forced = (jt == 0) | (jt == qblk) | (jt == qblk - 1)
    score = jnp.where(forced, BIG, jnp.where(jt > qblk, NEG, imp))
    rank = jnp.zeros((n_slc, sl), jnp.int32)
    for c in range(n_slc):
        row = score[c:c + 1, :]
        ahead = (row > score) | ((row == score) & (jt > c))
        rank = rank + ahead.astype(jnp.int32)
    selneg = jnp.where(rank < topk, 0.0, NEG).astype(BF16)
    wsel = jnp.concatenate([selneg[:, g * Q_BLOCK:(g + 1) * Q_BLOCK]
                            for g in range(B_KV_GROUPS) for _ in range(REP)], axis=1)
    w = jnp.concatenate([wq, wsel, jnp.zeros((AUG_LANES - gd - n_slc, nl), BF16)], axis=0)

    def scores(t):
        r0 = pl.multiple_of(t * SEL_TILE, SEL_TILE)
        return jnp.dot(kaug_ref[pl.ds(r0, SEL_TILE), :], w, preferred_element_type=F32)

    def update(carry, st, t):
        m_i, l_i, a0, a1 = carry
        m_n = jnp.maximum(m_i, jnp.max(st, axis=0, keepdims=True))
        alpha = jnp.exp2(m_i - m_n)
        pt = jnp.exp2(st - m_n)
        blk0 = t * (SEL_TILE // Q_BLOCK)
        vt = jnp.concatenate([vslt_ref[blk0 + j] for j in range(SEL_TILE // Q_BLOCK)], axis=1)
        pv, l_t = _pv_by_group(vt, pt, True)
        a0, a1 = (a + b for a, b in zip(_scale_groups((a0, a1), alpha), pv))
        return m_n, alpha * l_i + l_t, a0, a1

    last = i // 2
    par = i % 2
    n_far = jnp.maximum(last - 1, 0)
    carry = (jnp.full((1, nl), NEG, F32), jnp.zeros((1, nl), F32),
             jnp.zeros((HEAD_DIM, gl), F32), jnp.zeros((HEAD_DIM, gl), F32))
    s0_ref[...] = scores(0)

    def two_tiles(u, carry):
        t = 2 * u
        s1_ref[...] = scores(t + 1)
        carry = update(carry, s0_ref[...], t)
        s0_ref[...] = scores(t + 2)
        return update(carry, s1_ref[...], t + 1)

    def one_tile(t, carry):
        s1_ref[...] = scores(t + 1)
        carry = update(carry, s0_ref[...], t)
        s0_ref[...] = s1_ref[...]
        return carry

    carry = lax.fori_loop(0, n_far // 2, two_tiles, carry)
    carry = lax.fori_loop(n_far // 2 * 2, n_far, one_tile, carry)
    s1_ref[...] = scores(last)
    near0 = pl.multiple_of(par * (2 * SEL_TILE), SEL_TILE)
    before = bn_ref[pl.ds(near0, SEL_TILE), :] + jnp.where(last > 0, 0.0, NEG)
    carry = update(carry, s0_ref[...] + before, jnp.maximum(last - 1, 0))
    m_s, l_s, a0, a1 = update(carry, s1_ref[...] + bn_ref[pl.ds(near0 + SEL_TILE, SEL_TILE), :], last)
    o_s = _scale_groups((a0, a1), 1.0 / l_s)

    gt = jnp.transpose(gate_ref[...])
    heads = [gt[3 * h:3 * h + 1, :] * _head_block(o_c, h)
             + gt[3 * h + 1:3 * h + 2, :] * _head_block(o_s, h)
             + gt[3 * h + 2:3 * h + 3, :] * _head_block(o_w, h) for h in range(B_HEADS)]
    o_ref[...] = jnp.transpose(jnp.concatenate(heads, axis=0)).astype(o_ref.dtype)


def _mixers_call(qt, kk, vt, p32, kcmp, vcmpt, kaug, ba, sink_row, fc, bw, bn, ovt, bsz, seq,
                 col_gate, topk):
    nb = seq // Q_BLOCK
    assert A_HEADS == B_HEADS and A_KV_HEADS == B_KV_GROUPS
    qw = B_HEADS * HEAD_DIM
    kw = B_KV_GROUPS * HEAD_DIM
    per_batch = lambda a: pl.BlockSpec((None,) + a.shape[1:], lambda b, i: (b,) + (0,) * (a.ndim - 1))
    const = lambda a: pl.BlockSpec(a.shape, lambda b, i: (0,) * a.ndim)
    qt_seg = lambda a: pl.BlockSpec((None, None, None, qw, Q_BLOCK), lambda b, i: (a, b, i, 0, 0))
    kk_seg = lambda a: pl.BlockSpec((None, seq, kw), lambda b, i: (b, 0, a))
    vt_seg = lambda a: pl.BlockSpec((None, None) + vt.shape[2:], lambda b, i: (a, b, 0, 0, 0))
    y_spec = pl.BlockSpec((None, Q_BLOCK, qw), lambda b, i: (b, i, 0))
    kern = functools.partial(_mixers_kernel, topk=topk)
    return pl.pallas_call(
        kern,
        grid=(bsz, nb),
        in_specs=[qt_seg(QT_A), kk_seg(KK_A), vt_seg(VT_A), const(ba), const(sink_row),
                  qt_seg(QT_B), per_batch(kcmp), per_batch(vcmpt), per_batch(kaug), vt_seg(VT_SL),
                  kk_seg(KK_W), vt_seg(VT_W),
                  pl.BlockSpec((None, Q_BLOCK, V7X_LANES), lambda b, i: (b, i, col_gate // V7X_LANES)),
                  const(fc), const(bw), const(bn), const(ovt)],
        out_specs=[y_spec, y_spec],
        out_shape=[jax.ShapeDtypeStruct((bsz, seq, qw), BF16)] * 2,
        scratch_shapes=[pltpu.VMEM((kcmp.shape[1], B_HEADS * Q_BLOCK), F32),
                        pltpu.VMEM((SEL_TILE, B_HEADS * Q_BLOCK), F32),
                        pltpu.VMEM((SEL_TILE, B_HEADS * Q_BLOCK), F32)],
        compiler_params=_cparams(("parallel", "arbitrary")),
        name="mixers",
    )(qt, kk, vt, ba, sink_row, qt, kcmp, vcmpt, kaug, vt, kk, vt, p32, fc, bw, bn, ovt)


def _merge_kernel(x_ref, g_ref, ya_ref, yb_ref, wg_ref, wua_ref, wub_ref, wo_ref, o_ref):
    x = x_ref[...]
    d = x.shape[1]
    hb = _rms(x, g_ref[...]).astype(BF16)
    ga = jax.nn.sigmoid(jnp.dot(hb, wg_ref[:, :d], preferred_element_type=F32))
    gb = jax.nn.sigmoid(jnp.dot(hb, wg_ref[:, d:], preferred_element_type=F32))
    ua = jnp.dot(ya_ref[...], wua_ref[...], preferred_element_type=F32)
    ub = jnp.dot(yb_ref[...], wub_ref[...], preferred_element_type=F32)
    merged = ga * ua + gb * ub
    o_ref[...] = x + jnp.dot(merged.astype(BF16), wo_ref[...], preferred_element_type=F32)


def _merge(x2d, g, ya, yb, wg, wua, wub, wo, tm):
    n, d = x2d.shape
    const = lambda a: pl.BlockSpec(a.shape, lambda i: (0, 0))
    row = lambda a: pl.BlockSpec((tm, a.shape[1]), lambda i: (i, 0))
    return pl.pallas_call(
        _merge_kernel,
        grid=(n // tm,),
        in_specs=[row(x2d), const(g), row(ya), row(yb), const(wg), const(wua), const(wub), const(wo)],
        out_specs=row(x2d),
        out_shape=jax.ShapeDtypeStruct((n, d), F32),
        compiler_params=_cparams(("parallel",)),
        name="merge",
    )(x2d, g, ya, yb, wg, wua, wub, wo)


FFN_HALO = 16
FFN_GROUP = 6


def _ffn_kernel(xc_ref, xp_ref, gn_ref, wi_ref, cw_ref, cb_ref, wo_ref, gf_ref, o_ref, *, chunk):
    i = pl.program_id(1)
    xc = xc_ref[...]
    tm = xc.shape[0]
    gn = gn_ref[...]
    hp = _rms(xp_ref[...], gn) * jnp.where(i > 0, 1.0, 0.0)
    h = jnp.concatenate([hp, _rms(xc, gn)], axis=0).astype(BF16)
    d_ff = wo_ref.shape[0]

    def up(c0):
        return jnp.dot(h, wi_ref[:, c0:c0 + chunk], preferred_element_type=F32)

    def conv(ext, c0):
        cw = cw_ref[:, c0:c0 + chunk]
        out = cb_ref[:, c0:c0 + chunk]
        for k in range(CONV_WIDTH):
            off = FFN_HALO - (CONV_WIDTH - 1) + k
            out = out + cw[k:k + 1, :] * ext[off:off + tm]
        return out

    acc = None
    acts = []
    nxt = (up(0), up(d_ff))
    for c0 in range(0, d_ff, chunk):
        ext_u, ext_g = nxt
        if c0 + chunk < d_ff:
            nxt = (up(c0 + chunk), up(d_ff + c0 + chunk))
        acts.append((jax.nn.silu(conv(ext_g, d_ff + c0)) * conv(ext_u, c0)).astype(BF16))
        if len(acts) == FFN_GROUP or c0 + chunk >= d_ff:
            r0 = c0 + chunk - len(acts) * chunk
            part = jnp.dot(jnp.concatenate(acts, axis=1), wo_ref[r0:c0 + chunk, :],
                           preferred_element_type=F32)
            acc = part if acc is None else acc + part
            acts = []
    o_ref[...] = _rms(xc + acc, gf_ref[...])


def _ffn(x1, gn, wi, cw, cb, wo, gf, tm, chunk):
    bsz, seq, d = x1.shape
    const = lambda a: pl.BlockSpec(a.shape, lambda b, i: (0, 0))
    kern = functools.partial(_ffn_kernel, chunk=chunk)
    return pl.pallas_call(
        kern,
        grid=(bsz, seq // tm),
        in_specs=[pl.BlockSpec((None, tm, d), lambda b, i: (b, i, 0)),
                  pl.BlockSpec((None, FFN_HALO, d),
                               lambda b, i: (b, jnp.maximum(i * (tm // FFN_HALO) - 1, 0), 0)),
                  const(gn), const(wi), const(cw), const(cb), const(wo), const(gf)],
        out_specs=pl.BlockSpec((None, tm, d), lambda b, i: (b, i, 0)),
        out_shape=jax.ShapeDtypeStruct((bsz, seq, d), F32),
        compiler_params=_cparams(("parallel", "arbitrary")),
        name="ffn",
    )(x1, x1, gn, wi, cw, cb, wo, gf)


def _mixers(x, norm_mix, w_in, attn_sinks, cmp_pos_k, cmp_w1_k, cmp_w2_k, cmp_pos_v, cmp_w1_v,
            cmp_w2_v, table):
    bsz, seq, d = x.shape
    n = bsz * seq
    aq, akv = A_HEADS * HEAD_DIM, A_KV_HEADS * HEAD_DIM
    bq, bkv = B_HEADS * HEAD_DIM, B_KV_GROUPS * HEAD_DIM
    n_gate = 3 * B_HEADS
    assert seq % SEL_TILE == 0 and seq // SLC_BLOCK <= AUG_LANES - bkv
    splits = (aq, akv, akv, bq, bkv, bkv, bkv, bkv, bkv, bkv, n_gate, d, d)
    off = np.concatenate([[0], np.cumsum(splits)]).astype(int)
    seg = lambda k: w_in[:, off[k]:off[k + 1]]
    assert akv == bkv and aq == bq
    w_gate_nsa = jnp.pad(seg(10), ((0, 0), (0, V7X_LANES - n_gate)))
    w1 = jnp.concatenate([seg(0), seg(3), seg(1), seg(8), seg(6), seg(2), seg(7), seg(9),
                          seg(4), seg(5), w_gate_nsa], axis=1).astype(BF16)
    tm = min(512, seq)
    g_mix = norm_mix.reshape(1, d)

    qt, kk, kaug, vt, p32 = _proj(x.reshape(n, d), g_mix, w1, aq, seq, tm)
    nb = seq // Q_BLOCK
    qt = qt.reshape(2, bsz, nb, aq, Q_BLOCK)
    kk = kk.reshape(bsz, seq, 2 * bkv)
    kaug = kaug.reshape(bsz, seq, AUG_LANES)
    vt = vt.reshape(3, bsz, nb, bkv, Q_BLOCK)
    p32 = p32.reshape(bsz, seq, 2 * bkv + V7X_LANES)

    table = table * LOG2E
    table_a, table_b = table[:, :A_HEADS], table[:, A_HEADS:]
    sink_row = jnp.repeat(attn_sinks * LOG2E, Q_BLOCK).reshape(1, A_HEADS * Q_BLOCK)

    ncp = seq // CMP_STRIDE
    n_cmp = (seq - CMP_BLOCK) // CMP_STRIDE + 1
    n_slc = seq // SLC_BLOCK
    topk = min(SLC_TOPK, n_slc)

    k_cmp = _compress(p32, 0, cmp_pos_k, cmp_w1_k, cmp_w2_k, n_cmp, False)
    v_cmp_t = _compress(p32, bkv, cmp_pos_v, cmp_w1_v, cmp_w2_v, n_cmp, True)

    fc = _cmp_bias_rows(table_b)
    win_a, win_b = A_WINDOW + Q_BLOCK, NSA_WINDOW + Q_BLOCK
    bias_a, bw, bn = _toeplitz_tables(
        [_band_vector(table_a, win_a, A_WINDOW, A_WINDOW, False),
         _band_vector(table_b, win_b, NSA_WINDOW, NSA_WINDOW, False),
         _band_vector(table_b, 2 * SEL_TILE, SEL_TILE, None, True),
         _band_vector(table_b, 2 * SEL_TILE, SEL_TILE + Q_BLOCK, None, True)],
        [(0, 0), (1, 0), (2, 0), (2, 2 * SEL_TILE)],
        [win_a + A_WINDOW, win_b + NSA_WINDOW, 4 * SEL_TILE])
    ovt = jnp.asarray(_overlap_t(ncp, n_slc, n_cmp))
    y_a, y_b = _mixers_call(qt, kk, vt, p32, k_cmp, v_cmp_t, kaug, bias_a, sink_row, fc, bw, bn, ovt,
                            bsz, seq, 2 * bkv, topk)
    return y_a, y_b, w_in[:, off[11]:off[13]]


def _layer(x, norm_mix, w_in, attn_sinks, cmp_pos_k, cmp_w1_k, cmp_w2_k, cmp_pos_v, cmp_w1_v,
           cmp_w2_v, w_up_a, w_up_b, w_out, norm_ffn, w_ffn_in, conv_w, conv_b, w_ffn_out,
           table, norm_final):
    bsz, seq, d = x.shape
    n = bsz * seq
    y_a, y_b, w_merge_gates = _mixers(x, norm_mix, w_in, attn_sinks, cmp_pos_k, cmp_w1_k, cmp_w2_k,
                                      cmp_pos_v, cmp_w1_v, cmp_w2_v, table)

    x1 = _merge(x.reshape(n, d), norm_mix.reshape(1, d), y_a.reshape(n, -1), y_b.reshape(n, -1),
                w_merge_gates.astype(BF16), w_up_a.astype(BF16), w_up_b.astype(BF16),
                w_out.astype(BF16), min(512, n))

    return _ffn(x1.reshape(bsz, seq, d), norm_ffn.reshape(1, d), w_ffn_in.astype(BF16), conv_w,
                conv_b.reshape(1, -1), w_ffn_out.astype(BF16), norm_final.reshape(1, d),
                min(512, seq), 256)


def kernel(x, norm_mix, w_in, attn_sinks, cmp_pos_k, cmp_w1_k, cmp_w2_k, cmp_pos_v, cmp_w1_v, cmp_w2_v, w_up_a, w_up_b, w_out, norm_ffn, w_ffn_in, conv_w, conv_b, w_ffn_out, rel_bias_table, norm_final):
    assert norm_mix.shape[0] == 1, "single-layer block"
    return _layer(x, norm_mix[0], w_in[0], attn_sinks[0], cmp_pos_k[0], cmp_w1_k[0], cmp_w2_k[0],
                  cmp_pos_v[0], cmp_w1_v[0], cmp_w2_v[0], w_up_a[0], w_up_b[0], w_out[0],
                  norm_ffn[0], w_ffn_in[0], conv_w[0], conv_b[0], w_ffn_out[0], rel_bias_table,
                  norm_final)
```

```python
import functools
import math

import numpy as np
import jax
import jax.numpy as jnp
from jax import lax
from jax.experimental import pallas as pl
from jax.experimental.pallas import tpu as pltpu

F32 = jnp.float32
BF16 = jnp.bfloat16

HEAD_DIM = 64
A_HEADS = 8
A_KV_HEADS = 2
A_WINDOW = 128
B_HEADS = 8
B_KV_GROUPS = 2
REP = 4
CMP_BLOCK = 32
CMP_STRIDE = 16
SLC_BLOCK = 64
SLC_SHIFT = 6
SLC_TOPK = 16
NSA_WINDOW = 512
NUM_BUCKETS = 32
MAX_DISTANCE = 128
CONV_WIDTH = 3
Q_BLOCK = 128
EPS = 1e-6
NEG = -1e30
BIG = 1e30
SCALE = HEAD_DIM ** -0.5
LOG2E = math.log2(math.e)
Q_SCALE = SCALE * LOG2E
SUM_ROWS = 16

V7X_LANES = 128
V7X_SUBLANES = 8
V7X_VMEM_BYTES = 64 * 1024 * 1024
VMEM_LIMIT = 56 * 1024 * 1024


def _cparams(semantics):
    return pltpu.CompilerParams(dimension_semantics=semantics, vmem_limit_bytes=VMEM_LIMIT)


def _bucket_np(dist):
    dist = np.maximum(dist, 0)
    max_exact = NUM_BUCKETS // 2
    d = np.maximum(dist, 1).astype(np.float64)
    large = max_exact + (np.log(d / max_exact) / math.log(MAX_DISTANCE / max_exact)
                         * (NUM_BUCKETS - max_exact)).astype(np.int32)
    large = np.minimum(large, NUM_BUCKETS - 1)
    return np.where(dist < max_exact, dist, large).astype(np.int32)


def _table_lookup(table, dist):
    idx = _bucket_np(dist).reshape(-1)
    onehot = np.zeros((NUM_BUCKETS, idx.size), np.float32)
    onehot[idx, np.arange(idx.size)] = 1.0
    vals = jnp.dot(table.T, jnp.asarray(onehot), precision=lax.Precision.HIGHEST)
    return vals.reshape((table.shape[1],) + dist.shape)


def _band_vector(table, n_keys, offset, window, shift_far):
    length = n_keys + Q_BLOCK
    m = np.arange(length)
    m = np.where(m < Q_BLOCK, m, m - length)
    dist = m + offset
    ok = dist >= 0 if window is None else (dist >= 0) & (dist < window)
    u = _table_lookup(table, dist)
    if shift_far:
        u = u - table[NUM_BUCKETS - 1][:, None]
    return jnp.where(ok[None, :], u, NEG)


def _toeplitz_kernel(*refs, dest):
    n_tab = len(dest)
    covered = [0] * (len(refs) - n_tab)
    for u_ref, (j, row0) in zip(refs[:n_tab], dest):
        o_ref = refs[n_tab + j]
        n_heads, length = u_ref.shape
        n_keys = length - Q_BLOCK
        covered[j] = max(covered[j], row0 + n_keys)
        u2 = jnp.concatenate([u_ref[...], u_ref[...]], axis=1)
        for r0 in range(0, n_keys, Q_BLOCK):
            w0 = (-r0 - (Q_BLOCK - 1)) % length
            win = u2[:, w0:w0 + 2 * Q_BLOCK]
            for h in range(n_heads):
                x = jnp.broadcast_to(win[h:h + 1, :], (Q_BLOCK, 2 * Q_BLOCK))
                x = pltpu.roll(x, Q_BLOCK + 1, 1, stride=1, stride_axis=0)
                o_ref[row0 + r0:row0 + r0 + Q_BLOCK, h * Q_BLOCK:(h + 1) * Q_BLOCK] = x[:, :Q_BLOCK]
    for o_ref, done in zip(refs[n_tab:], covered):
        if o_ref.shape[0] > done:
            o_ref[done:, :] = jnp.full((o_ref.shape[0] - done, o_ref.shape[1]), NEG, F32)


def _toeplitz_tables(us, dest, n_rows):
    n_heads = us[0].shape[0]
    kern = functools.partial(_toeplitz_kernel, dest=tuple(dest))
    return pl.pallas_call(
        kern,
        out_shape=[jax.ShapeDtypeStruct((r, n_heads * Q_BLOCK), F32) for r in n_rows],
        compiler_params=pltpu.CompilerParams(vmem_limit_bytes=VMEM_LIMIT),
        name="bias_tables",
    )(*us)


def _cmp_bias_rows(table):
    n = MAX_DISTANCE // CMP_STRIDE
    q = np.arange(Q_BLOCK)
    rho = (q - (CMP_BLOCK - 1)) % CMP_STRIDE
    dist = CMP_STRIDE * np.arange(n + 1)[:, None] + rho[None, :]
    vals = _table_lookup(table, dist)
    vals = jnp.transpose(vals, (1, 0, 2)).reshape(n + 1, table.shape[1] * Q_BLOCK)
    vals = jnp.concatenate([vals[:n] - vals[n:], vals[n:]], axis=0)
    return jnp.pad(vals, ((0, n - 1), (0, 0)))


def _overlap_t(n_cmp_pad, n_slc, n_cmp):
    r = SLC_BLOCK // CMP_STRIDE
    c = CMP_BLOCK // CMP_STRIDE
    j, m, n = np.meshgrid(np.arange(n_slc), np.arange(r), np.arange(c), indexing='ij')
    i = r * j + m - n
    ok = (i >= 0) & (i < n_cmp)
    mat = np.zeros((n_slc, n_cmp_pad), np.float32)
    np.add.at(mat, (j[ok], i[ok]), 1.0)
    return mat


def _rms(x, g):
    return x * lax.rsqrt(jnp.mean(x * x, axis=-1, keepdims=True) + EPS) * g


PROJ_CHUNK = 256
VT_A, VT_SL, VT_W = 0, 1, 2
QT_A, QT_B = 0, 1
KK_A, KK_W = 0, 1


def _proj_kernel(x_ref, g_ref, w_ref, qt_ref, kk_ref, kaug_ref, vt_ref, o32_ref, *, seq):
    tm = x_ref.shape[0]
    kv = B_KV_GROUPS * HEAD_DIM
    qw = qt_ref.shape[2]
    hb = _rms(x_ref[...], g_ref[...]).astype(BF16)

    def cols(c0, width=PROJ_CHUNK):
        return jnp.dot(hb, w_ref[:, c0:c0 + width], preferred_element_type=F32)

    def put_transposed(ref, a, r, row0=0):
        for j in range(tm // Q_BLOCK):
            ref[a, j, row0:row0 + r.shape[1], :] = jnp.transpose(
                r[j * Q_BLOCK:(j + 1) * Q_BLOCK, :]).astype(BF16)

    for c0 in range(0, 2 * qw, PROJ_CHUNK):
        put_transposed(qt_ref, c0 // qw, cols(c0) * Q_SCALE, c0 % qw)
    kk_ref[...] = cols(2 * qw).astype(BF16)
    r = cols(2 * qw + 2 * kv)
    pos = (pl.program_id(0) * tm) % seq + lax.broadcasted_iota(jnp.int32, (tm, kv), 0)
    onehot = lax.broadcasted_iota(jnp.int32, (tm, kv), 1) == jnp.right_shift(pos, SLC_SHIFT)
    kaug_ref[:, :kv] = r[:, :kv].astype(BF16)
    kaug_ref[:, kv:] = jnp.where(onehot, 1.0, 0.0).astype(BF16)
    put_transposed(vt_ref, VT_A, r[:, kv:])
    r = cols(2 * qw + 4 * kv)
    put_transposed(vt_ref, VT_SL, r[:, :kv])
    put_transposed(vt_ref, VT_W, r[:, kv:])
    o32_ref[:, :2 * kv] = cols(2 * qw + 6 * kv)
    o32_ref[:, 2 * kv:] = jax.nn.sigmoid(cols(2 * qw + 8 * kv, V7X_LANES))


def _proj(x2d, g, w1, qw, seq, tm):
    n, d = x2d.shape
    kv = B_KV_GROUPS * HEAD_DIM
    n32 = 2 * kv + V7X_LANES
    nb = tm // Q_BLOCK
    assert AUG_LANES == 2 * kv == PROJ_CHUNK and seq % tm == 0 and tm % Q_BLOCK == 0
    rows = lambda width: pl.BlockSpec((tm, width), lambda i: (i, 0))
    blocks_t = lambda a, width: pl.BlockSpec((a, nb, width, Q_BLOCK), lambda i: (0, i, 0, 0))
    kern = functools.partial(_proj_kernel, seq=seq)
    return pl.pallas_call(
        kern,
        grid=(n // tm,),
        in_specs=[rows(d), pl.BlockSpec((1, d), lambda i: (0, 0)),
                  pl.BlockSpec(w1.shape, lambda i: (0, 0))],
        out_specs=[blocks_t(2, qw), rows(2 * kv), rows(AUG_LANES), blocks_t(3, kv), rows(n32)],
        out_shape=[jax.ShapeDtypeStruct((2, n // Q_BLOCK, qw, Q_BLOCK), BF16),
                   jax.ShapeDtypeStruct((n, 2 * kv), BF16),
                   jax.ShapeDtypeStruct((n, AUG_LANES), BF16),
                   jax.ShapeDtypeStruct((3, n // Q_BLOCK, kv, Q_BLOCK), BF16),
                   jax.ShapeDtypeStruct((n, n32), F32)],
        compiler_params=_cparams(("parallel",)),
        name="proj",
    )(x2d, g, w1)


CMP_PAIR = 2


def _compress_kernel(x_ref, pos_ref, w1_ref, w2_ref, o_ref, *, n_cmp, transpose_out):
    ncp = x_ref.shape[0] // CMP_STRIDE

    def group_diag(w):
        zero = jnp.zeros(w.shape, w.dtype)
        return jnp.concatenate(
            [jnp.concatenate([w if g == gg else zero for gg in range(B_KV_GROUPS)], axis=1)
             for g in range(B_KV_GROUPS)], axis=0)

    halves = []
    for half in range(CMP_BLOCK // CMP_STRIDE):
        acc = None
        for l0 in range(half * CMP_STRIDE, (half + 1) * CMP_STRIDE, CMP_PAIR):
            lhs = jnp.concatenate(
                [(x_ref[pl.ds(l % CMP_STRIDE, ncp, stride=CMP_STRIDE), :]
                  + jnp.concatenate([pos_ref[l:l + 1, :]] * B_KV_GROUPS, axis=1)).astype(BF16)
                 for l in range(l0, l0 + CMP_PAIR)], axis=1)
            rhs = jnp.concatenate([group_diag(w1_ref[l]) for l in range(l0, l0 + CMP_PAIR)], axis=0)
            part = jnp.dot(lhs, rhs, preferred_element_type=F32)
            acc = part if acc is None else acc + part
        halves.append(acc)
    top, bottom = halves
    h1 = top + jnp.concatenate([bottom[1:], bottom[:1]], axis=0)
    o = jnp.dot(jax.nn.gelu(h1).astype(BF16), group_diag(w2_ref[...]), preferred_element_type=F32)
    row = lax.broadcasted_iota(jnp.int32, o.shape, 0)
    o = jnp.where(row < n_cmp, o, 0.0)
    if transpose_out:
        o = jnp.transpose(o)
    o_ref[...] = o.astype(o_ref.dtype)


def _compress(p32, col, pos, w1, w2, n_cmp, transpose_out):
    bsz, seq, _ = p32.shape
    ncp = seq // CMP_STRIDE
    gd = B_KV_GROUPS * HEAD_DIM
    w1 = w1.reshape(CMP_BLOCK, HEAD_DIM, -1).astype(BF16)
    w2 = w2.astype(BF16)
    oshape = (gd, ncp) if transpose_out else (ncp, gd)
    const = lambda a: pl.BlockSpec(a.shape, lambda b: (0,) * a.ndim)
    kern = functools.partial(_compress_kernel, n_cmp=n_cmp, transpose_out=transpose_out)
    return pl.pallas_call(
        kern,
        grid=(bsz,),
        in_specs=[pl.BlockSpec((None, seq, gd), lambda b: (b, 0, col // gd)),
                  const(pos), const(w1), const(w2)],
        out_specs=pl.BlockSpec((None,) + oshape, lambda b: (b, 0, 0)),
        out_shape=jax.ShapeDtypeStruct((bsz,) + oshape, BF16),
        compiler_params=_cparams(("parallel",)),
        name="compress",
    )(p32, pos, w1, w2)


def _block_diag_qt(qt):
    zero = jnp.zeros((HEAD_DIM, Q_BLOCK), BF16)
    n_groups = qt.shape[0] // (REP * HEAD_DIM)
    return jnp.concatenate([
        jnp.concatenate([qt[(REP * g + r) * HEAD_DIM:(REP * g + r + 1) * HEAD_DIM] if gg == g else zero
                         for gg in range(n_groups) for r in range(REP)], axis=1)
        for g in range(n_groups)], axis=0)


def _pv_by_group(vt, p, with_sum):
    n_groups = vt.shape[0] // HEAD_DIM
    gl = p.shape[1] // n_groups
    pb = p.astype(BF16)
    outs, sums = [], []
    for g in range(n_groups):
        lhs = vt[g * HEAD_DIM:(g + 1) * HEAD_DIM]
        if with_sum:
            lhs = jnp.concatenate([lhs, jnp.ones((SUM_ROWS, vt.shape[1]), BF16)], axis=0)
        r = jnp.dot(lhs, pb[:, g * gl:(g + 1) * gl], preferred_element_type=F32)
        outs.append(r[:HEAD_DIM])
        sums.append(r[HEAD_DIM:HEAD_DIM + 1])
    return outs, (jnp.concatenate(sums, axis=1) if with_sum else None)


def _scale_groups(outs, row):
    gl = outs[0].shape[1]
    return [o * row[:, g * gl:(g + 1) * gl] for g, o in enumerate(outs)]


def _head_block(outs, h):
    r = h % REP
    return outs[h // REP][:, r * Q_BLOCK:(r + 1) * Q_BLOCK]


def _banded_scores(i, wq, k_ref, bias_ref, window):
    n_blk = window // Q_BLOCK + 1
    span = n_blk * Q_BLOCK
    kb0 = jnp.maximum(i - (n_blk - 1), 0)
    shift = jnp.maximum(n_blk - 1 - i, 0)
    s = jnp.dot(k_ref[pl.ds(pl.multiple_of(kb0 * Q_BLOCK, Q_BLOCK), span), :], wq,
                preferred_element_type=F32)
    return s + bias_ref[pl.ds(pl.multiple_of(shift * Q_BLOCK, Q_BLOCK), span), :], kb0


def _swa_block(i, s, kb0, vt_ref, sink_ref):
    sink = sink_ref[...]
    m = jnp.maximum(jnp.max(s, axis=0, keepdims=True), sink)
    p = jnp.exp2(s - m)
    vt = jnp.concatenate([vt_ref[kb0 + j] for j in range(A_WINDOW // Q_BLOCK + 1)], axis=1)
    outs, l = _pv_by_group(vt, p, True)
    outs = _scale_groups(outs, 1.0 / (l + jnp.exp2(sink - m)))
    heads = [_head_block(outs, h) for h in range(A_HEADS)]
    return jnp.transpose(jnp.concatenate(heads, axis=0))


SEL_TILE = 2 * Q_BLOCK
AUG_LANES = 2 * V7X_LANES
CMP_SHIFT = 4
CMP_PER_QB = Q_BLOCK // CMP_STRIDE
CMP_NEAR = MAX_DISTANCE // CMP_STRIDE
CMP_BAND = 3 * CMP_PER_QB


def _mixers_kernel(qa_ref, ka_ref, vat_ref, ba_ref, sink_ref,
                   q_ref, kcmp_ref, vcmpt_ref, kaug_ref, vslt_ref, kw_ref, vwt_ref, gate_ref,
                   fc_ref, bw_ref, bn_ref, ovt_ref, oa_ref, o_ref, sc_ref, s0_ref, s1_ref, *, topk):
    i = pl.program_id(1)
    n_slc, ncp = ovt_ref.shape
    gl = REP * Q_BLOCK
    nl = B_KV_GROUPS * gl
    gd = B_KV_GROUPS * HEAD_DIM
    n_win = NSA_WINDOW // Q_BLOCK + 1

    sa, kb0_a = _banded_scores(i, _block_diag_qt(qa_ref[...]), ka_ref, ba_ref, A_WINDOW)
    oa_ref[...] = _swa_block(i, sa, kb0_a, vat_ref, sink_ref).astype(oa_ref.dtype)

    wq = _block_diag_qt(q_ref[...])

    qlane = lax.broadcasted_iota(jnp.int32, (1, nl), 1) & (Q_BLOCK - 1)
    cmax = jnp.right_shift(qlane + i * Q_BLOCK - (CMP_BLOCK - 1), CMP_SHIFT)
    valid_c = lax.broadcasted_iota(jnp.int32, (ncp, nl), 0) <= cmax
    s = jnp.dot(kcmp_ref[...], wq, preferred_element_type=F32) + fc_ref[CMP_NEAR:CMP_NEAR + 1, :]
    sc_ref[...] = jnp.where(valid_c, s, NEG)
    b0 = pl.multiple_of(jnp.clip(CMP_PER_QB * (i - 2), 0, ncp - CMP_BAND), V7X_SUBLANES)
    kkb = cmax - (b0 + lax.broadcasted_iota(jnp.int32, (CMP_BAND, nl), 0))
    delta = jnp.zeros((CMP_BAND, nl), F32)
    for k in range(CMP_NEAR):
        delta = jnp.where(kkb == k, fc_ref[k:k + 1, :], delta)
    sc_ref[pl.ds(b0, CMP_BAND), :] = sc_ref[pl.ds(b0, CMP_BAND), :] + delta
    s = sc_ref[...]
    m = jnp.max(s, axis=0, keepdims=True)
    p = jnp.exp2(s - m)
    l = jnp.sum(p, axis=0, keepdims=True)
    p = p * jnp.where(cmax >= 0, 1.0 / l, 0.0)
    o_c, _ = _pv_by_group(vcmpt_ref[...], p, False)
    psum = jnp.concatenate(
        [sum(p[:, g * gl + r * Q_BLOCK:g * gl + (r + 1) * Q_BLOCK] for r in range(REP))
         for g in range(B_KV_GROUPS)], axis=1)
    imp = jnp.dot(ovt_ref[...], psum, preferred_element_type=F32,
                  precision=lax.Precision.HIGHEST)

    sw, kb0 = _banded_scores(i, wq, kw_ref, bw_ref, NSA_WINDOW)
    mw = jnp.max(sw, axis=0, keepdims=True)
    pw = jnp.exp2(sw - mw)
    vwt = jnp.concatenate([vwt_ref[kb0 + j] for j in range(n_win)], axis=1)
    o_w, l_w = _pv_by_group(vwt, pw, True)
    o_w = _scale_groups(o_w, 1.0 / l_w)

    sl = B_KV_GROUPS * Q_BLOCK
    jt = lax.broadcasted_iota(jnp.int32, (n_slc, sl), 0)
    second_half = (lax.broadcasted_iota(jnp.int32, (1, sl), 1) & (Q_BLOCK - 1)) >= SLC_BLOCK
    qblk = 2 * i + second_half.astype(jnp.int32)
    forced = (jt == 0) | (jt == qblk) | (jt == qblk - 1)
    score = jnp.where(forced, BIG, jnp.where(jt > qblk, NEG, imp))
    rank = jnp.zeros((n_slc, sl), jnp.int32)
    for c in range(n_slc):
        row = score[c:c + 1, :]
        ahead = (row > score) | ((row == score) & (jt > c))
        rank = rank + ahead.astype(jnp.int32)
    selneg = jnp.where(rank < topk, 0.0, NEG).astype(BF16)
    wsel = jnp.concatenate([selneg[:, g * Q_BLOCK:(g + 1) * Q_BLOCK]
                            for g in range(B_KV_GROUPS) for _ in range(REP)], axis=1)
    w = jnp.concatenate([wq, wsel, jnp.zeros((AUG_LANES - gd - n_slc, nl), BF16)], axis=0)

    def scores(t):
        r0 = pl.multiple_of(t * SEL_TILE, SEL_TILE)
        return jnp.dot(kaug_ref[pl.ds(r0, SEL_TILE), :], w, preferred_element_type=F32)

    def update(carry, st, t):
        m_i, l_i, a0, a1 = carry
        m_n = jnp.maximum(m_i, jnp.max(st, axis=0, keepdims=True))
        alpha = jnp.exp2(m_i - m_n)
        pt = jnp.exp2(st - m_n)
        blk0 = t * (SEL_TILE // Q_BLOCK)
        vt = jnp.concatenate([vslt_ref[blk0 + j] for j in range(SEL_TILE // Q_BLOCK)], axis=1)
        pv, l_t = _pv_by_group(vt, pt, True)
        a0, a1 = (a + b for a, b in zip(_scale_groups((a0, a1), alpha), pv))
        return m_n, alpha * l_i + l_t, a0, a1

    last = i // 2
    par = i % 2
    n_far = jnp.maximum(last - 1, 0)
    carry = (jnp.full((1, nl), NEG, F32), jnp.zeros((1, nl), F32),
             jnp.zeros((HEAD_DIM, gl), F32), jnp.zeros((HEAD_DIM, gl), F32))
    s0_ref[...] = scores(0)

    def two_tiles(u, carry):
        t = 2 * u
        s1_ref[...] = scores(t + 1)
        carry = update(carry, s0_ref[...], t)
        s0_ref[...] = scores(t + 2)
        return update(carry, s1_ref[...], t + 1)

    def one_tile(t, carry):
        s1_ref[...] = scores(t + 1)
        carry = update(carry, s0_ref[...], t)
        s0_ref[...] = s1_ref[...]
        return carry

    carry = lax.fori_loop(0, n_far // 2, two_tiles, carry)
    carry = lax.fori_loop(n_far // 2 * 2, n_far, one_tile, carry)
    s1_ref[...] = scores(last)
    near0 = pl.multiple_of(par * (2 * SEL_TILE), SEL_TILE)
    before = bn_ref[pl.ds(near0, SEL_TILE), :] + jnp.where(last > 0, 0.0, NEG)
    carry = update(carry, s0_ref[...] + before, jnp.maximum(last - 1, 0))
    m_s, l_s, a0, a1 = update(carry, s1_ref[...] + bn_ref[pl.ds(near0 + SEL_TILE, SEL_TILE), :], last)
    o_s = _scale_groups((a0, a1), 1.0 / l_s)

    gt = jnp.transpose(gate_ref[...])
    heads = [gt[3 * h:3 * h + 1, :] * _head_block(o_c, h)
             + gt[3 * h + 1:3 * h + 2, :] * _head_block(o_s, h)
             + gt[3 * h + 2:3 * h + 3, :] * _head_block(o_w, h) for h in range(B_HEADS)]
    o_ref[...] = jnp.transpose(jnp.concatenate(heads, axis=0)).astype(o_ref.dtype)


def _mixers_call(qt, kk, vt, p32, kcmp, vcmpt, kaug, ba, sink_row, fc, bw, bn, ovt, bsz, seq,
                 col_gate, topk):
    nb = seq // Q_BLOCK
    assert A_HEADS == B_HEADS and A_KV_HEADS == B_KV_GROUPS
    qw = B_HEADS * HEAD_DIM
    kw = B_KV_GROUPS * HEAD_DIM
    per_batch = lambda a: pl.BlockSpec((None,) + a.shape[1:], lambda b, i: (b,) + (0,) * (a.ndim - 1))
    const = lambda a: pl.BlockSpec(a.shape, lambda b, i: (0,) * a.ndim)
    qt_seg = lambda a: pl.BlockSpec((None, None, None, qw, Q_BLOCK), lambda b, i: (a, b, i, 0, 0))
    kk_seg = lambda a: pl.BlockSpec((None, seq, kw), lambda b, i: (b, 0, a))
    vt_seg = lambda a: pl.BlockSpec((None, None) + vt.shape[2:], lambda b, i: (a, b, 0, 0, 0))
    y_spec = pl.BlockSpec((None, Q_BLOCK, qw), lambda b, i: (b, i, 0))
    kern = functools.partial(_mixers_kernel, topk=topk)
    return pl.pallas_call(
        kern,
        grid=(bsz, nb),
        in_specs=[qt_seg(QT_A), kk_seg(KK_A), vt_seg(VT_A), const(ba), const(sink_row),
                  qt_seg(QT_B), per_batch(kcmp), per_batch(vcmpt), per_batch(kaug), vt_seg(VT_SL),
                  kk_seg(KK_W), vt_seg(VT_W),
                  pl.BlockSpec((None, Q_BLOCK, V7X_LANES), lambda b, i: (b, i, col_gate // V7X_LANES)),
                  const(fc), const(bw), const(bn), const(ovt)],
        out_specs=[y_spec, y_spec],
        out_shape=[jax.ShapeDtypeStruct((bsz, seq, qw), BF16)] * 2,
        scratch_shapes=[pltpu.VMEM((kcmp.shape[1], B_HEADS * Q_BLOCK), F32),
                        pltpu.VMEM((SEL_TILE, B_HEADS * Q_BLOCK), F32),
                        pltpu.VMEM((SEL_TILE, B_HEADS * Q_BLOCK), F32)],
        compiler_params=_cparams(("parallel", "arbitrary")),
        name="mixers",
    )(qt, kk, vt, ba, sink_row, qt, kcmp, vcmpt, kaug, vt, kk, vt, p32, fc, bw, bn, ovt)


def _merge_kernel(x_ref, g_ref, ya_ref, yb_ref, wg_ref, wua_ref, wub_ref, wo_ref, o_ref):
    x = x_ref[...]
    d = x.shape[1]
    hb = _rms(x, g_ref[...]).astype(BF16)
    ga = jax.nn.sigmoid(jnp.dot(hb, wg_ref[:, :d], preferred_element_type=F32))
    gb = jax.nn.sigmoid(jnp.dot(hb, wg_ref[:, d:], preferred_element_type=F32))
    ua = jnp.dot(ya_ref[...], wua_ref[...], preferred_element_type=F32)
    ub = jnp.dot(yb_ref[...], wub_ref[...], preferred_element_type=F32)
    merged = ga * ua + gb * ub
    o_ref[...] = x + jnp.dot(merged.astype(BF16), wo_ref[...], preferred_element_type=F32)


def _merge(x2d, g, ya, yb, wg, wua, wub, wo, tm):
    n, d = x2d.shape
    const = lambda a: pl.BlockSpec(a.shape, lambda i: (0, 0))
    row = lambda a: pl.BlockSpec((tm, a.shape[1]), lambda i: (i, 0))
    return pl.pallas_call(
        _merge_kernel,
        grid=(n // tm,),
        in_specs=[row(x2d), const(g), row(ya), row(yb), const(wg), const(wua), const(wub), const(wo)],
        out_specs=row(x2d),
        out_shape=jax.ShapeDtypeStruct((n, d), F32),
        compiler_params=_cparams(("parallel",)),
        name="merge",
    )(x2d, g, ya, yb, wg, wua, wub, wo)


FFN_HALO = 16
FFN_GROUP = 6


def _ffn_kernel(xc_ref, xp_ref, gn_ref, wi_ref, cw_ref, cb_ref, wo_ref, gf_ref, o_ref, *, chunk):
    i = pl.program_id(1)
    xc = xc_ref[...]
    tm = xc.shape[0]
    gn = gn_ref[...]
    hp = _rms(xp_ref[...], gn) * jnp.where(i > 0, 1.0, 0.0)
    h = jnp.concatenate([hp, _rms(xc, gn)], axis=0).astype(BF16)
    d_ff = wo_ref.shape[0]

    def up(c0):
        return jnp.dot(h, wi_ref[:, c0:c0 + chunk], preferred_element_type=F32)

    def conv(ext, c0):
        cw = cw_ref[:, c0:c0 + chunk]
        out = cb_ref[:, c0:c0 + chunk]
        for k in range(CONV_WIDTH):
            off = FFN_HALO - (CONV_WIDTH - 1) + k
            out = out + cw[k:k + 1, :] * ext[off:off + tm]
        return out

    acc = None
    acts = []
    nxt = (up(0), up(d_ff))
    for c0 in range(0, d_ff, chunk):
        ext_u, ext_g = nxt
        if c0 + chunk < d_ff:
            nxt = (up(c0 + chunk), up(d_ff + c0 + chunk))
        acts.append((jax.nn.silu(conv(ext_g, d_ff + c0)) * conv(ext_u, c0)).astype(BF16))
        if len(acts) == FFN_GROUP or c0 + chunk >= d_ff:
            r0 = c0 + chunk - len(acts) * chunk
            part = jnp.dot(jnp.concatenate(acts, axis=1), wo_ref[r0:c0 + chunk, :],
                           preferred_element_type=F32)
            acc = part if acc is None else acc + part
            acts = []
    o_ref[...] = _rms(xc + acc, gf_ref[...])


def _ffn(x1, gn, wi, cw, cb, wo, gf, tm, chunk):
    bsz, seq, d = x1.shape
    const = lambda a: pl.BlockSpec(a.shape, lambda b, i: (0, 0))
    kern = functools.partial(_ffn_kernel, chunk=chunk)
    return pl.pallas_call(
        kern,
        grid=(bsz, seq // tm),
        in_specs=[pl.BlockSpec((None, tm, d), lambda b, i: (b, i, 0)),
                  pl.BlockSpec((None, FFN_HALO, d),
                               lambda b, i: (b, jnp.maximum(i * (tm // FFN_HALO) - 1, 0), 0)),
                  const(gn), const(wi), const(cw), const(cb), const(wo), const(gf)],
        out_specs=pl.BlockSpec((None, tm, d), lambda b, i: (b, i, 0)),
        out_shape=jax.ShapeDtypeStruct((bsz, seq, d), F32),
        compiler_params=_cparams(("parallel", "arbitrary")),
        name="ffn",
    )(x1, x1, gn, wi, cw, cb, wo, gf)


def _mixers(x, norm_mix, w_in, attn_sinks, cmp_pos_k, cmp_w1_k, cmp_w2_k, cmp_pos_v, cmp_w1_v,
            cmp_w2_v, table):
    bsz, seq, d = x.shape
    n = bsz * seq
    aq, akv = A_HEADS * HEAD_DIM, A_KV_HEADS * HEAD_DIM
    bq, bkv = B_HEADS * HEAD_DIM, B_KV_GROUPS * HEAD_DIM
    n_gate = 3 * B_HEADS
    assert seq % SEL_TILE == 0 and seq // SLC_BLOCK <= AUG_LANES - bkv
    splits = (aq, akv, akv, bq, bkv, bkv, bkv, bkv, bkv, bkv, n_gate, d, d)
    off = np.concatenate([[0], np.cumsum(splits)]).astype(int)
    seg = lambda k: w_in[:, off[k]:off[k + 1]]
    assert akv == bkv and aq == bq
    w_gate_nsa = jnp.pad(seg(10), ((0, 0), (0, V7X_LANES - n_gate)))
    w1 = jnp.concatenate([seg(0), seg(3), seg(1), seg(8), seg(6), seg(2), seg(7), seg(9),
                          seg(4), seg(5), w_gate_nsa], axis=1).astype(BF16)
    tm = min(1024, seq)
    g_mix = norm_mix.reshape(1, d)

    qt, kk, kaug, vt, p32 = _proj(x.reshape(n, d), g_mix, w1, aq, seq, tm)
    nb = seq // Q_BLOCK
    qt = qt.reshape(2, bsz, nb, aq, Q_BLOCK)
    kk = kk.reshape(bsz, seq, 2 * bkv)
    kaug = kaug.reshape(bsz, seq, AUG_LANES)
    vt = vt.reshape(3, bsz, nb, bkv, Q_BLOCK)
    p32 = p32.reshape(bsz, seq, 2 * bkv + V7X_LANES)

    table = table * LOG2E
    table_a, table_b = table[:, :A_HEADS], table[:, A_HEADS:]
    sink_row = jnp.repeat(attn_sinks * LOG2E, Q_BLOCK).reshape(1, A_HEADS * Q_BLOCK)

    ncp = seq // CMP_STRIDE
    n_cmp = (seq - CMP_BLOCK) // CMP_STRIDE + 1
    n_slc = seq // SLC_BLOCK
    topk = min(SLC_TOPK, n_slc)

    k_cmp = _compress(p32, 0, cmp_pos_k, cmp_w1_k, cmp_w2_k, n_cmp, False)
    v_cmp_t = _compress(p32, bkv, cmp_pos_v, cmp_w1_v, cmp_w2_v, n_cmp, True)

    fc = _cmp_bias_rows(table_b)
    win_a, win_b = A_WINDOW + Q_BLOCK, NSA_WINDOW + Q_BLOCK
    bias_a, bw, bn = _toeplitz_tables(
        [_band_vector(table_a, win_a, A_WINDOW, A_WINDOW, False),
         _band_vector(table_b, win_b, NSA_WINDOW, NSA_WINDOW, False),
         _band_vector(table_b, 2 * SEL_TILE, SEL_TILE, None, True),
         _band_vector(table_b, 2 * SEL_TILE, SEL_TILE + Q_BLOCK, None, True)],
        [(0, 0), (1, 0), (2, 0), (2, 2 * SEL_TILE)],
        [win_a + A_WINDOW, win_b + NSA_WINDOW, 4 * SEL_TILE])
    ovt = jnp.asarray(_overlap_t(ncp, n_slc, n_cmp))
    y_a, y_b = _mixers_call(qt, kk, vt, p32, k_cmp, v_cmp_t, kaug, bias_a, sink_row, fc, bw, bn, ovt,
                            bsz, seq, 2 * bkv, topk)
    return y_a, y_b, w_in[:, off[11]:off[13]]


def _layer(x, norm_mix, w_in, attn_sinks, cmp_pos_k, cmp_w1_k, cmp_w2_k, cmp_pos_v, cmp_w1_v,
           cmp_w2_v, w_up_a, w_up_b, w_out, norm_ffn, w_ffn_in, conv_w, conv_b, w_ffn_out,
           table, norm_final):
    bsz, seq, d = x.shape
    n = bsz * seq
    y_a, y_b, w_merge_gates = _mixers(x, norm_mix, w_in, attn_sinks, cmp_pos_k, cmp_w1_k, cmp_w2_k,
                                      cmp_pos_v, cmp_w1_v, cmp_w2_v, table)

    x1 = _merge(x.reshape(n, d), norm_mix.reshape(1, d), y_a.reshape(n, -1), y_b.reshape(n, -1),
                w_merge_gates.astype(BF16), w_up_a.astype(BF16), w_up_b.astype(BF16),
                w_out.astype(BF16), min(1024, n))

    return _ffn(x1.reshape(bsz, seq, d), norm_ffn.reshape(1, d), w_ffn_in.astype(BF16), conv_w,
                conv_b.reshape(1, -1), w_ffn_out.astype(BF16), norm_final.reshape(1, d),
                min(512, seq), 256)


def kernel(x, norm_mix, w_in, attn_sinks, cmp_pos_k, cmp_w1_k, cmp_w2_k, cmp_pos_v, cmp_w1_v, cmp_w2_v, w_up_a, w_up_b, w_out, norm_ffn, w_ffn_in, conv_w, conv_b, w_ffn_out, rel_bias_table, norm_final):
    assert norm_mix.shape[0] == 1, "single-layer block"
    return _layer(x, norm_mix[0], w_in[0], attn_sinks[0], cmp_pos_k[0], cmp_w1_k[0], cmp_w2_k[0],
                  cmp_pos_v[0], cmp_w1_v[0], cmp_w2_v[0], w_up_a[0], w_up_b[0], w_out[0],
                  norm_ffn[0], w_ffn_in[0], conv_w[0], conv_b[0], w_ffn_out[0], rel_bias_table,
                  norm_final)
```

```python
import functools
import math

import numpy as np
import jax
import jax.numpy as jnp
from jax import lax
from jax.experimental import pallas as pl
from jax.experimental.pallas import tpu as pltpu

F32 = jnp.float32
BF16 = jnp.bfloat16

HEAD_DIM = 64
A_HEADS = 8
A_KV_HEADS = 2
A_WINDOW = 128
B_HEADS = 8
B_KV_GROUPS = 2
REP = 4
CMP_BLOCK = 32
CMP_STRIDE = 16
SLC_BLOCK = 64
SLC_SHIFT = 6
SLC_TOPK = 16
NSA_WINDOW = 512
NUM_BUCKETS = 32
MAX_DISTANCE = 128
CONV_WIDTH = 3
Q_BLOCK = 128
EPS = 1e-6
NEG = -1e30
BIG = 1e30
SCALE = HEAD_DIM ** -0.5
LOG2E = math.log2(math.e)
Q_SCALE = SCALE * LOG2E
SUM_ROWS = 16

V7X_LANES = 128
V7X_SUBLANES = 8
V7X_VMEM_BYTES = 64 * 1024 * 1024
VMEM_LIMIT = 56 * 1024 * 1024


def _cparams(semantics):
    return pltpu.CompilerParams(dimension_semantics=semantics, vmem_limit_bytes=VMEM_LIMIT)


def _bucket_np(dist):
    dist = np.maximum(dist, 0)
    max_exact = NUM_BUCKETS // 2
    d = np.maximum(dist, 1).astype(np.float64)
    large = max_exact + (np.log(d / max_exact) / math.log(MAX_DISTANCE / max_exact)
                         * (NUM_BUCKETS - max_exact)).astype(np.int32)
    large = np.minimum(large, NUM_BUCKETS - 1)
    return np.where(dist < max_exact, dist, large).astype(np.int32)


def _table_lookup(table, dist):
    idx = _bucket_np(dist).reshape(-1)
    onehot = np.zeros((NUM_BUCKETS, idx.size), np.float32)
    onehot[idx, np.arange(idx.size)] = 1.0
    vals = jnp.dot(table.T, jnp.asarray(onehot), precision=lax.Precision.HIGHEST)
    return vals.reshape((table.shape[1],) + dist.shape)


def _band_vector(table, n_keys, offset, window, shift_far):
    length = n_keys + Q_BLOCK
    m = np.arange(length)
    m = np.where(m < Q_BLOCK, m, m - length)
    dist = m + offset
    ok = dist >= 0 if window is None else (dist >= 0) & (dist < window)
    u = _table_lookup(table, dist)
    if shift_far:
        u = u - table[NUM_BUCKETS - 1][:, None]
    return jnp.where(ok[None, :], u, NEG)


def _toeplitz_kernel(*refs, dest):
    n_tab = len(dest)
    covered = [0] * (len(refs) - n_tab)
    for u_ref, (j, row0) in zip(refs[:n_tab], dest):
        o_ref = refs[n_tab + j]
        n_heads, length = u_ref.shape
        n_keys = length - Q_BLOCK
        covered[j] = max(covered[j], row0 + n_keys)
        u2 = jnp.concatenate([u_ref[...], u_ref[...]], axis=1)
        for r0 in range(0, n_keys, Q_BLOCK):
            w0 = (-r0 - (Q_BLOCK - 1)) % length
            win = u2[:, w0:w0 + 2 * Q_BLOCK]
            for h in range(n_heads):
                x = jnp.broadcast_to(win[h:h + 1, :], (Q_BLOCK, 2 * Q_BLOCK))
                x = pltpu.roll(x, Q_BLOCK + 1, 1, stride=1, stride_axis=0)
                o_ref[row0 + r0:row0 + r0 + Q_BLOCK, h * Q_BLOCK:(h + 1) * Q_BLOCK] = x[:, :Q_BLOCK]
    for o_ref, done in zip(refs[n_tab:], covered):
        if o_ref.shape[0] > done:
            o_ref[done:, :] = jnp.full((o_ref.shape[0] - done, o_ref.shape[1]), NEG, F32)


def _toeplitz_tables(us, dest, n_rows):
    n_heads = us[0].shape[0]
    kern = functools.partial(_toeplitz_kernel, dest=tuple(dest))
    return pl.pallas_call(
        kern,
        out_shape=[jax.ShapeDtypeStruct((r, n_heads * Q_BLOCK), F32) for r in n_rows],
        compiler_params=pltpu.CompilerParams(vmem_limit_bytes=VMEM_LIMIT),
        name="bias_tables",
    )(*us)


def _cmp_bias_rows(table):
    n = MAX_DISTANCE // CMP_STRIDE
    q = np.arange(Q_BLOCK)
    rho = (q - (CMP_BLOCK - 1)) % CMP_STRIDE
    dist = CMP_STRIDE * np.arange(n + 1)[:, None] + rho[None, :]
    vals = _table_lookup(table, dist)
    vals = jnp.transpose(vals, (1, 0, 2)).reshape(n + 1, table.shape[1] * Q_BLOCK)
    vals = jnp.concatenate([vals[:n] - vals[n:], vals[n:]], axis=0)
    return jnp.pad(vals, ((0, n - 1), (0, 0)))


def _overlap_t(n_cmp_pad, n_slc, n_cmp):
    r = SLC_BLOCK // CMP_STRIDE
    c = CMP_BLOCK // CMP_STRIDE
    j, m, n = np.meshgrid(np.arange(n_slc), np.arange(r), np.arange(c), indexing='ij')
    i = r * j + m - n
    ok = (i >= 0) & (i < n_cmp)
    mat = np.zeros((n_slc, n_cmp_pad), np.float32)
    np.add.at(mat, (j[ok], i[ok]), 1.0)
    return mat


def _rms(x, g):
    return x * lax.rsqrt(jnp.mean(x * x, axis=-1, keepdims=True) + EPS) * g


PROJ_CHUNK = 256
VT_A, VT_SL, VT_W = 0, 1, 2
QT_A, QT_B = 0, 1
KK_A, KK_W = 0, 1


def _proj_kernel(x_ref, g_ref, w_ref, qt_ref, kk_ref, kaug_ref, vt_ref, o32_ref, *, seq):
    tm = x_ref.shape[0]
    kv = B_KV_GROUPS * HEAD_DIM
    qw = qt_ref.shape[2]
    hb = _rms(x_ref[...], g_ref[...]).astype(BF16)

    def cols(c0, width=PROJ_CHUNK):
        return jnp.dot(hb, w_ref[:, c0:c0 + width], preferred_element_type=F32)

    def put_transposed(ref, a, r, row0=0):
        for j in range(tm // Q_BLOCK):
            ref[a, j, row0:row0 + r.shape[1], :] = jnp.transpose(
                r[j * Q_BLOCK:(j + 1) * Q_BLOCK, :]).astype(BF16)

    for c0 in range(0, 2 * qw, PROJ_CHUNK):
        put_transposed(qt_ref, c0 // qw, cols(c0) * Q_SCALE, c0 % qw)
    kk_ref[...] = cols(2 * qw).astype(BF16)
    r = cols(2 * qw + 2 * kv)
    pos = (pl.program_id(0) * tm) % seq + lax.broadcasted_iota(jnp.int32, (tm, kv), 0)
    onehot = lax.broadcasted_iota(jnp.int32, (tm, kv), 1) == jnp.right_shift(pos, SLC_SHIFT)
    kaug_ref[:, :kv] = r[:, :kv].astype(BF16)
    kaug_ref[:, kv:] = jnp.where(onehot, 1.0, 0.0).astype(BF16)
    put_transposed(vt_ref, VT_A, r[:, kv:])
    r = cols(2 * qw + 4 * kv)
    put_transposed(vt_ref, VT_SL, r[:, :kv])
    put_transposed(vt_ref, VT_W, r[:, kv:])
    o32_ref[:, :2 * kv] = cols(2 * qw + 6 * kv)
    o32_ref[:, 2 * kv:] = jax.nn.sigmoid(cols(2 * qw + 8 * kv, V7X_LANES))


def _proj(x2d, g, w1, qw, seq, tm):
    n, d = x2d.shape
    kv = B_KV_GROUPS * HEAD_DIM
    n32 = 2 * kv + V7X_LANES
    nb = tm // Q_BLOCK
    assert AUG_LANES == 2 * kv == PROJ_CHUNK and seq % tm == 0 and tm % Q_BLOCK == 0
    rows = lambda width: pl.BlockSpec((tm, width), lambda i: (i, 0))
    blocks_t = lambda a, width: pl.BlockSpec((a, nb, width, Q_BLOCK), lambda i: (0, i, 0, 0))
    kern = functools.partial(_proj_kernel, seq=seq)
    return pl.pallas_call(
        kern,
        grid=(n // tm,),
        in_specs=[rows(d), pl.BlockSpec((1, d), lambda i: (0, 0)),
                  pl.BlockSpec(w1.shape, lambda i: (0, 0))],
        out_specs=[blocks_t(2, qw), rows(2 * kv), rows(AUG_LANES), blocks_t(3, kv), rows(n32)],
        out_shape=[jax.ShapeDtypeStruct((2, n // Q_BLOCK, qw, Q_BLOCK), BF16),
                   jax.ShapeDtypeStruct((n, 2 * kv), BF16),
                   jax.ShapeDtypeStruct((n, AUG_LANES), BF16),
                   jax.ShapeDtypeStruct((3, n // Q_BLOCK, kv, Q_BLOCK), BF16),
                   jax.ShapeDtypeStruct((n, n32), F32)],
        compiler_params=_cparams(("parallel",)),
        name="proj",
    )(x2d, g, w1)


CMP_PAIR = 2


def _compress_kernel(x_ref, pos_ref, w1_ref, w2_ref, o_ref, *, n_cmp, transpose_out):
    ncp = x_ref.shape[0] // CMP_STRIDE

    def group_diag(w):
        zero = jnp.zeros(w.shape, w.dtype)
        return jnp.concatenate(
            [jnp.concatenate([w if g == gg else zero for gg in range(B_KV_GROUPS)], axis=1)
             for g in range(B_KV_GROUPS)], axis=0)

    halves = []
    for half in range(CMP_BLOCK // CMP_STRIDE):
        acc = None
        for l0 in range(half * CMP_STRIDE, (half + 1) * CMP_STRIDE, CMP_PAIR):
            lhs = jnp.concatenate(
                [(x_ref[pl.ds(l % CMP_STRIDE, ncp, stride=CMP_STRIDE), :]
                  + jnp.concatenate([pos_ref[l:l + 1, :]] * B_KV_GROUPS, axis=1)).astype(BF16)
                 for l in range(l0, l0 + CMP_PAIR)], axis=1)
            rhs = jnp.concatenate([group_diag(w1_ref[l]) for l in range(l0, l0 + CMP_PAIR)], axis=0)
            part = jnp.dot(lhs, rhs, preferred_element_type=F32)
            acc = part if acc is None else acc + part
        halves.append(acc)
    top, bottom = halves
    h1 = top + jnp.concatenate([bottom[1:], bottom[:1]], axis=0)
    o = jnp.dot(jax.nn.gelu(h1).astype(BF16), group_diag(w2_ref[...]), preferred_element_type=F32)
    row = lax.broadcasted_iota(jnp.int32, o.shape, 0)
    o = jnp.where(row < n_cmp, o, 0.0)
    if transpose_out:
        o = jnp.transpose(o)
    o_ref[...] = o.astype(o_ref.dtype)


def _compress(p32, col, pos, w1, w2, n_cmp, transpose_out):
    bsz, seq, _ = p32.shape
    ncp = seq // CMP_STRIDE
    gd = B_KV_GROUPS * HEAD_DIM
    w1 = w1.reshape(CMP_BLOCK, HEAD_DIM, -1).astype(BF16)
    w2 = w2.astype(BF16)
    oshape = (gd, ncp) if transpose_out else (ncp, gd)
    const = lambda a: pl.BlockSpec(a.shape, lambda b: (0,) * a.ndim)
    kern = functools.partial(_compress_kernel, n_cmp=n_cmp, transpose_out=transpose_out)
    return pl.pallas_call(
        kern,
        grid=(bsz,),
        in_specs=[pl.BlockSpec((None, seq, gd), lambda b: (b, 0, col // gd)),
                  const(pos), const(w1), const(w2)],
        out_specs=pl.BlockSpec((None,) + oshape, lambda b: (b, 0, 0)),
        out_shape=jax.ShapeDtypeStruct((bsz,) + oshape, BF16),
        compiler_params=_cparams(("parallel",)),
        name="compress",
    )(p32, pos, w1, w2)


def _block_diag_qt(qt):
    zero = jnp.zeros((HEAD_DIM, Q_BLOCK), BF16)
    n_groups = qt.shape[0] // (REP * HEAD_DIM)
    return jnp.concatenate([
        jnp.concatenate([qt[(REP * g + r) * HEAD_DIM:(REP * g + r + 1) * HEAD_DIM] if gg == g else zero
                         for gg in range(n_groups) for r in range(REP)], axis=1)
        for g in range(n_groups)], axis=0)


def _pv_by_group(vt, p, with_sum):
    n_groups = vt.shape[0] // HEAD_DIM
    gl = p.shape[1] // n_groups
    pb = p.astype(BF16)
    outs, sums = [], []
    for g in range(n_groups):
        lhs = vt[g * HEAD_DIM:(g + 1) * HEAD_DIM]
        if with_sum:
            lhs = jnp.concatenate([lhs, jnp.ones((SUM_ROWS, vt.shape[1]), BF16)], axis=0)
        r = jnp.dot(lhs, pb[:, g * gl:(g + 1) * gl], preferred_element_type=F32)
        outs.append(r[:HEAD_DIM])
        sums.append(r[HEAD_DIM:HEAD_DIM + 1])
    return outs, (jnp.concatenate(sums, axis=1) if with_sum else None)


def _scale_groups(outs, row):
    gl = outs[0].shape[1]
    return [o * row[:, g * gl:(g + 1) * gl] for g, o in enumerate(outs)]


def _head_block(outs, h):
    r = h % REP
    return outs[h // REP][:, r * Q_BLOCK:(r + 1) * Q_BLOCK]


def _banded_scores(i, wq, k_ref, bias_ref, window):
    n_blk = window // Q_BLOCK + 1
    span = n_blk * Q_BLOCK
    kb0 = jnp.maximum(i - (n_blk - 1), 0)
    shift = jnp.maximum(n_blk - 1 - i, 0)
    s = jnp.dot(k_ref[pl.ds(pl.multiple_of(kb0 * Q_BLOCK, Q_BLOCK), span), :], wq,
                preferred_element_type=F32)
    return s + bias_ref[pl.ds(pl.multiple_of(shift * Q_BLOCK, Q_BLOCK), span), :], kb0


def _swa_block(i, s, kb0, vt_ref, sink_ref):
    sink = sink_ref[...]
    m = jnp.maximum(jnp.max(s, axis=0, keepdims=True), sink)
    p = jnp.exp2(s - m)
    vt = jnp.concatenate([vt_ref[kb0 + j] for j in range(A_WINDOW // Q_BLOCK + 1)], axis=1)
    outs, l = _pv_by_group(vt, p, True)
    outs = _scale_groups(outs, 1.0 / (l + jnp.exp2(sink - m)))
    heads = [_head_block(outs, h) for h in range(A_HEADS)]
    return jnp.transpose(jnp.concatenate(heads, axis=0))


SEL_TILE = 2 * Q_BLOCK
AUG_LANES = 2 * V7X_LANES
CMP_SHIFT = 4
CMP_PER_QB = Q_BLOCK // CMP_STRIDE
CMP_NEAR = MAX_DISTANCE // CMP_STRIDE
CMP_BAND = 3 * CMP_PER_QB


def _mixers_kernel(qa_ref, ka_ref, vat_ref, ba_ref, sink_ref,
                   q_ref, kcmp_ref, vcmpt_ref, kaug_ref, vslt_ref, kw_ref, vwt_ref, gate_ref,
                   fc_ref, bw_ref, bn_ref, ovt_ref, oa_ref, o_ref, sc_ref, s0_ref, s1_ref, *, topk):
    i = pl.program_id(1)
    n_slc, ncp = ovt_ref.shape
    gl = REP * Q_BLOCK
    nl = B_KV_GROUPS * gl
    gd = B_KV_GROUPS * HEAD_DIM
    n_win = NSA_WINDOW // Q_BLOCK + 1

    wq = _block_diag_qt(q_ref[...])

    qlane = lax.broadcasted_iota(jnp.int32, (1, nl), 1) & (Q_BLOCK - 1)
    cmax = jnp.right_shift(qlane + i * Q_BLOCK - (CMP_BLOCK - 1), CMP_SHIFT)
    valid_c = lax.broadcasted_iota(jnp.int32, (ncp, nl), 0) <= cmax
    s = jnp.dot(kcmp_ref[...], wq, preferred_element_type=F32) + fc_ref[CMP_NEAR:CMP_NEAR + 1, :]
    sc_ref[...] = jnp.where(valid_c, s, NEG)
    b0 = pl.multiple_of(jnp.clip(CMP_PER_QB * (i - 2), 0, ncp - CMP_BAND), V7X_SUBLANES)
    kkb = cmax - (b0 + lax.broadcasted_iota(jnp.int32, (CMP_BAND, nl), 0))
    delta = jnp.zeros((CMP_BAND, nl), F32)
    for k in range(CMP_NEAR):
        delta = jnp.where(kkb == k, fc_ref[k:k + 1, :], delta)
    sc_ref[pl.ds(b0, CMP_BAND), :] = sc_ref[pl.ds(b0, CMP_BAND), :] + delta
    s = sc_ref[...]
    m = jnp.max(s, axis=0, keepdims=True)
    p = jnp.exp2(s - m)
    l = jnp.sum(p, axis=0, keepdims=True)
    p = p * jnp.where(cmax >= 0, 1.0 / l, 0.0)
    o_c, _ = _pv_by_group(vcmpt_ref[...], p, False)
    psum = jnp.concatenate(
        [sum(p[:, g * gl + r * Q_BLOCK:g * gl + (r + 1) * Q_BLOCK] for r in range(REP))
         for g in range(B_KV_GROUPS)], axis=1)
    imp = jnp.dot(ovt_ref[...], psum, preferred_element_type=F32,
                  precision=lax.Precision.HIGHEST)

    sa, kb0_a = _banded_scores(i, _block_diag_qt(qa_ref[...]), ka_ref, ba_ref, A_WINDOW)
    oa_ref[...] = _swa_block(i, sa, kb0_a, vat_ref, sink_ref).astype(oa_ref.dtype)

    sw, kb0 = _banded_scores(i, wq, kw_ref, bw_ref, NSA_WINDOW)
    mw = jnp.max(sw, axis=0, keepdims=True)
    pw = jnp.exp2(sw - mw)
    vwt = jnp.concatenate([vwt_ref[kb0 + j] for j in range(n_win)], axis=1)
    o_w, l_w = _pv_by_group(vwt, pw, True)
    o_w = _scale_groups(o_w, 1.0 / l_w)

    sl = B_KV_GROUPS * Q_BLOCK
    jt = lax.broadcasted_iota(jnp.int32, (n_slc, sl), 0)
    second_half = (lax.broadcasted_iota(jnp.int32, (1, sl), 1) & (Q_BLOCK - 1)) >= SLC_BLOCK
    qblk = 2 * i + second_half.astype(jnp.int32)
    forced = (jt == 0) | (jt == qblk) | (jt == qblk - 1)
    score = jnp.where(forced, BIG, jnp.where(jt > qblk, NEG, imp))
    rank = jnp.zeros((n_slc, sl), jnp.int32)
    for c in range(n_slc):
        row = score[c:c + 1, :]
        ahead = (row > score) | ((row == score) & (jt > c))
        rank = rank + ahead.astype(jnp.int32)
    selneg = jnp.where(rank < topk, 0.0, NEG).astype(BF16)
    wsel = jnp.concatenate([selneg[:, g * Q_BLOCK:(g + 1) * Q_BLOCK]
                            for g in range(B_KV_GROUPS) for _ in range(REP)], axis=1)
    w = jnp.concatenate([wq, wsel, jnp.zeros((AUG_LANES - gd - n_slc, nl), BF16)], axis=0)

    def scores(t):
        r0 = pl.multiple_of(t * SEL_TILE, SEL_TILE)
        return jnp.dot(kaug_ref[pl.ds(r0, SEL_TILE), :], w, preferred_element_type=F32)

    def update(carry, st, t):
        m_i, l_i, a0, a1 = carry
        m_n = jnp.maximum(m_i, jnp.max(st, axis=0, keepdims=True))
        alpha = jnp.exp2(m_i - m_n)
        pt = jnp.exp2(st - m_n)
        blk0 = t * (SEL_TILE // Q_BLOCK)
        vt = jnp.concatenate([vslt_ref[blk0 + j] for j in range(SEL_TILE // Q_BLOCK)], axis=1)
        pv, l_t = _pv_by_group(vt, pt, True)
        a0, a1 = (a + b for a, b in zip(_scale_groups((a0, a1), alpha), pv))
        return m_n, alpha * l_i + l_t, a0, a1

    last = i // 2
    par = i % 2
    n_far = jnp.maximum(last - 1, 0)
    carry = (jnp.full((1, nl), NEG, F32), jnp.zeros((1, nl), F32),
             jnp.zeros((HEAD_DIM, gl), F32), jnp.zeros((HEAD_DIM, gl), F32))
    s0_ref[...] = scores(0)

    def two_tiles(u, carry):
        t = 2 * u
        s1_ref[...] = scores(t + 1)
        carry = update(carry, s0_ref[...], t)
        s0_ref[...] = scores(t + 2)
        return update(carry, s1_ref[...], t + 1)

    def one_tile(t, carry):
        s1_ref[...] = scores(t + 1)
        carry = update(carry, s0_ref[...], t)
        s0_ref[...] = s1_ref[...]
        return carry

    carry = lax.fori_loop(0, n_far // 2, two_tiles, carry)
    carry = lax.fori_loop(n_far // 2 * 2, n_far, one_tile, carry)
    s1_ref[...] = scores(last)
    near0 = pl.multiple_of(par * (2 * SEL_TILE), SEL_TILE)
    before = bn_ref[pl.ds(near0, SEL_TILE), :] + jnp.where(last > 0, 0.0, NEG)
    carry = update(carry, s0_ref[...] + before, jnp.maximum(last - 1, 0))
    m_s, l_s, a0, a1 = update(carry, s1_ref[...] + bn_ref[pl.ds(near0 + SEL_TILE, SEL_TILE), :], last)
    o_s = _scale_groups((a0, a1), 1.0 / l_s)

    gt = jnp.transpose(gate_ref[...])
    heads = [gt[3 * h:3 * h + 1, :] * _head_block(o_c, h)
             + gt[3 * h + 1:3 * h + 2, :] * _head_block(o_s, h)
             + gt[3 * h + 2:3 * h + 3, :] * _head_block(o_w, h) for h in range(B_HEADS)]
    o_ref[...] = jnp.transpose(jnp.concatenate(heads, axis=0)).astype(o_ref.dtype)


def _mixers_call(qt, kk, vt, p32, kcmp, vcmpt, kaug, ba, sink_row, fc, bw, bn, ovt, bsz, seq,
                 col_gate, topk):
    nb = seq // Q_BLOCK
    assert A_HEADS == B_HEADS and A_KV_HEADS == B_KV_GROUPS
    qw = B_HEADS * HEAD_DIM
    kw = B_KV_GROUPS * HEAD_DIM
    per_batch = lambda a: pl.BlockSpec((None,) + a.shape[1:], lambda b, i: (b,) + (0,) * (a.ndim - 1))
    const = lambda a: pl.BlockSpec(a.shape, lambda b, i: (0,) * a.ndim)
    qt_seg = lambda a: pl.BlockSpec((None, None, None, qw, Q_BLOCK), lambda b, i: (a, b, i, 0, 0))
    kk_seg = lambda a: pl.BlockSpec((None, seq, kw), lambda b, i: (b, 0, a))
    vt_seg = lambda a: pl.BlockSpec((None, None) + vt.shape[2:], lambda b, i: (a, b, 0, 0, 0))
    y_spec = pl.BlockSpec((None, Q_BLOCK, qw), lambda b, i: (b, i, 0))
    kern = functools.partial(_mixers_kernel, topk=topk)
    return pl.pallas_call(
        kern,
        grid=(bsz, nb),
        in_specs=[qt_seg(QT_A), kk_seg(KK_A), vt_seg(VT_A), const(ba), const(sink_row),
                  qt_seg(QT_B), per_batch(kcmp), per_batch(vcmpt), per_batch(kaug), vt_seg(VT_SL),
                  kk_seg(KK_W), vt_seg(VT_W),
                  pl.BlockSpec((None, Q_BLOCK, V7X_LANES), lambda b, i: (b, i, col_gate // V7X_LANES)),
                  const(fc), const(bw), const(bn), const(ovt)],
        out_specs=[y_spec, y_spec],
        out_shape=[jax.ShapeDtypeStruct((bsz, seq, qw), BF16)] * 2,
        scratch_shapes=[pltpu.VMEM((kcmp.shape[1], B_HEADS * Q_BLOCK), F32),
                        pltpu.VMEM((SEL_TILE, B_HEADS * Q_BLOCK), F32),
                        pltpu.VMEM((SEL_TILE, B_HEADS * Q_BLOCK), F32)],
        compiler_params=_cparams(("parallel", "arbitrary")),
        name="mixers",
    )(qt, kk, vt, ba, sink_row, qt, kcmp, vcmpt, kaug, vt, kk, vt, p32, fc, bw, bn, ovt)


def _merge_kernel(x_ref, g_ref, ya_ref, yb_ref, wg_ref, wua_ref, wub_ref, wo_ref, o_ref):
    x = x_ref[...]
    d = x.shape[1]
    hb = _rms(x, g_ref[...]).astype(BF16)
    ga = jax.nn.sigmoid(jnp.dot(hb, wg_ref[:, :d], preferred_element_type=F32))
    gb = jax.nn.sigmoid(jnp.dot(hb, wg_ref[:, d:], preferred_element_type=F32))
    ua = jnp.dot(ya_ref[...], wua_ref[...], preferred_element_type=F32)
    ub = jnp.dot(yb_ref[...], wub_ref[...], preferred_element_type=F32)
    merged = ga * ua + gb * ub
    o_ref[...] = x + jnp.dot(merged.astype(BF16), wo_ref[...], preferred_element_type=F32)


def _merge(x2d, g, ya, yb, wg, wua, wub, wo, tm):
    n, d = x2d.shape
    const = lambda a: pl.BlockSpec(a.shape, lambda i: (0, 0))
    row = lambda a: pl.BlockSpec((tm, a.shape[1]), lambda i: (i, 0))
    return pl.pallas_call(
        _merge_kernel,
        grid=(n // tm,),
        in_specs=[row(x2d), const(g), row(ya), row(yb), const(wg), const(wua), const(wub), const(wo)],
        out_specs=row(x2d),
        out_shape=jax.ShapeDtypeStruct((n, d), F32),
        compiler_params=_cparams(("parallel",)),
        name="merge",
    )(x2d, g, ya, yb, wg, wua, wub, wo)


FFN_HALO = 16
FFN_GROUP = 6


def _ffn_kernel(xc_ref, xp_ref, gn_ref, wi_ref, cw_ref, cb_ref, wo_ref, gf_ref, o_ref, *, chunk):
    i = pl.program_id(1)
    xc = xc_ref[...]
    tm = xc.shape[0]
    gn = gn_ref[...]
    hp = _rms(xp_ref[...], gn) * jnp.where(i > 0, 1.0, 0.0)
    h = jnp.concatenate([hp, _rms(xc, gn)], axis=0).astype(BF16)
    d_ff = wo_ref.shape[0]

    def up(c0):
        return jnp.dot(h, wi_ref[:, c0:c0 + chunk], preferred_element_type=F32)

    def conv(ext, c0):
        cw = cw_ref[:, c0:c0 + chunk]
        out = cb_ref[:, c0:c0 + chunk]
        for k in range(CONV_WIDTH):
            off = FFN_HALO - (CONV_WIDTH - 1) + k
            out = out + cw[k:k + 1, :] * ext[off:off + tm]
        return out

    acc = None
    acts = []
    nxt = (up(0), up(d_ff))
    for c0 in range(0, d_ff, chunk):
        ext_u, ext_g = nxt
        if c0 + chunk < d_ff:
            nxt = (up(c0 + chunk), up(d_ff + c0 + chunk))
        acts.append((jax.nn.silu(conv(ext_g, d_ff + c0)) * conv(ext_u, c0)).astype(BF16))
        if len(acts) == FFN_GROUP or c0 + chunk >= d_ff:
            r0 = c0 + chunk - len(acts) * chunk
            part = jnp.dot(jnp.concatenate(acts, axis=1), wo_ref[r0:c0 + chunk, :],
                           preferred_element_type=F32)
            acc = part if acc is None else acc + part
            acts = []
    o_ref[...] = _rms(xc + acc, gf_ref[...])


def _ffn(x1, gn, wi, cw, cb, wo, gf, tm, chunk):
    bsz, seq, d = x1.shape
    const = lambda a: pl.BlockSpec(a.shape, lambda b, i: (0, 0))
    kern = functools.partial(_ffn_kernel, chunk=chunk)
    return pl.pallas_call(
        kern,
        grid=(bsz, seq // tm),
        in_specs=[pl.BlockSpec((None, tm, d), lambda b, i: (b, i, 0)),
                  pl.BlockSpec((None, FFN_HALO, d),
                               lambda b, i: (b, jnp.maximum(i * (tm // FFN_HALO) - 1, 0), 0)),
                  const(gn), const(wi), const(cw), const(cb), const(wo), const(gf)],
        out_specs=pl.BlockSpec((None, tm, d), lambda b, i: (b, i, 0)),
        out_shape=jax.ShapeDtypeStruct((bsz, seq, d), F32),
        compiler_params=_cparams(("parallel", "arbitrary")),
        name="ffn",
    )(x1, x1, gn, wi, cw, cb, wo, gf)


def _mixers(x, norm_mix, w_in, attn_sinks, cmp_pos_k, cmp_w1_k, cmp_w2_k, cmp_pos_v, cmp_w1_v,
            cmp_w2_v, table):
    bsz, seq, d = x.shape
    n = bsz * seq
    aq, akv = A_HEADS * HEAD_DIM, A_KV_HEADS * HEAD_DIM
    bq, bkv = B_HEADS * HEAD_DIM, B_KV_GROUPS * HEAD_DIM
    n_gate = 3 * B_HEADS
    assert seq % SEL_TILE == 0 and seq // SLC_BLOCK <= AUG_LANES - bkv
    splits = (aq, akv, akv, bq, bkv, bkv, bkv, bkv, bkv, bkv, n_gate, d, d)
    off = np.concatenate([[0], np.cumsum(splits)]).astype(int)
    seg = lambda k: w_in[:, off[k]:off[k + 1]]
    assert akv == bkv and aq == bq
    w_gate_nsa = jnp.pad(seg(10), ((0, 0), (0, V7X_LANES - n_gate)))
    w1 = jnp.concatenate([seg(0), seg(3), seg(1), seg(8), seg(6), seg(2), seg(7), seg(9),
                          seg(4), seg(5), w_gate_nsa], axis=1).astype(BF16)
    tm = min(1024, seq)
    g_mix = norm_mix.reshape(1, d)

    qt, kk, kaug, vt, p32 = _proj(x.reshape(n, d), g_mix, w1, aq, seq, tm)
    nb = seq // Q_BLOCK
    qt = qt.reshape(2, bsz, nb, aq, Q_BLOCK)
    kk = kk.reshape(bsz, seq, 2 * bkv)
    kaug = kaug.reshape(bsz, seq, AUG_LANES)
    vt = vt.reshape(3, bsz, nb, bkv, Q_BLOCK)
    p32 = p32.reshape(bsz, seq, 2 * bkv + V7X_LANES)

    table = table * LOG2E
    table_a, table_b = table[:, :A_HEADS], table[:, A_HEADS:]
    sink_row = jnp.repeat(attn_sinks * LOG2E, Q_BLOCK).reshape(1, A_HEADS * Q_BLOCK)

    ncp = seq // CMP_STRIDE
    n_cmp = (seq - CMP_BLOCK) // CMP_STRIDE + 1
    n_slc = seq // SLC_BLOCK
    topk = min(SLC_TOPK, n_slc)

    k_cmp = _compress(p32, 0, cmp_pos_k, cmp_w1_k, cmp_w2_k, n_cmp, False)
    v_cmp_t = _compress(p32, bkv, cmp_pos_v, cmp_w1_v, cmp_w2_v, n_cmp, True)

    fc = _cmp_bias_rows(table_b)
    win_a, win_b = A_WINDOW + Q_BLOCK, NSA_WINDOW + Q_BLOCK
    bias_a, bw, bn = _toeplitz_tables(
        [_band_vector(table_a, win_a, A_WINDOW, A_WINDOW, False),
         _band_vector(table_b, win_b, NSA_WINDOW, NSA_WINDOW, False),
         _band_vector(table_b, 2 * SEL_TILE, SEL_TILE, None, True),
         _band_vector(table_b, 2 * SEL_TILE, SEL_TILE + Q_BLOCK, None, True)],
        [(0, 0), (1, 0), (2, 0), (2, 2 * SEL_TILE)],
        [win_a + A_WINDOW, win_b + NSA_WINDOW, 4 * SEL_TILE])
    ovt = jnp.asarray(_overlap_t(ncp, n_slc, n_cmp))
    y_a, y_b = _mixers_call(qt, kk, vt, p32, k_cmp, v_cmp_t, kaug, bias_a, sink_row, fc, bw, bn, ovt,
                            bsz, seq, 2 * bkv, topk)
    return y_a, y_b, w_in[:, off[11]:off[13]]


def _layer(x, norm_mix, w_in, attn_sinks, cmp_pos_k, cmp_w1_k, cmp_w2_k, cmp_pos_v, cmp_w1_v,
           cmp_w2_v, w_up_a, w_up_b, w_out, norm_ffn, w_ffn_in, conv_w, conv_b, w_ffn_out,
           table, norm_final):
    bsz, seq, d = x.shape
    n = bsz * seq
    y_a, y_b, w_merge_gates = _mixers(x, norm_mix, w_in, attn_sinks, cmp_pos_k, cmp_w1_k, cmp_w2_k,
                                      cmp_pos_v, cmp_w1_v, cmp_w2_v, table)

    x1 = _merge(x.reshape(n, d), norm_mix.reshape(1, d), y_a.reshape(n, -1), y_b.reshape(n, -1),
                w_merge_gates.astype(BF16), w_up_a.astype(BF16), w_up_b.astype(BF16),
                w_out.astype(BF16), min(1024, n))

    return _ffn(x1.reshape(bsz, seq, d), norm_ffn.reshape(1, d), w_ffn_in.astype(BF16), conv_w,
                conv_b.reshape(1, -1), w_ffn_out.astype(BF16), norm_final.reshape(1, d),
                min(512, seq), 256)


def kernel(x, norm_mix, w_in, attn_sinks, cmp_pos_k, cmp_w1_k, cmp_w2_k, cmp_pos_v, cmp_w1_v, cmp_w2_v, w_up_a, w_up_b, w_out, norm_ffn, w_ffn_in, conv_w, conv_b, w_ffn_out, rel_bias_table, norm_final):
    assert norm_mix.shape[0] == 1, "single-layer block"
    return _layer(x, norm_mix[0], w_in[0], attn_sinks[0], cmp_pos_k[0], cmp_w1_k[0], cmp_w2_k[0],
                  cmp_pos_v[0], cmp_w1_v[0], cmp_w2_v[0], w_up_a[0], w_up_b[0], w_out[0],
                  norm_ffn[0], w_ffn_in[0], conv_w[0], conv_b[0], w_ffn_out[0], rel_bias_table,
                  norm_final)
```

```python
import functools
import math

import numpy as np
import jax
import jax.numpy as jnp
from jax import lax
from jax.experimental import pallas as pl
from jax.experimental.pallas import tpu as pltpu

F32 = jnp.float32
BF16 = jnp.bfloat16

HEAD_DIM = 64
A_HEADS = 8
A_KV_HEADS = 2
A_WINDOW = 128
B_HEADS = 8
B_KV_GROUPS = 2
REP = 4
CMP_BLOCK = 32
CMP_STRIDE = 16
SLC_BLOCK = 64
SLC_SHIFT = 6
SLC_TOPK = 16
NSA_WINDOW = 512
NUM_BUCKETS = 32
MAX_DISTANCE = 128
CONV_WIDTH = 3
Q_BLOCK = 128
EPS = 1e-6
NEG = -1e30
BIG = 1e30
SCALE = HEAD_DIM ** -0.5
LOG2E = math.log2(math.e)
Q_SCALE = SCALE * LOG2E
SUM_ROWS = 16

V7X_LANES = 128
V7X_SUBLANES = 8
V7X_VMEM_BYTES = 64 * 1024 * 1024
VMEM_LIMIT = 56 * 1024 * 1024


def _cparams(semantics):
    return pltpu.CompilerParams(dimension_semantics=semantics, vmem_limit_bytes=VMEM_LIMIT)


def _bucket_np(dist):
    dist = np.maximum(dist, 0)
    max_exact = NUM_BUCKETS // 2
    d = np.maximum(dist, 1).astype(np.float64)
    large = max_exact + (np.log(d / max_exact) / math.log(MAX_DISTANCE / max_exact)
                         * (NUM_BUCKETS - max_exact)).astype(np.int32)
    large = np.minimum(large, NUM_BUCKETS - 1)
    return np.where(dist < max_exact, dist, large).astype(np.int32)


def _table_lookup(table, dist):
    idx = _bucket_np(dist).reshape(-1)
    onehot = np.zeros((NUM_BUCKETS, idx.size), np.float32)
    onehot[idx, np.arange(idx.size)] = 1.0
    vals = jnp.dot(table.T, jnp.asarray(onehot), precision=lax.Precision.HIGHEST)
    return vals.reshape((table.shape[1],) + dist.shape)


def _band_vector(table, n_keys, offset, window, shift_far):
    length = n_keys + Q_BLOCK
    m = np.arange(length)
    m = np.where(m < Q_BLOCK, m, m - length)
    dist = m + offset
    ok = dist >= 0 if window is None else (dist >= 0) & (dist < window)
    u = _table_lookup(table, dist)
    if shift_far:
        u = u - table[NUM_BUCKETS - 1][:, None]
    return jnp.where(ok[None, :], u, NEG)


def _toeplitz_kernel(*refs, dest):
    n_tab = len(dest)
    covered = [0] * (len(refs) - n_tab)
    for u_ref, (j, row0) in zip(refs[:n_tab], dest):
        o_ref = refs[n_tab + j]
        n_heads, length = u_ref.shape
        n_keys = length - Q_BLOCK
        covered[j] = max(covered[j], row0 + n_keys)
        u2 = jnp.concatenate([u_ref[...], u_ref[...]], axis=1)
        for r0 in range(0, n_keys, Q_BLOCK):
            w0 = (-r0 - (Q_BLOCK - 1)) % length
            win = u2[:, w0:w0 + 2 * Q_BLOCK]
            for h in range(n_heads):
                x = jnp.broadcast_to(win[h:h + 1, :], (Q_BLOCK, 2 * Q_BLOCK))
                x = pltpu.roll(x, Q_BLOCK + 1, 1, stride=1, stride_axis=0)
                o_ref[row0 + r0:row0 + r0 + Q_BLOCK, h * Q_BLOCK:(h + 1) * Q_BLOCK] = x[:, :Q_BLOCK]
    for o_ref, done in zip(refs[n_tab:], covered):
        if o_ref.shape[0] > done:
            o_ref[done:, :] = jnp.full((o_ref.shape[0] - done, o_ref.shape[1]), NEG, F32)


def _toeplitz_tables(us, dest, n_rows):
    n_heads = us[0].shape[0]
    kern = functools.partial(_toeplitz_kernel, dest=tuple(dest))
    return pl.pallas_call(
        kern,
        out_shape=[jax.ShapeDtypeStruct((r, n_heads * Q_BLOCK), F32) for r in n_rows],
        compiler_params=pltpu.CompilerParams(vmem_limit_bytes=VMEM_LIMIT),
        name="bias_tables",
    )(*us)


def _cmp_bias_rows(table):
    n = MAX_DISTANCE // CMP_STRIDE
    q = np.arange(Q_BLOCK)
    rho = (q - (CMP_BLOCK - 1)) % CMP_STRIDE
    dist = CMP_STRIDE * np.arange(n + 1)[:, None] + rho[None, :]
    vals = _table_lookup(table, dist)
    vals = jnp.transpose(vals, (1, 0, 2)).reshape(n + 1, table.shape[1] * Q_BLOCK)
    vals = jnp.concatenate([vals[:n] - vals[n:], vals[n:]], axis=0)
    return jnp.pad(vals, ((0, n - 1), (0, 0)))


def _overlap_t(n_cmp_pad, n_slc, n_cmp):
    r = SLC_BLOCK // CMP_STRIDE
    c = CMP_BLOCK // CMP_STRIDE
    j, m, n = np.meshgrid(np.arange(n_slc), np.arange(r), np.arange(c), indexing='ij')
    i = r * j + m - n
    ok = (i >= 0) & (i < n_cmp)
    mat = np.zeros((n_slc, n_cmp_pad), np.float32)
    np.add.at(mat, (j[ok], i[ok]), 1.0)
    return mat


def _rms(x, g):
    return x * lax.rsqrt(jnp.mean(x * x, axis=-1, keepdims=True) + EPS) * g


PROJ_CHUNK = 256
VT_A, VT_SL, VT_W = 0, 1, 2
QT_A, QT_B = 0, 1
KK_A, KK_W = 0, 1


def _proj_kernel(x_ref, g_ref, w_ref, qt_ref, kk_ref, kaug_ref, vt_ref, o32_ref, *, seq):
    tm = x_ref.shape[0]
    kv = B_KV_GROUPS * HEAD_DIM
    qw = qt_ref.shape[2]
    hb = _rms(x_ref[...], g_ref[...]).astype(BF16)

    def cols(c0, width=PROJ_CHUNK):
        return jnp.dot(hb, w_ref[:, c0:c0 + width], preferred_element_type=F32)

    def put_transposed(ref, a, r, row0=0):
        for j in range(tm // Q_BLOCK):
            ref[a, j, row0:row0 + r.shape[1], :] = jnp.transpose(
                r[j * Q_BLOCK:(j + 1) * Q_BLOCK, :]).astype(BF16)

    for c0 in range(0, 2 * qw, PROJ_CHUNK):
        put_transposed(qt_ref, c0 // qw, cols(c0) * Q_SCALE, c0 % qw)
    kk_ref[...] = cols(2 * qw).astype(BF16)
    r = cols(2 * qw + 2 * kv)
    pos = (pl.program_id(0) * tm) % seq + lax.broadcasted_iota(jnp.int32, (tm, kv), 0)
    onehot = lax.broadcasted_iota(jnp.int32, (tm, kv), 1) == jnp.right_shift(pos, SLC_SHIFT)
    kaug_ref[:, :kv] = r[:, :kv].astype(BF16)
    kaug_ref[:, kv:] = jnp.where(onehot, 1.0, 0.0).astype(BF16)
    put_transposed(vt_ref, VT_A, r[:, kv:])
    r = cols(2 * qw + 4 * kv)
    put_transposed(vt_ref, VT_SL, r[:, :kv])
    put_transposed(vt_ref, VT_W, r[:, kv:])
    o32_ref[:, :2 * kv] = cols(2 * qw + 6 * kv)
    o32_ref[:, 2 * kv:] = jax.nn.sigmoid(cols(2 * qw + 8 * kv, V7X_LANES))


def _proj(x2d, g, w1, qw, seq, tm):
    n, d = x2d.shape
    kv = B_KV_GROUPS * HEAD_DIM
    n32 = 2 * kv + V7X_LANES
    nb = tm // Q_BLOCK
    assert AUG_LANES == 2 * kv == PROJ_CHUNK and seq % tm == 0 and tm % Q_BLOCK == 0
    rows = lambda width: pl.BlockSpec((tm, width), lambda i: (i, 0))
    blocks_t = lambda a, width: pl.BlockSpec((a, nb, width, Q_BLOCK), lambda i: (0, i, 0, 0))
    kern = functools.partial(_proj_kernel, seq=seq)
    return pl.pallas_call(
        kern,
        grid=(n // tm,),
        in_specs=[rows(d), pl.BlockSpec((1, d), lambda i: (0, 0)),
                  pl.BlockSpec(w1.shape, lambda i: (0, 0))],
        out_specs=[blocks_t(2, qw), rows(2 * kv), rows(AUG_LANES), blocks_t(3, kv), rows(n32)],
        out_shape=[jax.ShapeDtypeStruct((2, n // Q_BLOCK, qw, Q_BLOCK), BF16),
                   jax.ShapeDtypeStruct((n, 2 * kv), BF16),
                   jax.ShapeDtypeStruct((n, AUG_LANES), BF16),
                   jax.ShapeDtypeStruct((3, n // Q_BLOCK, kv, Q_BLOCK), BF16),
                   jax.ShapeDtypeStruct((n, n32), F32)],
        compiler_params=_cparams(("parallel",)),
        name="proj",
    )(x2d, g, w1)


CMP_PAIR = 2


def _compress_kernel(x_ref, pos_ref, w1_ref, w2_ref, o_ref, *, n_cmp, transpose_out):
    ncp = x_ref.shape[0] // CMP_STRIDE

    def group_diag(w):
        zero = jnp.zeros(w.shape, w.dtype)
        return jnp.concatenate(
            [jnp.concatenate([w if g == gg else zero for gg in range(B_KV_GROUPS)], axis=1)
             for g in range(B_KV_GROUPS)], axis=0)

    halves = []
    for half in range(CMP_BLOCK // CMP_STRIDE):
        acc = None
        for l0 in range(half * CMP_STRIDE, (half + 1) * CMP_STRIDE, CMP_PAIR):
            lhs = jnp.concatenate(
                [(x_ref[pl.ds(l % CMP_STRIDE, ncp, stride=CMP_STRIDE), :]
                  + jnp.concatenate([pos_ref[l:l + 1, :]] * B_KV_GROUPS, axis=1)).astype(BF16)
                 for l in range(l0, l0 + CMP_PAIR)], axis=1)
            rhs = jnp.concatenate([group_diag(w1_ref[l]) for l in range(l0, l0 + CMP_PAIR)], axis=0)
            part = jnp.dot(lhs, rhs, preferred_element_type=F32)
            acc = part if acc is None else acc + part
        halves.append(acc)
    top, bottom = halves
    h1 = top + jnp.concatenate([bottom[1:], bottom[:1]], axis=0)
    o = jnp.dot(jax.nn.gelu(h1).astype(BF16), group_diag(w2_ref[...]), preferred_element_type=F32)
    row = lax.broadcasted_iota(jnp.int32, o.shape, 0)
    o = jnp.where(row < n_cmp, o, 0.0)
    if transpose_out:
        o = jnp.transpose(o)
    o_ref[...] = o.astype(o_ref.dtype)


def _compress(p32, col, pos, w1, w2, n_cmp, transpose_out):
    bsz, seq, _ = p32.shape
    ncp = seq // CMP_STRIDE
    gd = B_KV_GROUPS * HEAD_DIM
    w1 = w1.reshape(CMP_BLOCK, HEAD_DIM, -1).astype(BF16)
    w2 = w2.astype(BF16)
    oshape = (gd, ncp) if transpose_out else (ncp, gd)
    const = lambda a: pl.BlockSpec(a.shape, lambda b: (0,) * a.ndim)
    kern = functools.partial(_compress_kernel, n_cmp=n_cmp, transpose_out=transpose_out)
    return pl.pallas_call(
        kern,
        grid=(bsz,),
        in_specs=[pl.BlockSpec((None, seq, gd), lambda b: (b, 0, col // gd)),
                  const(pos), const(w1), const(w2)],
        out_specs=pl.BlockSpec((None,) + oshape, lambda b: (b, 0, 0)),
        out_shape=jax.ShapeDtypeStruct((bsz,) + oshape, BF16),
        compiler_params=_cparams(("parallel",)),
        name="compress",
    )(p32, pos, w1, w2)


def _block_diag_qt(qt):
    zero = jnp.zeros((HEAD_DIM, Q_BLOCK), BF16)
    n_groups = qt.shape[0] // (REP * HEAD_DIM)
    return jnp.concatenate([
        jnp.concatenate([qt[(REP * g + r) * HEAD_DIM:(REP * g + r + 1) * HEAD_DIM] if gg == g else zero
                         for gg in range(n_groups) for r in range(REP)], axis=1)
        for g in range(n_groups)], axis=0)


def _pv_by_group(vt, p, with_sum):
    n_groups = vt.shape[0] // HEAD_DIM
    gl = p.shape[1] // n_groups
    pb = p.astype(BF16)
    outs, sums = [], []
    for g in range(n_groups):
        lhs = vt[g * HEAD_DIM:(g + 1) * HEAD_DIM]
        if with_sum:
            lhs = jnp.concatenate([lhs, jnp.ones((SUM_ROWS, vt.shape[1]), BF16)], axis=0)
        r = jnp.dot(lhs, pb[:, g * gl:(g + 1) * gl], preferred_element_type=F32)
        outs.append(r[:HEAD_DIM])
        sums.append(r[HEAD_DIM:HEAD_DIM + 1])
    return outs, (jnp.concatenate(sums, axis=1) if with_sum else None)


def _scale_groups(outs, row):
    gl = outs[0].shape[1]
    return [o * row[:, g * gl:(g + 1) * gl] for g, o in enumerate(outs)]


def _head_block(outs, h):
    r = h % REP
    return outs[h // REP][:, r * Q_BLOCK:(r + 1) * Q_BLOCK]


def _banded_scores(i, wq, k_ref, bias_ref, window):
    n_blk = window // Q_BLOCK + 1
    span = n_blk * Q_BLOCK
    kb0 = jnp.maximum(i - (n_blk - 1), 0)
    shift = jnp.maximum(n_blk - 1 - i, 0)
    s = jnp.dot(k_ref[pl.ds(pl.multiple_of(kb0 * Q_BLOCK, Q_BLOCK), span), :], wq,
                preferred_element_type=F32)
    return s + bias_ref[pl.ds(pl.multiple_of(shift * Q_BLOCK, Q_BLOCK), span), :], kb0


def _swa_block(i, s, kb0, vt_ref, sink_ref):
    sink = sink_ref[...]
    m = jnp.maximum(jnp.max(s, axis=0, keepdims=True), sink)
    p = jnp.exp2(s - m)
    vt = jnp.concatenate([vt_ref[kb0 + j] for j in range(A_WINDOW // Q_BLOCK + 1)], axis=1)
    outs, l = _pv_by_group(vt, p, True)
    outs = _scale_groups(outs, 1.0 / (l + jnp.exp2(sink - m)))
    heads = [_head_block(outs, h) for h in range(A_HEADS)]
    return jnp.transpose(jnp.concatenate(heads, axis=0))


SEL_TILE = 2 * Q_BLOCK
AUG_LANES = 2 * V7X_LANES
CMP_SHIFT = 4
CMP_PER_QB = Q_BLOCK // CMP_STRIDE
CMP_NEAR = MAX_DISTANCE // CMP_STRIDE
CMP_BAND = 3 * CMP_PER_QB


def _mixers_kernel(qa_ref, ka_ref, vat_ref, ba_ref, sink_ref,
                   q_ref, kcmp_ref, vcmpt_ref, kaug_ref, vslt_ref, kw_ref, vwt_ref, gate_ref,
                   fc_ref, bw_ref, bn_ref, ovt_ref, oa_ref, o_ref, sc_ref, s0_ref, s1_ref, *, topk):
    i = pl.program_id(1)
    n_slc, ncp = ovt_ref.shape
    gl = REP * Q_BLOCK
    nl = B_KV_GROUPS * gl
    gd = B_KV_GROUPS * HEAD_DIM
    n_win = NSA_WINDOW // Q_BLOCK + 1

    wq = _block_diag_qt(q_ref[...])

    qlane = lax.broadcasted_iota(jnp.int32, (1, nl), 1) & (Q_BLOCK - 1)
    cmax = jnp.right_shift(qlane + i * Q_BLOCK - (CMP_BLOCK - 1), CMP_SHIFT)
    valid_c = lax.broadcasted_iota(jnp.int32, (ncp, nl), 0) <= cmax
    s = jnp.dot(kcmp_ref[...], wq, preferred_element_type=F32) + fc_ref[CMP_NEAR:CMP_NEAR + 1, :]
    sc_ref[...] = jnp.where(valid_c, s, NEG)
    b0 = pl.multiple_of(jnp.clip(CMP_PER_QB * (i - 2), 0, ncp - CMP_BAND), V7X_SUBLANES)
    kkb = cmax - (b0 + lax.broadcasted_iota(jnp.int32, (CMP_BAND, nl), 0))
    delta = jnp.zeros((CMP_BAND, nl), F32)
    for k in range(CMP_NEAR):
        delta = jnp.where(kkb == k, fc_ref[k:k + 1, :], delta)
    sc_ref[pl.ds(b0, CMP_BAND), :] = sc_ref[pl.ds(b0, CMP_BAND), :] + delta
    s = sc_ref[...]
    m = jnp.max(s, axis=0, keepdims=True)
    p = jnp.exp2(s - m)
    l = jnp.sum(p, axis=0, keepdims=True)
    p = p * jnp.where(cmax >= 0, 1.0 / l, 0.0)
    o_c, _ = _pv_by_group(vcmpt_ref[...], p, False)
    psum = jnp.concatenate(
        [sum(p[:, g * gl + r * Q_BLOCK:g * gl + (r + 1) * Q_BLOCK] for r in range(REP))
         for g in range(B_KV_GROUPS)], axis=1)
    imp = jnp.dot(ovt_ref[...], psum, preferred_element_type=F32,
                  precision=lax.Precision.HIGHEST)

    sa, kb0_a = _banded_scores(i, _block_diag_qt(qa_ref[...]), ka_ref, ba_ref, A_WINDOW)
    oa_ref[...] = _swa_block(i, sa, kb0_a, vat_ref, sink_ref).astype(oa_ref.dtype)

    sw, kb0 = _banded_scores(i, wq, kw_ref, bw_ref, NSA_WINDOW)
    mw = jnp.max(sw, axis=0, keepdims=True)
    pw = jnp.exp2(sw - mw)
    vwt = jnp.concatenate([vwt_ref[kb0 + j] for j in range(n_win)], axis=1)
    o_w, l_w = _pv_by_group(vwt, pw, True)
    o_w = _scale_groups(o_w, 1.0 / l_w)

    sl = B_KV_GROUPS * Q_BLOCK
    jt = lax.broadcasted_iota(jnp.int32, (n_slc, sl), 0)
    second_half = (lax.broadcasted_iota(jnp.int32, (1, sl), 1) & (Q_BLOCK - 1)) >= SLC_BLOCK
    qblk = 2 * i + second_half.astype(jnp.int32)
    forced = (jt == 0) | (jt == qblk) | (jt == qblk - 1)
    score = jnp.where(forced, BIG, jnp.where(jt > qblk, NEG, imp))
    rank = jnp.zeros((n_slc, sl), jnp.int32)
    for c in range(n_slc):
        row = score[c:c + 1, :]
        ahead = (row > score) | ((row == score) & (jt > c))
        rank = rank + ahead.astype(jnp.int32)
    selneg = jnp.where(rank < topk, 0.0, NEG).astype(BF16)
    wsel = jnp.concatenate([selneg[:, g * Q_BLOCK:(g + 1) * Q_BLOCK]
                            for g in range(B_KV_GROUPS) for _ in range(REP)], axis=1)
    w = jnp.concatenate([wq, wsel, jnp.zeros((AUG_LANES - gd - n_slc, nl), BF16)], axis=0)

    def scores(t):
        r0 = pl.multiple_of(t * SEL_TILE, SEL_TILE)
        return jnp.dot(kaug_ref[pl.ds(r0, SEL_TILE), :], w, preferred_element_type=F32)

    def update(carry, st, t):
        m_i, l_i, a0, a1 = carry
        m_n = jnp.maximum(m_i, jnp.max(st, axis=0, keepdims=True))
        alpha = jnp.exp2(m_i - m_n)
        pt = jnp.exp2(st - m_n)
        blk0 = t * (SEL_TILE // Q_BLOCK)
        vt = jnp.concatenate([vslt_ref[blk0 + j] for j in range(SEL_TILE // Q_BLOCK)], axis=1)
        pv, l_t = _pv_by_group(vt, pt, True)
        a0, a1 = (a + b for a, b in zip(_scale_groups((a0, a1), alpha), pv))
        return m_n, alpha * l_i + l_t, a0, a1

    last = i // 2
    par = i % 2
    n_far = jnp.maximum(last - 1, 0)
    carry = (jnp.full((1, nl), NEG, F32), jnp.zeros((1, nl), F32),
             jnp.zeros((HEAD_DIM, gl), F32), jnp.zeros((HEAD_DIM, gl), F32))
    s0_ref[...] = scores(0)

    def two_tiles(u, carry):
        t = 2 * u
        s1_ref[...] = scores(t + 1)
        carry = update(carry, s0_ref[...], t)
        s0_ref[...] = scores(t + 2)
        return update(carry, s1_ref[...], t + 1)

    def one_tile(t, carry):
        s1_ref[...] = scores(t + 1)
        carry = update(carry, s0_ref[...], t)
        s0_ref[...] = s1_ref[...]
        return carry

    carry = lax.fori_loop(0, n_far // 2, two_tiles, carry)
    carry = lax.fori_loop(n_far // 2 * 2, n_far, one_tile, carry)
    s1_ref[...] = scores(last)
    near0 = pl.multiple_of(par * (2 * SEL_TILE), SEL_TILE)
    before = bn_ref[pl.ds(near0, SEL_TILE), :] + jnp.where(last > 0, 0.0, NEG)
    carry = update(carry, s0_ref[...] + before, jnp.maximum(last - 1, 0))
    m_s, l_s, a0, a1 = update(carry, s1_ref[...] + bn_ref[pl.ds(near0 + SEL_TILE, SEL_TILE), :], last)
    o_s = _scale_groups((a0, a1), 1.0 / l_s)

    gt = jnp.transpose(gate_ref[...])
    heads = [gt[3 * h:3 * h + 1, :] * _head_block(o_c, h)
             + gt[3 * h + 1:3 * h + 2, :] * _head_block(o_s, h)
             + gt[3 * h + 2:3 * h + 3, :] * _head_block(o_w, h) for h in range(B_HEADS)]
    o_ref[...] = jnp.transpose(jnp.concatenate(heads, axis=0)).astype(o_ref.dtype)


def _mixers_call(qt, kk, vt, p32, kcmp, vcmpt, kaug, ba, sink_row, fc, bw, bn, ovt, bsz, seq,
                 col_gate, topk):
    nb = seq // Q_BLOCK
    assert A_HEADS == B_HEADS and A_KV_HEADS == B_KV_GROUPS
    qw = B_HEADS * HEAD_DIM
    kw = B_KV_GROUPS * HEAD_DIM
    per_batch = lambda a: pl.BlockSpec((None,) + a.shape[1:], lambda b, i: (b,) + (0,) * (a.ndim - 1))
    const = lambda a: pl.BlockSpec(a.shape, lambda b, i: (0,) * a.ndim)
    qt_seg = lambda a: pl.BlockSpec((None, None, None, qw, Q_BLOCK), lambda b, i: (a, b, i, 0, 0))
    kk_seg = lambda a: pl.BlockSpec((None, seq, kw), lambda b, i: (b, 0, a))
    vt_seg = lambda a: pl.BlockSpec((None, None) + vt.shape[2:], lambda b, i: (a, b, 0, 0, 0))
    y_spec = pl.BlockSpec((None, Q_BLOCK, qw), lambda b, i: (b, i, 0))
    kern = functools.partial(_mixers_kernel, topk=topk)
    return pl.pallas_call(
        kern,
        grid=(bsz, nb),
        in_specs=[qt_seg(QT_A), kk_seg(KK_A), vt_seg(VT_A), const(ba), const(sink_row),
                  qt_seg(QT_B), per_batch(kcmp), per_batch(vcmpt), per_batch(kaug), vt_seg(VT_SL),
                  kk_seg(KK_W), vt_seg(VT_W),
                  pl.BlockSpec((None, Q_BLOCK, V7X_LANES), lambda b, i: (b, i, col_gate // V7X_LANES)),
                  const(fc), const(bw), const(bn), const(ovt)],
        out_specs=[y_spec, y_spec],
        out_shape=[jax.ShapeDtypeStruct((bsz, seq, qw), BF16)] * 2,
        scratch_shapes=[pltpu.VMEM((kcmp.shape[1], B_HEADS * Q_BLOCK), F32),
                        pltpu.VMEM((SEL_TILE, B_HEADS * Q_BLOCK), F32),
                        pltpu.VMEM((SEL_TILE, B_HEADS * Q_BLOCK), F32)],
        compiler_params=_cparams(("parallel", "arbitrary")),
        name="mixers",
    )(qt, kk, vt, ba, sink_row, qt, kcmp, vcmpt, kaug, vt, kk, vt, p32, fc, bw, bn, ovt)


def _merge_kernel(x_ref, g_ref, ya_ref, yb_ref, wg_ref, wua_ref, wub_ref, wo_ref, o_ref):
    x = x_ref[...]
    d = x.shape[1]
    hb = _rms(x, g_ref[...]).astype(BF16)
    ga = jax.nn.sigmoid(jnp.dot(hb, wg_ref[:, :d], preferred_element_type=F32))
    gb = jax.nn.sigmoid(jnp.dot(hb, wg_ref[:, d:], preferred_element_type=F32))
    ua = jnp.dot(ya_ref[...], wua_ref[...], preferred_element_type=F32)
    ub = jnp.dot(yb_ref[...], wub_ref[...], preferred_element_type=F32)
    merged = ga * ua + gb * ub
    o_ref[...] = x + jnp.dot(merged.astype(BF16), wo_ref[...], preferred_element_type=F32)


def _merge(x2d, g, ya, yb, wg, wua, wub, wo, tm):
    n, d = x2d.shape
    const = lambda a: pl.BlockSpec(a.shape, lambda i: (0, 0))
    row = lambda a: pl.BlockSpec((tm, a.shape[1]), lambda i: (i, 0))
    return pl.pallas_call(
        _merge_kernel,
        grid=(n // tm,),
        in_specs=[row(x2d), const(g), row(ya), row(yb), const(wg), const(wua), const(wub), const(wo)],
        out_specs=row(x2d),
        out_shape=jax.ShapeDtypeStruct((n, d), F32),
        compiler_params=_cparams(("parallel",)),
        name="merge",
    )(x2d, g, ya, yb, wg, wua, wub, wo)


FFN_HALO = 16
FFN_GROUP = 11


def _ffn_kernel(xc_ref, xp_ref, gn_ref, wi_ref, cw_ref, cb_ref, wo_ref, gf_ref, o_ref, *, chunk):
    i = pl.program_id(1)
    xc = xc_ref[...]
    tm = xc.shape[0]
    gn = gn_ref[...]
    hp = _rms(xp_ref[...], gn) * jnp.where(i > 0, 1.0, 0.0)
    h = jnp.concatenate([hp, _rms(xc, gn)], axis=0).astype(BF16)
    d_ff = wo_ref.shape[0]

    def up(c0):
        return jnp.dot(h, wi_ref[:, c0:c0 + chunk], preferred_element_type=F32)

    def conv(ext, c0):
        cw = cw_ref[:, c0:c0 + chunk]
        out = cb_ref[:, c0:c0 + chunk]
        for k in range(CONV_WIDTH):
            off = FFN_HALO - (CONV_WIDTH - 1) + k
            out = out + cw[k:k + 1, :] * ext[off:off + tm]
        return out

    acc = None
    acts = []
    nxt = (up(0), up(d_ff))
    for c0 in range(0, d_ff, chunk):
        ext_u, ext_g = nxt
        if c0 + chunk < d_ff:
            nxt = (up(c0 + chunk), up(d_ff + c0 + chunk))
        acts.append((jax.nn.silu(conv(ext_g, d_ff + c0)) * conv(ext_u, c0)).astype(BF16))
        if len(acts) == FFN_GROUP or c0 + chunk >= d_ff:
            r0 = c0 + chunk - len(acts) * chunk
            part = jnp.dot(jnp.concatenate(acts, axis=1), wo_ref[r0:c0 + chunk, :],
                           preferred_element_type=F32)
            acc = part if acc is None else acc + part
            acts = []
    o_ref[...] = _rms(xc + acc, gf_ref[...])


def _ffn(x1, gn, wi, cw, cb, wo, gf, tm, chunk):
    bsz, seq, d = x1.shape
    const = lambda a: pl.BlockSpec(a.shape, lambda b, i: (0, 0))
    kern = functools.partial(_ffn_kernel, chunk=chunk)
    return pl.pallas_call(
        kern,
        grid=(bsz, seq // tm),
        in_specs=[pl.BlockSpec((None, tm, d), lambda b, i: (b, i, 0)),
                  pl.BlockSpec((None, FFN_HALO, d),
                               lambda b, i: (b, jnp.maximum(i * (tm // FFN_HALO) - 1, 0), 0)),
                  const(gn), const(wi), const(cw), const(cb), const(wo), const(gf)],
        out_specs=pl.BlockSpec((None, tm, d), lambda b, i: (b, i, 0)),
        out_shape=jax.ShapeDtypeStruct((bsz, seq, d), F32),
        compiler_params=_cparams(("parallel", "arbitrary")),
        name="ffn",
    )(x1, x1, gn, wi, cw, cb, wo, gf)


def _mixers(x, norm_mix, w_in, attn_sinks, cmp_pos_k, cmp_w1_k, cmp_w2_k, cmp_pos_v, cmp_w1_v,
            cmp_w2_v, table):
    bsz, seq, d = x.shape
    n = bsz * seq
    aq, akv = A_HEADS * HEAD_DIM, A_KV_HEADS * HEAD_DIM
    bq, bkv = B_HEADS * HEAD_DIM, B_KV_GROUPS * HEAD_DIM
    n_gate = 3 * B_HEADS
    assert seq % SEL_TILE == 0 and seq // SLC_BLOCK <= AUG_LANES - bkv
    splits = (aq, akv, akv, bq, bkv, bkv, bkv, bkv, bkv, bkv, n_gate, d, d)
    off = np.concatenate([[0], np.cumsum(splits)]).astype(int)
    seg = lambda k: w_in[:, off[k]:off[k + 1]]
    assert akv == bkv and aq == bq
    w_gate_nsa = jnp.pad(seg(10), ((0, 0), (0, V7X_LANES - n_gate)))
    w1 = jnp.concatenate([seg(0), seg(3), seg(1), seg(8), seg(6), seg(2), seg(7), seg(9),
                          seg(4), seg(5), w_gate_nsa], axis=1).astype(BF16)
    tm = min(1024, seq)
    g_mix = norm_mix.reshape(1, d)

    qt, kk, kaug, vt, p32 = _proj(x.reshape(n, d), g_mix, w1, aq, seq, tm)
    nb = seq // Q_BLOCK
    qt = qt.reshape(2, bsz, nb, aq, Q_BLOCK)
    kk = kk.reshape(bsz, seq, 2 * bkv)
    kaug = kaug.reshape(bsz, seq, AUG_LANES)
    vt = vt.reshape(3, bsz, nb, bkv, Q_BLOCK)
    p32 = p32.reshape(bsz, seq, 2 * bkv + V7X_LANES)

    table = table * LOG2E
    table_a, table_b = table[:, :A_HEADS], table[:, A_HEADS:]
    sink_row = jnp.repeat(attn_sinks * LOG2E, Q_BLOCK).reshape(1, A_HEADS * Q_BLOCK)

    ncp = seq // CMP_STRIDE
    n_cmp = (seq - CMP_BLOCK) // CMP_STRIDE + 1
    n_slc = seq // SLC_BLOCK
    topk = min(SLC_TOPK, n_slc)

    k_cmp = _compress(p32, 0, cmp_pos_k, cmp_w1_k, cmp_w2_k, n_cmp, False)
    v_cmp_t = _compress(p32, bkv, cmp_pos_v, cmp_w1_v, cmp_w2_v, n_cmp, True)

    fc = _cmp_bias_rows(table_b)
    win_a, win_b = A_WINDOW + Q_BLOCK, NSA_WINDOW + Q_BLOCK
    bias_a, bw, bn = _toeplitz_tables(
        [_band_vector(table_a, win_a, A_WINDOW, A_WINDOW, False),
         _band_vector(table_b, win_b, NSA_WINDOW, NSA_WINDOW, False),
         _band_vector(table_b, 2 * SEL_TILE, SEL_TILE, None, True),
         _band_vector(table_b, 2 * SEL_TILE, SEL_TILE + Q_BLOCK, None, True)],
        [(0, 0), (1, 0), (2, 0), (2, 2 * SEL_TILE)],
        [win_a + A_WINDOW, win_b + NSA_WINDOW, 4 * SEL_TILE])
    ovt = jnp.asarray(_overlap_t(ncp, n_slc, n_cmp))
    y_a, y_b = _mixers_call(qt, kk, vt, p32, k_cmp, v_cmp_t, kaug, bias_a, sink_row, fc, bw, bn, ovt,
                            bsz, seq, 2 * bkv, topk)
    return y_a, y_b, w_in[:, off[11]:off[13]]


def _layer(x, norm_mix, w_in, attn_sinks, cmp_pos_k, cmp_w1_k, cmp_w2_k, cmp_pos_v, cmp_w1_v,
           cmp_w2_v, w_up_a, w_up_b, w_out, norm_ffn, w_ffn_in, conv_w, conv_b, w_ffn_out,
           table, norm_final):
    bsz, seq, d = x.shape
    n = bsz * seq
    y_a, y_b, w_merge_gates = _mixers(x, norm_mix, w_in, attn_sinks, cmp_pos_k, cmp_w1_k, cmp_w2_k,
                                      cmp_pos_v, cmp_w1_v, cmp_w2_v, table)

    x1 = _merge(x.reshape(n, d), norm_mix.reshape(1, d), y_a.reshape(n, -1), y_b.reshape(n, -1),
                w_merge_gates.astype(BF16), w_up_a.astype(BF16), w_up_b.astype(BF16),
                w_out.astype(BF16), min(1024, n))

    return _ffn(x1.reshape(bsz, seq, d), norm_ffn.reshape(1, d), w_ffn_in.astype(BF16), conv_w,
                conv_b.reshape(1, -1), w_ffn_out.astype(BF16), norm_final.reshape(1, d),
                min(512, seq), 256)


def kernel(x, norm_mix, w_in, attn_sinks, cmp_pos_k, cmp_w1_k, cmp_w2_k, cmp_pos_v, cmp_w1_v, cmp_w2_v, w_up_a, w_up_b, w_out, norm_ffn, w_ffn_in, conv_w, conv_b, w_ffn_out, rel_bias_table, norm_final):
    assert norm_mix.shape[0] == 1, "single-layer block"
    return _layer(x, norm_mix[0], w_in[0], attn_sinks[0], cmp_pos_k[0], cmp_w1_k[0], cmp_w2_k[0],
                  cmp_pos_v[0], cmp_w1_v[0], cmp_w2_v[0], w_up_a[0], w_up_b[0], w_out[0],
                  norm_ffn[0], w_ffn_in[0], conv_w[0], conv_b[0], w_ffn_out[0], rel_bias_table,
                  norm_final)
```

```python
import functools
import math

import numpy as np
import jax
import jax.numpy as jnp
from jax import lax
from jax.experimental import pallas as pl
from jax.experimental.pallas import tpu as pltpu

F32 = jnp.float32
BF16 = jnp.bfloat16

HEAD_DIM = 64
A_HEADS = 8
A_KV_HEADS = 2
A_WINDOW = 128
B_HEADS = 8
B_KV_GROUPS = 2
REP = 4
CMP_BLOCK = 32
CMP_STRIDE = 16
SLC_BLOCK = 64
SLC_SHIFT = 6
SLC_TOPK = 16
NSA_WINDOW = 512
NUM_BUCKETS = 32
MAX_DISTANCE = 128
CONV_WIDTH = 3
Q_BLOCK = 128
EPS = 1e-6
NEG = -1e30
BIG = 1e30
SCALE = HEAD_DIM ** -0.5
LOG2E = math.log2(math.e)
Q_SCALE = SCALE * LOG2E
SUM_ROWS = 16

V7X_LANES = 128
V7X_SUBLANES = 8
V7X_VMEM_BYTES = 64 * 1024 * 1024
VMEM_LIMIT = 56 * 1024 * 1024


def _cparams(semantics):
    return pltpu.CompilerParams(dimension_semantics=semantics, vmem_limit_bytes=VMEM_LIMIT)


def _bucket_np(dist):
    dist = np.maximum(dist, 0)
    max_exact = NUM_BUCKETS // 2
    d = np.maximum(dist, 1).astype(np.float64)
    large = max_exact + (np.log(d / max_exact) / math.log(MAX_DISTANCE / max_exact)
                         * (NUM_BUCKETS - max_exact)).astype(np.int32)
    large = np.minimum(large, NUM_BUCKETS - 1)
    return np.where(dist < max_exact, dist, large).astype(np.int32)


def _table_lookup(table, dist):
    idx = _bucket_np(dist).reshape(-1)
    onehot = np.zeros((NUM_BUCKETS, idx.size), np.float32)
    onehot[idx, np.arange(idx.size)] = 1.0
    vals = jnp.dot(table.T, jnp.asarray(onehot), precision=lax.Precision.HIGHEST)
    return vals.reshape((table.shape[1],) + dist.shape)


def _band_vector(table, n_keys, offset, window, shift_far):
    length = n_keys + Q_BLOCK
    m = np.arange(length)
    m = np.where(m < Q_BLOCK, m, m - length)
    dist = m + offset
    ok = dist >= 0 if window is None else (dist >= 0) & (dist < window)
    u = _table_lookup(table, dist)
    if shift_far:
        u = u - table[NUM_BUCKETS - 1][:, None]
    return jnp.where(ok[None, :], u, NEG)


def _toeplitz_kernel(*refs, dest):
    n_tab = len(dest)
    covered = [0] * (len(refs) - n_tab)
    for u_ref, (j, row0) in zip(refs[:n_tab], dest):
        o_ref = refs[n_tab + j]
        n_heads, length = u_ref.shape
        n_keys = length - Q_BLOCK
        covered[j] = max(covered[j], row0 + n_keys)
        u2 = jnp.concatenate([u_ref[...], u_ref[...]], axis=1)
        for r0 in range(0, n_keys, Q_BLOCK):
            w0 = (-r0 - (Q_BLOCK - 1)) % length
            win = u2[:, w0:w0 + 2 * Q_BLOCK]
            for h in range(n_heads):
                x = jnp.broadcast_to(win[h:h + 1, :], (Q_BLOCK, 2 * Q_BLOCK))
                x = pltpu.roll(x, Q_BLOCK + 1, 1, stride=1, stride_axis=0)
                o_ref[row0 + r0:row0 + r0 + Q_BLOCK, h * Q_BLOCK:(h + 1) * Q_BLOCK] = x[:, :Q_BLOCK]
    for o_ref, done in zip(refs[n_tab:], covered):
        if o_ref.shape[0] > done:
            o_ref[done:, :] = jnp.full((o_ref.shape[0] - done, o_ref.shape[1]), NEG, F32)


def _toeplitz_tables(us, dest, n_rows):
    n_heads = us[0].shape[0]
    kern = functools.partial(_toeplitz_kernel, dest=tuple(dest))
    return pl.pallas_call(
        kern,
        out_shape=[jax.ShapeDtypeStruct((r, n_heads * Q_BLOCK), F32) for r in n_rows],
        compiler_params=pltpu.CompilerParams(vmem_limit_bytes=VMEM_LIMIT),
        name="bias_tables",
    )(*us)


def _cmp_bias_rows(table):
    n = MAX_DISTANCE // CMP_STRIDE
    q = np.arange(Q_BLOCK)
    rho = (q - (CMP_BLOCK - 1)) % CMP_STRIDE
    dist = CMP_STRIDE * np.arange(n + 1)[:, None] + rho[None, :]
    vals = _table_lookup(table, dist)
    vals = jnp.transpose(vals, (1, 0, 2)).reshape(n + 1, table.shape[1] * Q_BLOCK)
    vals = jnp.concatenate([vals[:n] - vals[n:], vals[n:]], axis=0)
    return jnp.pad(vals, ((0, n - 1), (0, 0)))


def _overlap_t(n_cmp_pad, n_slc, n_cmp):
    r = SLC_BLOCK // CMP_STRIDE
    c = CMP_BLOCK // CMP_STRIDE
    j, m, n = np.meshgrid(np.arange(n_slc), np.arange(r), np.arange(c), indexing='ij')
    i = r * j + m - n
    ok = (i >= 0) & (i < n_cmp)
    mat = np.zeros((n_slc, n_cmp_pad), np.float32)
    np.add.at(mat, (j[ok], i[ok]), 1.0)
    return mat


def _rms(x, g):
    return x * lax.rsqrt(jnp.mean(x * x, axis=-1, keepdims=True) + EPS) * g


PROJ_CHUNK = 256
VT_A, VT_SL, VT_W = 0, 1, 2
QT_A, QT_B = 0, 1
KK_A, KK_W = 0, 1


def _proj_kernel(x_ref, g_ref, w_ref, qt_ref, kk_ref, kaug_ref, vt_ref, o32_ref, *, seq):
    tm = x_ref.shape[0]
    kv = B_KV_GROUPS * HEAD_DIM
    qw = qt_ref.shape[2]
    hb = _rms(x_ref[...], g_ref[...]).astype(BF16)

    def cols(c0, width=PROJ_CHUNK):
        return jnp.dot(hb, w_ref[:, c0:c0 + width], preferred_element_type=F32)

    def put_transposed(ref, a, r, row0=0):
        for j in range(tm // Q_BLOCK):
            ref[a, j, row0:row0 + r.shape[1], :] = jnp.transpose(
                r[j * Q_BLOCK:(j + 1) * Q_BLOCK, :]).astype(BF16)

    for c0 in range(0, 2 * qw, PROJ_CHUNK):
        put_transposed(qt_ref, c0 // qw, cols(c0) * Q_SCALE, c0 % qw)
    kk_ref[...] = cols(2 * qw).astype(BF16)
    r = cols(2 * qw + 2 * kv)
    pos = (pl.program_id(0) * tm) % seq + lax.broadcasted_iota(jnp.int32, (tm, kv), 0)
    onehot = lax.broadcasted_iota(jnp.int32, (tm, kv), 1) == jnp.right_shift(pos, SLC_SHIFT)
    kaug_ref[:, :kv] = r[:, :kv].astype(BF16)
    kaug_ref[:, kv:] = jnp.where(onehot, 1.0, 0.0).astype(BF16)
    put_transposed(vt_ref, VT_A, r[:, kv:])
    r = cols(2 * qw + 4 * kv)
    put_transposed(vt_ref, VT_SL, r[:, :kv])
    put_transposed(vt_ref, VT_W, r[:, kv:])
    o32_ref[:, :2 * kv] = cols(2 * qw + 6 * kv)
    o32_ref[:, 2 * kv:] = jax.nn.sigmoid(cols(2 * qw + 8 * kv, V7X_LANES))


def _proj(x2d, g, w1, qw, seq, tm):
    n, d = x2d.shape
    kv = B_KV_GROUPS * HEAD_DIM
    n32 = 2 * kv + V7X_LANES
    nb = tm // Q_BLOCK
    assert AUG_LANES == 2 * kv == PROJ_CHUNK and seq % tm == 0 and tm % Q_BLOCK == 0
    rows = lambda width: pl.BlockSpec((tm, width), lambda i: (i, 0))
    blocks_t = lambda a, width: pl.BlockSpec((a, nb, width, Q_BLOCK), lambda i: (0, i, 0, 0))
    kern = functools.partial(_proj_kernel, seq=seq)
    return pl.pallas_call(
        kern,
        grid=(n // tm,),
        in_specs=[rows(d), pl.BlockSpec((1, d), lambda i: (0, 0)),
                  pl.BlockSpec(w1.shape, lambda i: (0, 0))],
        out_specs=[blocks_t(2, qw), rows(2 * kv), rows(AUG_LANES), blocks_t(3, kv), rows(n32)],
        out_shape=[jax.ShapeDtypeStruct((2, n // Q_BLOCK, qw, Q_BLOCK), BF16),
                   jax.ShapeDtypeStruct((n, 2 * kv), BF16),
                   jax.ShapeDtypeStruct((n, AUG_LANES), BF16),
                   jax.ShapeDtypeStruct((3, n // Q_BLOCK, kv, Q_BLOCK), BF16),
                   jax.ShapeDtypeStruct((n, n32), F32)],
        compiler_params=_cparams(("parallel",)),
        name="proj",
    )(x2d, g, w1)


CMP_PAIR = 2


def _compress_kernel(x_ref, pos_ref, w1_ref, w2_ref, o_ref, *, n_cmp, transpose_out):
    ncp = x_ref.shape[0] // CMP_STRIDE

    def group_diag(w):
        zero = jnp.zeros(w.shape, w.dtype)
        return jnp.concatenate(
            [jnp.concatenate([w if g == gg else zero for gg in range(B_KV_GROUPS)], axis=1)
             for g in range(B_KV_GROUPS)], axis=0)

    halves = []
    for half in range(CMP_BLOCK // CMP_STRIDE):
        acc = None
        for l0 in range(half * CMP_STRIDE, (half + 1) * CMP_STRIDE, CMP_PAIR):
            lhs = jnp.concatenate(
                [(x_ref[pl.ds(l % CMP_STRIDE, ncp, stride=CMP_STRIDE), :]
                  + jnp.concatenate([pos_ref[l:l + 1, :]] * B_KV_GROUPS, axis=1)).astype(BF16)
                 for l in range(l0, l0 + CMP_PAIR)], axis=1)
            rhs = jnp.concatenate([group_diag(w1_ref[l]) for l in range(l0, l0 + CMP_PAIR)], axis=0)
            part = jnp.dot(lhs, rhs, preferred_element_type=F32)
            acc = part if acc is None else acc + part
        halves.append(acc)
    top, bottom = halves
    h1 = top + jnp.concatenate([bottom[1:], bottom[:1]], axis=0)
    o = jnp.dot(jax.nn.gelu(h1).astype(BF16), group_diag(w2_ref[...]), preferred_element_type=F32)
    row = lax.broadcasted_iota(jnp.int32, o.shape, 0)
    o = jnp.where(row < n_cmp, o, 0.0)
    if transpose_out:
        o = jnp.transpose(o)
    o_ref[...] = o.astype(o_ref.dtype)


def _compress(p32, col, pos, w1, w2, n_cmp, transpose_out):
    bsz, seq, _ = p32.shape
    ncp = seq // CMP_STRIDE
    gd = B_KV_GROUPS * HEAD_DIM
    w1 = w1.reshape(CMP_BLOCK, HEAD_DIM, -1).astype(BF16)
    w2 = w2.astype(BF16)
    oshape = (gd, ncp) if transpose_out else (ncp, gd)
    const = lambda a: pl.BlockSpec(a.shape, lambda b: (0,) * a.ndim)
    kern = functools.partial(_compress_kernel, n_cmp=n_cmp, transpose_out=transpose_out)
    return pl.pallas_call(
        kern,
        grid=(bsz,),
        in_specs=[pl.BlockSpec((None, seq, gd), lambda b: (b, 0, col // gd)),
                  const(pos), const(w1), const(w2)],
        out_specs=pl.BlockSpec((None,) + oshape, lambda b: (b, 0, 0)),
        out_shape=jax.ShapeDtypeStruct((bsz,) + oshape, BF16),
        compiler_params=_cparams(("parallel",)),
        name="compress",
    )(p32, pos, w1, w2)


def _block_diag_qt(qt):
    zero = jnp.zeros((HEAD_DIM, Q_BLOCK), BF16)
    n_groups = qt.shape[0] // (REP * HEAD_DIM)
    return jnp.concatenate([
        jnp.concatenate([qt[(REP * g + r) * HEAD_DIM:(REP * g + r + 1) * HEAD_DIM] if gg == g else zero
                         for gg in range(n_groups) for r in range(REP)], axis=1)
        for g in range(n_groups)], axis=0)


def _pv_by_group(vt, p, with_sum):
    n_groups = vt.shape[0] // HEAD_DIM
    gl = p.shape[1] // n_groups
    pb = p.astype(BF16)
    outs, sums = [], []
    for g in range(n_groups):
        lhs = vt[g * HEAD_DIM:(g + 1) * HEAD_DIM]
        if with_sum:
            lhs = jnp.concatenate([lhs, jnp.ones((SUM_ROWS, vt.shape[1]), BF16)], axis=0)
        r = jnp.dot(lhs, pb[:, g * gl:(g + 1) * gl], preferred_element_type=F32)
        outs.append(r[:HEAD_DIM])
        sums.append(r[HEAD_DIM:HEAD_DIM + 1])
    return outs, (jnp.concatenate(sums, axis=1) if with_sum else None)


def _scale_groups(outs, row):
    gl = outs[0].shape[1]
    return [o * row[:, g * gl:(g + 1) * gl] for g, o in enumerate(outs)]


def _head_block(outs, h):
    r = h % REP
    return outs[h // REP][:, r * Q_BLOCK:(r + 1) * Q_BLOCK]


def _banded_scores(i, wq, k_ref, bias_ref, window):
    n_blk = window // Q_BLOCK + 1
    span = n_blk * Q_BLOCK
    kb0 = jnp.maximum(i - (n_blk - 1), 0)
    shift = jnp.maximum(n_blk - 1 - i, 0)
    s = jnp.dot(k_ref[pl.ds(pl.multiple_of(kb0 * Q_BLOCK, Q_BLOCK), span), :], wq,
                preferred_element_type=F32)
    return s + bias_ref[pl.ds(pl.multiple_of(shift * Q_BLOCK, Q_BLOCK), span), :], kb0


def _swa_block(i, s, kb0, vt_ref, sink_ref):
    sink = sink_ref[...]
    m = jnp.maximum(jnp.max(s, axis=0, keepdims=True), sink)
    p = jnp.exp2(s - m)
    vt = jnp.concatenate([vt_ref[kb0 + j] for j in range(A_WINDOW // Q_BLOCK + 1)], axis=1)
    outs, l = _pv_by_group(vt, p, True)
    outs = _scale_groups(outs, 1.0 / (l + jnp.exp2(sink - m)))
    heads = [_head_block(outs, h) for h in range(A_HEADS)]
    return jnp.transpose(jnp.concatenate(heads, axis=0))


SEL_TILE = 2 * Q_BLOCK
AUG_LANES = 2 * V7X_LANES
CMP_SHIFT = 4
CMP_PER_QB = Q_BLOCK // CMP_STRIDE
CMP_NEAR = MAX_DISTANCE // CMP_STRIDE
CMP_BAND = 3 * CMP_PER_QB


def _mixers_kernel(qa_ref, ka_ref, vat_ref, ba_ref, sink_ref,
                   q_ref, kcmp_ref, vcmpt_ref, kaug_ref, vslt_ref, kw_ref, vwt_ref, gate_ref,
                   fc_ref, bw_ref, bn_ref, ovt_ref, oa_ref, o_ref, sc_ref, s0_ref, s1_ref, *, topk):
    i = pl.program_id(1)
    n_slc, ncp = ovt_ref.shape
    gl = REP * Q_BLOCK
    nl = B_KV_GROUPS * gl
    gd = B_KV_GROUPS * HEAD_DIM
    n_win = NSA_WINDOW // Q_BLOCK + 1

    wq = _block_diag_qt(q_ref[...])

    qlane = lax.broadcasted_iota(jnp.int32, (1, nl), 1) & (Q_BLOCK - 1)
    cmax = jnp.right_shift(qlane + i * Q_BLOCK - (CMP_BLOCK - 1), CMP_SHIFT)
    valid_c = lax.broadcasted_iota(jnp.int32, (ncp, nl), 0) <= cmax
    s = jnp.dot(kcmp_ref[...], wq, preferred_element_type=F32) + fc_ref[CMP_NEAR:CMP_NEAR + 1, :]
    sc_ref[...] = jnp.where(valid_c, s, NEG)
    b0 = pl.multiple_of(jnp.clip(CMP_PER_QB * (i - 2), 0, ncp - CMP_BAND), V7X_SUBLANES)
    kkb = cmax - (b0 + lax.broadcasted_iota(jnp.int32, (CMP_BAND, nl), 0))
    delta = jnp.zeros((CMP_BAND, nl), F32)
    for k in range(CMP_NEAR):
        delta = jnp.where(kkb == k, fc_ref[k:k + 1, :], delta)
    sc_ref[pl.ds(b0, CMP_BAND), :] = sc_ref[pl.ds(b0, CMP_BAND), :] + delta
    s = sc_ref[...]
    m = jnp.max(s, axis=0, keepdims=True)
    p = jnp.exp2(s - m)
    l = jnp.sum(p, axis=0, keepdims=True)
    p = p * jnp.where(cmax >= 0, 1.0 / l, 0.0)
    o_c, _ = _pv_by_group(vcmpt_ref[...], p, False)
    psum = jnp.concatenate(
        [sum(p[:, g * gl + r * Q_BLOCK:g * gl + (r + 1) * Q_BLOCK] for r in range(REP))
         for g in range(B_KV_GROUPS)], axis=1)
    imp = jnp.dot(ovt_ref[...], psum, preferred_element_type=F32,
                  precision=lax.Precision.HIGHEST)

    sa, kb0_a = _banded_scores(i, _block_diag_qt(qa_ref[...]), ka_ref, ba_ref, A_WINDOW)
    oa_ref[...] = _swa_block(i, sa, kb0_a, vat_ref, sink_ref).astype(oa_ref.dtype)

    sw, kb0 = _banded_scores(i, wq, kw_ref, bw_ref, NSA_WINDOW)
    mw = jnp.max(sw, axis=0, keepdims=True)
    pw = jnp.exp2(sw - mw)
    vwt = jnp.concatenate([vwt_ref[kb0 + j] for j in range(n_win)], axis=1)
    o_w, l_w = _pv_by_group(vwt, pw, True)
    o_w = _scale_groups(o_w, 1.0 / l_w)

    sl = B_KV_GROUPS * Q_BLOCK
    jt = lax.broadcasted_iota(jnp.int32, (n_slc, sl), 0)
    second_half = (lax.broadcasted_iota(jnp.int32, (1, sl), 1) & (Q_BLOCK - 1)) >= SLC_BLOCK
    qblk = 2 * i + second_half.astype(jnp.int32)
    forced = (jt == 0) | (jt == qblk) | (jt == qblk - 1)
    score = jnp.where(forced, BIG, jnp.where(jt > qblk, NEG, imp))
    rank = jnp.zeros((n_slc, sl), jnp.int32)
    for c in range(n_slc):
        row = score[c:c + 1, :]
        ahead = (row > score) | ((row == score) & (jt > c))
        rank = rank + ahead.astype(jnp.int32)
    selneg = jnp.where(rank < topk, 0.0, NEG).astype(BF16)
    wsel = jnp.concatenate([selneg[:, g * Q_BLOCK:(g + 1) * Q_BLOCK]
                            for g in range(B_KV_GROUPS) for _ in range(REP)], axis=1)
    w = jnp.concatenate([wq, wsel, jnp.zeros((AUG_LANES - gd - n_slc, nl), BF16)], axis=0)

    def scores(t):
        r0 = pl.multiple_of(t * SEL_TILE, SEL_TILE)
        return jnp.dot(kaug_ref[pl.ds(r0, SEL_TILE), :], w, preferred_element_type=F32)

    def update(carry, st, t, st_max=None):
        m_i, l_i, a0, a1 = carry
        if st_max is None:
            st_max = jnp.max(st, axis=0, keepdims=True)
        m_n = jnp.maximum(m_i, st_max)
        alpha = jnp.exp2(m_i - m_n)
        pt = jnp.exp2(st - m_n)
        blk0 = t * (SEL_TILE // Q_BLOCK)
        vt = jnp.concatenate([vslt_ref[blk0 + j] for j in range(SEL_TILE // Q_BLOCK)], axis=1)
        pv, l_t = _pv_by_group(vt, pt, True)
        a0, a1 = (a + b for a, b in zip(_scale_groups((a0, a1), alpha), pv))
        return m_n, alpha * l_i + l_t, a0, a1

    last = i // 2
    par = i % 2
    n_far = jnp.maximum(last - 1, 0)
    carry = (jnp.full((1, nl), NEG, F32), jnp.zeros((1, nl), F32),
             jnp.zeros((HEAD_DIM, gl), F32), jnp.zeros((HEAD_DIM, gl), F32))
    def stash(ref, t):
        st = scores(t)
        ref[...] = st
        return jnp.max(st, axis=0, keepdims=True)

    def two_tiles(u, carry_mx):
        carry, mx0 = carry_mx
        t = 2 * u
        mx1 = stash(s1_ref, t + 1)
        carry = update(carry, s0_ref[...], t, mx0)
        mx0 = stash(s0_ref, t + 2)
        return update(carry, s1_ref[...], t + 1, mx1), mx0

    def one_tile(t, carry_mx):
        carry, mx0 = carry_mx
        mx1 = stash(s1_ref, t + 1)
        carry = update(carry, s0_ref[...], t, mx0)
        s0_ref[...] = s1_ref[...]
        return carry, mx1

    carry_mx = (carry, stash(s0_ref, 0))
    carry_mx = lax.fori_loop(0, n_far // 4,
                             lambda v, c: two_tiles(2 * v + 1, two_tiles(2 * v, c)), carry_mx)
    carry_mx = lax.fori_loop(n_far // 4 * 2, n_far // 2, two_tiles, carry_mx)
    carry, _ = lax.fori_loop(n_far // 2 * 2, n_far, one_tile, carry_mx)
    s1_ref[...] = scores(last)
    near0 = pl.multiple_of(par * (2 * SEL_TILE), SEL_TILE)
    before = bn_ref[pl.ds(near0, SEL_TILE), :] + jnp.where(last > 0, 0.0, NEG)
    carry = update(carry, s0_ref[...] + before, jnp.maximum(last - 1, 0))
    m_s, l_s, a0, a1 = update(carry, s1_ref[...] + bn_ref[pl.ds(near0 + SEL_TILE, SEL_TILE), :], last)
    o_s = _scale_groups((a0, a1), 1.0 / l_s)

    gt = jnp.transpose(gate_ref[...])
    heads = [gt[3 * h:3 * h + 1, :] * _head_block(o_c, h)
             + gt[3 * h + 1:3 * h + 2, :] * _head_block(o_s, h)
             + gt[3 * h + 2:3 * h + 3, :] * _head_block(o_w, h) for h in range(B_HEADS)]
    o_ref[...] = jnp.transpose(jnp.concatenate(heads, axis=0)).astype(o_ref.dtype)


def _mixers_call(qt, kk, vt, p32, kcmp, vcmpt, kaug, ba, sink_row, fc, bw, bn, ovt, bsz, seq,
                 col_gate, topk):
    nb = seq // Q_BLOCK
    assert A_HEADS == B_HEADS and A_KV_HEADS == B_KV_GROUPS
    qw = B_HEADS * HEAD_DIM
    kw = B_KV_GROUPS * HEAD_DIM
    per_batch = lambda a: pl.BlockSpec((None,) + a.shape[1:], lambda b, i: (b,) + (0,) * (a.ndim - 1))
    const = lambda a: pl.BlockSpec(a.shape, lambda b, i: (0,) * a.ndim)
    qt_seg = lambda a: pl.BlockSpec((None, None, None, qw, Q_BLOCK), lambda b, i: (a, b, i, 0, 0))
    kk_seg = lambda a: pl.BlockSpec((None, seq, kw), lambda b, i: (b, 0, a))
    vt_seg = lambda a: pl.BlockSpec((None, None) + vt.shape[2:], lambda b, i: (a, b, 0, 0, 0))
    y_spec = pl.BlockSpec((None, Q_BLOCK, qw), lambda b, i: (b, i, 0))
    kern = functools.partial(_mixers_kernel, topk=topk)
    return pl.pallas_call(
        kern,
        grid=(bsz, nb),
        in_specs=[qt_seg(QT_A), kk_seg(KK_A), vt_seg(VT_A), const(ba), const(sink_row),
                  qt_seg(QT_B), per_batch(kcmp), per_batch(vcmpt), per_batch(kaug), vt_seg(VT_SL),
                  kk_seg(KK_W), vt_seg(VT_W),
                  pl.BlockSpec((None, Q_BLOCK, V7X_LANES), lambda b, i: (b, i, col_gate // V7X_LANES)),
                  const(fc), const(bw), const(bn), const(ovt)],
        out_specs=[y_spec, y_spec],
        out_shape=[jax.ShapeDtypeStruct((bsz, seq, qw), BF16)] * 2,
        scratch_shapes=[pltpu.VMEM((kcmp.shape[1], B_HEADS * Q_BLOCK), F32),
                        pltpu.VMEM((SEL_TILE, B_HEADS * Q_BLOCK), F32),
                        pltpu.VMEM((SEL_TILE, B_HEADS * Q_BLOCK), F32)],
        compiler_params=_cparams(("parallel", "arbitrary")),
        name="mixers",
    )(qt, kk, vt, ba, sink_row, qt, kcmp, vcmpt, kaug, vt, kk, vt, p32, fc, bw, bn, ovt)


def _merge_kernel(x_ref, g_ref, ya_ref, yb_ref, wg_ref, wua_ref, wub_ref, wo_ref, o_ref):
    x = x_ref[...]
    d = x.shape[1]
    hb = _rms(x, g_ref[...]).astype(BF16)
    ga = jax.nn.sigmoid(jnp.dot(hb, wg_ref[:, :d], preferred_element_type=F32))
    gb = jax.nn.sigmoid(jnp.dot(hb, wg_ref[:, d:], preferred_element_type=F32))
    ua = jnp.dot(ya_ref[...], wua_ref[...], preferred_element_type=F32)
    ub = jnp.dot(yb_ref[...], wub_ref[...], preferred_element_type=F32)
    merged = ga * ua + gb * ub
    o_ref[...] = x + jnp.dot(merged.astype(BF16), wo_ref[...], preferred_element_type=F32)


def _merge(x2d, g, ya, yb, wg, wua, wub, wo, tm):
    n, d = x2d.shape
    const = lambda a: pl.BlockSpec(a.shape, lambda i: (0, 0))
    row = lambda a: pl.BlockSpec((tm, a.shape[1]), lambda i: (i, 0))
    return pl.pallas_call(
        _merge_kernel,
        grid=(n // tm,),
        in_specs=[row(x2d), const(g), row(ya), row(yb), const(wg), const(wua), const(wub), const(wo)],
        out_specs=row(x2d),
        out_shape=jax.ShapeDtypeStruct((n, d), F32),
        compiler_params=_cparams(("parallel",)),
        name="merge",
    )(x2d, g, ya, yb, wg, wua, wub, wo)


FFN_HALO = 16
FFN_GROUP = 6


def _ffn_kernel(xc_ref, xp_ref, gn_ref, wi_ref, cw_ref, cb_ref, wo_ref, gf_ref, o_ref, *, chunk):
    i = pl.program_id(1)
    xc = xc_ref[...]
    tm = xc.shape[0]
    gn = gn_ref[...]
    hp = _rms(xp_ref[...], gn) * jnp.where(i > 0, 1.0, 0.0)
    h = jnp.concatenate([hp, _rms(xc, gn)], axis=0).astype(BF16)
    d_ff = wo_ref.shape[0]

    def up(c0):
        return jnp.dot(h, wi_ref[:, c0:c0 + chunk], preferred_element_type=F32)

    def conv(ext, c0):
        cw = cw_ref[:, c0:c0 + chunk]
        out = cb_ref[:, c0:c0 + chunk]
        for k in range(CONV_WIDTH):
            off = FFN_HALO - (CONV_WIDTH - 1) + k
            out = out + cw[k:k + 1, :] * ext[off:off + tm]
        return out

    acc = None
    acts = []
    nxt = (up(0), up(d_ff))
    for c0 in range(0, d_ff, chunk):
        ext_u, ext_g = nxt
        if c0 + chunk < d_ff:
            nxt = (up(c0 + chunk), up(d_ff + c0 + chunk))
        acts.append((jax.nn.silu(conv(ext_g, d_ff + c0)) * conv(ext_u, c0)).astype(BF16))
        if len(acts) == FFN_GROUP or c0 + chunk >= d_ff:
            r0 = c0 + chunk - len(acts) * chunk
            part = jnp.dot(jnp.concatenate(acts, axis=1), wo_ref[r0:c0 + chunk, :],
                           preferred_element_type=F32)
            acc = part if acc is None else acc + part
            acts = []
    o_ref[...] = _rms(xc + acc, gf_ref[...])


def _ffn(x1, gn, wi, cw, cb, wo, gf, tm, chunk):
    bsz, seq, d = x1.shape
    const = lambda a: pl.BlockSpec(a.shape, lambda b, i: (0, 0))
    kern = functools.partial(_ffn_kernel, chunk=chunk)
    return pl.pallas_call(
        kern,
        grid=(bsz, seq // tm),
        in_specs=[pl.BlockSpec((None, tm, d), lambda b, i: (b, i, 0)),
                  pl.BlockSpec((None, FFN_HALO, d),
                               lambda b, i: (b, jnp.maximum(i * (tm // FFN_HALO) - 1, 0), 0)),
                  const(gn), const(wi), const(cw), const(cb), const(wo), const(gf)],
        out_specs=pl.BlockSpec((None, tm, d), lambda b, i: (b, i, 0)),
        out_shape=jax.ShapeDtypeStruct((bsz, seq, d), F32),
        compiler_params=_cparams(("parallel", "arbitrary")),
        name="ffn",
    )(x1, x1, gn, wi, cw, cb, wo, gf)


def _mixers(x, norm_mix, w_in, attn_sinks, cmp_pos_k, cmp_w1_k, cmp_w2_k, cmp_pos_v, cmp_w1_v,
            cmp_w2_v, table):
    bsz, seq, d = x.shape
    n = bsz * seq
    aq, akv = A_HEADS * HEAD_DIM, A_KV_HEADS * HEAD_DIM
    bq, bkv = B_HEADS * HEAD_DIM, B_KV_GROUPS * HEAD_DIM
    n_gate = 3 * B_HEADS
    assert seq % SEL_TILE == 0 and seq // SLC_BLOCK <= AUG_LANES - bkv
    splits = (aq, akv, akv, bq, bkv, bkv, bkv, bkv, bkv, bkv, n_gate, d, d)
    off = np.concatenate([[0], np.cumsum(splits)]).astype(int)
    seg = lambda k: w_in[:, off[k]:off[k + 1]]
    assert akv == bkv and aq == bq
    w_gate_nsa = jnp.pad(seg(10), ((0, 0), (0, V7X_LANES - n_gate)))
    w1 = jnp.concatenate([seg(0), seg(3), seg(1), seg(8), seg(6), seg(2), seg(7), seg(9),
                          seg(4), seg(5), w_gate_nsa], axis=1).astype(BF16)
    tm = min(1024, seq)
    g_mix = norm_mix.reshape(1, d)

    qt, kk, kaug, vt, p32 = _proj(x.reshape(n, d), g_mix, w1, aq, seq, tm)
    nb = seq // Q_BLOCK
    qt = qt.reshape(2, bsz, nb, aq, Q_BLOCK)
    kk = kk.reshape(bsz, seq, 2 * bkv)
    kaug = kaug.reshape(bsz, seq, AUG_LANES)
    vt = vt.reshape(3, bsz, nb, bkv, Q_BLOCK)
    p32 = p32.reshape(bsz, seq, 2 * bkv + V7X_LANES)

    table = table * LOG2E
    table_a, table_b = table[:, :A_HEADS], table[:, A_HEADS:]
    sink_row = jnp.repeat(attn_sinks * LOG2E, Q_BLOCK).reshape(1, A_HEADS * Q_BLOCK)

    ncp = seq // CMP_STRIDE
    n_cmp = (seq - CMP_BLOCK) // CMP_STRIDE + 1
    n_slc = seq // SLC_BLOCK
    topk = min(SLC_TOPK, n_slc)

    k_cmp = _compress(p32, 0, cmp_pos_k, cmp_w1_k, cmp_w2_k, n_cmp, False)
    v_cmp_t = _compress(p32, bkv, cmp_pos_v, cmp_w1_v, cmp_w2_v, n_cmp, True)

    fc = _cmp_bias_rows(table_b)
    win_a, win_b = A_WINDOW + Q_BLOCK, NSA_WINDOW + Q_BLOCK
    bias_a, bw, bn = _toeplitz_tables(
        [_band_vector(table_a, win_a, A_WINDOW, A_WINDOW, False),
         _band_vector(table_b, win_b, NSA_WINDOW, NSA_WINDOW, False),
         _band_vector(table_b, 2 * SEL_TILE, SEL_TILE, None, True),
         _band_vector(table_b, 2 * SEL_TILE, SEL_TILE + Q_BLOCK, None, True)],
        [(0, 0), (1, 0), (2, 0), (2, 2 * SEL_TILE)],
        [win_a + A_WINDOW, win_b + NSA_WINDOW, 4 * SEL_TILE])
    ovt = jnp.asarray(_overlap_t(ncp, n_slc, n_cmp))
    y_a, y_b = _mixers_call(qt, kk, vt, p32, k_cmp, v_cmp_t, kaug, bias_a, sink_row, fc, bw, bn, ovt,
                            bsz, seq, 2 * bkv, topk)
    return y_a, y_b, w_in[:, off[11]:off[13]]


def _layer(x, norm_mix, w_in, attn_sinks, cmp_pos_k, cmp_w1_k, cmp_w2_k, cmp_pos_v, cmp_w1_v,
           cmp_w2_v, w_up_a, w_up_b, w_out, norm_ffn, w_ffn_in, conv_w, conv_b, w_ffn_out,
           table, norm_final):
    bsz, seq, d = x.shape
    n = bsz * seq
    y_a, y_b, w_merge_gates = _mixers(x, norm_mix, w_in, attn_sinks, cmp_pos_k, cmp_w1_k, cmp_w2_k,
                                      cmp_pos_v, cmp_w1_v, cmp_w2_v, table)

    x1 = _merge(x.reshape(n, d), norm_mix.reshape(1, d), y_a.reshape(n, -1), y_b.reshape(n, -1),
                w_merge_gates.astype(BF16), w_up_a.astype(BF16), w_up_b.astype(BF16),
                w_out.astype(BF16), min(1024, n))

    return _ffn(x1.reshape(bsz, seq, d), norm_ffn.reshape(1, d), w_ffn_in.astype(BF16), conv_w,
                conv_b.reshape(1, -1), w_ffn_out.astype(BF16), norm_final.reshape(1, d),
                min(512, seq), 256)


def kernel(x, norm_mix, w_in, attn_sinks, cmp_pos_k, cmp_w1_k, cmp_w2_k, cmp_pos_v, cmp_w1_v, cmp_w2_v, w_up_a, w_up_b, w_out, norm_ffn, w_ffn_in, conv_w, conv_b, w_ffn_out, rel_bias_table, norm_final):
    assert norm_mix.shape[0] == 1, "single-layer block"
    return _layer(x, norm_mix[0], w_in[0], attn_sinks[0], cmp_pos_k[0], cmp_w1_k[0], cmp_w2_k[0],
                  cmp_pos_v[0], cmp_w1_v[0], cmp_w2_v[0], w_up_a[0], w_up_b[0], w_out[0],
                  norm_ffn[0], w_ffn_in[0], conv_w[0], conv_b[0], w_ffn_out[0], rel_bias_table,
                  norm_final)
```

```python
import functools
import math

import numpy as np
import jax
import jax.numpy as jnp
from jax import lax
from jax.experimental import pallas as pl
from jax.experimental.pallas import tpu as pltpu

F32 = jnp.float32
BF16 = jnp.bfloat16

HEAD_DIM = 64
A_HEADS = 8
A_KV_HEADS = 2
A_WINDOW = 128
B_HEADS = 8
B_KV_GROUPS = 2
REP = 4
CMP_BLOCK = 32
CMP_STRIDE = 16
SLC_BLOCK = 64
SLC_SHIFT = 6
SLC_TOPK = 16
NSA_WINDOW = 512
NUM_BUCKETS = 32
MAX_DISTANCE = 128
CONV_WIDTH = 3
Q_BLOCK = 128
EPS = 1e-6
NEG = -1e30
BIG = 1e30
SCALE = HEAD_DIM ** -0.5
LOG2E = math.log2(math.e)
Q_SCALE = SCALE * LOG2E
SUM_ROWS = 16

V7X_LANES = 128
V7X_SUBLANES = 8
V7X_VMEM_BYTES = 64 * 1024 * 1024
VMEM_LIMIT = 56 * 1024 * 1024


def _cparams(semantics):
    return pltpu.CompilerParams(dimension_semantics=semantics, vmem_limit_bytes=VMEM_LIMIT)


def _bucket_np(dist):
    dist = np.maximum(dist, 0)
    max_exact = NUM_BUCKETS // 2
    d = np.maximum(dist, 1).astype(np.float64)
    large = max_exact + (np.log(d / max_exact) / math.log(MAX_DISTANCE / max_exact)
                         * (NUM_BUCKETS - max_exact)).astype(np.int32)
    large = np.minimum(large, NUM_BUCKETS - 1)
    return np.where(dist < max_exact, dist, large).astype(np.int32)


def _table_lookup(table, dist):
    idx = _bucket_np(dist).reshape(-1)
    onehot = np.zeros((NUM_BUCKETS, idx.size), np.float32)
    onehot[idx, np.arange(idx.size)] = 1.0
    vals = jnp.dot(table.T, jnp.asarray(onehot), precision=lax.Precision.HIGHEST)
    return vals.reshape((table.shape[1],) + dist.shape)


def _band_vector(table, n_keys, offset, window, shift_far):
    length = n_keys + Q_BLOCK
    m = np.arange(length)
    m = np.where(m < Q_BLOCK, m, m - length)
    dist = m + offset
    ok = dist >= 0 if window is None else (dist >= 0) & (dist < window)
    u = _table_lookup(table, dist)
    if shift_far:
        u = u - table[NUM_BUCKETS - 1][:, None]
    return jnp.where(ok[None, :], u, NEG)


def _toeplitz_kernel(*refs, dest):
    n_tab = len(dest)
    covered = [0] * (len(refs) - n_tab)
    for u_ref, (j, row0) in zip(refs[:n_tab], dest):
        o_ref = refs[n_tab + j]
        n_heads, length = u_ref.shape
        n_keys = length - Q_BLOCK
        covered[j] = max(covered[j], row0 + n_keys)
        u2 = jnp.concatenate([u_ref[...], u_ref[...]], axis=1)
        for r0 in range(0, n_keys, Q_BLOCK):
            w0 = (-r0 - (Q_BLOCK - 1)) % length
            win = u2[:, w0:w0 + 2 * Q_BLOCK]
            for h in range(n_heads):
                x = jnp.broadcast_to(win[h:h + 1, :], (Q_BLOCK, 2 * Q_BLOCK))
                x = pltpu.roll(x, Q_BLOCK + 1, 1, stride=1, stride_axis=0)
                o_ref[row0 + r0:row0 + r0 + Q_BLOCK, h * Q_BLOCK:(h + 1) * Q_BLOCK] = x[:, :Q_BLOCK]
    for o_ref, done in zip(refs[n_tab:], covered):
        if o_ref.shape[0] > done:
            o_ref[done:, :] = jnp.full((o_ref.shape[0] - done, o_ref.shape[1]), NEG, F32)


def _toeplitz_tables(us, dest, n_rows):
    n_heads = us[0].shape[0]
    kern = functools.partial(_toeplitz_kernel, dest=tuple(dest))
    return pl.pallas_call(
        kern,
        out_shape=[jax.ShapeDtypeStruct((r, n_heads * Q_BLOCK), F32) for r in n_rows],
        compiler_params=pltpu.CompilerParams(vmem_limit_bytes=VMEM_LIMIT),
        name="bias_tables",
    )(*us)


def _cmp_bias_rows(table):
    n = MAX_DISTANCE // CMP_STRIDE
    q = np.arange(Q_BLOCK)
    rho = (q - (CMP_BLOCK - 1)) % CMP_STRIDE
    dist = CMP_STRIDE * np.arange(n + 1)[:, None] + rho[None, :]
    vals = _table_lookup(table, dist)
    vals = jnp.transpose(vals, (1, 0, 2)).reshape(n + 1, table.shape[1] * Q_BLOCK)
    vals = jnp.concatenate([vals[:n] - vals[n:], vals[n:]], axis=0)
    return jnp.pad(vals, ((0, n - 1), (0, 0)))


def _overlap_t(n_cmp_pad, n_slc, n_cmp):
    r = SLC_BLOCK // CMP_STRIDE
    c = CMP_BLOCK // CMP_STRIDE
    j, m, n = np.meshgrid(np.arange(n_slc), np.arange(r), np.arange(c), indexing='ij')
    i = r * j + m - n
    ok = (i >= 0) & (i < n_cmp)
    mat = np.zeros((n_slc, n_cmp_pad), np.float32)
    np.add.at(mat, (j[ok], i[ok]), 1.0)
    return mat


def _rms(x, g):
    return x * lax.rsqrt(jnp.mean(x * x, axis=-1, keepdims=True) + EPS) * g


PROJ_CHUNK = 256
VT_A, VT_SL, VT_W = 0, 1, 2
QT_A, QT_B = 0, 1
KK_A, KK_W = 0, 1


def _proj_kernel(x_ref, g_ref, w_ref, qt_ref, kk_ref, kaug_ref, vt_ref, o32_ref, *, seq):
    tm = x_ref.shape[0]
    kv = B_KV_GROUPS * HEAD_DIM
    qw = qt_ref.shape[2]
    hb = _rms(x_ref[...], g_ref[...]).astype(BF16)

    def cols(c0, width=PROJ_CHUNK):
        return jnp.dot(hb, w_ref[:, c0:c0 + width], preferred_element_type=F32)

    def put_transposed(ref, a, r, row0=0):
        for j in range(tm // Q_BLOCK):
            ref[a, j, row0:row0 + r.shape[1], :] = jnp.transpose(
                r[j * Q_BLOCK:(j + 1) * Q_BLOCK, :]).astype(BF16)

    for c0 in range(0, 2 * qw, PROJ_CHUNK):
        put_transposed(qt_ref, c0 // qw, cols(c0) * Q_SCALE, c0 % qw)
    kk_ref[...] = cols(2 * qw).astype(BF16)
    r = cols(2 * qw + 2 * kv)
    pos = (pl.program_id(0) * tm) % seq + lax.broadcasted_iota(jnp.int32, (tm, kv), 0)
    onehot = lax.broadcasted_iota(jnp.int32, (tm, kv), 1) == jnp.right_shift(pos, SLC_SHIFT)
    kaug_ref[:, :kv] = r[:, :kv].astype(BF16)
    kaug_ref[:, kv:] = jnp.where(onehot, 1.0, 0.0).astype(BF16)
    put_transposed(vt_ref, VT_A, r[:, kv:])
    r = cols(2 * qw + 4 * kv)
    put_transposed(vt_ref, VT_SL, r[:, :kv])
    put_transposed(vt_ref, VT_W, r[:, kv:])
    o32_ref[:, :2 * kv] = cols(2 * qw + 6 * kv)
    o32_ref[:, 2 * kv:] = jax.nn.sigmoid(cols(2 * qw + 8 * kv, V7X_LANES))


def _proj(x2d, g, w1, qw, seq, tm):
    n, d = x2d.shape
    kv = B_KV_GROUPS * HEAD_DIM
    n32 = 2 * kv + V7X_LANES
    nb = tm // Q_BLOCK
    assert AUG_LANES == 2 * kv == PROJ_CHUNK and seq % tm == 0 and tm % Q_BLOCK == 0
    rows = lambda width: pl.BlockSpec((tm, width), lambda i: (i, 0))
    blocks_t = lambda a, width: pl.BlockSpec((a, nb, width, Q_BLOCK), lambda i: (0, i, 0, 0))
    kern = functools.partial(_proj_kernel, seq=seq)
    return pl.pallas_call(
        kern,
        grid=(n // tm,),
        in_specs=[rows(d), pl.BlockSpec((1, d), lambda i: (0, 0)),
                  pl.BlockSpec(w1.shape, lambda i: (0, 0))],
        out_specs=[blocks_t(2, qw), rows(2 * kv), rows(AUG_LANES), blocks_t(3, kv), rows(n32)],
        out_shape=[jax.ShapeDtypeStruct((2, n // Q_BLOCK, qw, Q_BLOCK), BF16),
                   jax.ShapeDtypeStruct((n, 2 * kv), BF16),
                   jax.ShapeDtypeStruct((n, AUG_LANES), BF16),
                   jax.ShapeDtypeStruct((3, n // Q_BLOCK, kv, Q_BLOCK), BF16),
                   jax.ShapeDtypeStruct((n, n32), F32)],
        compiler_params=_cparams(("parallel",)),
        name="proj",
    )(x2d, g, w1)


CMP_PAIR = 2


def _compress_kernel(x_ref, pos_ref, w1_ref, w2_ref, o_ref, *, n_cmp, transpose_out):
    ncp = x_ref.shape[0] // CMP_STRIDE

    def group_diag(w):
        zero = jnp.zeros(w.shape, w.dtype)
        return jnp.concatenate(
            [jnp.concatenate([w if g == gg else zero for gg in range(B_KV_GROUPS)], axis=1)
             for g in range(B_KV_GROUPS)], axis=0)

    halves = []
    for half in range(CMP_BLOCK // CMP_STRIDE):
        acc = None
        for l0 in range(half * CMP_STRIDE, (half + 1) * CMP_STRIDE, CMP_PAIR):
            lhs = jnp.concatenate(
                [(x_ref[pl.ds(l % CMP_STRIDE, ncp, stride=CMP_STRIDE), :]
                  + jnp.concatenate([pos_ref[l:l + 1, :]] * B_KV_GROUPS, axis=1)).astype(BF16)
                 for l in range(l0, l0 + CMP_PAIR)], axis=1)
            rhs = jnp.concatenate([group_diag(w1_ref[l]) for l in range(l0, l0 + CMP_PAIR)], axis=0)
            part = jnp.dot(lhs, rhs, preferred_element_type=F32)
            acc = part if acc is None else acc + part
        halves.append(acc)
    top, bottom = halves
    h1 = top + jnp.concatenate([bottom[1:], bottom[:1]], axis=0)
    o = jnp.dot(jax.nn.gelu(h1).astype(BF16), group_diag(w2_ref[...]), preferred_element_type=F32)
    row = lax.broadcasted_iota(jnp.int32, o.shape, 0)
    o = jnp.where(row < n_cmp, o, 0.0)
    if transpose_out:
        o = jnp.transpose(o)
    o_ref[...] = o.astype(o_ref.dtype)


def _compress(p32, col, pos, w1, w2, n_cmp, transpose_out):
    bsz, seq, _ = p32.shape
    ncp = seq // CMP_STRIDE
    gd = B_KV_GROUPS * HEAD_DIM
    w1 = w1.reshape(CMP_BLOCK, HEAD_DIM, -1).astype(BF16)
    w2 = w2.astype(BF16)
    oshape = (gd, ncp) if transpose_out else (ncp, gd)
    const = lambda a: pl.BlockSpec(a.shape, lambda b: (0,) * a.ndim)
    kern = functools.partial(_compress_kernel, n_cmp=n_cmp, transpose_out=transpose_out)
    return pl.pallas_call(
        kern,
        grid=(bsz,),
        in_specs=[pl.BlockSpec((None, seq, gd), lambda b: (b, 0, col // gd)),
                  const(pos), const(w1), const(w2)],
        out_specs=pl.BlockSpec((None,) + oshape, lambda b: (b, 0, 0)),
        out_shape=jax.ShapeDtypeStruct((bsz,) + oshape, BF16),
        compiler_params=_cparams(("parallel",)),
        name="compress",
    )(p32, pos, w1, w2)


def _block_diag_qt(qt):
    zero = jnp.zeros((HEAD_DIM, Q_BLOCK), BF16)
    n_groups = qt.shape[0] // (REP * HEAD_DIM)
    return jnp.concatenate([
        jnp.concatenate([qt[(REP * g + r) * HEAD_DIM:(REP * g + r + 1) * HEAD_DIM] if gg == g else zero
                         for gg in range(n_groups) for r in range(REP)], axis=1)
        for g in range(n_groups)], axis=0)


def _pv_by_group(vt, p, with_sum):
    n_groups = vt.shape[0] // HEAD_DIM
    gl = p.shape[1] // n_groups
    pb = p.astype(BF16)
    outs, sums = [], []
    for g in range(n_groups):
        lhs = vt[g * HEAD_DIM:(g + 1) * HEAD_DIM]
        if with_sum:
            lhs = jnp.concatenate([lhs, jnp.ones((SUM_ROWS, vt.shape[1]), BF16)], axis=0)
        r = jnp.dot(lhs, pb[:, g * gl:(g + 1) * gl], preferred_element_type=F32)
        outs.append(r[:HEAD_DIM])
        sums.append(r[HEAD_DIM:HEAD_DIM + 1])
    return outs, (jnp.concatenate(sums, axis=1) if with_sum else None)


def _scale_groups(outs, row):
    gl = outs[0].shape[1]
    return [o * row[:, g * gl:(g + 1) * gl] for g, o in enumerate(outs)]


def _head_block(outs, h):
    r = h % REP
    return outs[h // REP][:, r * Q_BLOCK:(r + 1) * Q_BLOCK]


def _banded_scores(i, wq, k_ref, bias_ref, window):
    n_blk = window // Q_BLOCK + 1
    span = n_blk * Q_BLOCK
    kb0 = jnp.maximum(i - (n_blk - 1), 0)
    shift = jnp.maximum(n_blk - 1 - i, 0)
    s = jnp.dot(k_ref[pl.ds(pl.multiple_of(kb0 * Q_BLOCK, Q_BLOCK), span), :], wq,
                preferred_element_type=F32)
    return s + bias_ref[pl.ds(pl.multiple_of(shift * Q_BLOCK, Q_BLOCK), span), :], kb0


def _swa_block(i, s, kb0, vt_ref, sink_ref):
    sink = sink_ref[...]
    m = jnp.maximum(jnp.max(s, axis=0, keepdims=True), sink)
    p = jnp.exp2(s - m)
    vt = jnp.concatenate([vt_ref[kb0 + j] for j in range(A_WINDOW // Q_BLOCK + 1)], axis=1)
    outs, l = _pv_by_group(vt, p, True)
    outs = _scale_groups(outs, 1.0 / (l + jnp.exp2(sink - m)))
    heads = [_head_block(outs, h) for h in range(A_HEADS)]
    return jnp.transpose(jnp.concatenate(heads, axis=0))


SEL_TILE = 2 * Q_BLOCK
AUG_LANES = 2 * V7X_LANES
CMP_SHIFT = 4
CMP_PER_QB = Q_BLOCK // CMP_STRIDE
CMP_NEAR = MAX_DISTANCE // CMP_STRIDE
CMP_BAND = 3 * CMP_PER_QB


def _mixers_kernel(qa_ref, ka_ref, vat_ref, ba_ref, sink_ref,
                   q_ref, kcmp_ref, vcmpt_ref, kaug_ref, vslt_ref, kw_ref, vwt_ref, gate_ref,
                   fc_ref, bw_ref, bn_ref, ovt_ref, oa_ref, o_ref, sc_ref, s0_ref, s1_ref, sr_ref, *, topk):
    i = pl.program_id(1)
    n_slc, ncp = ovt_ref.shape
    gl = REP * Q_BLOCK
    nl = B_KV_GROUPS * gl
    gd = B_KV_GROUPS * HEAD_DIM
    n_win = NSA_WINDOW // Q_BLOCK + 1

    wq = _block_diag_qt(q_ref[...])

    qlane = lax.broadcasted_iota(jnp.int32, (1, nl), 1) & (Q_BLOCK - 1)
    cmax = jnp.right_shift(qlane + i * Q_BLOCK - (CMP_BLOCK - 1), CMP_SHIFT)
    valid_c = lax.broadcasted_iota(jnp.int32, (ncp, nl), 0) <= cmax
    s = jnp.dot(kcmp_ref[...], wq, preferred_element_type=F32) + fc_ref[CMP_NEAR:CMP_NEAR + 1, :]
    sc_ref[...] = jnp.where(valid_c, s, NEG)
    b0 = pl.multiple_of(jnp.clip(CMP_PER_QB * (i - 2), 0, ncp - CMP_BAND), V7X_SUBLANES)
    kkb = cmax - (b0 + lax.broadcasted_iota(jnp.int32, (CMP_BAND, nl), 0))
    delta = jnp.zeros((CMP_BAND, nl), F32)
    for k in range(CMP_NEAR):
        delta = jnp.where(kkb == k, fc_ref[k:k + 1, :], delta)
    sc_ref[pl.ds(b0, CMP_BAND), :] = sc_ref[pl.ds(b0, CMP_BAND), :] + delta
    s = sc_ref[...]
    m = jnp.max(s, axis=0, keepdims=True)
    p = jnp.exp2(s - m)
    l = jnp.sum(p, axis=0, keepdims=True)
    p = p * jnp.where(cmax >= 0, 1.0 / l, 0.0)
    o_c, _ = _pv_by_group(vcmpt_ref[...], p, False)
    psum = jnp.concatenate(
        [sum(p[:, g * gl + r * Q_BLOCK:g * gl + (r + 1) * Q_BLOCK] for r in range(REP))
         for g in range(B_KV_GROUPS)], axis=1)
    imp = jnp.dot(ovt_ref[...], psum, preferred_element_type=F32,
                  precision=lax.Precision.HIGHEST)

    sa, kb0_a = _banded_scores(i, _block_diag_qt(qa_ref[...]), ka_ref, ba_ref, A_WINDOW)
    oa_ref[...] = _swa_block(i, sa, kb0_a, vat_ref, sink_ref).astype(oa_ref.dtype)

    sw, kb0 = _banded_scores(i, wq, kw_ref, bw_ref, NSA_WINDOW)
    mw = jnp.max(sw, axis=0, keepdims=True)
    pw = jnp.exp2(sw - mw)
    vwt = jnp.concatenate([vwt_ref[kb0 + j] for j in range(n_win)], axis=1)
    o_w, l_w = _pv_by_group(vwt, pw, True)

    sl = B_KV_GROUPS * Q_BLOCK
    jt = lax.broadcasted_iota(jnp.int32, (n_slc, sl), 0)
    second_half = (lax.broadcasted_iota(jnp.int32, (1, sl), 1) & (Q_BLOCK - 1)) >= SLC_BLOCK
    qblk = 2 * i + second_half.astype(jnp.int32)
    forced = (jt == 0) | (jt == qblk) | (jt == qblk - 1)
    score = jnp.where(forced, BIG, jnp.where(jt > qblk, NEG, imp))
    sr_ref[...] = score
    sub = V7X_SUBLANES
    sub_iota = lax.broadcasted_iota(jnp.int32, (sub, sl), 0)
    ranks = []
    for v in range(0, n_slc, sub):
        tile = sr_ref[v:v + sub, :]
        acc = jnp.zeros((sub, sl), F32)
        for c in range(n_slc):
            row = sr_ref[c:c + 1, :]
            if c < v:
                ahead = row >= tile
            elif c >= v + sub:
                ahead = row > tile
            else:
                ahead = (row > tile) | ((row == tile) & (sub_iota > c - v))
            acc = acc + jnp.where(ahead, 1.0, 0.0)
        ranks.append(acc)
    selneg = jnp.where(jnp.concatenate(ranks, axis=0) < topk, 0.0, NEG).astype(BF16)
    wsel = jnp.concatenate([selneg[:, g * Q_BLOCK:(g + 1) * Q_BLOCK]
                            for g in range(B_KV_GROUPS) for _ in range(REP)], axis=1)
    w = jnp.concatenate([wq, wsel, jnp.zeros((AUG_LANES - gd - n_slc, nl), BF16)], axis=0)

    def scores(t):
        r0 = pl.multiple_of(t * SEL_TILE, SEL_TILE)
        return jnp.dot(kaug_ref[pl.ds(r0, SEL_TILE), :], w, preferred_element_type=F32)

    def update(carry, st, t, st_max=None):
        m_i, l_i, a0, a1 = carry
        if st_max is None:
            st_max = jnp.max(st, axis=0, keepdims=True)
        m_n = jnp.maximum(m_i, st_max)
        alpha = jnp.exp2(m_i - m_n)
        pt = jnp.exp2(st - m_n)
        blk0 = t * (SEL_TILE // Q_BLOCK)
        vt = jnp.concatenate([vslt_ref[blk0 + j] for j in range(SEL_TILE // Q_BLOCK)], axis=1)
        pv, l_t = _pv_by_group(vt, pt, True)
        a0, a1 = (a + b for a, b in zip(_scale_groups((a0, a1), alpha), pv))
        return m_n, alpha * l_i + l_t, a0, a1

    last = i // 2
    par = i % 2
    n_far = jnp.maximum(last - 1, 0)
    carry = (jnp.full((1, nl), NEG, F32), jnp.zeros((1, nl), F32),
             jnp.zeros((HEAD_DIM, gl), F32), jnp.zeros((HEAD_DIM, gl), F32))
    def stash(ref, t):
        st = scores(t)
        ref[...] = st
        return jnp.max(st, axis=0, keepdims=True)

    def two_tiles(u, carry_mx):
        carry, mx0 = carry_mx
        t = 2 * u
        mx1 = stash(s1_ref, t + 1)
        carry = update(carry, s0_ref[...], t, mx0)
        mx0 = stash(s0_ref, t + 2)
        return update(carry, s1_ref[...], t + 1, mx1), mx0

    def one_tile(t, carry_mx):
        carry, mx0 = carry_mx
        mx1 = stash(s1_ref, t + 1)
        carry = update(carry, s0_ref[...], t, mx0)
        s0_ref[...] = s1_ref[...]
        return carry, mx1

    carry_mx = (carry, stash(s0_ref, 0))
    carry_mx = lax.fori_loop(0, n_far // 4,
                             lambda v, c: two_tiles(2 * v + 1, two_tiles(2 * v, c)), carry_mx)
    carry_mx = lax.fori_loop(n_far // 4 * 2, n_far // 2, two_tiles, carry_mx)
    carry, _ = lax.fori_loop(n_far // 2 * 2, n_far, one_tile, carry_mx)
    s1_ref[...] = scores(last)
    near0 = pl.multiple_of(par * (2 * SEL_TILE), SEL_TILE)
    before = bn_ref[pl.ds(near0, SEL_TILE), :] + jnp.where(last > 0, 0.0, NEG)
    carry = update(carry, s0_ref[...] + before, jnp.maximum(last - 1, 0))
    m_s, l_s, a0, a1 = update(carry, s1_ref[...] + bn_ref[pl.ds(near0 + SEL_TILE, SEL_TILE), :], last)
    gt = jnp.transpose(gate_ref[...])
    inv_s, inv_w = 1.0 / l_s, 1.0 / l_w
    heads = []
    for h in range(B_HEADS):
        hl = slice(h * Q_BLOCK, (h + 1) * Q_BLOCK)
        heads.append(gt[3 * h:3 * h + 1, :] * _head_block(o_c, h)
                     + (gt[3 * h + 1:3 * h + 2, :] * inv_s[:, hl]) * _head_block((a0, a1), h)
                     + (gt[3 * h + 2:3 * h + 3, :] * inv_w[:, hl]) * _head_block(o_w, h))
    o_ref[...] = jnp.transpose(jnp.concatenate(heads, axis=0)).astype(o_ref.dtype)


def _mixers_call(qt, kk, vt, p32, kcmp, vcmpt, kaug, ba, sink_row, fc, bw, bn, ovt, bsz, seq,
                 col_gate, topk):
    nb = seq // Q_BLOCK
    assert A_HEADS == B_HEADS and A_KV_HEADS == B_KV_GROUPS
    qw = B_HEADS * HEAD_DIM
    kw = B_KV_GROUPS * HEAD_DIM
    per_batch = lambda a: pl.BlockSpec((None,) + a.shape[1:], lambda b, i: (b,) + (0,) * (a.ndim - 1))
    const = lambda a: pl.BlockSpec(a.shape, lambda b, i: (0,) * a.ndim)
    qt_seg = lambda a: pl.BlockSpec((None, None, None, qw, Q_BLOCK), lambda b, i: (a, b, i, 0, 0))
    kk_seg = lambda a: pl.BlockSpec((None, seq, kw), lambda b, i: (b, 0, a))
    vt_seg = lambda a: pl.BlockSpec((None, None) + vt.shape[2:], lambda b, i: (a, b, 0, 0, 0))
    y_spec = pl.BlockSpec((None, Q_BLOCK, qw), lambda b, i: (b, i, 0))
    kern = functools.partial(_mixers_kernel, topk=topk)
    return pl.pallas_call(
        kern,
        grid=(bsz, nb),
        in_specs=[qt_seg(QT_A), kk_seg(KK_A), vt_seg(VT_A), const(ba), const(sink_row),
                  qt_seg(QT_B), per_batch(kcmp), per_batch(vcmpt), per_batch(kaug), vt_seg(VT_SL),
                  kk_seg(KK_W), vt_seg(VT_W),
                  pl.BlockSpec((None, Q_BLOCK, V7X_LANES), lambda b, i: (b, i, col_gate // V7X_LANES)),
                  const(fc), const(bw), const(bn), const(ovt)],
        out_specs=[y_spec, y_spec],
        out_shape=[jax.ShapeDtypeStruct((bsz, seq, qw), BF16)] * 2,
        scratch_shapes=[pltpu.VMEM((kcmp.shape[1], B_HEADS * Q_BLOCK), F32),
                        pltpu.VMEM((SEL_TILE, B_HEADS * Q_BLOCK), F32),
                        pltpu.VMEM((SEL_TILE, B_HEADS * Q_BLOCK), F32),
                        pltpu.VMEM((ovt.shape[0], B_KV_GROUPS * Q_BLOCK), F32)],
        compiler_params=_cparams(("parallel", "arbitrary")),
        name="mixers",
    )(qt, kk, vt, ba, sink_row, qt, kcmp, vcmpt, kaug, vt, kk, vt, p32, fc, bw, bn, ovt)


def _merge_kernel(x_ref, g_ref, ya_ref, yb_ref, wg_ref, wua_ref, wub_ref, wo_ref, o_ref):
    x = x_ref[...]
    d = x.shape[1]
    hb = _rms(x, g_ref[...]).astype(BF16)
    ga = jax.nn.sigmoid(jnp.dot(hb, wg_ref[:, :d], preferred_element_type=F32))
    gb = jax.nn.sigmoid(jnp.dot(hb, wg_ref[:, d:], preferred_element_type=F32))
    ua = jnp.dot(ya_ref[...], wua_ref[...], preferred_element_type=F32)
    ub = jnp.dot(yb_ref[...], wub_ref[...], preferred_element_type=F32)
    merged = ga * ua + gb * ub
    o_ref[...] = x + jnp.dot(merged.astype(BF16), wo_ref[...], preferred_element_type=F32)


def _merge(x2d, g, ya, yb, wg, wua, wub, wo, tm):
    n, d = x2d.shape
    const = lambda a: pl.BlockSpec(a.shape, lambda i: (0, 0))
    row = lambda a: pl.BlockSpec((tm, a.shape[1]), lambda i: (i, 0))
    return pl.pallas_call(
        _merge_kernel,
        grid=(n // tm,),
        in_specs=[row(x2d), const(g), row(ya), row(yb), const(wg), const(wua), const(wub), const(wo)],
        out_specs=row(x2d),
        out_shape=jax.ShapeDtypeStruct((n, d), F32),
        compiler_params=_cparams(("parallel",)),
        name="merge",
    )(x2d, g, ya, yb, wg, wua, wub, wo)


FFN_HALO = 16
FFN_GROUP = 6


def _ffn_kernel(xc_ref, xp_ref, gn_ref, wi_ref, cw_ref, cb_ref, wo_ref, gf_ref, o_ref, *, chunk):
    i = pl.program_id(1)
    xc = xc_ref[...]
    tm = xc.shape[0]
    gn = gn_ref[...]
    hp = _rms(xp_ref[...], gn) * jnp.where(i > 0, 1.0, 0.0)
    h = jnp.concatenate([hp, _rms(xc, gn)], axis=0).astype(BF16)
    d_ff = wo_ref.shape[0]

    def up(c0):
        return jnp.dot(h, wi_ref[:, c0:c0 + chunk], preferred_element_type=F32)

    def conv(ext, c0):
        cw = cw_ref[:, c0:c0 + chunk]
        out = cb_ref[:, c0:c0 + chunk]
        for k in range(CONV_WIDTH):
            off = FFN_HALO - (CONV_WIDTH - 1) + k
            out = out + cw[k:k + 1, :] * ext[off:off + tm]
        return out

    acc = None
    acts = []
    nxt = (up(0), up(d_ff))
    for c0 in range(0, d_ff, chunk):
        ext_u, ext_g = nxt
        if c0 + chunk < d_ff:
            nxt = (up(c0 + chunk), up(d_ff + c0 + chunk))
        acts.append((jax.nn.silu(conv(ext_g, d_ff + c0)) * conv(ext_u, c0)).astype(BF16))
        if len(acts) == FFN_GROUP or c0 + chunk >= d_ff:
            r0 = c0 + chunk - len(acts) * chunk
            part = jnp.dot(jnp.concatenate(acts, axis=1), wo_ref[r0:c0 + chunk, :],
                           preferred_element_type=F32)
            acc = part if acc is None else acc + part
            acts = []
    o_ref[...] = _rms(xc + acc, gf_ref[...])


def _ffn(x1, gn, wi, cw, cb, wo, gf, tm, chunk):
    bsz, seq, d = x1.shape
    const = lambda a: pl.BlockSpec(a.shape, lambda b, i: (0, 0))
    kern = functools.partial(_ffn_kernel, chunk=chunk)
    return pl.pallas_call(
        kern,
        grid=(bsz, seq // tm),
        in_specs=[pl.BlockSpec((None, tm, d), lambda b, i: (b, i, 0)),
                  pl.BlockSpec((None, FFN_HALO, d),
                               lambda b, i: (b, jnp.maximum(i * (tm // FFN_HALO) - 1, 0), 0)),
                  const(gn), const(wi), const(cw), const(cb), const(wo), const(gf)],
        out_specs=pl.BlockSpec((None, tm, d), lambda b, i: (b, i, 0)),
        out_shape=jax.ShapeDtypeStruct((bsz, seq, d), F32),
        compiler_params=_cparams(("parallel", "arbitrary")),
        name="ffn",
    )(x1, x1, gn, wi, cw, cb, wo, gf)


def _mixers(x, norm_mix, w_in, attn_sinks, cmp_pos_k, cmp_w1_k, cmp_w2_k, cmp_pos_v, cmp_w1_v,
            cmp_w2_v, table):
    bsz, seq, d = x.shape
    n = bsz * seq
    aq, akv = A_HEADS * HEAD_DIM, A_KV_HEADS * HEAD_DIM
    bq, bkv = B_HEADS * HEAD_DIM, B_KV_GROUPS * HEAD_DIM
    n_gate = 3 * B_HEADS
    assert seq % SEL_TILE == 0 and seq // SLC_BLOCK <= AUG_LANES - bkv
    splits = (aq, akv, akv, bq, bkv, bkv, bkv, bkv, bkv, bkv, n_gate, d, d)
    off = np.concatenate([[0], np.cumsum(splits)]).astype(int)
    seg = lambda k: w_in[:, off[k]:off[k + 1]]
    assert akv == bkv and aq == bq
    w_gate_nsa = jnp.pad(seg(10), ((0, 0), (0, V7X_LANES - n_gate)))
    w1 = jnp.concatenate([seg(0), seg(3), seg(1), seg(8), seg(6), seg(2), seg(7), seg(9),
                          seg(4), seg(5), w_gate_nsa], axis=1).astype(BF16)
    tm = min(1024, seq)
    g_mix = norm_mix.reshape(1, d)

    qt, kk, kaug, vt, p32 = _proj(x.reshape(n, d), g_mix, w1, aq, seq, tm)
    nb = seq // Q_BLOCK
    qt = qt.reshape(2, bsz, nb, aq, Q_BLOCK)
    kk = kk.reshape(bsz, seq, 2 * bkv)
    kaug = kaug.reshape(bsz, seq, AUG_LANES)
    vt = vt.reshape(3, bsz, nb, bkv, Q_BLOCK)
    p32 = p32.reshape(bsz, seq, 2 * bkv + V7X_LANES)

    table = table * LOG2E
    table_a, table_b = table[:, :A_HEADS], table[:, A_HEADS:]
    sink_row = jnp.repeat(attn_sinks * LOG2E, Q_BLOCK).reshape(1, A_HEADS * Q_BLOCK)

    ncp = seq // CMP_STRIDE
    n_cmp = (seq - CMP_BLOCK) // CMP_STRIDE + 1
    n_slc = seq // SLC_BLOCK
    topk = min(SLC_TOPK, n_slc)

    k_cmp = _compress(p32, 0, cmp_pos_k, cmp_w1_k, cmp_w2_k, n_cmp, False)
    v_cmp_t = _compress(p32, bkv, cmp_pos_v, cmp_w1_v, cmp_w2_v, n_cmp, True)

    fc = _cmp_bias_rows(table_b)
    win_a, win_b = A_WINDOW + Q_BLOCK, NSA_WINDOW + Q_BLOCK
    bias_a, bw, bn = _toeplitz_tables(
        [_band_vector(table_a, win_a, A_WINDOW, A_WINDOW, False),
         _band_vector(table_b, win_b, NSA_WINDOW, NSA_WINDOW, False),
         _band_vector(table_b, 2 * SEL_TILE, SEL_TILE, None, True),
         _band_vector(table_b, 2 * SEL_TILE, SEL_TILE + Q_BLOCK, None, True)],
        [(0, 0), (1, 0), (2, 0), (2, 2 * SEL_TILE)],
        [win_a + A_WINDOW, win_b + NSA_WINDOW, 4 * SEL_TILE])
    ovt = jnp.asarray(_overlap_t(ncp, n_slc, n_cmp))
    y_a, y_b = _mixers_call(qt, kk, vt, p32, k_cmp, v_cmp_t, kaug, bias_a, sink_row, fc, bw, bn, ovt,
                            bsz, seq, 2 * bkv, topk)
    return y_a, y_b, w_in[:, off[11]:off[13]]


def _layer(x, norm_mix, w_in, attn_sinks, cmp_pos_k, cmp_w1_k, cmp_w2_k, cmp_pos_v, cmp_w1_v,
           cmp_w2_v, w_up_a, w_up_b, w_out, norm_ffn, w_ffn_in, conv_w, conv_b, w_ffn_out,
           table, norm_final):
    bsz, seq, d = x.shape
    n = bsz * seq
    y_a, y_b, w_merge_gates = _mixers(x, norm_mix, w_in, attn_sinks, cmp_pos_k, cmp_w1_k, cmp_w2_k,
                                      cmp_pos_v, cmp_w1_v, cmp_w2_v, table)

    x1 = _merge(x.reshape(n, d), norm_mix.reshape(1, d), y_a.reshape(n, -1), y_b.reshape(n, -1),
                w_merge_gates.astype(BF16), w_up_a.astype(BF16), w_up_b.astype(BF16),
                w_out.astype(BF16), min(1024, n))

    return _ffn(x1.reshape(bsz, seq, d), norm_ffn.reshape(1, d), w_ffn_in.astype(BF16), conv_w,
                conv_b.reshape(1, -1), w_ffn_out.astype(BF16), norm_final.reshape(1, d),
                min(512, seq), 256)


def kernel(x, norm_mix, w_in, attn_sinks, cmp_pos_k, cmp_w1_k, cmp_w2_k, cmp_pos_v, cmp_w1_v, cmp_w2_v, w_up_a, w_up_b, w_out, norm_ffn, w_ffn_in, conv_w, conv_b, w_ffn_out, rel_bias_table, norm_final):
    assert norm_mix.shape[0] == 1, "single-layer block"
    return _layer(x, norm_mix[0], w_in[0], attn_sinks[0], cmp_pos_k[0], cmp_w1_k[0], cmp_w2_k[0],
                  cmp_pos_v[0], cmp_w1_v[0], cmp_w2_v[0], w_up_a[0], w_up_b[0], w_out[0],
                  norm_ffn[0], w_ffn_in[0], conv_w[0], conv_b[0], w_ffn_out[0], rel_bias_table,
                  norm_final)
```

```python
import functools
import math

import numpy as np
import jax
import jax.numpy as jnp
from jax import lax
from jax.experimental import pallas as pl
from jax.experimental.pallas import tpu as pltpu

F32 = jnp.float32
BF16 = jnp.bfloat16

HEAD_DIM = 64
A_HEADS = 8
A_KV_HEADS = 2
A_WINDOW = 128
B_HEADS = 8
B_KV_GROUPS = 2
REP = 4
CMP_BLOCK = 32
CMP_STRIDE = 16
SLC_BLOCK = 64
SLC_SHIFT = 6
SLC_TOPK = 16
NSA_WINDOW = 512
NUM_BUCKETS = 32
MAX_DISTANCE = 128
CONV_WIDTH = 3
Q_BLOCK = 128
EPS = 1e-6
NEG = -1e30
BIG = 1e30
SCALE = HEAD_DIM ** -0.5
LOG2E = math.log2(math.e)
Q_SCALE = SCALE * LOG2E
SUM_ROWS = 16

V7X_LANES = 128
V7X_SUBLANES = 8
V7X_VMEM_BYTES = 64 * 1024 * 1024
VMEM_LIMIT = 56 * 1024 * 1024


def _cparams(semantics):
    return pltpu.CompilerParams(dimension_semantics=semantics, vmem_limit_bytes=VMEM_LIMIT)


def _bucket_np(dist):
    dist = np.maximum(dist, 0)
    max_exact = NUM_BUCKETS // 2
    d = np.maximum(dist, 1).astype(np.float64)
    large = max_exact + (np.log(d / max_exact) / math.log(MAX_DISTANCE / max_exact)
                         * (NUM_BUCKETS - max_exact)).astype(np.int32)
    large = np.minimum(large, NUM_BUCKETS - 1)
    return np.where(dist < max_exact, dist, large).astype(np.int32)


def _table_lookup(table, dist):
    idx = _bucket_np(dist).reshape(-1)
    onehot = np.zeros((NUM_BUCKETS, idx.size), np.float32)
    onehot[idx, np.arange(idx.size)] = 1.0
    vals = jnp.dot(table.T, jnp.asarray(onehot), precision=lax.Precision.HIGHEST)
    return vals.reshape((table.shape[1],) + dist.shape)


def _band_vector(table, n_keys, offset, window, shift_far):
    length = n_keys + Q_BLOCK
    m = np.arange(length)
    m = np.where(m < Q_BLOCK, m, m - length)
    dist = m + offset
    ok = dist >= 0 if window is None else (dist >= 0) & (dist < window)
    u = _table_lookup(table, dist)
    if shift_far:
        u = u - table[NUM_BUCKETS - 1][:, None]
    return jnp.where(ok[None, :], u, NEG)


def _toeplitz_kernel(*refs, dest):
    n_tab = len(dest)
    covered = [0] * (len(refs) - n_tab)
    for u_ref, (j, row0) in zip(refs[:n_tab], dest):
        o_ref = refs[n_tab + j]
        n_heads, length = u_ref.shape
        n_keys = length - Q_BLOCK
        covered[j] = max(covered[j], row0 + n_keys)
        u2 = jnp.concatenate([u_ref[...], u_ref[...]], axis=1)
        for r0 in range(0, n_keys, Q_BLOCK):
            w0 = (-r0 - (Q_BLOCK - 1)) % length
            win = u2[:, w0:w0 + 2 * Q_BLOCK]
            for h in range(n_heads):
                x = jnp.broadcast_to(win[h:h + 1, :], (Q_BLOCK, 2 * Q_BLOCK))
                x = pltpu.roll(x, Q_BLOCK + 1, 1, stride=1, stride_axis=0)
                o_ref[row0 + r0:row0 + r0 + Q_BLOCK, h * Q_BLOCK:(h + 1) * Q_BLOCK] = x[:, :Q_BLOCK]
    for o_ref, done in zip(refs[n_tab:], covered):
        if o_ref.shape[0] > done:
            o_ref[done:, :] = jnp.full((o_ref.shape[0] - done, o_ref.shape[1]), NEG, F32)


def _toeplitz_tables(us, dest, n_rows):
    n_heads = us[0].shape[0]
    kern = functools.partial(_toeplitz_kernel, dest=tuple(dest))
    return pl.pallas_call(
        kern,
        out_shape=[jax.ShapeDtypeStruct((r, n_heads * Q_BLOCK), F32) for r in n_rows],
        compiler_params=pltpu.CompilerParams(vmem_limit_bytes=VMEM_LIMIT),
        name="bias_tables",
    )(*us)


def _cmp_bias_table(table, ncp):
    q = np.arange(Q_BLOCK)
    last_pos = CMP_BLOCK - 1
    c_lo = -((MAX_DISTANCE - 1 + last_pos) // CMP_STRIDE)
    c_hi = (Q_BLOCK - 1 - last_pos) // CMP_STRIDE
    cs = np.arange(c_lo, c_hi + 1)
    dist = q[None, :] - CMP_STRIDE * cs[:, None] - last_pos
    near = jnp.where((dist >= 0)[None], _table_lookup(table, dist), NEG)
    n_heads = table.shape[1]
    near = jnp.transpose(near, (1, 0, 2)).reshape(len(cs), n_heads * Q_BLOCK)
    far = jnp.repeat(table[NUM_BUCKETS - 1], Q_BLOCK)[None, :]
    below = jnp.broadcast_to(far, (ncp + c_lo, n_heads * Q_BLOCK))
    above = jnp.full((ncp - 1 - c_hi, n_heads * Q_BLOCK), NEG, F32)
    return jnp.concatenate([below, near, above], axis=0)


def _overlap_t(n_cmp_pad, n_slc, n_cmp):
    r = SLC_BLOCK // CMP_STRIDE
    c = CMP_BLOCK // CMP_STRIDE
    j, m, n = np.meshgrid(np.arange(n_slc), np.arange(r), np.arange(c), indexing='ij')
    i = r * j + m - n
    ok = (i >= 0) & (i < n_cmp)
    mat = np.zeros((n_slc, n_cmp_pad), np.float32)
    np.add.at(mat, (j[ok], i[ok]), 1.0)
    return mat


def _rms(x, g):
    return x * lax.rsqrt(jnp.mean(x * x, axis=-1, keepdims=True) + EPS) * g


PROJ_CHUNK = 256
VT_A, VT_SL, VT_W = 0, 1, 2
QT_A, QT_B = 0, 1
KK_A, KK_W = 0, 1


def _proj_kernel(x_ref, g_ref, w_ref, qt_ref, kk_ref, kaug_ref, vt_ref, o32_ref, *, seq):
    tm = x_ref.shape[0]
    kv = B_KV_GROUPS * HEAD_DIM
    qw = qt_ref.shape[2]
    hb = _rms(x_ref[...], g_ref[...]).astype(BF16)

    def cols(c0, width=PROJ_CHUNK):
        return jnp.dot(hb, w_ref[:, c0:c0 + width], preferred_element_type=F32)

    def put_transposed(ref, a, r, row0=0):
        for j in range(tm // Q_BLOCK):
            ref[a, j, row0:row0 + r.shape[1], :] = jnp.transpose(
                r[j * Q_BLOCK:(j + 1) * Q_BLOCK, :]).astype(BF16)

    for c0 in range(0, 2 * qw, PROJ_CHUNK):
        put_transposed(qt_ref, c0 // qw, cols(c0) * Q_SCALE, c0 % qw)
    kk_ref[...] = cols(2 * qw).astype(BF16)
    r = cols(2 * qw + 2 * kv)
    pos = (pl.program_id(0) * tm) % seq + lax.broadcasted_iota(jnp.int32, (tm, kv), 0)
    onehot = lax.broadcasted_iota(jnp.int32, (tm, kv), 1) == jnp.right_shift(pos, SLC_SHIFT)
    kaug_ref[:, :kv] = r[:, :kv].astype(BF16)
    kaug_ref[:, kv:] = jnp.where(onehot, 1.0, 0.0).astype(BF16)
    put_transposed(vt_ref, VT_A, r[:, kv:])
    r = cols(2 * qw + 4 * kv)
    put_transposed(vt_ref, VT_SL, r[:, :kv])
    put_transposed(vt_ref, VT_W, r[:, kv:])
    o32_ref[:, :2 * kv] = cols(2 * qw + 6 * kv)
    o32_ref[:, 2 * kv:] = jax.nn.sigmoid(cols(2 * qw + 8 * kv, V7X_LANES))


def _proj(x2d, g, w1, qw, seq, tm):
    n, d = x2d.shape
    kv = B_KV_GROUPS * HEAD_DIM
    n32 = 2 * kv + V7X_LANES
    nb = tm // Q_BLOCK
    assert AUG_LANES == 2 * kv == PROJ_CHUNK and seq % tm == 0 and tm % Q_BLOCK == 0
    rows = lambda width: pl.BlockSpec((tm, width), lambda i: (i, 0))
    blocks_t = lambda a, width: pl.BlockSpec((a, nb, width, Q_BLOCK), lambda i: (0, i, 0, 0))
    kern = functools.partial(_proj_kernel, seq=seq)
    return pl.pallas_call(
        kern,
        grid=(n // tm,),
        in_specs=[rows(d), pl.BlockSpec((1, d), lambda i: (0, 0)),
                  pl.BlockSpec(w1.shape, lambda i: (0, 0))],
        out_specs=[blocks_t(2, qw), rows(2 * kv), rows(AUG_LANES), blocks_t(3, kv), rows(n32)],
        out_shape=[jax.ShapeDtypeStruct((2, n // Q_BLOCK, qw, Q_BLOCK), BF16),
                   jax.ShapeDtypeStruct((n, 2 * kv), BF16),
                   jax.ShapeDtypeStruct((n, AUG_LANES), BF16),
                   jax.ShapeDtypeStruct((3, n // Q_BLOCK, kv, Q_BLOCK), BF16),
                   jax.ShapeDtypeStruct((n, n32), F32)],
        compiler_params=_cparams(("parallel",)),
        name="proj",
    )(x2d, g, w1)


CMP_PAIR = 2


def _compress_kernel(x_ref, pos_ref, w1_ref, w2_ref, o_ref, *, n_cmp, transpose_out):
    ncp = x_ref.shape[0] // CMP_STRIDE

    def group_diag(w):
        zero = jnp.zeros(w.shape, w.dtype)
        return jnp.concatenate(
            [jnp.concatenate([w if g == gg else zero for gg in range(B_KV_GROUPS)], axis=1)
             for g in range(B_KV_GROUPS)], axis=0)

    halves = []
    for half in range(CMP_BLOCK // CMP_STRIDE):
        acc = None
        for l0 in range(half * CMP_STRIDE, (half + 1) * CMP_STRIDE, CMP_PAIR):
            lhs = jnp.concatenate(
                [(x_ref[pl.ds(l % CMP_STRIDE, ncp, stride=CMP_STRIDE), :]
                  + jnp.concatenate([pos_ref[l:l + 1, :]] * B_KV_GROUPS, axis=1)).astype(BF16)
                 for l in range(l0, l0 + CMP_PAIR)], axis=1)
            rhs = jnp.concatenate([group_diag(w1_ref[l]) for l in range(l0, l0 + CMP_PAIR)], axis=0)
            part = jnp.dot(lhs, rhs, preferred_element_type=F32)
            acc = part if acc is None else acc + part
        halves.append(acc)
    top, bottom = halves
    h1 = top + jnp.concatenate([bottom[1:], bottom[:1]], axis=0)
    o = jnp.dot(jax.nn.gelu(h1).astype(BF16), group_diag(w2_ref[...]), preferred_element_type=F32)
    row = lax.broadcasted_iota(jnp.int32, o.shape, 0)
    o = jnp.where(row < n_cmp, o, 0.0)
    if transpose_out:
        o = jnp.transpose(o)
    o_ref[...] = o.astype(o_ref.dtype)


def _compress(p32, col, pos, w1, w2, n_cmp, transpose_out):
    bsz, seq, _ = p32.shape
    ncp = seq // CMP_STRIDE
    gd = B_KV_GROUPS * HEAD_DIM
    w1 = w1.reshape(CMP_BLOCK, HEAD_DIM, -1).astype(BF16)
    w2 = w2.astype(BF16)
    oshape = (gd, ncp) if transpose_out else (ncp, gd)
    const = lambda a: pl.BlockSpec(a.shape, lambda b: (0,) * a.ndim)
    kern = functools.partial(_compress_kernel, n_cmp=n_cmp, transpose_out=transpose_out)
    return pl.pallas_call(
        kern,
        grid=(bsz,),
        in_specs=[pl.BlockSpec((None, seq, gd), lambda b: (b, 0, col // gd)),
                  const(pos), const(w1), const(w2)],
        out_specs=pl.BlockSpec((None,) + oshape, lambda b: (b, 0, 0)),
        out_shape=jax.ShapeDtypeStruct((bsz,) + oshape, BF16),
        compiler_params=_cparams(("parallel",)),
        name="compress",
    )(p32, pos, w1, w2)


def _block_diag_qt(qt):
    zero = jnp.zeros((HEAD_DIM, Q_BLOCK), BF16)
    n_groups = qt.shape[0] // (REP * HEAD_DIM)
    return jnp.concatenate([
        jnp.concatenate([qt[(REP * g + r) * HEAD_DIM:(REP * g + r + 1) * HEAD_DIM] if gg == g else zero
                         for gg in range(n_groups) for r in range(REP)], axis=1)
        for g in range(n_groups)], axis=0)


def _pv_by_group(vt, p, with_sum):
    n_groups = vt.shape[0] // HEAD_DIM
    gl = p.shape[1] // n_groups
    pb = p.astype(BF16)
    outs, sums = [], []
    for g in range(n_groups):
        lhs = vt[g * HEAD_DIM:(g + 1) * HEAD_DIM]
        if with_sum:
            lhs = jnp.concatenate([lhs, jnp.ones((SUM_ROWS, vt.shape[1]), BF16)], axis=0)
        r = jnp.dot(lhs, pb[:, g * gl:(g + 1) * gl], preferred_element_type=F32)
        outs.append(r[:HEAD_DIM])
        sums.append(r[HEAD_DIM:HEAD_DIM + 1])
    return outs, (jnp.concatenate(sums, axis=1) if with_sum else None)


def _scale_groups(outs, row):
    gl = outs[0].shape[1]
    return [o * row[:, g * gl:(g + 1) * gl] for g, o in enumerate(outs)]


def _head_block(outs, h):
    r = h % REP
    return outs[h // REP][:, r * Q_BLOCK:(r + 1) * Q_BLOCK]


def _banded_scores(i, wq, k_ref, bias_ref, window):
    n_blk = window // Q_BLOCK + 1
    span = n_blk * Q_BLOCK
    kb0 = jnp.maximum(i - (n_blk - 1), 0)
    shift = jnp.maximum(n_blk - 1 - i, 0)
    s = jnp.dot(k_ref[pl.ds(pl.multiple_of(kb0 * Q_BLOCK, Q_BLOCK), span), :], wq,
                preferred_element_type=F32)
    return s + bias_ref[pl.ds(pl.multiple_of(shift * Q_BLOCK, Q_BLOCK), span), :], kb0


def _swa_block(i, s, kb0, vt_ref, sink_ref):
    sink = sink_ref[...]
    m = jnp.maximum(jnp.max(s, axis=0, keepdims=True), sink)
    p = jnp.exp2(s - m)
    vt = jnp.concatenate([vt_ref[kb0 + j] for j in range(A_WINDOW // Q_BLOCK + 1)], axis=1)
    outs, l = _pv_by_group(vt, p, True)
    outs = _scale_groups(outs, 1.0 / (l + jnp.exp2(sink - m)))
    heads = [_head_block(outs, h) for h in range(A_HEADS)]
    return jnp.transpose(jnp.concatenate(heads, axis=0))


SEL_TILE = 2 * Q_BLOCK
AUG_LANES = 2 * V7X_LANES
CMP_SHIFT = 4
CMP_PER_QB = Q_BLOCK // CMP_STRIDE


def _mixers_kernel(qa_ref, ka_ref, vat_ref, ba_ref, sink_ref,
                   q_ref, kcmp_ref, vcmpt_ref, kaug_ref, vslt_ref, kw_ref, vwt_ref, gate_ref,
                   fc_ref, bw_ref, bn_ref, ovt_ref, oa_ref, o_ref, s0_ref, s1_ref, sr_ref, *, topk):
    i = pl.program_id(1)
    n_slc, ncp = ovt_ref.shape
    gl = REP * Q_BLOCK
    nl = B_KV_GROUPS * gl
    gd = B_KV_GROUPS * HEAD_DIM
    n_win = NSA_WINDOW // Q_BLOCK + 1

    wq = _block_diag_qt(q_ref[...])

    qlane = lax.broadcasted_iota(jnp.int32, (1, nl), 1) & (Q_BLOCK - 1)
    cmax = jnp.right_shift(qlane + i * Q_BLOCK - (CMP_BLOCK - 1), CMP_SHIFT)
    c0 = pl.multiple_of(ncp - CMP_PER_QB * i, V7X_SUBLANES)
    s = jnp.dot(kcmp_ref[...], wq, preferred_element_type=F32) + fc_ref[pl.ds(c0, ncp), :]
    m = jnp.max(s, axis=0, keepdims=True)
    p = jnp.exp2(s - m)
    l = jnp.sum(p, axis=0, keepdims=True)
    p = p * jnp.where(cmax >= 0, 1.0 / l, 0.0)
    o_c, _ = _pv_by_group(vcmpt_ref[...], p, False)
    psum = jnp.concatenate(
        [sum(p[:, g * gl + r * Q_BLOCK:g * gl + (r + 1) * Q_BLOCK] for r in range(REP))
         for g in range(B_KV_GROUPS)], axis=1)
    imp = jnp.dot(ovt_ref[...], psum, preferred_element_type=F32,
                  precision=lax.Precision.HIGHEST)

    sa, kb0_a = _banded_scores(i, _block_diag_qt(qa_ref[...]), ka_ref, ba_ref, A_WINDOW)
    oa_ref[...] = _swa_block(i, sa, kb0_a, vat_ref, sink_ref).astype(oa_ref.dtype)

    sw, kb0 = _banded_scores(i, wq, kw_ref, bw_ref, NSA_WINDOW)
    mw = jnp.max(sw, axis=0, keepdims=True)
    pw = jnp.exp2(sw - mw)
    vwt = jnp.concatenate([vwt_ref[kb0 + j] for j in range(n_win)], axis=1)
    o_w, l_w = _pv_by_group(vwt, pw, True)

    sl = B_KV_GROUPS * Q_BLOCK
    jt = lax.broadcasted_iota(jnp.int32, (n_slc, sl), 0)
    second_half = (lax.broadcasted_iota(jnp.int32, (1, sl), 1) & (Q_BLOCK - 1)) >= SLC_BLOCK
    qblk = 2 * i + second_half.astype(jnp.int32)
    forced = (jt == 0) | (jt == qblk) | (jt == qblk - 1)
    score = jnp.where(forced, BIG, jnp.where(jt > qblk, NEG, imp))
    sr_ref[...] = score
    sub = V7X_SUBLANES
    sub_iota = lax.broadcasted_iota(jnp.int32, (sub, sl), 0)
    ranks = []
    for v in range(0, n_slc, sub):
        tile = sr_ref[v:v + sub, :]
        acc = jnp.zeros((sub, sl), F32)
        for c in range(n_slc):
            row = sr_ref[c:c + 1, :]
            if c < v:
                ahead = row >= tile
            elif c >= v + sub:
                ahead = row > tile
            else:
                ahead = (row > tile) | ((row == tile) & (sub_iota > c - v))
            acc = acc + jnp.where(ahead, 1.0, 0.0)
        ranks.append(acc)
    selneg = jnp.where(jnp.concatenate(ranks, axis=0) < topk, 0.0, NEG).astype(BF16)
    wsel = jnp.concatenate([selneg[:, g * Q_BLOCK:(g + 1) * Q_BLOCK]
                            for g in range(B_KV_GROUPS) for _ in range(REP)], axis=1)
    w = jnp.concatenate([wq, wsel, jnp.zeros((AUG_LANES - gd - n_slc, nl), BF16)], axis=0)

    def scores(t):
        r0 = pl.multiple_of(t * SEL_TILE, SEL_TILE)
        return jnp.dot(kaug_ref[pl.ds(r0, SEL_TILE), :], w, preferred_element_type=F32)

    def update(carry, st, t, st_max=None):
        m_i, l_i, a0, a1 = carry
        if st_max is None:
            st_max = jnp.max(st, axis=0, keepdims=True)
        m_n = jnp.maximum(m_i, st_max)
        alpha = jnp.exp2(m_i - m_n)
        pt = jnp.exp2(st - m_n)
        blk0 = t * (SEL_TILE // Q_BLOCK)
        vt = jnp.concatenate([vslt_ref[blk0 + j] for j in range(SEL_TILE // Q_BLOCK)], axis=1)
        pv, l_t = _pv_by_group(vt, pt, True)
        a0, a1 = (a + b for a, b in zip(_scale_groups((a0, a1), alpha), pv))
        return m_n, alpha * l_i + l_t, a0, a1

    last = i // 2
    par = i % 2
    n_far = jnp.maximum(last - 1, 0)
    carry = (jnp.full((1, nl), NEG, F32), jnp.zeros((1, nl), F32),
             jnp.zeros((HEAD_DIM, gl), F32), jnp.zeros((HEAD_DIM, gl), F32))
    def stash(ref, t):
        st = scores(t)
        ref[...] = st
        return jnp.max(st, axis=0, keepdims=True)

    def two_tiles(u, carry_mx):
        carry, mx0 = carry_mx
        t = 2 * u
        mx1 = stash(s1_ref, t + 1)
        carry = update(carry, s0_ref[...], t, mx0)
        mx0 = stash(s0_ref, t + 2)
        return update(carry, s1_ref[...], t + 1, mx1), mx0

    def one_tile(t, carry_mx):
        carry, mx0 = carry_mx
        mx1 = stash(s1_ref, t + 1)
        carry = update(carry, s0_ref[...], t, mx0)
        s0_ref[...] = s1_ref[...]
        return carry, mx1

    carry_mx = (carry, stash(s0_ref, 0))
    carry_mx = lax.fori_loop(0, n_far // 4,
                             lambda v, c: two_tiles(2 * v + 1, two_tiles(2 * v, c)), carry_mx)
    carry_mx = lax.fori_loop(n_far // 4 * 2, n_far // 2, two_tiles, carry_mx)
    carry, _ = lax.fori_loop(n_far // 2 * 2, n_far, one_tile, carry_mx)
    s1_ref[...] = scores(last)
    near0 = pl.multiple_of(par * (2 * SEL_TILE), SEL_TILE)
    before = bn_ref[pl.ds(near0, SEL_TILE), :] + jnp.where(last > 0, 0.0, NEG)
    carry = update(carry, s0_ref[...] + before, jnp.maximum(last - 1, 0))
    m_s, l_s, a0, a1 = update(carry, s1_ref[...] + bn_ref[pl.ds(near0 + SEL_TILE, SEL_TILE), :], last)
    gt = jnp.transpose(gate_ref[...])
    inv_s, inv_w = 1.0 / l_s, 1.0 / l_w
    heads = []
    for h in range(B_HEADS):
        hl = slice(h * Q_BLOCK, (h + 1) * Q_BLOCK)
        heads.append(gt[3 * h:3 * h + 1, :] * _head_block(o_c, h)
                     + (gt[3 * h + 1:3 * h + 2, :] * inv_s[:, hl]) * _head_block((a0, a1), h)
                     + (gt[3 * h + 2:3 * h + 3, :] * inv_w[:, hl]) * _head_block(o_w, h))
    o_ref[...] = jnp.transpose(jnp.concatenate(heads, axis=0)).astype(o_ref.dtype)


def _mixers_call(qt, kk, vt, p32, kcmp, vcmpt, kaug, ba, sink_row, fc, bw, bn, ovt, bsz, seq,
                 col_gate, topk):
    nb = seq // Q_BLOCK
    assert A_HEADS == B_HEADS and A_KV_HEADS == B_KV_GROUPS
    qw = B_HEADS * HEAD_DIM
    kw = B_KV_GROUPS * HEAD_DIM
    per_batch = lambda a: pl.BlockSpec((None,) + a.shape[1:], lambda b, i: (b,) + (0,) * (a.ndim - 1))
    const = lambda a: pl.BlockSpec(a.shape, lambda b, i: (0,) * a.ndim)
    qt_seg = lambda a: pl.BlockSpec((None, None, None, qw, Q_BLOCK), lambda b, i: (a, b, i, 0, 0))
    kk_seg = lambda a: pl.BlockSpec((None, seq, kw), lambda b, i: (b, 0, a))
    vt_seg = lambda a: pl.BlockSpec((None, None) + vt.shape[2:], lambda b, i: (a, b, 0, 0, 0))
    y_spec = pl.BlockSpec((None, Q_BLOCK, qw), lambda b, i: (b, i, 0))
    kern = functools.partial(_mixers_kernel, topk=topk)
    return pl.pallas_call(
        kern,
        grid=(bsz, nb),
        in_specs=[qt_seg(QT_A), kk_seg(KK_A), vt_seg(VT_A), const(ba), const(sink_row),
                  qt_seg(QT_B), per_batch(kcmp), per_batch(vcmpt), per_batch(kaug), vt_seg(VT_SL),
                  kk_seg(KK_W), vt_seg(VT_W),
                  pl.BlockSpec((None, Q_BLOCK, V7X_LANES), lambda b, i: (b, i, col_gate // V7X_LANES)),
                  const(fc), const(bw), const(bn), const(ovt)],
        out_specs=[y_spec, y_spec],
        out_shape=[jax.ShapeDtypeStruct((bsz, seq, qw), BF16)] * 2,
        scratch_shapes=[pltpu.VMEM((SEL_TILE, B_HEADS * Q_BLOCK), F32),
                        pltpu.VMEM((SEL_TILE, B_HEADS * Q_BLOCK), F32),
                        pltpu.VMEM((ovt.shape[0], B_KV_GROUPS * Q_BLOCK), F32)],
        compiler_params=_cparams(("parallel", "arbitrary")),
        name="mixers",
    )(qt, kk, vt, ba, sink_row, qt, kcmp, vcmpt, kaug, vt, kk, vt, p32, fc, bw, bn, ovt)


def _merge_kernel(x_ref, g_ref, ya_ref, yb_ref, wg_ref, wua_ref, wub_ref, wo_ref, o_ref):
    x = x_ref[...]
    d = x.shape[1]
    hb = _rms(x, g_ref[...]).astype(BF16)
    ga = jax.nn.sigmoid(jnp.dot(hb, wg_ref[:, :d], preferred_element_type=F32))
    gb = jax.nn.sigmoid(jnp.dot(hb, wg_ref[:, d:], preferred_element_type=F32))
    ua = jnp.dot(ya_ref[...], wua_ref[...], preferred_element_type=F32)
    ub = jnp.dot(yb_ref[...], wub_ref[...], preferred_element_type=F32)
    merged = ga * ua + gb * ub
    o_ref[...] = x + jnp.dot(merged.astype(BF16), wo_ref[...], preferred_element_type=F32)


def _merge(x2d, g, ya, yb, wg, wua, wub, wo, tm):
    n, d = x2d.shape
    const = lambda a: pl.BlockSpec(a.shape, lambda i: (0, 0))
    row = lambda a: pl.BlockSpec((tm, a.shape[1]), lambda i: (i, 0))
    return pl.pallas_call(
        _merge_kernel,
        grid=(n // tm,),
        in_specs=[row(x2d), const(g), row(ya), row(yb), const(wg), const(wua), const(wub), const(wo)],
        out_specs=row(x2d),
        out_shape=jax.ShapeDtypeStruct((n, d), F32),
        compiler_params=_cparams(("parallel",)),
        name="merge",
    )(x2d, g, ya, yb, wg, wua, wub, wo)


FFN_HALO = 16
FFN_GROUP = 6


def _ffn_kernel(xc_ref, xp_ref, gn_ref, wi_ref, cw_ref, cb_ref, wo_ref, gf_ref, o_ref, *, chunk):
    i = pl.program_id(1)
    xc = xc_ref[...]
    tm = xc.shape[0]
    gn = gn_ref[...]
    hp = _rms(xp_ref[...], gn) * jnp.where(i > 0, 1.0, 0.0)
    h = jnp.concatenate([hp, _rms(xc, gn)], axis=0).astype(BF16)
    d_ff = wo_ref.shape[0]

    def up(c0):
        return jnp.dot(h, wi_ref[:, c0:c0 + chunk], preferred_element_type=F32)

    def conv(ext, c0):
        cw = cw_ref[:, c0:c0 + chunk]
        out = cb_ref[:, c0:c0 + chunk]
        for k in range(CONV_WIDTH):
            off = FFN_HALO - (CONV_WIDTH - 1) + k
            out = out + cw[k:k + 1, :] * ext[off:off + tm]
        return out

    acc = None
    acts = []
    nxt = (up(0), up(d_ff))
    for c0 in range(0, d_ff, chunk):
        ext_u, ext_g = nxt
        if c0 + chunk < d_ff:
            nxt = (up(c0 + chunk), up(d_ff + c0 + chunk))
        acts.append((jax.nn.silu(conv(ext_g, d_ff + c0)) * conv(ext_u, c0)).astype(BF16))
        if len(acts) == FFN_GROUP or c0 + chunk >= d_ff:
            r0 = c0 + chunk - len(acts) * chunk
            part = jnp.dot(jnp.concatenate(acts, axis=1), wo_ref[r0:c0 + chunk, :],
                           preferred_element_type=F32)
            acc = part if acc is None else acc + part
            acts = []
    o_ref[...] = _rms(xc + acc, gf_ref[...])


def _ffn(x1, gn, wi, cw, cb, wo, gf, tm, chunk):
    bsz, seq, d = x1.shape
    const = lambda a: pl.BlockSpec(a.shape, lambda b, i: (0, 0))
    kern = functools.partial(_ffn_kernel, chunk=chunk)
    return pl.pallas_call(
        kern,
        grid=(bsz, seq // tm),
        in_specs=[pl.BlockSpec((None, tm, d), lambda b, i: (b, i, 0)),
                  pl.BlockSpec((None, FFN_HALO, d),
                               lambda b, i: (b, jnp.maximum(i * (tm // FFN_HALO) - 1, 0), 0)),
                  const(gn), const(wi), const(cw), const(cb), const(wo), const(gf)],
        out_specs=pl.BlockSpec((None, tm, d), lambda b, i: (b, i, 0)),
        out_shape=jax.ShapeDtypeStruct((bsz, seq, d), F32),
        compiler_params=_cparams(("parallel", "arbitrary")),
        name="ffn",
    )(x1, x1, gn, wi, cw, cb, wo, gf)


def _mixers(x, norm_mix, w_in, attn_sinks, cmp_pos_k, cmp_w1_k, cmp_w2_k, cmp_pos_v, cmp_w1_v,
            cmp_w2_v, table):
    bsz, seq, d = x.shape
    n = bsz * seq
    aq, akv = A_HEADS * HEAD_DIM, A_KV_HEADS * HEAD_DIM
    bq, bkv = B_HEADS * HEAD_DIM, B_KV_GROUPS * HEAD_DIM
    n_gate = 3 * B_HEADS
    assert seq % SEL_TILE == 0 and seq // SLC_BLOCK <= AUG_LANES - bkv
    splits = (aq, akv, akv, bq, bkv, bkv, bkv, bkv, bkv, bkv, n_gate, d, d)
    off = np.concatenate([[0], np.cumsum(splits)]).astype(int)
    seg = lambda k: w_in[:, off[k]:off[k + 1]]
    assert akv == bkv and aq == bq
    w_gate_nsa = jnp.pad(seg(10), ((0, 0), (0, V7X_LANES - n_gate)))
    w1 = jnp.concatenate([seg(0), seg(3), seg(1), seg(8), seg(6), seg(2), seg(7), seg(9),
                          seg(4), seg(5), w_gate_nsa], axis=1).astype(BF16)
    tm = min(1024, seq)
    g_mix = norm_mix.reshape(1, d)

    qt, kk, kaug, vt, p32 = _proj(x.reshape(n, d), g_mix, w1, aq, seq, tm)
    nb = seq // Q_BLOCK
    qt = qt.reshape(2, bsz, nb, aq, Q_BLOCK)
    kk = kk.reshape(bsz, seq, 2 * bkv)
    kaug = kaug.reshape(bsz, seq, AUG_LANES)
    vt = vt.reshape(3, bsz, nb, bkv, Q_BLOCK)
    p32 = p32.reshape(bsz, seq, 2 * bkv + V7X_LANES)

    table = table * LOG2E
    table_a, table_b = table[:, :A_HEADS], table[:, A_HEADS:]
    sink_row = jnp.repeat(attn_sinks * LOG2E, Q_BLOCK).reshape(1, A_HEADS * Q_BLOCK)

    ncp = seq // CMP_STRIDE
    n_cmp = (seq - CMP_BLOCK) // CMP_STRIDE + 1
    n_slc = seq // SLC_BLOCK
    topk = min(SLC_TOPK, n_slc)

    k_cmp = _compress(p32, 0, cmp_pos_k, cmp_w1_k, cmp_w2_k, n_cmp, False)
    v_cmp_t = _compress(p32, bkv, cmp_pos_v, cmp_w1_v, cmp_w2_v, n_cmp, True)

    fc = _cmp_bias_table(table_b, ncp)
    win_a, win_b = A_WINDOW + Q_BLOCK, NSA_WINDOW + Q_BLOCK
    bias_a, bw, bn = _toeplitz_tables(
        [_band_vector(table_a, win_a, A_WINDOW, A_WINDOW, False),
         _band_vector(table_b, win_b, NSA_WINDOW, NSA_WINDOW, False),
         _band_vector(table_b, 2 * SEL_TILE, SEL_TILE, None, True),
         _band_vector(table_b, 2 * SEL_TILE, SEL_TILE + Q_BLOCK, None, True)],
        [(0, 0), (1, 0), (2, 0), (2, 2 * SEL_TILE)],
        [win_a + A_WINDOW, win_b + NSA_WINDOW, 4 * SEL_TILE])
    ovt = jnp.asarray(_overlap_t(ncp, n_slc, n_cmp))
    y_a, y_b = _mixers_call(qt, kk, vt, p32, k_cmp, v_cmp_t, kaug, bias_a, sink_row, fc, bw, bn, ovt,
                            bsz, seq, 2 * bkv, topk)
    return y_a, y_b, w_in[:, off[11]:off[13]]


def _layer(x, norm_mix, w_in, attn_sinks, cmp_pos_k, cmp_w1_k, cmp_w2_k, cmp_pos_v, cmp_w1_v,
           cmp_w2_v, w_up_a, w_up_b, w_out, norm_ffn, w_ffn_in, conv_w, conv_b, w_ffn_out,
           table, norm_final):
    bsz, seq, d = x.shape
    n = bsz * seq
    y_a, y_b, w_merge_gates = _mixers(x, norm_mix, w_in, attn_sinks, cmp_pos_k, cmp_w1_k, cmp_w2_k,
                                      cmp_pos_v, cmp_w1_v, cmp_w2_v, table)

    x1 = _merge(x.reshape(n, d), norm_mix.reshape(1, d), y_a.reshape(n, -1), y_b.reshape(n, -1),
                w_merge_gates.astype(BF16), w_up_a.astype(BF16), w_up_b.astype(BF16),
                w_out.astype(BF16), min(1024, n))

    return _ffn(x1.reshape(bsz, seq, d), norm_ffn.reshape(1, d), w_ffn_in.astype(BF16), conv_w,
                conv_b.reshape(1, -1), w_ffn_out.astype(BF16), norm_final.reshape(1, d),
                min(512, seq), 256)


def kernel(x, norm_mix, w_in, attn_sinks, cmp_pos_k, cmp_w1_k, cmp_w2_k, cmp_pos_v, cmp_w1_v, cmp_w2_v, w_up_a, w_up_b, w_out, norm_ffn, w_ffn_in, conv_w, conv_b, w_ffn_out, rel_bias_table, norm_final):
    assert norm_mix.shape[0] == 1, "single-layer block"
    return _layer(x, norm_mix[0], w_in[0], attn_sinks[0], cmp_pos_k[0], cmp_w1_k[0], cmp_w2_k[0],
                  cmp_pos_v[0], cmp_w1_v[0], cmp_w2_v[0], w_up_a[0], w_up_b[0], w_out[0],
                  norm_ffn[0], w_ffn_in[0], conv_w[0], conv_b[0], w_ffn_out[0], rel_bias_table,
                  norm_final)
```

```python
import functools
import math

import numpy as np
import jax
import jax.numpy as jnp
from jax import lax
from jax.experimental import pallas as pl
from jax.experimental.pallas import tpu as pltpu

F32 = jnp.float32
BF16 = jnp.bfloat16

HEAD_DIM = 64
A_HEADS = 8
A_KV_HEADS = 2
A_WINDOW = 128
B_HEADS = 8
B_KV_GROUPS = 2
REP = 4
CMP_BLOCK = 32
CMP_STRIDE = 16
SLC_BLOCK = 64
SLC_SHIFT = 6
SLC_TOPK = 16
NSA_WINDOW = 512
NUM_BUCKETS = 32
MAX_DISTANCE = 128
CONV_WIDTH = 3
Q_BLOCK = 128
EPS = 1e-6
NEG = -1e30
BIG = 1e30
SCALE = HEAD_DIM ** -0.5
LOG2E = math.log2(math.e)
Q_SCALE = SCALE * LOG2E
SUM_ROWS = 16

V7X_LANES = 128
V7X_SUBLANES = 8
V7X_VMEM_BYTES = 64 * 1024 * 1024
VMEM_LIMIT = V7X_VMEM_BYTES * 7 // 8


def _cparams(semantics):
    return pltpu.CompilerParams(dimension_semantics=semantics, vmem_limit_bytes=VMEM_LIMIT)


def _bucket_np(dist):
    dist = np.maximum(dist, 0)
    max_exact = NUM_BUCKETS // 2
    d = np.maximum(dist, 1).astype(np.float64)
    large = max_exact + (np.log(d / max_exact) / math.log(MAX_DISTANCE / max_exact)
                         * (NUM_BUCKETS - max_exact)).astype(np.int32)
    large = np.minimum(large, NUM_BUCKETS - 1)
    return np.where(dist < max_exact, dist, large).astype(np.int32)


def _table_lookup(table, dist):
    idx = _bucket_np(dist).reshape(-1)
    onehot = np.zeros((NUM_BUCKETS, idx.size), np.float32)
    onehot[idx, np.arange(idx.size)] = 1.0
    vals = jnp.dot(table.T, jnp.asarray(onehot), precision=lax.Precision.HIGHEST)
    return vals.reshape((table.shape[1],) + dist.shape)


def _band_vector(table, n_keys, offset, window, shift_far):
    length = n_keys + Q_BLOCK
    m = np.arange(length)
    m = np.where(m < Q_BLOCK, m, m - length)
    dist = m + offset
    ok = dist >= 0 if window is None else (dist >= 0) & (dist < window)
    u = _table_lookup(table, dist)
    if shift_far:
        u = u - table[NUM_BUCKETS - 1][:, None]
    return jnp.where(ok[None, :], u, NEG)


def _toeplitz_kernel(*refs, dest):
    n_tab = len(dest)
    covered = [0] * (len(refs) - n_tab)
    for u_ref, (j, row0) in zip(refs[:n_tab], dest):
        o_ref = refs[n_tab + j]
        n_heads, length = u_ref.shape
        n_keys = length - Q_BLOCK
        covered[j] = max(covered[j], row0 + n_keys)
        u2 = jnp.concatenate([u_ref[...], u_ref[...]], axis=1)
        for r0 in range(0, n_keys, Q_BLOCK):
            w0 = (-r0 - (Q_BLOCK - 1)) % length
            win = u2[:, w0:w0 + 2 * Q_BLOCK]
            for h in range(n_heads):
                x = jnp.broadcast_to(win[h:h + 1, :], (Q_BLOCK, 2 * Q_BLOCK))
                x = pltpu.roll(x, Q_BLOCK + 1, 1, stride=1, stride_axis=0)
                o_ref[row0 + r0:row0 + r0 + Q_BLOCK, h * Q_BLOCK:(h + 1) * Q_BLOCK] = x[:, :Q_BLOCK]
    for o_ref, done in zip(refs[n_tab:], covered):
        if o_ref.shape[0] > done:
            o_ref[done:, :] = jnp.full((o_ref.shape[0] - done, o_ref.shape[1]), NEG, F32)


def _toeplitz_tables(us, dest, n_rows):
    n_heads = us[0].shape[0]
    kern = functools.partial(_toeplitz_kernel, dest=tuple(dest))
    return pl.pallas_call(
        kern,
        out_shape=[jax.ShapeDtypeStruct((r, n_heads * Q_BLOCK), F32) for r in n_rows],
        compiler_params=pltpu.CompilerParams(vmem_limit_bytes=VMEM_LIMIT),
        name="bias_tables",
    )(*us)


def _cmp_bias_table(table, ncp):
    q = np.arange(Q_BLOCK)
    last_pos = CMP_BLOCK - 1
    c_lo = -((MAX_DISTANCE - 1 + last_pos) // CMP_STRIDE)
    c_hi = (Q_BLOCK - 1 - last_pos) // CMP_STRIDE
    cs = np.arange(c_lo, c_hi + 1)
    dist = q[None, :] - CMP_STRIDE * cs[:, None] - last_pos
    near = jnp.where((dist >= 0)[None], _table_lookup(table, dist), NEG)
    n_heads = table.shape[1]
    near = jnp.transpose(near, (1, 0, 2)).reshape(len(cs), n_heads * Q_BLOCK)
    far = jnp.repeat(table[NUM_BUCKETS - 1], Q_BLOCK)[None, :]
    below = jnp.broadcast_to(far, (ncp + c_lo, n_heads * Q_BLOCK))
    above = jnp.full((ncp - 1 - c_hi, n_heads * Q_BLOCK), NEG, F32)
    return jnp.concatenate([below, near, above], axis=0)


def _overlap_t(n_cmp_pad, n_slc, n_cmp):
    r = SLC_BLOCK // CMP_STRIDE
    c = CMP_BLOCK // CMP_STRIDE
    j, m, n = np.meshgrid(np.arange(n_slc), np.arange(r), np.arange(c), indexing='ij')
    i = r * j + m - n
    ok = (i >= 0) & (i < n_cmp)
    mat = np.zeros((n_slc, n_cmp_pad), np.float32)
    np.add.at(mat, (j[ok], i[ok]), 1.0)
    return mat


def _rms(x, g):
    return x * lax.rsqrt(jnp.mean(x * x, axis=-1, keepdims=True) + EPS) * g


PROJ_CHUNK = 256
VT_A, VT_SL, VT_W = 0, 1, 2
QT_A, QT_B = 0, 1
KK_A, KK_W = 0, 1


def _proj_kernel(x_ref, g_ref, w_ref, qt_ref, kk_ref, kaug_ref, vt_ref, o32_ref, *, seq):
    tm = x_ref.shape[0]
    kv = B_KV_GROUPS * HEAD_DIM
    qw = qt_ref.shape[2]
    hb = _rms(x_ref[...], g_ref[...]).astype(BF16)

    def cols(c0, width=PROJ_CHUNK):
        return jnp.dot(hb, w_ref[:, c0:c0 + width], preferred_element_type=F32)

    def put_transposed(ref, a, r, row0=0):
        for j in range(tm // Q_BLOCK):
            ref[a, j, row0:row0 + r.shape[1], :] = jnp.transpose(
                r[j * Q_BLOCK:(j + 1) * Q_BLOCK, :]).astype(BF16)

    for c0 in range(0, 2 * qw, PROJ_CHUNK):
        put_transposed(qt_ref, c0 // qw, cols(c0) * Q_SCALE, c0 % qw)
    kk_ref[...] = cols(2 * qw).astype(BF16)
    r = cols(2 * qw + 2 * kv)
    pos = (pl.program_id(0) * tm) % seq + lax.broadcasted_iota(jnp.int32, (tm, kv), 0)
    onehot = lax.broadcasted_iota(jnp.int32, (tm, kv), 1) == jnp.right_shift(pos, SLC_SHIFT)
    kaug_ref[:, :kv] = r[:, :kv].astype(BF16)
    kaug_ref[:, kv:] = jnp.where(onehot, 1.0, 0.0).astype(BF16)
    put_transposed(vt_ref, VT_A, r[:, kv:])
    r = cols(2 * qw + 4 * kv)
    put_transposed(vt_ref, VT_SL, r[:, :kv])
    put_transposed(vt_ref, VT_W, r[:, kv:])
    o32_ref[:, :2 * kv] = cols(2 * qw + 6 * kv)
    o32_ref[:, 2 * kv:] = jax.nn.sigmoid(cols(2 * qw + 8 * kv, V7X_LANES))


def _proj(x2d, g, w1, qw, seq, tm):
    n, d = x2d.shape
    kv = B_KV_GROUPS * HEAD_DIM
    n32 = 2 * kv + V7X_LANES
    nb = tm // Q_BLOCK
    assert AUG_LANES == 2 * kv == PROJ_CHUNK and seq % tm == 0 and tm % Q_BLOCK == 0
    rows = lambda width: pl.BlockSpec((tm, width), lambda i: (i, 0))
    blocks_t = lambda a, width: pl.BlockSpec((a, nb, width, Q_BLOCK), lambda i: (0, i, 0, 0))
    kern = functools.partial(_proj_kernel, seq=seq)
    return pl.pallas_call(
        kern,
        grid=(n // tm,),
        in_specs=[rows(d), pl.BlockSpec((1, d), lambda i: (0, 0)),
                  pl.BlockSpec(w1.shape, lambda i: (0, 0))],
        out_specs=[blocks_t(2, qw), rows(2 * kv), rows(AUG_LANES), blocks_t(3, kv), rows(n32)],
        out_shape=[jax.ShapeDtypeStruct((2, n // Q_BLOCK, qw, Q_BLOCK), BF16),
                   jax.ShapeDtypeStruct((n, 2 * kv), BF16),
                   jax.ShapeDtypeStruct((n, AUG_LANES), BF16),
                   jax.ShapeDtypeStruct((3, n // Q_BLOCK, kv, Q_BLOCK), BF16),
                   jax.ShapeDtypeStruct((n, n32), F32)],
        compiler_params=_cparams(("parallel",)),
        name="proj",
    )(x2d, g, w1)


CMP_PAIR = 2


def _compress_kernel(x_ref, pos_ref, w1_ref, w2_ref, o_ref, *, n_cmp, transpose_out):
    ncp = x_ref.shape[0] // CMP_STRIDE

    def group_diag(w):
        zero = jnp.zeros(w.shape, w.dtype)
        return jnp.concatenate(
            [jnp.concatenate([w if g == gg else zero for gg in range(B_KV_GROUPS)], axis=1)
             for g in range(B_KV_GROUPS)], axis=0)

    halves = []
    for half in range(CMP_BLOCK // CMP_STRIDE):
        acc = None
        for l0 in range(half * CMP_STRIDE, (half + 1) * CMP_STRIDE, CMP_PAIR):
            lhs = jnp.concatenate(
                [(x_ref[pl.ds(l % CMP_STRIDE, ncp, stride=CMP_STRIDE), :]
                  + jnp.concatenate([pos_ref[l:l + 1, :]] * B_KV_GROUPS, axis=1)).astype(BF16)
                 for l in range(l0, l0 + CMP_PAIR)], axis=1)
            rhs = jnp.concatenate([group_diag(w1_ref[l]) for l in range(l0, l0 + CMP_PAIR)], axis=0)
            part = jnp.dot(lhs, rhs, preferred_element_type=F32)
            acc = part if acc is None else acc + part
        halves.append(acc)
    top, bottom = halves
    h1 = top + jnp.concatenate([bottom[1:], bottom[:1]], axis=0)
    o = jnp.dot(jax.nn.gelu(h1).astype(BF16), group_diag(w2_ref[...]), preferred_element_type=F32)
    row = lax.broadcasted_iota(jnp.int32, o.shape, 0)
    o = jnp.where(row < n_cmp, o, 0.0)
    if transpose_out:
        o = jnp.transpose(o)
    o_ref[...] = o.astype(o_ref.dtype)


def _compress(p32, col, pos, w1, w2, n_cmp, transpose_out):
    bsz, seq, _ = p32.shape
    ncp = seq // CMP_STRIDE
    gd = B_KV_GROUPS * HEAD_DIM
    w1 = w1.reshape(CMP_BLOCK, HEAD_DIM, -1).astype(BF16)
    w2 = w2.astype(BF16)
    oshape = (gd, ncp) if transpose_out else (ncp, gd)
    const = lambda a: pl.BlockSpec(a.shape, lambda b: (0,) * a.ndim)
    kern = functools.partial(_compress_kernel, n_cmp=n_cmp, transpose_out=transpose_out)
    return pl.pallas_call(
        kern,
        grid=(bsz,),
        in_specs=[pl.BlockSpec((None, seq, gd), lambda b: (b, 0, col // gd)),
                  const(pos), const(w1), const(w2)],
        out_specs=pl.BlockSpec((None,) + oshape, lambda b: (b, 0, 0)),
        out_shape=jax.ShapeDtypeStruct((bsz,) + oshape, BF16),
        compiler_params=_cparams(("parallel",)),
        name="compress",
    )(p32, pos, w1, w2)


def _block_diag_qt(qt):
    zero = jnp.zeros((HEAD_DIM, Q_BLOCK), BF16)
    n_groups = qt.shape[0] // (REP * HEAD_DIM)
    return jnp.concatenate([
        jnp.concatenate([qt[(REP * g + r) * HEAD_DIM:(REP * g + r + 1) * HEAD_DIM] if gg == g else zero
                         for gg in range(n_groups) for r in range(REP)], axis=1)
        for g in range(n_groups)], axis=0)


def _pv_by_group(vt, p, with_sum):
    n_groups = vt.shape[0] // HEAD_DIM
    gl = p.shape[1] // n_groups
    pb = p.astype(BF16)
    outs, sums = [], []
    for g in range(n_groups):
        lhs = vt[g * HEAD_DIM:(g + 1) * HEAD_DIM]
        if with_sum:
            lhs = jnp.concatenate([lhs, jnp.ones((SUM_ROWS, vt.shape[1]), BF16)], axis=0)
        r = jnp.dot(lhs, pb[:, g * gl:(g + 1) * gl], preferred_element_type=F32)
        outs.append(r[:HEAD_DIM])
        sums.append(r[HEAD_DIM:HEAD_DIM + 1])
    return outs, (jnp.concatenate(sums, axis=1) if with_sum else None)


def _scale_groups(outs, row):
    gl = outs[0].shape[1]
    return [o * row[:, g * gl:(g + 1) * gl] for g, o in enumerate(outs)]


def _head_block(outs, h):
    r = h % REP
    return outs[h // REP][:, r * Q_BLOCK:(r + 1) * Q_BLOCK]


def _banded_scores(i, wq, k_ref, bias_ref, window):
    n_blk = window // Q_BLOCK + 1
    span = n_blk * Q_BLOCK
    kb0 = jnp.maximum(i - (n_blk - 1), 0)
    shift = jnp.maximum(n_blk - 1 - i, 0)
    s = jnp.dot(k_ref[pl.ds(pl.multiple_of(kb0 * Q_BLOCK, Q_BLOCK), span), :], wq,
                preferred_element_type=F32)
    return s + bias_ref[pl.ds(pl.multiple_of(shift * Q_BLOCK, Q_BLOCK), span), :], kb0


def _swa_block(i, s, kb0, vt_ref, sink_ref):
    sink = sink_ref[...]
    m = jnp.maximum(jnp.max(s, axis=0, keepdims=True), sink)
    p = jnp.exp2(s - m)
    vt = jnp.concatenate([vt_ref[kb0 + j] for j in range(A_WINDOW // Q_BLOCK + 1)], axis=1)
    outs, l = _pv_by_group(vt, p, True)
    outs = _scale_groups(outs, 1.0 / (l + jnp.exp2(sink - m)))
    heads = [_head_block(outs, h) for h in range(A_HEADS)]
    return jnp.transpose(jnp.concatenate(heads, axis=0))


SEL_TILE = 2 * Q_BLOCK
AUG_LANES = 2 * V7X_LANES
CMP_SHIFT = 4
CMP_PER_QB = Q_BLOCK // CMP_STRIDE


def _mixers_kernel(qa_ref, ka_ref, vat_ref, ba_ref, sink_ref,
                   q_ref, kcmp_ref, vcmpt_ref, kaug_ref, vslt_ref, kw_ref, vwt_ref, gate_ref,
                   fc_ref, bw_ref, bn_ref, ovt_ref, oa_ref, o_ref, s0_ref, s1_ref, sr_ref, *, topk):
    i = pl.program_id(1)
    n_slc, ncp = ovt_ref.shape
    gl = REP * Q_BLOCK
    nl = B_KV_GROUPS * gl
    gd = B_KV_GROUPS * HEAD_DIM
    n_win = NSA_WINDOW // Q_BLOCK + 1

    wq = _block_diag_qt(q_ref[...])

    qlane = lax.broadcasted_iota(jnp.int32, (1, nl), 1) & (Q_BLOCK - 1)
    cmax = jnp.right_shift(qlane + i * Q_BLOCK - (CMP_BLOCK - 1), CMP_SHIFT)
    c0 = pl.multiple_of(ncp - CMP_PER_QB * i, V7X_SUBLANES)
    s = jnp.dot(kcmp_ref[...], wq, preferred_element_type=F32) + fc_ref[pl.ds(c0, ncp), :]
    m = jnp.max(s, axis=0, keepdims=True)
    p = jnp.exp2(s - m)
    l = jnp.sum(p, axis=0, keepdims=True)
    p = p * jnp.where(cmax >= 0, 1.0 / l, 0.0)
    o_c, _ = _pv_by_group(vcmpt_ref[...], p, False)
    psum = jnp.concatenate(
        [sum(p[:, g * gl + r * Q_BLOCK:g * gl + (r + 1) * Q_BLOCK] for r in range(REP))
         for g in range(B_KV_GROUPS)], axis=1)
    imp = jnp.dot(ovt_ref[...], psum, preferred_element_type=F32,
                  precision=lax.Precision.HIGHEST)

    sa, kb0_a = _banded_scores(i, _block_diag_qt(qa_ref[...]), ka_ref, ba_ref, A_WINDOW)
    oa_ref[...] = _swa_block(i, sa, kb0_a, vat_ref, sink_ref).astype(oa_ref.dtype)

    sw, kb0 = _banded_scores(i, wq, kw_ref, bw_ref, NSA_WINDOW)
    mw = jnp.max(sw, axis=0, keepdims=True)
    pw = jnp.exp2(sw - mw)
    vwt = jnp.concatenate([vwt_ref[kb0 + j] for j in range(n_win)], axis=1)
    o_w, l_w = _pv_by_group(vwt, pw, True)

    sl = B_KV_GROUPS * Q_BLOCK
    jt = lax.broadcasted_iota(jnp.int32, (n_slc, sl), 0)
    second_half = (lax.broadcasted_iota(jnp.int32, (1, sl), 1) & (Q_BLOCK - 1)) >= SLC_BLOCK
    qblk = 2 * i + second_half.astype(jnp.int32)
    forced = (jt == 0) | (jt == qblk) | (jt == qblk - 1)
    score = jnp.where(forced, BIG, jnp.where(jt > qblk, NEG, imp))
    sr_ref[...] = score
    sub = V7X_SUBLANES
    sub_iota = lax.broadcasted_iota(jnp.int32, (sub, sl), 0)
    ranks = []
    for v in range(0, n_slc, sub):
        tile = sr_ref[v:v + sub, :]
        acc = jnp.zeros((sub, sl), F32)
        for c in range(n_slc):
            row = sr_ref[c:c + 1, :]
            if c < v:
                ahead = row >= tile
            elif c >= v + sub:
                ahead = row > tile
            else:
                ahead = (row > tile) | ((row == tile) & (sub_iota > c - v))
            acc = acc + jnp.where(ahead, 1.0, 0.0)
        ranks.append(acc)
    selneg = jnp.where(jnp.concatenate(ranks, axis=0) < topk, 0.0, NEG).astype(BF16)
    wsel = jnp.concatenate([selneg[:, g * Q_BLOCK:(g + 1) * Q_BLOCK]
                            for g in range(B_KV_GROUPS) for _ in range(REP)], axis=1)
    w = jnp.concatenate([wq, wsel, jnp.zeros((AUG_LANES - gd - n_slc, nl), BF16)], axis=0)

    def scores(t):
        r0 = pl.multiple_of(t * SEL_TILE, SEL_TILE)
        return jnp.dot(kaug_ref[pl.ds(r0, SEL_TILE), :], w, preferred_element_type=F32)

    def update(carry, st, t, st_max=None):
        m_i, l_i, a0, a1 = carry
        if st_max is None:
            st_max = jnp.max(st, axis=0, keepdims=True)
        m_n = jnp.maximum(m_i, st_max)
        alpha = jnp.exp2(m_i - m_n)
        pt = jnp.exp2(st - m_n)
        blk0 = t * (SEL_TILE // Q_BLOCK)
        vt = jnp.concatenate([vslt_ref[blk0 + j] for j in range(SEL_TILE // Q_BLOCK)], axis=1)
        pv, l_t = _pv_by_group(vt, pt, True)
        a0, a1 = (a + b for a, b in zip(_scale_groups((a0, a1), alpha), pv))
        return m_n, alpha * l_i + l_t, a0, a1

    last = i // 2
    par = i % 2
    n_far = jnp.maximum(last - 1, 0)
    carry = (jnp.full((1, nl), NEG, F32), jnp.zeros((1, nl), F32),
             jnp.zeros((HEAD_DIM, gl), F32), jnp.zeros((HEAD_DIM, gl), F32))

    def stash(ref, t):
        st = scores(t)
        ref[...] = st
        return jnp.max(st, axis=0, keepdims=True)

    def two_tiles(u, carry_mx):
        carry, mx0 = carry_mx
        t = 2 * u
        mx1 = stash(s1_ref, t + 1)
        carry = update(carry, s0_ref[...], t, mx0)
        mx0 = stash(s0_ref, t + 2)
        return update(carry, s1_ref[...], t + 1, mx1), mx0

    def one_tile(t, carry_mx):
        carry, mx0 = carry_mx
        mx1 = stash(s1_ref, t + 1)
        carry = update(carry, s0_ref[...], t, mx0)
        s0_ref[...] = s1_ref[...]
        return carry, mx1

    carry_mx = (carry, stash(s0_ref, 0))
    carry_mx = lax.fori_loop(0, n_far // 4,
                             lambda v, c: two_tiles(2 * v + 1, two_tiles(2 * v, c)), carry_mx)
    carry_mx = lax.fori_loop(n_far // 4 * 2, n_far // 2, two_tiles, carry_mx)
    carry, _ = lax.fori_loop(n_far // 2 * 2, n_far, one_tile, carry_mx)
    s1_ref[...] = scores(last)
    near0 = pl.multiple_of(par * (2 * SEL_TILE), SEL_TILE)
    before = bn_ref[pl.ds(near0, SEL_TILE), :] + jnp.where(last > 0, 0.0, NEG)
    carry = update(carry, s0_ref[...] + before, jnp.maximum(last - 1, 0))
    m_s, l_s, a0, a1 = update(carry, s1_ref[...] + bn_ref[pl.ds(near0 + SEL_TILE, SEL_TILE), :], last)
    gt = jnp.transpose(gate_ref[...])
    inv_s, inv_w = 1.0 / l_s, 1.0 / l_w
    heads = []
    for h in range(B_HEADS):
        hl = slice(h * Q_BLOCK, (h + 1) * Q_BLOCK)
        heads.append(gt[3 * h:3 * h + 1, :] * _head_block(o_c, h)
                     + (gt[3 * h + 1:3 * h + 2, :] * inv_s[:, hl]) * _head_block((a0, a1), h)
                     + (gt[3 * h + 2:3 * h + 3, :] * inv_w[:, hl]) * _head_block(o_w, h))
    o_ref[...] = jnp.transpose(jnp.concatenate(heads, axis=0)).astype(o_ref.dtype)


def _mixers_call(qt, kk, vt, p32, kcmp, vcmpt, kaug, ba, sink_row, fc, bw, bn, ovt, bsz, seq,
                 col_gate, topk):
    nb = seq // Q_BLOCK
    assert A_HEADS == B_HEADS and A_KV_HEADS == B_KV_GROUPS
    qw = B_HEADS * HEAD_DIM
    kw = B_KV_GROUPS * HEAD_DIM
    per_batch = lambda a: pl.BlockSpec((None,) + a.shape[1:], lambda b, i: (b,) + (0,) * (a.ndim - 1))
    const = lambda a: pl.BlockSpec(a.shape, lambda b, i: (0,) * a.ndim)
    qt_seg = lambda a: pl.BlockSpec((None, None, None, qw, Q_BLOCK), lambda b, i: (a, b, i, 0, 0))
    kk_seg = lambda a: pl.BlockSpec((None, seq, kw), lambda b, i: (b, 0, a))
    vt_seg = lambda a: pl.BlockSpec((None, None) + vt.shape[2:], lambda b, i: (a, b, 0, 0, 0))
    y_spec = pl.BlockSpec((None, Q_BLOCK, qw), lambda b, i: (b, i, 0))
    kern = functools.partial(_mixers_kernel, topk=topk)
    return pl.pallas_call(
        kern,
        grid=(bsz, nb),
        in_specs=[qt_seg(QT_A), kk_seg(KK_A), vt_seg(VT_A), const(ba), const(sink_row),
                  qt_seg(QT_B), per_batch(kcmp), per_batch(vcmpt), per_batch(kaug), vt_seg(VT_SL),
                  kk_seg(KK_W), vt_seg(VT_W),
                  pl.BlockSpec((None, Q_BLOCK, V7X_LANES), lambda b, i: (b, i, col_gate // V7X_LANES)),
                  const(fc), const(bw), const(bn), const(ovt)],
        out_specs=[y_spec, y_spec],
        out_shape=[jax.ShapeDtypeStruct((bsz, seq, qw), BF16)] * 2,
        scratch_shapes=[pltpu.VMEM((SEL_TILE, B_HEADS * Q_BLOCK), F32),
                        pltpu.VMEM((SEL_TILE, B_HEADS * Q_BLOCK), F32),
                        pltpu.VMEM((ovt.shape[0], B_KV_GROUPS * Q_BLOCK), F32)],
        compiler_params=_cparams(("parallel", "arbitrary")),
        name="mixers",
    )(qt, kk, vt, ba, sink_row, qt, kcmp, vcmpt, kaug, vt, kk, vt, p32, fc, bw, bn, ovt)


def _merge_kernel(x_ref, g_ref, ya_ref, yb_ref, wg_ref, wua_ref, wub_ref, wo_ref, o_ref):
    x = x_ref[...]
    d = x.shape[1]
    hb = _rms(x, g_ref[...]).astype(BF16)
    ga = jax.nn.sigmoid(jnp.dot(hb, wg_ref[:, :d], preferred_element_type=F32))
    gb = jax.nn.sigmoid(jnp.dot(hb, wg_ref[:, d:], preferred_element_type=F32))
    ua = jnp.dot(ya_ref[...], wua_ref[...], preferred_element_type=F32)
    ub = jnp.dot(yb_ref[...], wub_ref[...], preferred_element_type=F32)
    merged = ga * ua + gb * ub
    o_ref[...] = x + jnp.dot(merged.astype(BF16), wo_ref[...], preferred_element_type=F32)


def _merge(x2d, g, ya, yb, wg, wua, wub, wo, tm):
    n, d = x2d.shape
    const = lambda a: pl.BlockSpec(a.shape, lambda i: (0, 0))
    row = lambda a: pl.BlockSpec((tm, a.shape[1]), lambda i: (i, 0))
    return pl.pallas_call(
        _merge_kernel,
        grid=(n // tm,),
        in_specs=[row(x2d), const(g), row(ya), row(yb), const(wg), const(wua), const(wub), const(wo)],
        out_specs=row(x2d),
        out_shape=jax.ShapeDtypeStruct((n, d), F32),
        compiler_params=_cparams(("parallel",)),
        name="merge",
    )(x2d, g, ya, yb, wg, wua, wub, wo)


FFN_HALO = 16
FFN_GROUP = 6


def _ffn_kernel(xc_ref, xp_ref, gn_ref, wi_ref, cw_ref, cb_ref, wo_ref, gf_ref, o_ref, *, chunk):
    i = pl.program_id(1)
    xc = xc_ref[...]
    tm = xc.shape[0]
    gn = gn_ref[...]
    hp = _rms(xp_ref[...], gn) * jnp.where(i > 0, 1.0, 0.0)
    h = jnp.concatenate([hp, _rms(xc, gn)], axis=0).astype(BF16)
    d_ff = wo_ref.shape[0]

    def up(c0):
        return jnp.dot(h, wi_ref[:, c0:c0 + chunk], preferred_element_type=F32)

    def conv(ext, c0):
        cw = cw_ref[:, c0:c0 + chunk]
        out = cb_ref[:, c0:c0 + chunk]
        for k in range(CONV_WIDTH):
            off = FFN_HALO - (CONV_WIDTH - 1) + k
            out = out + cw[k:k + 1, :] * ext[off:off + tm]
        return out

    acc = None
    acts = []
    nxt = (up(0), up(d_ff))
    for c0 in range(0, d_ff, chunk):
        ext_u, ext_g = nxt
        if c0 + chunk < d_ff:
            nxt = (up(c0 + chunk), up(d_ff + c0 + chunk))
        acts.append((jax.nn.silu(conv(ext_g, d_ff + c0)) * conv(ext_u, c0)).astype(BF16))
        if len(acts) == FFN_GROUP or c0 + chunk >= d_ff:
            r0 = c0 + chunk - len(acts) * chunk
            part = jnp.dot(jnp.concatenate(acts, axis=1), wo_ref[r0:c0 + chunk, :],
                           preferred_element_type=F32)
            acc = part if acc is None else acc + part
            acts = []
    o_ref[...] = _rms(xc + acc, gf_ref[...])


def _ffn(x1, gn, wi, cw, cb, wo, gf, tm, chunk):
    bsz, seq, d = x1.shape
    const = lambda a: pl.BlockSpec(a.shape, lambda b, i: (0, 0))
    kern = functools.partial(_ffn_kernel, chunk=chunk)
    return pl.pallas_call(
        kern,
        grid=(bsz, seq // tm),
        in_specs=[pl.BlockSpec((None, tm, d), lambda b, i: (b, i, 0)),
                  pl.BlockSpec((None, FFN_HALO, d),
                               lambda b, i: (b, jnp.maximum(i * (tm // FFN_HALO) - 1, 0), 0)),
                  const(gn), const(wi), const(cw), const(cb), const(wo), const(gf)],
        out_specs=pl.BlockSpec((None, tm, d), lambda b, i: (b, i, 0)),
        out_shape=jax.ShapeDtypeStruct((bsz, seq, d), F32),
        compiler_params=_cparams(("parallel", "arbitrary")),
        name="ffn",
    )(x1, x1, gn, wi, cw, cb, wo, gf)


def _mixers(x, norm_mix, w_in, attn_sinks, cmp_pos_k, cmp_w1_k, cmp_w2_k, cmp_pos_v, cmp_w1_v,
            cmp_w2_v, table):
    bsz, seq, d = x.shape
    n = bsz * seq
    aq, akv = A_HEADS * HEAD_DIM, A_KV_HEADS * HEAD_DIM
    bq, bkv = B_HEADS * HEAD_DIM, B_KV_GROUPS * HEAD_DIM
    n_gate = 3 * B_HEADS
    assert seq % SEL_TILE == 0 and seq // SLC_BLOCK <= AUG_LANES - bkv
    splits = (aq, akv, akv, bq, bkv, bkv, bkv, bkv, bkv, bkv, n_gate, d, d)
    off = np.concatenate([[0], np.cumsum(splits)]).astype(int)
    seg = lambda k: w_in[:, off[k]:off[k + 1]]
    assert akv == bkv and aq == bq
    w_gate_nsa = jnp.pad(seg(10), ((0, 0), (0, V7X_LANES - n_gate)))
    w1 = jnp.concatenate([seg(0), seg(3), seg(1), seg(8), seg(6), seg(2), seg(7), seg(9),
                          seg(4), seg(5), w_gate_nsa], axis=1).astype(BF16)
    tm = min(1024, seq)
    g_mix = norm_mix.reshape(1, d)

    qt, kk, kaug, vt, p32 = _proj(x.reshape(n, d), g_mix, w1, aq, seq, tm)
    nb = seq // Q_BLOCK
    qt = qt.reshape(2, bsz, nb, aq, Q_BLOCK)
    kk = kk.reshape(bsz, seq, 2 * bkv)
    kaug = kaug.reshape(bsz, seq, AUG_LANES)
    vt = vt.reshape(3, bsz, nb, bkv, Q_BLOCK)
    p32 = p32.reshape(bsz, seq, 2 * bkv + V7X_LANES)

    table = table * LOG2E
    table_a, table_b = table[:, :A_HEADS], table[:, A_HEADS:]
    sink_row = jnp.repeat(attn_sinks * LOG2E, Q_BLOCK).reshape(1, A_HEADS * Q_BLOCK)

    ncp = seq // CMP_STRIDE
    n_cmp = (seq - CMP_BLOCK) // CMP_STRIDE + 1
    n_slc = seq // SLC_BLOCK
    topk = min(SLC_TOPK, n_slc)

    k_cmp = _compress(p32, 0, cmp_pos_k, cmp_w1_k, cmp_w2_k, n_cmp, False)
    v_cmp_t = _compress(p32, bkv, cmp_pos_v, cmp_w1_v, cmp_w2_v, n_cmp, True)

    fc = _cmp_bias_table(table_b, ncp)
    win_a, win_b = A_WINDOW + Q_BLOCK, NSA_WINDOW + Q_BLOCK
    bias_a, bw, bn = _toeplitz_tables(
        [_band_vector(table_a, win_a, A_WINDOW, A_WINDOW, False),
         _band_vector(table_b, win_b, NSA_WINDOW, NSA_WINDOW, False),
         _band_vector(table_b, 2 * SEL_TILE, SEL_TILE, None, True),
         _band_vector(table_b, 2 * SEL_TILE, SEL_TILE + Q_BLOCK, None, True)],
        [(0, 0), (1, 0), (2, 0), (2, 2 * SEL_TILE)],
        [win_a + A_WINDOW, win_b + NSA_WINDOW, 4 * SEL_TILE])
    ovt = jnp.asarray(_overlap_t(ncp, n_slc, n_cmp))
    y_a, y_b = _mixers_call(qt, kk, vt, p32, k_cmp, v_cmp_t, kaug, bias_a, sink_row, fc, bw, bn, ovt,
                            bsz, seq, 2 * bkv, topk)
    return y_a, y_b, w_in[:, off[11]:off[13]]


def _layer(x, norm_mix, w_in, attn_sinks, cmp_pos_k, cmp_w1_k, cmp_w2_k, cmp_pos_v, cmp_w1_v,
           cmp_w2_v, w_up_a, w_up_b, w_out, norm_ffn, w_ffn_in, conv_w, conv_b, w_ffn_out,
           table, norm_final):
    bsz, seq, d = x.shape
    n = bsz * seq
    y_a, y_b, w_merge_gates = _mixers(x, norm_mix, w_in, attn_sinks, cmp_pos_k, cmp_w1_k, cmp_w2_k,
                                      cmp_pos_v, cmp_w1_v, cmp_w2_v, table)

    x1 = _merge(x.reshape(n, d), norm_mix.reshape(1, d), y_a.reshape(n, -1), y_b.reshape(n, -1),
                w_merge_gates.astype(BF16), w_up_a.astype(BF16), w_up_b.astype(BF16),
                w_out.astype(BF16), min(1024, n))

    return _ffn(x1.reshape(bsz, seq, d), norm_ffn.reshape(1, d), w_ffn_in.astype(BF16), conv_w,
                conv_b.reshape(1, -1), w_ffn_out.astype(BF16), norm_final.reshape(1, d),
                min(512, seq), 256)


def kernel(x, norm_mix, w_in, attn_sinks, cmp_pos_k, cmp_w1_k, cmp_w2_k, cmp_pos_v, cmp_w1_v, cmp_w2_v, w_up_a, w_up_b, w_out, norm_ffn, w_ffn_in, conv_w, conv_b, w_ffn_out, rel_bias_table, norm_final):
    assert norm_mix.shape[0] == 1, "single-layer block"
    return _layer(x, norm_mix[0], w_in[0], attn_sinks[0], cmp_pos_k[0], cmp_w1_k[0], cmp_w2_k[0],
                  cmp_pos_v[0], cmp_w1_v[0], cmp_w2_v[0], w_up_a[0], w_up_b[0], w_out[0],
                  norm_ffn[0], w_ffn_in[0], conv_w[0], conv_b[0], w_ffn_out[0], rel_bias_table,
                  norm_final)
```

```python
import functools
import math

import numpy as np
import jax
import jax.numpy as jnp
from jax import lax
from jax.experimental import pallas as pl
from jax.experimental.pallas import tpu as pltpu

F32 = jnp.float32
BF16 = jnp.bfloat16

HEAD_DIM = 64
A_HEADS = 8
A_KV_HEADS = 2
A_WINDOW = 128
B_HEADS = 8
B_KV_GROUPS = 2
REP = 4
CMP_BLOCK = 32
CMP_STRIDE = 16
SLC_BLOCK = 64
SLC_SHIFT = 6
SLC_TOPK = 16
NSA_WINDOW = 512
NUM_BUCKETS = 32
MAX_DISTANCE = 128
CONV_WIDTH = 3
Q_BLOCK = 128
EPS = 1e-6
NEG = -1e30
BIG = 1e30
SCALE = HEAD_DIM ** -0.5
LOG2E = math.log2(math.e)
Q_SCALE = SCALE * LOG2E
SUM_ROWS = 16

V7X_LANES = 128
V7X_SUBLANES = 8
V7X_VMEM_BYTES = 64 * 1024 * 1024
VMEM_LIMIT = V7X_VMEM_BYTES * 7 // 8


def _cparams(semantics):
    return pltpu.CompilerParams(dimension_semantics=semantics, vmem_limit_bytes=VMEM_LIMIT)


def _bucket_np(dist):
    dist = np.maximum(dist, 0)
    max_exact = NUM_BUCKETS // 2
    d = np.maximum(dist, 1).astype(np.float64)
    large = max_exact + (np.log(d / max_exact) / math.log(MAX_DISTANCE / max_exact)
                         * (NUM_BUCKETS - max_exact)).astype(np.int32)
    large = np.minimum(large, NUM_BUCKETS - 1)
    return np.where(dist < max_exact, dist, large).astype(np.int32)


def _table_lookup(table, dist):
    idx = _bucket_np(dist).reshape(-1)
    onehot = np.zeros((NUM_BUCKETS, idx.size), np.float32)
    onehot[idx, np.arange(idx.size)] = 1.0
    vals = jnp.dot(table.T, jnp.asarray(onehot), precision=lax.Precision.HIGHEST)
    return vals.reshape((table.shape[1],) + dist.shape)


def _band_vector(table, n_keys, offset, window, shift_far):
    length = n_keys + Q_BLOCK
    m = np.arange(length)
    m = np.where(m < Q_BLOCK, m, m - length)
    dist = m + offset
    ok = dist >= 0 if window is None else (dist >= 0) & (dist < window)
    u = _table_lookup(table, dist)
    if shift_far:
        u = u - table[NUM_BUCKETS - 1][:, None]
    return jnp.where(ok[None, :], u, NEG)


def _toeplitz_kernel(*refs, dest):
    n_tab = len(dest)
    covered = [0] * (len(refs) - n_tab)
    for u_ref, (j, row0) in zip(refs[:n_tab], dest):
        o_ref = refs[n_tab + j]
        n_heads, length = u_ref.shape
        n_keys = length - Q_BLOCK
        covered[j] = max(covered[j], row0 + n_keys)
        u2 = jnp.concatenate([u_ref[...], u_ref[...]], axis=1)
        for r0 in range(0, n_keys, Q_BLOCK):
            w0 = (-r0 - (Q_BLOCK - 1)) % length
            win = u2[:, w0:w0 + 2 * Q_BLOCK]
            for h in range(n_heads):
                x = jnp.broadcast_to(win[h:h + 1, :], (Q_BLOCK, 2 * Q_BLOCK))
                x = pltpu.roll(x, Q_BLOCK + 1, 1, stride=1, stride_axis=0)
                o_ref[row0 + r0:row0 + r0 + Q_BLOCK, h * Q_BLOCK:(h + 1) * Q_BLOCK] = x[:, :Q_BLOCK]
    for o_ref, done in zip(refs[n_tab:], covered):
        if o_ref.shape[0] > done:
            o_ref[done:, :] = jnp.full((o_ref.shape[0] - done, o_ref.shape[1]), NEG, F32)


def _toeplitz_tables(us, dest, n_rows):
    n_heads = us[0].shape[0]
    kern = functools.partial(_toeplitz_kernel, dest=tuple(dest))
    return pl.pallas_call(
        kern,
        out_shape=[jax.ShapeDtypeStruct((r, n_heads * Q_BLOCK), F32) for r in n_rows],
        compiler_params=pltpu.CompilerParams(vmem_limit_bytes=VMEM_LIMIT),
        name="bias_tables",
    )(*us)


def _cmp_bias_table(table, ncp):
    q = np.arange(Q_BLOCK)
    last_pos = CMP_BLOCK - 1
    c_lo = -((MAX_DISTANCE - 1 + last_pos) // CMP_STRIDE)
    c_hi = (Q_BLOCK - 1 - last_pos) // CMP_STRIDE
    cs = np.arange(c_lo, c_hi + 1)
    dist = q[None, :] - CMP_STRIDE * cs[:, None] - last_pos
    near = jnp.where((dist >= 0)[None], _table_lookup(table, dist), NEG)
    n_heads = table.shape[1]
    near = jnp.transpose(near, (1, 0, 2)).reshape(len(cs), n_heads * Q_BLOCK)
    far = jnp.repeat(table[NUM_BUCKETS - 1], Q_BLOCK)[None, :]
    below = jnp.broadcast_to(far, (ncp + c_lo, n_heads * Q_BLOCK))
    above = jnp.full((ncp - 1 - c_hi, n_heads * Q_BLOCK), NEG, F32)
    return jnp.concatenate([below, near, above], axis=0)


def _overlap_t(n_cmp_pad, n_slc, n_cmp):
    r = SLC_BLOCK // CMP_STRIDE
    c = CMP_BLOCK // CMP_STRIDE
    j, m, n = np.meshgrid(np.arange(n_slc), np.arange(r), np.arange(c), indexing='ij')
    i = r * j + m - n
    ok = (i >= 0) & (i < n_cmp)
    mat = np.zeros((n_slc, n_cmp_pad), np.float32)
    np.add.at(mat, (j[ok], i[ok]), 1.0)
    return mat


def _rms(x, g):
    return x * lax.rsqrt(jnp.mean(x * x, axis=-1, keepdims=True) + EPS) * g


PROJ_CHUNK = 256
VT_A, VT_SL, VT_W = 0, 1, 2
QT_A, QT_B = 0, 1
KK_A, KK_W = 0, 1


def _proj_kernel(x_ref, g_ref, w_ref, qt_ref, kk_ref, kaug_ref, vt_ref, o32_ref, *, seq):
    tm = x_ref.shape[0]
    kv = B_KV_GROUPS * HEAD_DIM
    qw = qt_ref.shape[2]
    hb = _rms(x_ref[...], g_ref[...]).astype(BF16)

    def cols(c0, width=PROJ_CHUNK):
        return jnp.dot(hb, w_ref[:, c0:c0 + width], preferred_element_type=F32)

    def put_transposed(ref, a, r, row0=0):
        for j in range(tm // Q_BLOCK):
            ref[a, j, row0:row0 + r.shape[1], :] = jnp.transpose(
                r[j * Q_BLOCK:(j + 1) * Q_BLOCK, :]).astype(BF16)

    for c0 in range(0, 2 * qw, PROJ_CHUNK):
        put_transposed(qt_ref, c0 // qw, cols(c0) * Q_SCALE, c0 % qw)
    kk_ref[...] = cols(2 * qw).astype(BF16)
    r = cols(2 * qw + 2 * kv)
    pos = (pl.program_id(0) * tm) % seq + lax.broadcasted_iota(jnp.int32, (tm, kv), 0)
    onehot = lax.broadcasted_iota(jnp.int32, (tm, kv), 1) == jnp.right_shift(pos, SLC_SHIFT)
    kaug_ref[:, :kv] = r[:, :kv].astype(BF16)
    kaug_ref[:, kv:] = jnp.where(onehot, 1.0, 0.0).astype(BF16)
    put_transposed(vt_ref, VT_A, r[:, kv:])
    r = cols(2 * qw + 4 * kv)
    put_transposed(vt_ref, VT_SL, r[:, :kv])
    put_transposed(vt_ref, VT_W, r[:, kv:])
    o32_ref[:, :2 * kv] = cols(2 * qw + 6 * kv)
    o32_ref[:, 2 * kv:] = jax.nn.sigmoid(cols(2 * qw + 8 * kv, V7X_LANES))


def _proj(x2d, g, w1, qw, seq, tm):
    n, d = x2d.shape
    kv = B_KV_GROUPS * HEAD_DIM
    n32 = 2 * kv + V7X_LANES
    nb = tm // Q_BLOCK
    assert AUG_LANES == 2 * kv == PROJ_CHUNK and seq % tm == 0 and tm % Q_BLOCK == 0
    rows = lambda width: pl.BlockSpec((tm, width), lambda i: (i, 0))
    blocks_t = lambda a, width: pl.BlockSpec((a, nb, width, Q_BLOCK), lambda i: (0, i, 0, 0))
    kern = functools.partial(_proj_kernel, seq=seq)
    return pl.pallas_call(
        kern,
        grid=(n // tm,),
        in_specs=[rows(d), pl.BlockSpec((1, d), lambda i: (0, 0)),
                  pl.BlockSpec(w1.shape, lambda i: (0, 0))],
        out_specs=[blocks_t(2, qw), rows(2 * kv), rows(AUG_LANES), blocks_t(3, kv), rows(n32)],
        out_shape=[jax.ShapeDtypeStruct((2, n // Q_BLOCK, qw, Q_BLOCK), BF16),
                   jax.ShapeDtypeStruct((n, 2 * kv), BF16),
                   jax.ShapeDtypeStruct((n, AUG_LANES), BF16),
                   jax.ShapeDtypeStruct((3, n // Q_BLOCK, kv, Q_BLOCK), BF16),
                   jax.ShapeDtypeStruct((n, n32), F32)],
        compiler_params=_cparams(("parallel",)),
        name="proj",
    )(x2d, g, w1)


CMP_PAIR = 2


def _compress_kernel(x_ref, pos_ref, w1_ref, w2_ref, o_ref, *, n_cmp, transpose_out):
    ncp = x_ref.shape[0] // CMP_STRIDE

    def group_diag(w):
        zero = jnp.zeros(w.shape, w.dtype)
        return jnp.concatenate(
            [jnp.concatenate([w if g == gg else zero for gg in range(B_KV_GROUPS)], axis=1)
             for g in range(B_KV_GROUPS)], axis=0)

    halves = []
    for half in range(CMP_BLOCK // CMP_STRIDE):
        acc = None
        for l0 in range(half * CMP_STRIDE, (half + 1) * CMP_STRIDE, CMP_PAIR):
            lhs = jnp.concatenate(
                [(x_ref[pl.ds(l % CMP_STRIDE, ncp, stride=CMP_STRIDE), :]
                  + jnp.concatenate([pos_ref[l:l + 1, :]] * B_KV_GROUPS, axis=1)).astype(BF16)
                 for l in range(l0, l0 + CMP_PAIR)], axis=1)
            rhs = jnp.concatenate([group_diag(w1_ref[l]) for l in range(l0, l0 + CMP_PAIR)], axis=0)
            part = jnp.dot(lhs, rhs, preferred_element_type=F32)
            acc = part if acc is None else acc + part
        halves.append(acc)
    top, bottom = halves
    h1 = top + jnp.concatenate([bottom[1:], bottom[:1]], axis=0)
    o = jnp.dot(jax.nn.gelu(h1).astype(BF16), group_diag(w2_ref[...]), preferred_element_type=F32)
    row = lax.broadcasted_iota(jnp.int32, o.shape, 0)
    o = jnp.where(row < n_cmp, o, 0.0)
    if transpose_out:
        o = jnp.transpose(o)
    o_ref[...] = o.astype(o_ref.dtype)


def _compress(p32, col, pos, w1, w2, n_cmp, transpose_out):
    bsz, seq, _ = p32.shape
    ncp = seq // CMP_STRIDE
    gd = B_KV_GROUPS * HEAD_DIM
    w1 = w1.reshape(CMP_BLOCK, HEAD_DIM, -1).astype(BF16)
    w2 = w2.astype(BF16)
    oshape = (gd, ncp) if transpose_out else (ncp, gd)
    const = lambda a: pl.BlockSpec(a.shape, lambda b: (0,) * a.ndim)
    kern = functools.partial(_compress_kernel, n_cmp=n_cmp, transpose_out=transpose_out)
    return pl.pallas_call(
        kern,
        grid=(bsz,),
        in_specs=[pl.BlockSpec((None, seq, gd), lambda b: (b, 0, col // gd)),
                  const(pos), const(w1), const(w2)],
        out_specs=pl.BlockSpec((None,) + oshape, lambda b: (b, 0, 0)),
        out_shape=jax.ShapeDtypeStruct((bsz,) + oshape, BF16),
        compiler_params=_cparams(("parallel",)),
        name="compress",
    )(p32, pos, w1, w2)


def _block_diag_qt(qt):
    zero = jnp.zeros((HEAD_DIM, Q_BLOCK), BF16)
    n_groups = qt.shape[0] // (REP * HEAD_DIM)
    return jnp.concatenate([
        jnp.concatenate([qt[(REP * g + r) * HEAD_DIM:(REP * g + r + 1) * HEAD_DIM] if gg == g else zero
                         for gg in range(n_groups) for r in range(REP)], axis=1)
        for g in range(n_groups)], axis=0)


def _pv_by_group(vt, p, with_sum):
    n_groups = vt.shape[0] // HEAD_DIM
    gl = p.shape[1] // n_groups
    pb = p.astype(BF16)
    outs, sums = [], []
    for g in range(n_groups):
        lhs = vt[g * HEAD_DIM:(g + 1) * HEAD_DIM]
        if with_sum:
            lhs = jnp.concatenate([lhs, jnp.ones((SUM_ROWS, vt.shape[1]), BF16)], axis=0)
        r = jnp.dot(lhs, pb[:, g * gl:(g + 1) * gl], preferred_element_type=F32)
        outs.append(r[:HEAD_DIM])
        sums.append(r[HEAD_DIM:HEAD_DIM + 1])
    return outs, (jnp.concatenate(sums, axis=1) if with_sum else None)


def _scale_groups(outs, row):
    gl = outs[0].shape[1]
    return [o * row[:, g * gl:(g + 1) * gl] for g, o in enumerate(outs)]


def _head_block(outs, h):
    r = h % REP
    return outs[h // REP][:, r * Q_BLOCK:(r + 1) * Q_BLOCK]


def _banded_scores(i, wq, k_ref, bias_ref, window):
    n_blk = window // Q_BLOCK + 1
    span = n_blk * Q_BLOCK
    kb0 = jnp.maximum(i - (n_blk - 1), 0)
    shift = jnp.maximum(n_blk - 1 - i, 0)
    s = jnp.dot(k_ref[pl.ds(pl.multiple_of(kb0 * Q_BLOCK, Q_BLOCK), span), :], wq,
                preferred_element_type=F32)
    return s + bias_ref[pl.ds(pl.multiple_of(shift * Q_BLOCK, Q_BLOCK), span), :], kb0


def _swa_block(i, s, kb0, vt_ref, sink_ref):
    sink = sink_ref[...]
    m = jnp.maximum(jnp.max(s, axis=0, keepdims=True), sink)
    p = jnp.exp2(s - m)
    vt = jnp.concatenate([vt_ref[kb0 + j] for j in range(A_WINDOW // Q_BLOCK + 1)], axis=1)
    outs, l = _pv_by_group(vt, p, True)
    outs = _scale_groups(outs, 1.0 / (l + jnp.exp2(sink - m)))
    heads = [_head_block(outs, h) for h in range(A_HEADS)]
    return jnp.transpose(jnp.concatenate(heads, axis=0))


SEL_TILE = 2 * Q_BLOCK
AUG_LANES = 2 * V7X_LANES
CMP_SHIFT = 4
CMP_PER_QB = Q_BLOCK // CMP_STRIDE


def _select_kernel(qa_ref, ka_ref, vat_ref, ba_ref, sink_ref,
                   q_ref, kcmp_ref, vcmpt_ref, kw_ref, vwt_ref, gate_ref,
                   fc_ref, bw_ref, ovt_ref, oa_ref, part_ref, sel_ref, gs_ref, sr_ref, *, topk):
    i = pl.program_id(1)
    n_slc, ncp = ovt_ref.shape
    gl = REP * Q_BLOCK
    nl = B_KV_GROUPS * gl
    gd = B_KV_GROUPS * HEAD_DIM
    n_win = NSA_WINDOW // Q_BLOCK + 1

    wq = _block_diag_qt(q_ref[...])

    qlane = lax.broadcasted_iota(jnp.int32, (1, nl), 1) & (Q_BLOCK - 1)
    cmax = jnp.right_shift(qlane + i * Q_BLOCK - (CMP_BLOCK - 1), CMP_SHIFT)
    c0 = pl.multiple_of(ncp - CMP_PER_QB * i, V7X_SUBLANES)
    s = jnp.dot(kcmp_ref[...], wq, preferred_element_type=F32) + fc_ref[pl.ds(c0, ncp), :]
    m = jnp.max(s, axis=0, keepdims=True)
    p = jnp.exp2(s - m)
    l = jnp.sum(p, axis=0, keepdims=True)
    p = p * jnp.where(cmax >= 0, 1.0 / l, 0.0)
    o_c, _ = _pv_by_group(vcmpt_ref[...], p, False)
    psum = jnp.concatenate(
        [sum(p[:, g * gl + r * Q_BLOCK:g * gl + (r + 1) * Q_BLOCK] for r in range(REP))
         for g in range(B_KV_GROUPS)], axis=1)
    imp = jnp.dot(ovt_ref[...], psum, preferred_element_type=F32,
                  precision=lax.Precision.HIGHEST)

    sa, kb0_a = _banded_scores(i, _block_diag_qt(qa_ref[...]), ka_ref, ba_ref, A_WINDOW)
    oa_ref[...] = _swa_block(i, sa, kb0_a, vat_ref, sink_ref).astype(oa_ref.dtype)

    sw, kb0 = _banded_scores(i, wq, kw_ref, bw_ref, NSA_WINDOW)
    mw = jnp.max(sw, axis=0, keepdims=True)
    pw = jnp.exp2(sw - mw)
    vwt = jnp.concatenate([vwt_ref[kb0 + j] for j in range(n_win)], axis=1)
    o_w, l_w = _pv_by_group(vwt, pw, True)

    sl = B_KV_GROUPS * Q_BLOCK
    jt = lax.broadcasted_iota(jnp.int32, (n_slc, sl), 0)
    second_half = (lax.broadcasted_iota(jnp.int32, (1, sl), 1) & (Q_BLOCK - 1)) >= SLC_BLOCK
    qblk = 2 * i + second_half.astype(jnp.int32)
    forced = (jt == 0) | (jt == qblk) | (jt == qblk - 1)
    score = jnp.where(forced, BIG, jnp.where(jt > qblk, NEG, imp))
    sr_ref[...] = score
    sub = V7X_SUBLANES
    sub_iota = lax.broadcasted_iota(jnp.int32, (sub, sl), 0)
    ranks = []
    for v in range(0, n_slc, sub):
        tile = sr_ref[v:v + sub, :]
        acc = jnp.zeros((sub, sl), F32)
        for c in range(n_slc):
            row = sr_ref[c:c + 1, :]
            if c < v:
                ahead = row >= tile
            elif c >= v + sub:
                ahead = row > tile
            else:
                ahead = (row > tile) | ((row == tile) & (sub_iota > c - v))
            acc = acc + jnp.where(ahead, 1.0, 0.0)
        ranks.append(acc)
    sel_ref[...] = jnp.where(jnp.concatenate(ranks, axis=0) < topk, 0.0, NEG).astype(BF16)

    gt = jnp.transpose(gate_ref[...])
    inv_w = 1.0 / l_w
    part_ref[...] = jnp.concatenate(
        [gt[3 * h:3 * h + 1, :] * _head_block(o_c, h)
         + (gt[3 * h + 2:3 * h + 3, :] * inv_w[:, h * Q_BLOCK:(h + 1) * Q_BLOCK]) * _head_block(o_w, h)
         for h in range(B_HEADS)], axis=0)
    gs_ref[...] = jnp.concatenate([gt[3 * h + 1:3 * h + 2, :] for h in range(B_HEADS)], axis=0)


def _sweep_kernel(q_ref, kaug_ref, vslt_ref, bn_ref, sel_ref, part_ref, gs_ref, o_ref,
                  s0_ref, s1_ref):
    i = pl.program_id(1)
    n_slc = sel_ref.shape[0]
    gl = REP * Q_BLOCK
    nl = B_KV_GROUPS * gl
    gd = B_KV_GROUPS * HEAD_DIM
    wq = _block_diag_qt(q_ref[...])
    selneg = sel_ref[...]
    wsel = jnp.concatenate([selneg[:, g * Q_BLOCK:(g + 1) * Q_BLOCK]
                            for g in range(B_KV_GROUPS) for _ in range(REP)], axis=1)
    w = jnp.concatenate([wq, wsel, jnp.zeros((AUG_LANES - gd - n_slc, nl), BF16)], axis=0)

    def scores(t):
        r0 = pl.multiple_of(t * SEL_TILE, SEL_TILE)
        return jnp.dot(kaug_ref[pl.ds(r0, SEL_TILE), :], w, preferred_element_type=F32)

    def update(carry, st, t, st_max=None):
        m_i, l_i, a0, a1 = carry
        if st_max is None:
            st_max = jnp.max(st, axis=0, keepdims=True)
        m_n = jnp.maximum(m_i, st_max)
        alpha = jnp.exp2(m_i - m_n)
        pt = jnp.exp2(st - m_n)
        blk0 = t * (SEL_TILE // Q_BLOCK)
        vt = jnp.concatenate([vslt_ref[blk0 + j] for j in range(SEL_TILE // Q_BLOCK)], axis=1)
        pv, l_t = _pv_by_group(vt, pt, True)
        a0, a1 = (a + b for a, b in zip(_scale_groups((a0, a1), alpha), pv))
        return m_n, alpha * l_i + l_t, a0, a1

    last = i // 2
    par = i % 2
    n_far = jnp.maximum(last - 1, 0)
    carry = (jnp.full((1, nl), NEG, F32), jnp.zeros((1, nl), F32),
             jnp.zeros((HEAD_DIM, gl), F32), jnp.zeros((HEAD_DIM, gl), F32))

    def stash(ref, t):
        st = scores(t)
        ref[...] = st
        return jnp.max(st, axis=0, keepdims=True)

    def two_tiles(u, carry_mx):
        carry, mx0 = carry_mx
        t = 2 * u
        mx1 = stash(s1_ref, t + 1)
        carry = update(carry, s0_ref[...], t, mx0)
        mx0 = stash(s0_ref, t + 2)
        return update(carry, s1_ref[...], t + 1, mx1), mx0

    def one_tile(t, carry_mx):
        carry, mx0 = carry_mx
        mx1 = stash(s1_ref, t + 1)
        carry = update(carry, s0_ref[...], t, mx0)
        s0_ref[...] = s1_ref[...]
        return carry, mx1

    carry_mx = (carry, stash(s0_ref, 0))
    carry_mx = lax.fori_loop(0, n_far // 4,
                             lambda v, c: two_tiles(2 * v + 1, two_tiles(2 * v, c)), carry_mx)
    carry_mx = lax.fori_loop(n_far // 4 * 2, n_far // 2, two_tiles, carry_mx)
    carry, _ = lax.fori_loop(n_far // 2 * 2, n_far, one_tile, carry_mx)
    s1_ref[...] = scores(last)
    near0 = pl.multiple_of(par * (2 * SEL_TILE), SEL_TILE)
    before = bn_ref[pl.ds(near0, SEL_TILE), :] + jnp.where(last > 0, 0.0, NEG)
    carry = update(carry, s0_ref[...] + before, jnp.maximum(last - 1, 0))
    m_s, l_s, a0, a1 = update(carry, s1_ref[...] + bn_ref[pl.ds(near0 + SEL_TILE, SEL_TILE), :], last)
    inv_s = 1.0 / l_s
    heads = []
    for h in range(B_HEADS):
        hl = slice(h * Q_BLOCK, (h + 1) * Q_BLOCK)
        heads.append(part_ref[h * HEAD_DIM:(h + 1) * HEAD_DIM, :]
                     + (gs_ref[h:h + 1, :] * inv_s[:, hl]) * _head_block((a0, a1), h))
    o_ref[...] = jnp.transpose(jnp.concatenate(heads, axis=0)).astype(o_ref.dtype)


def _mixers_call(qt, kk, vt, p32, kcmp, vcmpt, kaug, ba, sink_row, fc, bw, bn, ovt, bsz, seq,
                 col_gate, topk):
    nb = seq // Q_BLOCK
    assert A_HEADS == B_HEADS and A_KV_HEADS == B_KV_GROUPS
    qw = B_HEADS * HEAD_DIM
    kw = B_KV_GROUPS * HEAD_DIM
    per_batch = lambda a: pl.BlockSpec((None,) + a.shape[1:], lambda b, i: (b,) + (0,) * (a.ndim - 1))
    const = lambda a: pl.BlockSpec(a.shape, lambda b, i: (0,) * a.ndim)
    qt_seg = lambda a: pl.BlockSpec((None, None, None, qw, Q_BLOCK), lambda b, i: (a, b, i, 0, 0))
    kk_seg = lambda a: pl.BlockSpec((None, seq, kw), lambda b, i: (b, 0, a))
    vt_seg = lambda a: pl.BlockSpec((None, None) + vt.shape[2:], lambda b, i: (a, b, 0, 0, 0))
    y_spec = pl.BlockSpec((None, Q_BLOCK, qw), lambda b, i: (b, i, 0))
    n_slc = ovt.shape[0]
    sl = B_KV_GROUPS * Q_BLOCK
    blk = lambda r, c: pl.BlockSpec((None, None, r, c), lambda b, i: (b, i, 0, 0))
    y_a, part, sel, gs = pl.pallas_call(
        functools.partial(_select_kernel, topk=topk),
        grid=(bsz, nb),
        in_specs=[qt_seg(QT_A), kk_seg(KK_A), vt_seg(VT_A), const(ba), const(sink_row),
                  qt_seg(QT_B), per_batch(kcmp), per_batch(vcmpt), kk_seg(KK_W), vt_seg(VT_W),
                  pl.BlockSpec((None, Q_BLOCK, V7X_LANES), lambda b, i: (b, i, col_gate // V7X_LANES)),
                  const(fc), const(bw), const(ovt)],
        out_specs=[y_spec, blk(qw, Q_BLOCK), blk(n_slc, sl), blk(B_HEADS, Q_BLOCK)],
        out_shape=[jax.ShapeDtypeStruct((bsz, seq, qw), BF16),
                   jax.ShapeDtypeStruct((bsz, nb, qw, Q_BLOCK), F32),
                   jax.ShapeDtypeStruct((bsz, nb, n_slc, sl), BF16),
                   jax.ShapeDtypeStruct((bsz, nb, B_HEADS, Q_BLOCK), F32)],
        scratch_shapes=[pltpu.VMEM((n_slc, sl), F32)],
        compiler_params=_cparams(("parallel", "arbitrary")),
        name="select",
    )(qt, kk, vt, ba, sink_row, qt, kcmp, vcmpt, kk, vt, p32, fc, bw, ovt)
    y_b = pl.pallas_call(
        _sweep_kernel,
        grid=(bsz, nb),
        in_specs=[qt_seg(QT_B), per_batch(kaug), vt_seg(VT_SL), const(bn),
                  blk(n_slc, sl), blk(qw, Q_BLOCK), blk(B_HEADS, Q_BLOCK)],
        out_specs=y_spec,
        out_shape=jax.ShapeDtypeStruct((bsz, seq, qw), BF16),
        scratch_shapes=[pltpu.VMEM((SEL_TILE, B_HEADS * Q_BLOCK), F32),
                        pltpu.VMEM((SEL_TILE, B_HEADS * Q_BLOCK), F32)],
        compiler_params=_cparams(("parallel", "arbitrary")),
        name="sweep",
    )(qt, kaug, vt, bn, sel, part, gs)
    return y_a, y_b


def _merge_kernel(x_ref, g_ref, ya_ref, yb_ref, wg_ref, wua_ref, wub_ref, wo_ref, o_ref):
    x = x_ref[...]
    d = x.shape[1]
    hb = _rms(x, g_ref[...]).astype(BF16)
    ga = jax.nn.sigmoid(jnp.dot(hb, wg_ref[:, :d], preferred_element_type=F32))
    gb = jax.nn.sigmoid(jnp.dot(hb, wg_ref[:, d:], preferred_element_type=F32))
    ua = jnp.dot(ya_ref[...], wua_ref[...], preferred_element_type=F32)
    ub = jnp.dot(yb_ref[...], wub_ref[...], preferred_element_type=F32)
    merged = ga * ua + gb * ub
    o_ref[...] = x + jnp.dot(merged.astype(BF16), wo_ref[...], preferred_element_type=F32)


def _merge(x2d, g, ya, yb, wg, wua, wub, wo, tm):
    n, d = x2d.shape
    const = lambda a: pl.BlockSpec(a.shape, lambda i: (0, 0))
    row = lambda a: pl.BlockSpec((tm, a.shape[1]), lambda i: (i, 0))
    return pl.pallas_call(
        _merge_kernel,
        grid=(n // tm,),
        in_specs=[row(x2d), const(g), row(ya), row(yb), const(wg), const(wua), const(wub), const(wo)],
        out_specs=row(x2d),
        out_shape=jax.ShapeDtypeStruct((n, d), F32),
        compiler_params=_cparams(("parallel",)),
        name="merge",
    )(x2d, g, ya, yb, wg, wua, wub, wo)


FFN_HALO = 16
FFN_GROUP = 6


def _ffn_kernel(xc_ref, xp_ref, gn_ref, wi_ref, cw_ref, cb_ref, wo_ref, gf_ref, o_ref, *, chunk):
    i = pl.program_id(1)
    xc = xc_ref[...]
    tm = xc.shape[0]
    gn = gn_ref[...]
    hp = _rms(xp_ref[...], gn) * jnp.where(i > 0, 1.0, 0.0)
    h = jnp.concatenate([hp, _rms(xc, gn)], axis=0).astype(BF16)
    d_ff = wo_ref.shape[0]

    def up(c0):
        return jnp.dot(h, wi_ref[:, c0:c0 + chunk], preferred_element_type=F32)

    def conv(ext, c0):
        cw = cw_ref[:, c0:c0 + chunk]
        out = cb_ref[:, c0:c0 + chunk]
        for k in range(CONV_WIDTH):
            off = FFN_HALO - (CONV_WIDTH - 1) + k
            out = out + cw[k:k + 1, :] * ext[off:off + tm]
        return out

    acc = None
    acts = []
    nxt = (up(0), up(d_ff))
    for c0 in range(0, d_ff, chunk):
        ext_u, ext_g = nxt
        if c0 + chunk < d_ff:
            nxt = (up(c0 + chunk), up(d_ff + c0 + chunk))
        acts.append((jax.nn.silu(conv(ext_g, d_ff + c0)) * conv(ext_u, c0)).astype(BF16))
        if len(acts) == FFN_GROUP or c0 + chunk >= d_ff:
            r0 = c0 + chunk - len(acts) * chunk
            part = jnp.dot(jnp.concatenate(acts, axis=1), wo_ref[r0:c0 + chunk, :],
                           preferred_element_type=F32)
            acc = part if acc is None else acc + part
            acts = []
    o_ref[...] = _rms(xc + acc, gf_ref[...])


def _ffn(x1, gn, wi, cw, cb, wo, gf, tm, chunk):
    bsz, seq, d = x1.shape
    const = lambda a: pl.BlockSpec(a.shape, lambda b, i: (0, 0))
    kern = functools.partial(_ffn_kernel, chunk=chunk)
    return pl.pallas_call(
        kern,
        grid=(bsz, seq // tm),
        in_specs=[pl.BlockSpec((None, tm, d), lambda b, i: (b, i, 0)),
                  pl.BlockSpec((None, FFN_HALO, d),
                               lambda b, i: (b, jnp.maximum(i * (tm // FFN_HALO) - 1, 0), 0)),
                  const(gn), const(wi), const(cw), const(cb), const(wo), const(gf)],
        out_specs=pl.BlockSpec((None, tm, d), lambda b, i: (b, i, 0)),
        out_shape=jax.ShapeDtypeStruct((bsz, seq, d), F32),
        compiler_params=_cparams(("parallel", "arbitrary")),
        name="ffn",
    )(x1, x1, gn, wi, cw, cb, wo, gf)


def _mixers(x, norm_mix, w_in, attn_sinks, cmp_pos_k, cmp_w1_k, cmp_w2_k, cmp_pos_v, cmp_w1_v,
            cmp_w2_v, table):
    bsz, seq, d = x.shape
    n = bsz * seq
    aq, akv = A_HEADS * HEAD_DIM, A_KV_HEADS * HEAD_DIM
    bq, bkv = B_HEADS * HEAD_DIM, B_KV_GROUPS * HEAD_DIM
    n_gate = 3 * B_HEADS
    assert seq % SEL_TILE == 0 and seq // SLC_BLOCK <= AUG_LANES - bkv
    splits = (aq, akv, akv, bq, bkv, bkv, bkv, bkv, bkv, bkv, n_gate, d, d)
    off = np.concatenate([[0], np.cumsum(splits)]).astype(int)
    seg = lambda k: w_in[:, off[k]:off[k + 1]]
    assert akv == bkv and aq == bq
    w_gate_nsa = jnp.pad(seg(10), ((0, 0), (0, V7X_LANES - n_gate)))
    w1 = jnp.concatenate([seg(0), seg(3), seg(1), seg(8), seg(6), seg(2), seg(7), seg(9),
                          seg(4), seg(5), w_gate_nsa], axis=1).astype(BF16)
    tm = min(1024, seq)
    g_mix = norm_mix.reshape(1, d)

    qt, kk, kaug, vt, p32 = _proj(x.reshape(n, d), g_mix, w1, aq, seq, tm)
    nb = seq // Q_BLOCK
    qt = qt.reshape(2, bsz, nb, aq, Q_BLOCK)
    kk = kk.reshape(bsz, seq, 2 * bkv)
    kaug = kaug.reshape(bsz, seq, AUG_LANES)
    vt = vt.reshape(3, bsz, nb, bkv, Q_BLOCK)
    p32 = p32.reshape(bsz, seq, 2 * bkv + V7X_LANES)

    table = table * LOG2E
    table_a, table_b = table[:, :A_HEADS], table[:, A_HEADS:]
    sink_row = jnp.repeat(attn_sinks * LOG2E, Q_BLOCK).reshape(1, A_HEADS * Q_BLOCK)

    ncp = seq // CMP_STRIDE
    n_cmp = (seq - CMP_BLOCK) // CMP_STRIDE + 1
    n_slc = seq // SLC_BLOCK
    topk = min(SLC_TOPK, n_slc)

    k_cmp = _compress(p32, 0, cmp_pos_k, cmp_w1_k, cmp_w2_k, n_cmp, False)
    v_cmp_t = _compress(p32, bkv, cmp_pos_v, cmp_w1_v, cmp_w2_v, n_cmp, True)

    fc = _cmp_bias_table(table_b, ncp)
    win_a, win_b = A_WINDOW + Q_BLOCK, NSA_WINDOW + Q_BLOCK
    bias_a, bw, bn = _toeplitz_tables(
        [_band_vector(table_a, win_a, A_WINDOW, A_WINDOW, False),
         _band_vector(table_b, win_b, NSA_WINDOW, NSA_WINDOW, False),
         _band_vector(table_b, 2 * SEL_TILE, SEL_TILE, None, True),
         _band_vector(table_b, 2 * SEL_TILE, SEL_TILE + Q_BLOCK, None, True)],
        [(0, 0), (1, 0), (2, 0), (2, 2 * SEL_TILE)],
        [win_a + A_WINDOW, win_b + NSA_WINDOW, 4 * SEL_TILE])
    ovt = jnp.asarray(_overlap_t(ncp, n_slc, n_cmp))
    y_a, y_b = _mixers_call(qt, kk, vt, p32, k_cmp, v_cmp_t, kaug, bias_a, sink_row, fc, bw, bn, ovt,
                            bsz, seq, 2 * bkv, topk)
    return y_a, y_b, w_in[:, off[11]:off[13]]


def _layer(x, norm_mix, w_in, attn_sinks, cmp_pos_k, cmp_w1_k, cmp_w2_k, cmp_pos_v, cmp_w1_v,
           cmp_w2_v, w_up_a, w_up_b, w_out, norm_ffn, w_ffn_in, conv_w, conv_b, w_ffn_out,
           table, norm_final):
    bsz, seq, d = x.shape
    n = bsz * seq
    y_a, y_b, w_merge_gates = _mixers(x, norm_mix, w_in, attn_sinks, cmp_pos_k, cmp_w1_k, cmp_w2_k,
                                      cmp_pos_v, cmp_w1_v, cmp_w2_v, table)

    x1 = _merge(x.reshape(n, d), norm_mix.reshape(1, d), y_a.reshape(n, -1), y_b.reshape(n, -1),
                w_merge_gates.astype(BF16), w_up_a.astype(BF16), w_up_b.astype(BF16),
                w_out.astype(BF16), min(1024, n))

    return _ffn(x1.reshape(bsz, seq, d), norm_ffn.reshape(1, d), w_ffn_in.astype(BF16), conv_w,
                conv_b.reshape(1, -1), w_ffn_out.astype(BF16), norm_final.reshape(1, d),
                min(512, seq), 256)


def kernel(x, norm_mix, w_in, attn_sinks, cmp_pos_k, cmp_w1_k, cmp_w2_k, cmp_pos_v, cmp_w1_v, cmp_w2_v, w_up_a, w_up_b, w_out, norm_ffn, w_ffn_in, conv_w, conv_b, w_ffn_out, rel_bias_table, norm_final):
    assert norm_mix.shape[0] == 1, "single-layer block"
    return _layer(x, norm_mix[0], w_in[0], attn_sinks[0], cmp_pos_k[0], cmp_w1_k[0], cmp_w2_k[0],
                  cmp_pos_v[0], cmp_w1_v[0], cmp_w2_v[0], w_up_a[0], w_up_b[0], w_out[0],
                  norm_ffn[0], w_ffn_in[0], conv_w[0], conv_b[0], w_ffn_out[0], rel_bias_table,
                  norm_final)
```

```python
import functools
import math

import numpy as np
import jax
import jax.numpy as jnp
from jax import lax
from jax.experimental import pallas as pl
from jax.experimental.pallas import tpu as pltpu

F32 = jnp.float32
BF16 = jnp.bfloat16

HEAD_DIM = 64
A_HEADS = 8
A_KV_HEADS = 2
A_WINDOW = 128
B_HEADS = 8
B_KV_GROUPS = 2
REP = 4
CMP_BLOCK = 32
CMP_STRIDE = 16
SLC_BLOCK = 64
SLC_SHIFT = 6
SLC_TOPK = 16
NSA_WINDOW = 512
NUM_BUCKETS = 32
MAX_DISTANCE = 128
CONV_WIDTH = 3
Q_BLOCK = 128
EPS = 1e-6
NEG = -1e30
BIG = 1e30
SCALE = HEAD_DIM ** -0.5
LOG2E = math.log2(math.e)
Q_SCALE = SCALE * LOG2E
SUM_ROWS = 16

V7X_LANES = 128
V7X_SUBLANES = 8
V7X_VMEM_BYTES = 64 * 1024 * 1024
VMEM_LIMIT = V7X_VMEM_BYTES * 7 // 8


def _cparams(semantics, n_inputs=None, fused_inputs=()):
    fusion = None if n_inputs is None else [k in fused_inputs for k in range(n_inputs)]
    return pltpu.CompilerParams(dimension_semantics=semantics, vmem_limit_bytes=VMEM_LIMIT,
                                allow_input_fusion=fusion)


def _bucket_np(dist):
    dist = np.maximum(dist, 0)
    max_exact = NUM_BUCKETS // 2
    d = np.maximum(dist, 1).astype(np.float64)
    large = max_exact + (np.log(d / max_exact) / math.log(MAX_DISTANCE / max_exact)
                         * (NUM_BUCKETS - max_exact)).astype(np.int32)
    large = np.minimum(large, NUM_BUCKETS - 1)
    return np.where(dist < max_exact, dist, large).astype(np.int32)


def _table_lookup(table, dist):
    idx = _bucket_np(dist).reshape(-1)
    onehot = np.zeros((NUM_BUCKETS, idx.size), np.float32)
    onehot[idx, np.arange(idx.size)] = 1.0
    vals = jnp.dot(table.T, jnp.asarray(onehot), precision=lax.Precision.HIGHEST)
    return vals.reshape((table.shape[1],) + dist.shape)


def _band_vector(table, n_keys, offset, window, shift_far):
    length = n_keys + Q_BLOCK
    m = np.arange(length)
    m = np.where(m < Q_BLOCK, m, m - length)
    dist = m + offset
    ok = dist >= 0 if window is None else (dist >= 0) & (dist < window)
    u = _table_lookup(table, dist)
    if shift_far:
        u = u - table[NUM_BUCKETS - 1][:, None]
    return jnp.where(ok[None, :], u, NEG)


def _toeplitz_kernel(*refs, dest):
    n_tab = len(dest)
    covered = [0] * (len(refs) - n_tab)
    for u_ref, (j, row0) in zip(refs[:n_tab], dest):
        o_ref = refs[n_tab + j]
        n_heads, length = u_ref.shape
        n_keys = length - Q_BLOCK
        covered[j] = max(covered[j], row0 + n_keys)
        u2 = jnp.concatenate([u_ref[...], u_ref[...]], axis=1)
        for r0 in range(0, n_keys, Q_BLOCK):
            w0 = (-r0 - (Q_BLOCK - 1)) % length
            win = u2[:, w0:w0 + 2 * Q_BLOCK]
            for h in range(n_heads):
                x = jnp.broadcast_to(win[h:h + 1, :], (Q_BLOCK, 2 * Q_BLOCK))
                x = pltpu.roll(x, Q_BLOCK + 1, 1, stride=1, stride_axis=0)
                o_ref[row0 + r0:row0 + r0 + Q_BLOCK, h * Q_BLOCK:(h + 1) * Q_BLOCK] = x[:, :Q_BLOCK]
    for o_ref, done in zip(refs[n_tab:], covered):
        if o_ref.shape[0] > done:
            o_ref[done:, :] = jnp.full((o_ref.shape[0] - done, o_ref.shape[1]), NEG, F32)


def _toeplitz_tables(us, dest, n_rows):
    n_heads = us[0].shape[0]
    kern = functools.partial(_toeplitz_kernel, dest=tuple(dest))
    return pl.pallas_call(
        kern,
        out_shape=[jax.ShapeDtypeStruct((r, n_heads * Q_BLOCK), F32) for r in n_rows],
        compiler_params=pltpu.CompilerParams(vmem_limit_bytes=VMEM_LIMIT),
        name="bias_tables",
    )(*us)


def _cmp_bias_table(table, ncp):
    q = np.arange(Q_BLOCK)
    last_pos = CMP_BLOCK - 1
    c_lo = -((MAX_DISTANCE - 1 + last_pos) // CMP_STRIDE)
    c_hi = (Q_BLOCK - 1 - last_pos) // CMP_STRIDE
    cs = np.arange(c_lo, c_hi + 1)
    dist = q[None, :] - CMP_STRIDE * cs[:, None] - last_pos
    near = jnp.where((dist >= 0)[None], _table_lookup(table, dist), NEG)
    n_heads = table.shape[1]
    near = jnp.transpose(near, (1, 0, 2)).reshape(len(cs), n_heads * Q_BLOCK)
    far = jnp.repeat(table[NUM_BUCKETS - 1], Q_BLOCK)[None, :]
    below = jnp.broadcast_to(far, (ncp + c_lo, n_heads * Q_BLOCK))
    above = jnp.full((ncp - 1 - c_hi, n_heads * Q_BLOCK), NEG, F32)
    return jnp.concatenate([below, near, above], axis=0)


def _overlap_t(n_cmp_pad, n_slc, n_cmp):
    r = SLC_BLOCK // CMP_STRIDE
    c = CMP_BLOCK // CMP_STRIDE
    j, m, n = np.meshgrid(np.arange(n_slc), np.arange(r), np.arange(c), indexing='ij')
    i = r * j + m - n
    ok = (i >= 0) & (i < n_cmp)
    mat = np.zeros((n_slc, n_cmp_pad), np.float32)
    np.add.at(mat, (j[ok], i[ok]), 1.0)
    return mat


def _rms(x, g):
    return x * lax.rsqrt(jnp.mean(x * x, axis=-1, keepdims=True) + EPS) * g


PROJ_CHUNK = 256
VT_A, VT_SL, VT_W = 0, 1, 2
QT_A, QT_B = 0, 1
KK_A, KK_W = 0, 1


def _proj_kernel(x_ref, g_ref, w_ref, qt_ref, kk_ref, kaug_ref, vt_ref, o32_ref, *, seq):
    tm = x_ref.shape[0]
    kv = B_KV_GROUPS * HEAD_DIM
    qw = qt_ref.shape[2]
    hb = _rms(x_ref[...], g_ref[...]).astype(BF16)

    def cols(c0, width=PROJ_CHUNK):
        return jnp.dot(hb, w_ref[:, c0:c0 + width], preferred_element_type=F32)

    def put_transposed(ref, a, r, row0=0):
        for j in range(tm // Q_BLOCK):
            ref[a, j, row0:row0 + r.shape[1], :] = jnp.transpose(
                r[j * Q_BLOCK:(j + 1) * Q_BLOCK, :]).astype(BF16)

    for c0 in range(0, 2 * qw, PROJ_CHUNK):
        put_transposed(qt_ref, c0 // qw, cols(c0) * Q_SCALE, c0 % qw)
    kk_ref[...] = cols(2 * qw).astype(BF16)
    r = cols(2 * qw + 2 * kv)
    pos = (pl.program_id(0) * tm) % seq + lax.broadcasted_iota(jnp.int32, (tm, kv), 0)
    onehot = lax.broadcasted_iota(jnp.int32, (tm, kv), 1) == jnp.right_shift(pos, SLC_SHIFT)
    kaug_ref[:, :kv] = r[:, :kv].astype(BF16)
    kaug_ref[:, kv:] = jnp.where(onehot, 1.0, 0.0).astype(BF16)
    put_transposed(vt_ref, VT_A, r[:, kv:])
    r = cols(2 * qw + 4 * kv)
    put_transposed(vt_ref, VT_SL, r[:, :kv])
    put_transposed(vt_ref, VT_W, r[:, kv:])
    o32_ref[:, :2 * kv] = cols(2 * qw + 6 * kv)
    o32_ref[:, 2 * kv:] = jax.nn.sigmoid(cols(2 * qw + 8 * kv, V7X_LANES))


def _proj(x2d, g, w1, qw, seq, tm):
    n, d = x2d.shape
    kv = B_KV_GROUPS * HEAD_DIM
    n32 = 2 * kv + V7X_LANES
    nb = tm // Q_BLOCK
    assert AUG_LANES == 2 * kv == PROJ_CHUNK and seq % tm == 0 and tm % Q_BLOCK == 0
    rows = lambda width: pl.BlockSpec((tm, width), lambda i: (i, 0))
    blocks_t = lambda a, width: pl.BlockSpec((a, nb, width, Q_BLOCK), lambda i: (0, i, 0, 0))
    kern = functools.partial(_proj_kernel, seq=seq)
    return pl.pallas_call(
        kern,
        grid=(n // tm,),
        in_specs=[rows(d), pl.BlockSpec((1, d), lambda i: (0, 0)),
                  pl.BlockSpec(w1.shape, lambda i: (0, 0))],
        out_specs=[blocks_t(2, qw), rows(2 * kv), rows(AUG_LANES), blocks_t(3, kv), rows(n32)],
        out_shape=[jax.ShapeDtypeStruct((2, n // Q_BLOCK, qw, Q_BLOCK), BF16),
                   jax.ShapeDtypeStruct((n, 2 * kv), BF16),
                   jax.ShapeDtypeStruct((n, AUG_LANES), BF16),
                   jax.ShapeDtypeStruct((3, n // Q_BLOCK, kv, Q_BLOCK), BF16),
                   jax.ShapeDtypeStruct((n, n32), F32)],
        compiler_params=_cparams(("parallel",)),
        name="proj",
    )(x2d, g, w1)


CMP_PAIR = 2


def _compress_kernel(x_ref, pos_ref, w1_ref, w2_ref, o_ref, *, n_cmp, transpose_out):
    ncp = x_ref.shape[0] // CMP_STRIDE

    def group_diag(w):
        zero = jnp.zeros(w.shape, w.dtype)
        return jnp.concatenate(
            [jnp.concatenate([w if g == gg else zero for gg in range(B_KV_GROUPS)], axis=1)
             for g in range(B_KV_GROUPS)], axis=0)

    halves = []
    for half in range(CMP_BLOCK // CMP_STRIDE):
        acc = None
        for l0 in range(half * CMP_STRIDE, (half + 1) * CMP_STRIDE, CMP_PAIR):
            lhs = jnp.concatenate(
                [(x_ref[pl.ds(l % CMP_STRIDE, ncp, stride=CMP_STRIDE), :]
                  + jnp.concatenate([pos_ref[l:l + 1, :]] * B_KV_GROUPS, axis=1)).astype(BF16)
                 for l in range(l0, l0 + CMP_PAIR)], axis=1)
            rhs = jnp.concatenate([group_diag(w1_ref[l]) for l in range(l0, l0 + CMP_PAIR)], axis=0)
            part = jnp.dot(lhs, rhs, preferred_element_type=F32)
            acc = part if acc is None else acc + part
        halves.append(acc)
    top, bottom = halves
    h1 = top + jnp.concatenate([bottom[1:], bottom[:1]], axis=0)
    o = jnp.dot(jax.nn.gelu(h1).astype(BF16), group_diag(w2_ref[...]), preferred_element_type=F32)
    row = lax.broadcasted_iota(jnp.int32, o.shape, 0)
    o = jnp.where(row < n_cmp, o, 0.0)
    if transpose_out:
        o = jnp.transpose(o)
    o_ref[...] = o.astype(o_ref.dtype)


def _compress(p32, col, pos, w1, w2, n_cmp, transpose_out):
    bsz, seq, _ = p32.shape
    ncp = seq // CMP_STRIDE
    gd = B_KV_GROUPS * HEAD_DIM
    w1 = w1.reshape(CMP_BLOCK, HEAD_DIM, -1).astype(BF16)
    w2 = w2.astype(BF16)
    oshape = (gd, ncp) if transpose_out else (ncp, gd)
    const = lambda a: pl.BlockSpec(a.shape, lambda b: (0,) * a.ndim)
    kern = functools.partial(_compress_kernel, n_cmp=n_cmp, transpose_out=transpose_out)
    return pl.pallas_call(
        kern,
        grid=(bsz,),
        in_specs=[pl.BlockSpec((None, seq, gd), lambda b: (b, 0, col // gd)),
                  const(pos), const(w1), const(w2)],
        out_specs=pl.BlockSpec((None,) + oshape, lambda b: (b, 0, 0)),
        out_shape=jax.ShapeDtypeStruct((bsz,) + oshape, BF16),
        compiler_params=_cparams(("parallel",)),
        name="compress",
    )(p32, pos, w1, w2)


def _block_diag_qt(qt):
    zero = jnp.zeros((HEAD_DIM, Q_BLOCK), BF16)
    n_groups = qt.shape[0] // (REP * HEAD_DIM)
    return jnp.concatenate([
        jnp.concatenate([qt[(REP * g + r) * HEAD_DIM:(REP * g + r + 1) * HEAD_DIM] if gg == g else zero
                         for gg in range(n_groups) for r in range(REP)], axis=1)
        for g in range(n_groups)], axis=0)


def _pv_by_group(vt, p, with_sum):
    n_groups = vt.shape[0] // HEAD_DIM
    gl = p.shape[1] // n_groups
    pb = p.astype(BF16)
    outs, sums = [], []
    for g in range(n_groups):
        lhs = vt[g * HEAD_DIM:(g + 1) * HEAD_DIM]
        if with_sum:
            lhs = jnp.concatenate([lhs, jnp.ones((SUM_ROWS, vt.shape[1]), BF16)], axis=0)
        r = jnp.dot(lhs, pb[:, g * gl:(g + 1) * gl], preferred_element_type=F32)
        outs.append(r[:HEAD_DIM])
        sums.append(r[HEAD_DIM:HEAD_DIM + 1])
    return outs, (jnp.concatenate(sums, axis=1) if with_sum else None)


def _scale_groups(outs, row):
    gl = outs[0].shape[1]
    return [o * row[:, g * gl:(g + 1) * gl] for g, o in enumerate(outs)]


def _head_block(outs, h):
    r = h % REP
    return outs[h // REP][:, r * Q_BLOCK:(r + 1) * Q_BLOCK]


def _banded_scores(i, wq, k_ref, bias_ref, window):
    n_blk = window // Q_BLOCK + 1
    span = n_blk * Q_BLOCK
    kb0 = jnp.maximum(i - (n_blk - 1), 0)
    shift = jnp.maximum(n_blk - 1 - i, 0)
    s = jnp.dot(k_ref[pl.ds(pl.multiple_of(kb0 * Q_BLOCK, Q_BLOCK), span), :], wq,
                preferred_element_type=F32)
    return s + bias_ref[pl.ds(pl.multiple_of(shift * Q_BLOCK, Q_BLOCK), span), :], kb0


def _swa_block(i, s, kb0, vt_ref, sink_ref):
    sink = sink_ref[...]
    m = jnp.maximum(jnp.max(s, axis=0, keepdims=True), sink)
    p = jnp.exp2(s - m)
    vt = jnp.concatenate([vt_ref[kb0 + j] for j in range(A_WINDOW // Q_BLOCK + 1)], axis=1)
    outs, l = _pv_by_group(vt, p, True)
    outs = _scale_groups(outs, 1.0 / (l + jnp.exp2(sink - m)))
    heads = [_head_block(outs, h) for h in range(A_HEADS)]
    return jnp.transpose(jnp.concatenate(heads, axis=0))


SEL_TILE = 2 * Q_BLOCK
AUG_LANES = 2 * V7X_LANES
CMP_SHIFT = 4
CMP_PER_QB = Q_BLOCK // CMP_STRIDE


def _mixers_kernel(qa_ref, ka_ref, vat_ref, ba_ref, sink_ref,
                   q_ref, kcmp_ref, vcmpt_ref, kaug_ref, vslt_ref, kw_ref, vwt_ref, gate_ref,
                   fc_ref, bw_ref, bn_ref, ovt_ref, oa_ref, o_ref, s0_ref, s1_ref, sr_ref, *, topk):
    i = pl.program_id(1)
    n_slc, ncp = ovt_ref.shape
    gl = REP * Q_BLOCK
    nl = B_KV_GROUPS * gl
    gd = B_KV_GROUPS * HEAD_DIM
    n_win = NSA_WINDOW // Q_BLOCK + 1

    wq = _block_diag_qt(q_ref[...])

    qlane = lax.broadcasted_iota(jnp.int32, (1, nl), 1) & (Q_BLOCK - 1)
    cmax = jnp.right_shift(qlane + i * Q_BLOCK - (CMP_BLOCK - 1), CMP_SHIFT)
    c0 = pl.multiple_of(ncp - CMP_PER_QB * i, V7X_SUBLANES)
    s = jnp.dot(kcmp_ref[...], wq, preferred_element_type=F32) + fc_ref[pl.ds(c0, ncp), :]
    m = jnp.max(s, axis=0, keepdims=True)
    p = jnp.exp2(s - m)
    l = jnp.sum(p, axis=0, keepdims=True)
    p = p * jnp.where(cmax >= 0, 1.0 / l, 0.0)
    o_c, _ = _pv_by_group(vcmpt_ref[...], p, False)
    psum = jnp.concatenate(
        [sum(p[:, g * gl + r * Q_BLOCK:g * gl + (r + 1) * Q_BLOCK] for r in range(REP))
         for g in range(B_KV_GROUPS)], axis=1)
    imp = jnp.dot(ovt_ref[...], psum, preferred_element_type=F32,
                  precision=lax.Precision.HIGHEST)

    sa, kb0_a = _banded_scores(i, _block_diag_qt(qa_ref[...]), ka_ref, ba_ref, A_WINDOW)
    oa_ref[...] = _swa_block(i, sa, kb0_a, vat_ref, sink_ref).astype(oa_ref.dtype)

    sw, kb0 = _banded_scores(i, wq, kw_ref, bw_ref, NSA_WINDOW)
    mw = jnp.max(sw, axis=0, keepdims=True)
    pw = jnp.exp2(sw - mw)
    vwt = jnp.concatenate([vwt_ref[kb0 + j] for j in range(n_win)], axis=1)
    o_w, l_w = _pv_by_group(vwt, pw, True)

    sl = B_KV_GROUPS * Q_BLOCK
    jt = lax.broadcasted_iota(jnp.int32, (n_slc, sl), 0)
    second_half = (lax.broadcasted_iota(jnp.int32, (1, sl), 1) & (Q_BLOCK - 1)) >= SLC_BLOCK
    qblk = 2 * i + second_half.astype(jnp.int32)
    forced = (jt == 0) | (jt == qblk) | (jt == qblk - 1)
    score = jnp.where(forced, BIG, jnp.where(jt > qblk, NEG, imp))
    sr_ref[...] = score
    sub = V7X_SUBLANES
    sub_iota = lax.broadcasted_iota(jnp.int32, (sub, sl), 0)
    ranks = []
    for v in range(0, n_slc, sub):
        tile = sr_ref[v:v + sub, :]
        acc = jnp.zeros((sub, sl), F32)
        for c in range(n_slc):
            row = sr_ref[c:c + 1, :]
            if c < v:
                ahead = row >= tile
            elif c >= v + sub:
                ahead = row > tile
            else:
                ahead = (row > tile) | ((row == tile) & (sub_iota > c - v))
            acc = acc + jnp.where(ahead, 1.0, 0.0)
        ranks.append(acc)
    selneg = jnp.where(jnp.concatenate(ranks, axis=0) < topk, 0.0, NEG).astype(BF16)
    wsel = jnp.concatenate([selneg[:, g * Q_BLOCK:(g + 1) * Q_BLOCK]
                            for g in range(B_KV_GROUPS) for _ in range(REP)], axis=1)
    w = jnp.concatenate([wq, wsel, jnp.zeros((AUG_LANES - gd - n_slc, nl), BF16)], axis=0)

    def scores(t):
        r0 = pl.multiple_of(t * SEL_TILE, SEL_TILE)
        return jnp.dot(kaug_ref[pl.ds(r0, SEL_TILE), :], w, preferred_element_type=F32)

    def update(carry, st, t, st_max=None):
        m_i, l_i, a0, a1 = carry
        if st_max is None:
            st_max = jnp.max(st, axis=0, keepdims=True)
        m_n = jnp.maximum(m_i, st_max)
        alpha = jnp.exp2(m_i - m_n)
        pt = jnp.exp2(st - m_n)
        blk0 = t * (SEL_TILE // Q_BLOCK)
        vt = jnp.concatenate([vslt_ref[blk0 + j] for j in range(SEL_TILE // Q_BLOCK)], axis=1)
        pv, l_t = _pv_by_group(vt, pt, True)
        a0, a1 = (a + b for a, b in zip(_scale_groups((a0, a1), alpha), pv))
        return m_n, alpha * l_i + l_t, a0, a1

    last = i // 2
    par = i % 2
    n_far = jnp.maximum(last - 1, 0)
    carry = (jnp.full((1, nl), NEG, F32), jnp.zeros((1, nl), F32),
             jnp.zeros((HEAD_DIM, gl), F32), jnp.zeros((HEAD_DIM, gl), F32))

    def stash(ref, t):
        st = scores(t)
        ref[...] = st
        return jnp.max(st, axis=0, keepdims=True)

    def two_tiles(u, carry_mx):
        carry, mx0 = carry_mx
        t = 2 * u
        mx1 = stash(s1_ref, t + 1)
        carry = update(carry, s0_ref[...], t, mx0)
        mx0 = stash(s0_ref, t + 2)
        return update(carry, s1_ref[...], t + 1, mx1), mx0

    def one_tile(t, carry_mx):
        carry, mx0 = carry_mx
        mx1 = stash(s1_ref, t + 1)
        carry = update(carry, s0_ref[...], t, mx0)
        s0_ref[...] = s1_ref[...]
        return carry, mx1

    carry_mx = (carry, stash(s0_ref, 0))
    carry_mx = lax.fori_loop(0, n_far // 4,
                             lambda v, c: two_tiles(2 * v + 1, two_tiles(2 * v, c)), carry_mx)
    carry_mx = lax.fori_loop(n_far // 4 * 2, n_far // 2, two_tiles, carry_mx)
    carry, _ = lax.fori_loop(n_far // 2 * 2, n_far, one_tile, carry_mx)
    s1_ref[...] = scores(last)
    near0 = pl.multiple_of(par * (2 * SEL_TILE), SEL_TILE)
    before = bn_ref[pl.ds(near0, SEL_TILE), :] + jnp.where(last > 0, 0.0, NEG)
    carry = update(carry, s0_ref[...] + before, jnp.maximum(last - 1, 0))
    m_s, l_s, a0, a1 = update(carry, s1_ref[...] + bn_ref[pl.ds(near0 + SEL_TILE, SEL_TILE), :], last)
    gt = jnp.transpose(gate_ref[...])
    inv_s, inv_w = 1.0 / l_s, 1.0 / l_w
    heads = []
    for h in range(B_HEADS):
        hl = slice(h * Q_BLOCK, (h + 1) * Q_BLOCK)
        heads.append(gt[3 * h:3 * h + 1, :] * _head_block(o_c, h)
                     + (gt[3 * h + 1:3 * h + 2, :] * inv_s[:, hl]) * _head_block((a0, a1), h)
                     + (gt[3 * h + 2:3 * h + 3, :] * inv_w[:, hl]) * _head_block(o_w, h))
    o_ref[...] = jnp.transpose(jnp.concatenate(heads, axis=0)).astype(o_ref.dtype)


def _mixers_call(qt, kk, vt, p32, kcmp, vcmpt, kaug, ba, sink_row, fc, bw, bn, ovt, bsz, seq,
                 col_gate, topk):
    nb = seq // Q_BLOCK
    assert A_HEADS == B_HEADS and A_KV_HEADS == B_KV_GROUPS
    qw = B_HEADS * HEAD_DIM
    kw = B_KV_GROUPS * HEAD_DIM
    per_batch = lambda a: pl.BlockSpec((None,) + a.shape[1:], lambda b, i: (b,) + (0,) * (a.ndim - 1))
    const = lambda a: pl.BlockSpec(a.shape, lambda b, i: (0,) * a.ndim)
    qt_seg = lambda a: pl.BlockSpec((None, None, None, qw, Q_BLOCK), lambda b, i: (a, b, i, 0, 0))
    kk_seg = lambda a: pl.BlockSpec((None, seq, kw), lambda b, i: (b, 0, a))
    vt_seg = lambda a: pl.BlockSpec((None, None) + vt.shape[2:], lambda b, i: (a, b, 0, 0, 0))
    y_spec = pl.BlockSpec((None, Q_BLOCK, qw), lambda b, i: (b, i, 0))
    kern = functools.partial(_mixers_kernel, topk=topk)
    return pl.pallas_call(
        kern,
        grid=(bsz, nb),
        in_specs=[qt_seg(QT_A), kk_seg(KK_A), vt_seg(VT_A), const(ba), const(sink_row),
                  qt_seg(QT_B), per_batch(kcmp), per_batch(vcmpt), per_batch(kaug), vt_seg(VT_SL),
                  kk_seg(KK_W), vt_seg(VT_W),
                  pl.BlockSpec((None, Q_BLOCK, V7X_LANES), lambda b, i: (b, i, col_gate // V7X_LANES)),
                  const(fc), const(bw), const(bn), const(ovt)],
        out_specs=[y_spec, y_spec],
        out_shape=[jax.ShapeDtypeStruct((bsz, seq, qw), BF16)] * 2,
        scratch_shapes=[pltpu.VMEM((SEL_TILE, B_HEADS * Q_BLOCK), F32),
                        pltpu.VMEM((SEL_TILE, B_HEADS * Q_BLOCK), F32),
                        pltpu.VMEM((ovt.shape[0], B_KV_GROUPS * Q_BLOCK), F32)],
        compiler_params=_cparams(("parallel", "arbitrary")),
        name="mixers",
    )(qt, kk, vt, ba, sink_row, qt, kcmp, vcmpt, kaug, vt, kk, vt, p32, fc, bw, bn, ovt)


def _merge_kernel(x_ref, g_ref, ya_ref, yb_ref, wg_ref, wua_ref, wub_ref, wo_ref, o_ref):
    x = x_ref[...]
    d = x.shape[1]
    hb = _rms(x, g_ref[...]).astype(BF16)
    ga = jax.nn.sigmoid(jnp.dot(hb, wg_ref[:, :d], preferred_element_type=F32))
    gb = jax.nn.sigmoid(jnp.dot(hb, wg_ref[:, d:], preferred_element_type=F32))
    ua = jnp.dot(ya_ref[...], wua_ref[...], preferred_element_type=F32)
    ub = jnp.dot(yb_ref[...], wub_ref[...], preferred_element_type=F32)
    merged = ga * ua + gb * ub
    o_ref[...] = x + jnp.dot(merged.astype(BF16), wo_ref[...], preferred_element_type=F32)


def _merge(x2d, g, ya, yb, wg, wua, wub, wo, tm):
    n, d = x2d.shape
    const = lambda a: pl.BlockSpec(a.shape, lambda i: (0, 0))
    row = lambda a: pl.BlockSpec((tm, a.shape[1]), lambda i: (i, 0))
    return pl.pallas_call(
        _merge_kernel,
        grid=(n // tm,),
        in_specs=[row(x2d), const(g), row(ya), row(yb), const(wg), const(wua), const(wub), const(wo)],
        out_specs=row(x2d),
        out_shape=jax.ShapeDtypeStruct((n, d), F32),
        compiler_params=_cparams(("parallel",), 8, (4, 5, 6, 7)),
        name="merge",
    )(x2d, g, ya, yb, wg, wua, wub, wo)


FFN_HALO = 16
FFN_GROUP = 6


def _ffn_kernel(xc_ref, xp_ref, gn_ref, wi_ref, cw_ref, cb_ref, wo_ref, gf_ref, o_ref, *, chunk):
    i = pl.program_id(1)
    xc = xc_ref[...]
    tm = xc.shape[0]
    gn = gn_ref[...]
    hp = _rms(xp_ref[...], gn) * jnp.where(i > 0, 1.0, 0.0)
    h = jnp.concatenate([hp, _rms(xc, gn)], axis=0).astype(BF16)
    d_ff = wo_ref.shape[0]

    def up(c0):
        return jnp.dot(h, wi_ref[:, c0:c0 + chunk], preferred_element_type=F32)

    def conv(ext, c0):
        cw = cw_ref[:, c0:c0 + chunk]
        out = cb_ref[:, c0:c0 + chunk]
        for k in range(CONV_WIDTH):
            off = FFN_HALO - (CONV_WIDTH - 1) + k
            out = out + cw[k:k + 1, :] * ext[off:off + tm]
        return out

    acc = None
    acts = []
    nxt = (up(0), up(d_ff))
    for c0 in range(0, d_ff, chunk):
        ext_u, ext_g = nxt
        if c0 + chunk < d_ff:
            nxt = (up(c0 + chunk), up(d_ff + c0 + chunk))
        acts.append((jax.nn.silu(conv(ext_g, d_ff + c0)) * conv(ext_u, c0)).astype(BF16))
        if len(acts) == FFN_GROUP or c0 + chunk >= d_ff:
            r0 = c0 + chunk - len(acts) * chunk
            part = jnp.dot(jnp.concatenate(acts, axis=1), wo_ref[r0:c0 + chunk, :],
                           preferred_element_type=F32)
            acc = part if acc is None else acc + part
            acts = []
    o_ref[...] = _rms(xc + acc, gf_ref[...])


def _ffn(x1, gn, wi, cw, cb, wo, gf, tm, chunk):
    bsz, seq, d = x1.shape
    const = lambda a: pl.BlockSpec(a.shape, lambda b, i: (0, 0))
    kern = functools.partial(_ffn_kernel, chunk=chunk)
    return pl.pallas_call(
        kern,
        grid=(bsz, seq // tm),
        in_specs=[pl.BlockSpec((None, tm, d), lambda b, i: (b, i, 0)),
                  pl.BlockSpec((None, FFN_HALO, d),
                               lambda b, i: (b, jnp.maximum(i * (tm // FFN_HALO) - 1, 0), 0)),
                  const(gn), const(wi), const(cw), const(cb), const(wo), const(gf)],
        out_specs=pl.BlockSpec((None, tm, d), lambda b, i: (b, i, 0)),
        out_shape=jax.ShapeDtypeStruct((bsz, seq, d), F32),
        compiler_params=_cparams(("parallel", "arbitrary"), 8, (3, 6)),
        name="ffn",
    )(x1, x1, gn, wi, cw, cb, wo, gf)


def _mixers(x, norm_mix, w_in, attn_sinks, cmp_pos_k, cmp_w1_k, cmp_w2_k, cmp_pos_v, cmp_w1_v,
            cmp_w2_v, table):
    bsz, seq, d = x.shape
    n = bsz * seq
    aq, akv = A_HEADS * HEAD_DIM, A_KV_HEADS * HEAD_DIM
    bq, bkv = B_HEADS * HEAD_DIM, B_KV_GROUPS * HEAD_DIM
    n_gate = 3 * B_HEADS
    assert seq % SEL_TILE == 0 and seq // SLC_BLOCK <= AUG_LANES - bkv
    splits = (aq, akv, akv, bq, bkv, bkv, bkv, bkv, bkv, bkv, n_gate, d, d)
    off = np.concatenate([[0], np.cumsum(splits)]).astype(int)
    seg = lambda k: w_in[:, off[k]:off[k + 1]]
    assert akv == bkv and aq == bq
    w_gate_nsa = jnp.pad(seg(10), ((0, 0), (0, V7X_LANES - n_gate)))
    w1 = jnp.concatenate([seg(0), seg(3), seg(1), seg(8), seg(6), seg(2), seg(7), seg(9),
                          seg(4), seg(5), w_gate_nsa], axis=1).astype(BF16)
    tm = min(1024, seq)
    g_mix = norm_mix.reshape(1, d)

    qt, kk, kaug, vt, p32 = _proj(x.reshape(n, d), g_mix, w1, aq, seq, tm)
    nb = seq // Q_BLOCK
    qt = qt.reshape(2, bsz, nb, aq, Q_BLOCK)
    kk = kk.reshape(bsz, seq, 2 * bkv)
    kaug = kaug.reshape(bsz, seq, AUG_LANES)
    vt = vt.reshape(3, bsz, nb, bkv, Q_BLOCK)
    p32 = p32.reshape(bsz, seq, 2 * bkv + V7X_LANES)

    table = table * LOG2E
    table_a, table_b = table[:, :A_HEADS], table[:, A_HEADS:]
    sink_row = jnp.repeat(attn_sinks * LOG2E, Q_BLOCK).reshape(1, A_HEADS * Q_BLOCK)

    ncp = seq // CMP_STRIDE
    n_cmp = (seq - CMP_BLOCK) // CMP_STRIDE + 1
    n_slc = seq // SLC_BLOCK
    topk = min(SLC_TOPK, n_slc)

    k_cmp = _compress(p32, 0, cmp_pos_k, cmp_w1_k, cmp_w2_k, n_cmp, False)
    v_cmp_t = _compress(p32, bkv, cmp_pos_v, cmp_w1_v, cmp_w2_v, n_cmp, True)

    fc = _cmp_bias_table(table_b, ncp)
    win_a, win_b = A_WINDOW + Q_BLOCK, NSA_WINDOW + Q_BLOCK
    bias_a, bw, bn = _toeplitz_tables(
        [_band_vector(table_a, win_a, A_WINDOW, A_WINDOW, False),
         _band_vector(table_b, win_b, NSA_WINDOW, NSA_WINDOW, False),
         _band_vector(table_b, 2 * SEL_TILE, SEL_TILE, None, True),
         _band_vector(table_b, 2 * SEL_TILE, SEL_TILE + Q_BLOCK, None, True)],
        [(0, 0), (1, 0), (2, 0), (2, 2 * SEL_TILE)],
        [win_a + A_WINDOW, win_b + NSA_WINDOW, 4 * SEL_TILE])
    ovt = jnp.asarray(_overlap_t(ncp, n_slc, n_cmp))
    y_a, y_b = _mixers_call(qt, kk, vt, p32, k_cmp, v_cmp_t, kaug, bias_a, sink_row, fc, bw, bn, ovt,
                            bsz, seq, 2 * bkv, topk)
    return y_a, y_b, w_in[:, off[11]:off[13]]


def _layer(x, norm_mix, w_in, attn_sinks, cmp_pos_k, cmp_w1_k, cmp_w2_k, cmp_pos_v, cmp_w1_v,
           cmp_w2_v, w_up_a, w_up_b, w_out, norm_ffn, w_ffn_in, conv_w, conv_b, w_ffn_out,
           table, norm_final):
    bsz, seq, d = x.shape
    n = bsz * seq
    y_a, y_b, w_merge_gates = _mixers(x, norm_mix, w_in, attn_sinks, cmp_pos_k, cmp_w1_k, cmp_w2_k,
                                      cmp_pos_v, cmp_w1_v, cmp_w2_v, table)

    x1 = _merge(x.reshape(n, d), norm_mix.reshape(1, d), y_a.reshape(n, -1), y_b.reshape(n, -1),
                w_merge_gates.astype(BF16), w_up_a.astype(BF16), w_up_b.astype(BF16),
                w_out.astype(BF16), min(1024, n))

    return _ffn(x1.reshape(bsz, seq, d), norm_ffn.reshape(1, d), w_ffn_in.astype(BF16), conv_w,
                conv_b.reshape(1, -1), w_ffn_out.astype(BF16), norm_final.reshape(1, d),
                min(512, seq), 256)


def kernel(x, norm_mix, w_in, attn_sinks, cmp_pos_k, cmp_w1_k, cmp_w2_k, cmp_pos_v, cmp_w1_v, cmp_w2_v, w_up_a, w_up_b, w_out, norm_ffn, w_ffn_in, conv_w, conv_b, w_ffn_out, rel_bias_table, norm_final):
    assert norm_mix.shape[0] == 1, "single-layer block"
    return _layer(x, norm_mix[0], w_in[0], attn_sinks[0], cmp_pos_k[0], cmp_w1_k[0], cmp_w2_k[0],
                  cmp_pos_v[0], cmp_w1_v[0], cmp_w2_v[0], w_up_a[0], w_up_b[0], w_out[0],
                  norm_ffn[0], w_ffn_in[0], conv_w[0], conv_b[0], w_ffn_out[0], rel_bias_table,
                  norm_final)
```
